```python
import math
import jax, jax.numpy as jnp
from jax import lax
import numpy as np

D_MODEL = 2048
BATCH = 1
SEQ = 8192
DEPTH = 4

GRID_W = 64
CTX_LEN = 256
A_HEADS = D_MODEL // 256
A_KV_HEADS = A_HEADS // 4
A_GROUP = A_HEADS // A_KV_HEADS
A_HEAD_DIM = 128
A_WIDTH = A_HEADS * A_HEAD_DIM
WINDOW = 128
ATTN_BLOCK = 128
ROPE_BASE = 10000.0
B_HEADS = 4
B_DV = (D_MODEL // 2) // B_HEADS
B_DK = B_DV // 2
B_WIDTH = B_HEADS * B_DV
GATE_RANK = 16
GATE_TAU = 16.0
GLA_CHUNK = 64
D_MIX = A_WIDTH + B_WIDTH
POOL_GROUPS = 4
POOL_WINDOWS = (2, 4, 8, 16)
FFN_DIM = 256 * ((8 * D_MODEL // 3 + 255) // 256)
N_EXPERTS = 8
TOP_K = 2
N_EVEN = (DEPTH + 1) // 2
N_ODD = DEPTH // 2
NORM_EPS = 1e-6
NEG_INF = -1e30
IN_SIZES = (A_KV_HEADS * A_HEAD_DIM, A_KV_HEADS * A_HEAD_DIM, B_HEADS * B_DK, B_WIDTH,
            GATE_RANK, GATE_RANK, A_WIDTH, B_HEADS * B_DK, B_WIDTH)
CTX_COLS = sum(IN_SIZES[:6])
IN_COLS = sum(IN_SIZES)

kernel_name = 'hybrid_dit_swa_gla_pool_moe'


def rmsnorm(x, g):
    xf = x.astype(jnp.float32)
    y = xf * lax.rsqrt(jnp.mean(xf * xf, axis=-1, keepdims=True) + NORM_EPS)
    return (y * g.astype(jnp.float32)).astype(x.dtype)


def modulate(h, shift, scale):
    return h * (1 + scale) + shift


def ada_params(cond, w, b, n):
    m = jax.nn.silu(cond) @ w[:, :n * D_MODEL] + b[:n * D_MODEL]
    return jnp.split(m, n, axis=-1)


def swiglu(h, wg, wu, wd):
    return (jax.nn.silu(h @ wg) * (h @ wu)) @ wd


def split_cols(p, sizes):
    return jnp.split(p, [int(i) for i in np.cumsum(sizes)[:-1]], axis=-1)


def grid_positions(T):
    rows = T // GRID_W
    row = jnp.broadcast_to(jnp.arange(rows)[:, None], (rows, GRID_W)).reshape(-1)
    col = jnp.broadcast_to(jnp.arange(GRID_W)[None, :], (rows, GRID_W)).reshape(-1)
    return row, col


def rope_1d(x, pos):
    half = x.shape[-1] // 2
    freqs = ROPE_BASE ** (-jnp.arange(half, dtype=jnp.float32) / half)
    ang = pos.astype(jnp.float32)[:, None] * freqs
    cos, sin = jnp.cos(ang)[:, None, :], jnp.sin(ang)[:, None, :]
    xf = x.astype(jnp.float32)
    x1, x2 = xf[..., :half], xf[..., half:]
    return jnp.concatenate([x1 * cos - x2 * sin, x1 * sin + x2 * cos], axis=-1).astype(x.dtype)


def rope_2d(x, row, col):
    d = x.shape[-1] // 2
    return jnp.concatenate([rope_1d(x[..., :d], row), rope_1d(x[..., d:], col)], axis=-1)


def window_attention(q, k, v, k_ctx, v_ctx, sink):
    f32 = jnp.float32
    B, T = q.shape[:2]
    L = k_ctx.shape[1]
    NB = T // ATTN_BLOCK
    qb = q.reshape(B, NB, ATTN_BLOCK, A_KV_HEADS, A_GROUP, A_HEAD_DIM).astype(f32) * A_HEAD_DIM ** -0.5

    def neighbours(t):
        tb = jnp.pad(t.reshape(B, NB, ATTN_BLOCK, A_KV_HEADS, A_HEAD_DIM).astype(f32),
                     ((0, 0), (1, 1), (0, 0), (0, 0), (0, 0)))
        return jnp.concatenate([tb[:, :-2], tb[:, 1:-1], tb[:, 2:]], axis=2)

    kw, vw = neighbours(k), neighbours(v)
    s_loc = jnp.einsum('bnqkgd,bnskd->bkgnqs', qb, kw)
    blk = jnp.arange(NB)[:, None, None]
    qpos = blk * ATTN_BLOCK + jnp.arange(ATTN_BLOCK)[None, :, None]
    kpos = (blk - 1) * ATTN_BLOCK + jnp.arange(3 * ATTN_BLOCK)[None, None, :]
    valid = (jnp.abs(qpos - kpos) <= WINDOW) & (kpos >= 0) & (kpos < T)
    s_loc = jnp.where(valid, s_loc, NEG_INF)
    s_ctx = jnp.einsum('bnqkgd,blkd->bkgnql', qb, k_ctx.astype(f32))
    s_sink = jnp.broadcast_to(sink.astype(f32).reshape(1, A_KV_HEADS, A_GROUP, 1, 1, 1), s_loc.shape[:-1] + (1,))
    p = jax.nn.softmax(jnp.concatenate([s_loc, s_ctx, s_sink], axis=-1), axis=-1)
    nl = 3 * ATTN_BLOCK
    o = (jnp.einsum('bkgnqs,bnskd->bnqkgd', p[..., :nl], vw)
         + jnp.einsum('bkgnql,blkd->bnqkgd', p[..., nl:nl + L], v_ctx.astype(f32)))
    return o.reshape(B, T, A_WIDTH).astype(q.dtype)


def context_attention(q, k, v, sink):
    f32 = jnp.float32
    B, L = q.shape[:2]
    qg = q.reshape(B, L, A_KV_HEADS, A_GROUP, A_HEAD_DIM).astype(f32) * A_HEAD_DIM ** -0.5
    s = jnp.einsum('bqkgd,blkd->bkgql', qg, k.astype(f32))
    s_sink = jnp.broadcast_to(sink.astype(f32).reshape(1, A_KV_HEADS, A_GROUP, 1, 1), s.shape[:-1] + (1,))
    p = jax.nn.softmax(jnp.concatenate([s, s_sink], axis=-1), axis=-1)
    o = jnp.einsum('bkgql,blkd->bqkgd', p[..., :L], v.astype(f32))
    return o.reshape(B, L, A_WIDTH).astype(q.dtype)


def to_heads_bhtd(t, h):
    B, T = t.shape[:2]
    return t.reshape(B, T, h, -1).transpose(0, 2, 1, 3)


def log_decay(lr, w_up, b_up):
    z = (lr @ w_up + b_up).astype(jnp.float32)
    return to_heads_bhtd(jax.nn.log_sigmoid(z) / GATE_TAU, B_HEADS)


def _chunk(t):
    B, H, T, F = t.shape
    return t.astype(jnp.float32).reshape(B, H, T // GLA_CHUNK, GLA_CHUNK, F)


def gla_states(k, v, log_a, s0):
    kc, vc = _chunk(k), _chunk(v)
    b = jnp.cumsum(_chunk(log_a), axis=3)
    b_last = b[:, :, :, -1:, :]
    u = jnp.einsum('bhnck,bhncv->bhnkv', kc * jnp.exp(b_last - b), vc)
    decay = jnp.exp(b_last[:, :, :, 0, :])

    def step(s, inp):
        d, du = inp
        return d[..., None] * s + du, s

    s_final, s_start = lax.scan(step, s0, (jnp.moveaxis(decay, 2, 0), jnp.moveaxis(u, 2, 0)))
    return jnp.moveaxis(s_start, 0, 2), s_final, b


def gla_outputs(q, k, v, b, s_start):
    qc, kc, vc = _chunk(q), _chunk(k), _chunk(v)
    qe, ke = qc * jnp.exp(b), kc * jnp.exp(-b)
    lower = jnp.tril(jnp.ones((GLA_CHUNK, GLA_CHUNK), dtype=bool))
    a = jnp.where(lower, jnp.einsum('bhnik,bhnjk->bhnij', qe, ke), 0.0)
    o = jnp.einsum('bhnij,bhnjv->bhniv', a, vc) + jnp.einsum('bhnik,bhnkv->bhniv', qe, s_start)
    B, H, N, C, DV = o.shape
    return o.reshape(B, H, N * C, DV)


def gla_direction(q, k, v, log_a, s0):
    s_start, s_final, b = gla_states(k, v, log_a, s0)
    if q is None:
        return None, s_final
    return gla_outputs(q, k, v, b, s_start), s_final


def gla_bidirectional(q, k, v, la_fwd, la_bwd, s0_fwd, s0_bwd):
    flip = lambda t: None if t is None else jnp.flip(t, axis=2)
    o_f, s_f = gla_direction(q, k, v, la_fwd, s0_fwd)
    o_b, s_b = gla_direction(flip(q), flip(k), flip(v), flip(la_bwd), s0_bwd)
    o = None if q is None else o_f + jnp.flip(o_b, axis=2)
    return o, s_f, s_b


def gla_output_merge(o, gate, g_gla):
    B, H, T, DV = o.shape
    o = rmsnorm(o.transpose(0, 2, 1, 3), g_gla).reshape(B, T, B_WIDTH)
    return (o * jax.nn.silu(gate.astype(jnp.float32))).astype(gate.dtype)


def even_mixer(hx, hc, w_in, g_qn, g_kn, sink, w_gate_up, b_gate_up, g_gla, w_out, ctx_out):
    B, T, _ = hx.shape
    row, col = grid_positions(T)
    px = split_cols(hx @ w_in, IN_SIZES)
    if ctx_out:
        pc = split_cols(hc @ w_in, IN_SIZES)
    else:
        pc = split_cols(hc @ w_in[:, :CTX_COLS], IN_SIZES[:6])
    heads = lambda t, h: t.reshape(t.shape[0], t.shape[1], h, -1)
    qa = rope_2d(rmsnorm(heads(px[6], A_HEADS), g_qn), row, col)
    ka = rope_2d(rmsnorm(heads(px[0], A_KV_HEADS), g_kn), row, col)
    va = heads(px[1], A_KV_HEADS)
    ka_c = rmsnorm(heads(pc[0], A_KV_HEADS), g_kn)
    va_c = heads(pc[1], A_KV_HEADS)
    oa = window_attention(qa, ka, va, ka_c, va_c, sink)

    def gla_in(p, with_q):
        q = to_heads_bhtd(p[7], B_HEADS) * B_DK ** -0.5 if with_q else None
        return (q, to_heads_bhtd(p[2], B_HEADS), to_heads_bhtd(p[3], B_HEADS),
                log_decay(p[4], w_gate_up[0], b_gate_up[0]), log_decay(p[5], w_gate_up[1], b_gate_up[1]))

    s0 = jnp.zeros((B, B_HEADS, B_DK, B_DV), jnp.float32)
    ob_c, s_fwd, s_bwd = gla_bidirectional(*gla_in(pc, ctx_out), s0, s0)
    ob, _, _ = gla_bidirectional(*gla_in(px, True), s_fwd, s_bwd)
    out_x = jnp.concatenate([oa, gla_output_merge(ob, px[8], g_gla)], axis=-1) @ w_out
    if not ctx_out:
        return out_x, None
    qa_c = rmsnorm(heads(pc[6], A_HEADS), g_qn)
    oa_c = context_attention(qa_c, ka_c, va_c, sink)
    out_c = jnp.concatenate([oa_c, gla_output_merge(ob_c, pc[8], g_gla)], axis=-1) @ w_out
    return out_x, out_c


def centred_mean(h, w):
    T = h.shape[1]
    cs = jnp.pad(jnp.cumsum(h.astype(jnp.float32), axis=1), ((0, 0), (1, 0), (0, 0)))
    t = jnp.arange(T)
    lo = jnp.maximum(t - w // 2, 0)
    hi = jnp.minimum(t + (w - w // 2), T)
    return ((cs[:, hi] - cs[:, lo]) / (hi - lo).astype(jnp.float32)[None, :, None]).astype(h.dtype)


def pool_mixer(h, w_pool, scale):
    B, T, D = h.shape
    hg = h.reshape(B, T, POOL_GROUPS, D // POOL_GROUPS)
    pooled = jnp.stack([centred_mean(hg[:, :, g], POOL_WINDOWS[g]) - hg[:, :, g] for g in range(POOL_GROUPS)], axis=2)
    y = jnp.einsum('btgc,gcd->btgd', pooled, w_pool).reshape(B, T, D)
    return y * scale


def moe_ffn(h, w_router, w_g, w_u, w_d):
    logits = (h @ w_router).astype(jnp.float32)
    top_v, top_i = lax.top_k(logits, TOP_K)
    gates = jax.nn.softmax(top_v, axis=-1)
    combine = jnp.sum(jax.nn.one_hot(top_i, N_EXPERTS, dtype=jnp.float32) * gates[..., None], axis=-2).astype(h.dtype)
    out = combine[..., 0:1] * swiglu(h, w_g[0], w_u[0], w_d[0])
    for e in range(1, N_EXPERTS):
        out = out + combine[..., e:e + 1] * swiglu(h, w_g[e], w_u[e], w_d[e])
    return out


def setup_inputs(seed: int = 0) -> dict:
    key = jax.random.key(seed)
    ks = jax.random.split(key, 24)
    f32 = jnp.float32
    D = D_MODEL
    GC = D // POOL_GROUPS

    def nrm(k, shape, s):
        return jax.random.normal(k, shape, f32) * s

    def gain(k, shape):
        return 1.0 + nrm(k, shape, 0.02)

    return {
        'x': nrm(ks[0], (BATCH, SEQ, D), 1.0),
        'c': nrm(ks[1], (BATCH, D), 1.0),
        'ctx': nrm(ks[2], (BATCH, CTX_LEN, D), 1.0),
        'c_ctx': nrm(ks[3], (D,), 1.0),
        'w_ada': nrm(ks[4], (DEPTH, D, 6 * D), 0.5 * D ** -0.5),
        'b_ada': nrm(ks[5], (DEPTH, 6 * D), 0.02),
        'norm_g': gain(ks[6], (DEPTH, 2, D)),
        'w_in': nrm(ks[7], (N_EVEN, D, IN_COLS), D ** -0.5),
        'g_qn': gain(ks[8], (N_EVEN, A_HEAD_DIM)),
        'g_kn': gain(ks[9], (N_EVEN, A_HEAD_DIM)),
        'attn_sink': nrm(ks[10], (N_EVEN, A_HEADS), 1.0),
        'w_gate_up': nrm(ks[11], (N_EVEN, 2, GATE_RANK, B_HEADS * B_DK), GATE_RANK ** -0.5),
        'b_gate_up': nrm(ks[12], (N_EVEN, 2, B_HEADS * B_DK), 0.1),
        'g_gla': gain(ks[13], (N_EVEN, B_DV)),
        'w_out': nrm(ks[14], (N_EVEN, D_MIX, D), D_MIX ** -0.5),
        'w_ffn_gate': nrm(ks[15], (N_EVEN, D, FFN_DIM), D ** -0.5),
        'w_ffn_up': nrm(ks[16], (N_EVEN, D, FFN_DIM), D ** -0.5),
        'w_ffn_down': nrm(ks[17], (N_EVEN, FFN_DIM, D), FFN_DIM ** -0.5),
        'w_pool': nrm(ks[18], (N_ODD, POOL_GROUPS, GC, GC), GC ** -0.5),
        'pool_scale': gain(ks[19], (N_ODD, D)),
        'w_router': nrm(ks[20], (N_ODD, D, N_EXPERTS), D ** -0.5),
        'w_exp_gate': nrm(ks[21], (N_ODD, N_EXPERTS, D, FFN_DIM), D ** -0.5),
        'w_exp_up': nrm(ks[22], (N_ODD, N_EXPERTS, D, FFN_DIM), D ** -0.5),
        'w_exp_down': nrm(ks[23], (N_ODD, N_EXPERTS, FFN_DIM, D), FFN_DIM ** -0.5),
    }


def reference(x, c, ctx, c_ctx, w_ada, b_ada, norm_g, w_in, g_qn, g_kn, attn_sink, w_gate_up, b_gate_up,
              g_gla, w_out, w_ffn_gate, w_ffn_up, w_ffn_down, w_pool, pool_scale, w_router,
              w_exp_gate, w_exp_up, w_exp_down):
    for l in range(DEPTH):
        ctx_later = any(j % 2 == 0 for j in range(l + 1, DEPTH))
        sh1, sc1, gt1, sh2, sc2, gt2 = [m[:, None, :] for m in ada_params(c, w_ada[l], b_ada[l], 6)]
        if l % 2 == 0:
            e = l // 2
            mc = ada_params(c_ctx, w_ada[l], b_ada[l], 6 if ctx_later else 2)
            hx = modulate(rmsnorm(x, norm_g[l, 0]), sh1, sc1)
            hc = modulate(rmsnorm(ctx, norm_g[l, 0]), mc[0], mc[1])
            ox, oc = even_mixer(hx, hc, w_in[e], g_qn[e], g_kn[e], attn_sink[e], w_gate_up[e], b_gate_up[e],
                                g_gla[e], w_out[e], ctx_later)
            x = x + gt1 * ox
            x = x + gt2 * swiglu(modulate(rmsnorm(x, norm_g[l, 1]), sh2, sc2), w_ffn_gate[e], w_ffn_up[e], w_ffn_down[e])
            if ctx_later:
                ctx = ctx + mc[2] * oc
                ctx = ctx + mc[5] * swiglu(modulate(rmsnorm(ctx, norm_g[l, 1]), mc[3], mc[4]),
                                           w_ffn_gate[e], w_ffn_up[e], w_ffn_down[e])
        else:
            o = l // 2
            x = x + gt1 * pool_mixer(modulate(rmsnorm(x, norm_g[l, 0]), sh1, sc1), w_pool[o], pool_scale[o])
            x = x + gt2 * moe_ffn(modulate(rmsnorm(x, norm_g[l, 1]), sh2, sc2), w_router[o],
                                  w_exp_gate[o], w_exp_up[o], w_exp_down[o])
            if ctx_later:
                mc = ada_params(c_ctx, w_ada[l], b_ada[l], 6)
                ctx = ctx + mc[2] * pool_mixer(modulate(rmsnorm(ctx, norm_g[l, 0]), mc[0], mc[1]),
                                               w_pool[o], pool_scale[o])
                ctx = ctx + mc[5] * moe_ffn(modulate(rmsnorm(ctx, norm_g[l, 1]), mc[3], mc[4]), w_router[o],
                                            w_exp_gate[o], w_exp_up[o], w_exp_down[o])
    return x
```

```python
import functools

import jax
import jax.numpy as jnp
import numpy as np
from jax import lax
from jax.experimental import pallas as pl
from jax.experimental.pallas import tpu as pltpu

F32 = jnp.float32
BF16 = jnp.bfloat16
HIGHEST = lax.Precision.HIGHEST

D_MODEL = 2048
GRID_W = 64
A_HEADS = 8
A_KV_HEADS = 2
A_GROUP = 4
A_HEAD_DIM = 128
A_WIDTH = A_HEADS * A_HEAD_DIM
ATTN_BLOCK = 128
ROPE_BASE = 10000.0
B_HEADS = 4
B_DV = 256
B_DK = 128
B_WIDTH = B_HEADS * B_DV
GATE_RANK = 16
GATE_TAU = 16.0
GLA_CHUNK = 64
POOL_GROUPS = 4
POOL_WINDOWS = (2, 4, 8, 16)
POOL_HALO = 8
N_EXPERTS = 8
NORM_EPS = 1e-6
NEG_INF = -1e30
LANES = 128

P_COLS = 4608
VMEM_LIMIT = 56 * 1024 * 1024


def _params(sem, vmem=VMEM_LIMIT):
    return pltpu.CompilerParams(dimension_semantics=sem, vmem_limit_bytes=vmem)


def _dot(a, b, precision=None):
    return jnp.dot(a, b, preferred_element_type=F32, precision=precision)


def _dot_nt(a, b):
    return lax.dot_general(a, b, (((1,), (1,)), ((), ())), preferred_element_type=F32)


def _dot_tn(a, b):
    return lax.dot_general(a, b, (((0,), (0,)), ((), ())), preferred_element_type=F32)


def _silu(x):
    return x * jax.nn.sigmoid(x)


def _norm_mod(x, g, shift, scale):
    ms = jnp.mean(x * x, axis=-1, keepdims=True)
    return (x * lax.rsqrt(ms + NORM_EPS) * g) * (1.0 + scale) + shift


def _ada_kernel(cond_ref, w_ref, b_ref, o_ref):
    s = _silu(cond_ref[...])
    o_ref[0] = _dot(s, w_ref[0], HIGHEST) + b_ref[0]


def _ada_all(cond, w_ada, b_ada):
    depth, d, n = w_ada.shape
    tn = 1024
    return pl.pallas_call(
        _ada_kernel,
        grid=(depth, n // tn),
        in_specs=[pl.BlockSpec((8, d), lambda l, j: (0, 0)),
                  pl.BlockSpec((1, d, tn), lambda l, j: (l, 0, j)),
                  pl.BlockSpec((1, 1, tn), lambda l, j: (l, 0, j))],
        out_specs=pl.BlockSpec((1, 8, tn), lambda l, j: (l, 0, j)),
        out_shape=jax.ShapeDtypeStruct((depth, 8, n), F32),
        compiler_params=_params(("parallel", "parallel")),
    )(cond, w_ada, b_ada.reshape(depth, 1, n))


def _ada_spec(k):
    return pl.BlockSpec((8, D_MODEL), lambda *_: (0, k))


def _inproj_kernel(x_ref, g_ref, sh_ref, sc_ref, w_ref, wg_ref, wup_ref, bup_ref, p_ref, la_ref, hn_ref, *, row):
    @pl.when(pl.program_id(1) == 0)
    def _():
        h = _norm_mod(x_ref[...], g_ref[...], sh_ref[row:row + 1, :], sc_ref[row:row + 1, :]).astype(BF16)
        hn_ref[...] = h
        lr = _dot(h, wg_ref[...])
        z = _dot(lr, wup_ref[...], HIGHEST) + bup_ref[...]
        la_ref[...] = (jnp.minimum(z, 0.0) - jnp.log1p(jnp.exp(-jnp.abs(z)))) * (1.0 / GATE_TAU)

    p_ref[...] = _dot(hn_ref[...], w_ref[...])


def _inproj(x, ada_l, row, g, w_main, w_gate, w_up, b_up):
    t, d = x.shape
    tm = min(512, t)
    tn = 512
    nla = 2 * B_HEADS * B_DK
    return pl.pallas_call(
        functools.partial(_inproj_kernel, row=row),
        grid=(t // tm, P_COLS // tn),
        in_specs=[pl.BlockSpec((tm, d), lambda i, j: (i, 0)),
                  pl.BlockSpec((1, d), lambda i, j: (0, 0)),
                  _ada_spec(0), _ada_spec(1),
                  pl.BlockSpec((d, tn), lambda i, j: (0, j)),
                  pl.BlockSpec((d, LANES), lambda i, j: (0, 0)),
                  pl.BlockSpec((LANES, nla), lambda i, j: (0, 0)),
                  pl.BlockSpec((1, nla), lambda i, j: (0, 0))],
        out_specs=[pl.BlockSpec((tm, tn), lambda i, j: (i, j)),
                   pl.BlockSpec((tm, nla), lambda i, j: (i, 0))],
        out_shape=[jax.ShapeDtypeStruct((t, P_COLS), F32),
                   jax.ShapeDtypeStruct((t, nla), F32)],
        scratch_shapes=[pltpu.VMEM((tm, d), BF16)],
        compiler_params=_params(("parallel", "arbitrary")),
    )(x, g, ada_l, ada_l, w_main, w_gate, w_up, b_up)


def _qkprep_kernel(q_ref, k_ref, cos_ref, sin_ref, gq_ref, gk_ref, qo_ref, ko_ref, *, rope):
    tm = q_ref.shape[0]
    lane = lax.broadcasted_iota(jnp.int32, (tm, A_HEAD_DIM), 1)
    first_half = (lane % 64) < 32

    def prep(xh, g, scale):
        ms = jnp.mean(xh * xh, axis=-1, keepdims=True)
        y = xh * lax.rsqrt(ms + NORM_EPS) * g
        if rope:
            partner = jnp.where(first_half, pltpu.roll(y, 96, 1), pltpu.roll(y, 32, 1))
            y = y * cos_ref[...] + partner * sin_ref[...]
        return (y * scale).astype(BF16)

    for h in range(A_HEADS):
        cols = slice(h * A_HEAD_DIM, (h + 1) * A_HEAD_DIM)
        qo_ref[:, cols] = prep(q_ref[:, cols], gq_ref[...], A_HEAD_DIM ** -0.5)
    for h in range(A_KV_HEADS):
        cols = slice(h * A_HEAD_DIM, (h + 1) * A_HEAD_DIM)
        ko_ref[:, cols] = prep(k_ref[:, cols], gk_ref[...], 1.0)


def _qkprep(p, cos, sin, g_qn, g_kn, rope):
    t = p.shape[0]
    tm = min(512, t)
    kw = A_KV_HEADS * A_HEAD_DIM
    return pl.pallas_call(
        functools.partial(_qkprep_kernel, rope=rope),
        grid=(t // tm,),
        in_specs=[pl.BlockSpec((tm, A_WIDTH), lambda i: (i, 0)),
                  pl.BlockSpec((tm, kw), lambda i: (i, 4096 // kw)),
                  pl.BlockSpec((tm, A_HEAD_DIM), lambda i: (i, 0)),
                  pl.BlockSpec((tm, A_HEAD_DIM), lambda i: (i, 0)),
                  pl.BlockSpec((1, A_HEAD_DIM), lambda i: (0, 0)),
                  pl.BlockSpec((1, A_HEAD_DIM), lambda i: (0, 0))],
        out_specs=[pl.BlockSpec((tm, A_WIDTH), lambda i: (i, 0)),
                   pl.BlockSpec((tm, kw), lambda i: (i, 0))],
        out_shape=[jax.ShapeDtypeStruct((t, A_WIDTH), BF16),
                   jax.ShapeDtypeStruct((t, kw), BF16)],
        compiler_params=_params(("parallel",)),
    )(p, p, cos, sin, g_qn, g_kn)


def _rope_tables(t):
    half = A_HEAD_DIM // 4
    freqs = ROPE_BASE ** (-jnp.arange(half, dtype=F32) / half)
    tok = jnp.arange(t)
    row = (tok // GRID_W).astype(F32)[:, None] * freqs
    col = (tok % GRID_W).astype(F32)[:, None] * freqs
    cos = jnp.concatenate([jnp.cos(row), jnp.cos(row), jnp.cos(col), jnp.cos(col)], axis=-1)
    sin = jnp.concatenate([-jnp.sin(row), jnp.sin(row), -jnp.sin(col), jnp.sin(col)], axis=-1)
    return cos, sin


def _softmax_pv(scores, values, sink_col):
    m = sink_col
    for s in scores:
        m = jnp.maximum(m, jnp.max(s, axis=-1, keepdims=True))
    denom = jnp.exp(sink_col - m)
    acc = None
    for s, v in zip(scores, values):
        p = jnp.exp(s - m)
        denom = denom + jnp.sum(p, axis=-1, keepdims=True)
        pv = _dot(p.astype(BF16), v)
        acc = pv if acc is None else acc + pv
    return acc / denom


def _attn_kernel(*refs, local):
    if local:
        q_ref, kp_ref, ko_ref, kn_ref, vp_ref, vo_ref, vn_ref, kc_ref, vc_ref, sink_ref, o_ref = refs
    else:
        q_ref, kc_ref, vc_ref, sink_ref, o_ref = refs
    n = pl.program_id(0)
    nb = pl.num_programs(0)
    rows = A_GROUP * ATTN_BLOCK
    if local:
        qi = lax.broadcasted_iota(jnp.int32, (rows, ATTN_BLOCK), 0) % ATTN_BLOCK
        kj = lax.broadcasted_iota(jnp.int32, (rows, ATTN_BLOCK), 1)
        mask_prev = (kj >= qi) & (n > 0)
        mask_next = (kj <= qi) & (n < nb - 1)
    for kv in range(A_KV_HEADS):
        kc = slice(kv * A_HEAD_DIM, (kv + 1) * A_HEAD_DIM)
        heads = [kv * A_GROUP + g for g in range(A_GROUP)]
        q4 = jnp.concatenate([q_ref[:, h * A_HEAD_DIM:(h + 1) * A_HEAD_DIM] for h in heads], axis=0)
        sink_col = jnp.concatenate(
            [jnp.broadcast_to(sink_ref[h:h + 1, 0:1], (ATTN_BLOCK, 1)) for h in heads], axis=0)
        scores, values = [], []
        if local:
            scores.append(jnp.where(mask_prev, _dot_nt(q4, kp_ref[:, kc]), NEG_INF))
            scores.append(_dot_nt(q4, ko_ref[:, kc]))
            scores.append(jnp.where(mask_next, _dot_nt(q4, kn_ref[:, kc]), NEG_INF))
            values += [vp_ref[:, kc], vo_ref[:, kc], vn_ref[:, kc]]
        scores.append(_dot_nt(q4, kc_ref[:, kc]))
        values.append(vc_ref[:, kc])
        o = _softmax_pv(scores, values, sink_col).astype(o_ref.dtype)
        for g, h in enumerate(heads):
            o_ref[:, h * A_HEAD_DIM:(h + 1) * A_HEAD_DIM] = o[g * ATTN_BLOCK:(g + 1) * ATTN_BLOCK, :]


def _attention(q, k, v, k_ctx, v_ctx, sink_tab, local):
    t = q.shape[0]
    nb = t // ATTN_BLOCK
    l = k_ctx.shape[0]
    kw = A_KV_HEADS * A_HEAD_DIM
    blk = lambda f: pl.BlockSpec((ATTN_BLOCK, kw), f)
    prev = lambda i: (jnp.maximum(i - 1, 0), 0)
    own = lambda i: (i, 0)
    nxt = lambda i: (jnp.minimum(i + 1, nb - 1), 0)
    full = pl.BlockSpec((l, kw), lambda i: (0, 0))
    in_specs = [pl.BlockSpec((ATTN_BLOCK, A_WIDTH), own)]
    args = [q]
    if local:
        in_specs += [blk(prev), blk(own), blk(nxt), blk(prev), blk(own), blk(nxt)]
        args += [k, k, k, v, v, v]
    in_specs += [full, full, pl.BlockSpec((A_HEADS, LANES), lambda i: (0, 0))]
    args += [k_ctx, v_ctx, sink_tab]
    return pl.pallas_call(
        functools.partial(_attn_kernel, local=local),
        grid=(nb,),
        in_specs=in_specs,
        out_specs=pl.BlockSpec((ATTN_BLOCK, A_WIDTH), own),
        out_shape=jax.ShapeDtypeStruct((t, A_WIDTH), BF16),
        compiler_params=_params(("parallel",)),
    )(*args)


def _gla_kernel(q_ref, k_ref, v_ref, la_ref, s0_ref, o_ref, sf_ref, st_ref, *, rev, nchunk):
    @pl.when(pl.program_id(1) == 0)
    def _():
        st_ref[...] = s0_ref[0]

    ii = lax.broadcasted_iota(jnp.int32, (GLA_CHUNK, GLA_CHUNK), 0)
    jj = lax.broadcasted_iota(jnp.int32, (GLA_CHUNK, GLA_CHUNK), 1)
    tri = (jj >= ii) if rev else (jj <= ii)
    tri_f = tri.astype(F32)
    order = range(nchunk - 1, -1, -1) if rev else range(nchunk)
    for c in order:
        rows = slice(c * GLA_CHUNK, (c + 1) * GLA_CHUNK)
        b = _dot(tri_f, la_ref[rows, :], HIGHEST)
        b_end = b[0:1, :] if rev else b[GLA_CHUNK - 1:GLA_CHUNK, :]
        k = k_ref[rows, :]
        qe = (q_ref[rows, :] * (B_DK ** -0.5) * jnp.exp(b)).astype(BF16)
        ke = (k * jnp.exp(-b)).astype(BF16)
        kd = (k * jnp.exp(b_end - b)).astype(BF16)
        v = v_ref[rows, :].astype(BF16)
        a = jnp.where(tri, _dot_nt(qe, ke), 0.0).astype(BF16)
        st = st_ref[...]
        o_ref[0, rows, :] = _dot(a, v) + _dot_nt(qe, st.astype(BF16))
        st_ref[...] = st * jnp.exp(b_end) + _dot_tn(v, kd)
    sf_ref[0] = st_ref[...]


def _gla_scan(p, la, s0, rev):
    t = p.shape[0]
    r = min(512, t)
    nblk = t // r
    rb = (lambda h, c: nblk - 1 - c) if rev else (lambda h, c: c)
    la_off = B_HEADS if rev else 0
    o, sf = pl.pallas_call(
        functools.partial(_gla_kernel, rev=rev, nchunk=r // GLA_CHUNK),
        grid=(B_HEADS, nblk),
        in_specs=[pl.BlockSpec((r, B_DK), lambda h, c: (rb(h, c), 3584 // B_DK + h)),
                  pl.BlockSpec((r, B_DK), lambda h, c: (rb(h, c), 3072 // B_DK + h)),
                  pl.BlockSpec((r, B_DV), lambda h, c: (rb(h, c), 1024 // B_DV + h)),
                  pl.BlockSpec((r, B_DK), lambda h, c: (rb(h, c), la_off + h)),
                  pl.BlockSpec((1, B_DV, B_DK), lambda h, c: (h, 0, 0))],
        out_specs=[pl.BlockSpec((1, r, B_DV), lambda h, c: (0, rb(h, c), h)),
                   pl.BlockSpec((1, B_DV, B_DK), lambda h, c: (h, 0, 0))],
        out_shape=[jax.ShapeDtypeStruct((1, t, B_WIDTH), F32),
                   jax.ShapeDtypeStruct((B_HEADS, B_DV, B_DK), F32)],
        scratch_shapes=[pltpu.VMEM((B_DV, B_DK), F32)],
        compiler_params=_params(("parallel", "arbitrary")),
    )(p, p, p, la, s0)
    return o[0], sf


def _outproj_kernel(oa_ref, of_ref, ob_ref, og_ref, gg_ref, w_ref, x_ref, gt_ref, o_ref, mix_ref, *, row):
    @pl.when(pl.program_id(1) == 0)
    def _():
        mix_ref[:, 0:A_WIDTH] = oa_ref[...]
        for h in range(B_HEADS):
            cols = slice(h * B_DV, (h + 1) * B_DV)
            o = of_ref[:, cols] + ob_ref[:, cols]
            ms = jnp.mean(o * o, axis=-1, keepdims=True)
            y = o * lax.rsqrt(ms + NORM_EPS) * gg_ref[...]
            mix_ref[:, A_WIDTH + h * B_DV:A_WIDTH + (h + 1) * B_DV] = (y * _silu(og_ref[:, cols])).astype(BF16)

    o_ref[...] = x_ref[...] + gt_ref[row:row + 1, :] * _dot(mix_ref[...], w_ref[...])


def _outproj(oa, o_f, o_b, p, g_gla, w_out, x, ada_l, row):
    t, d = x.shape
    tm = min(512, t)
    tn = 512
    return pl.pallas_call(
        functools.partial(_outproj_kernel, row=row),
        grid=(t // tm, d // tn),
        in_specs=[pl.BlockSpec((tm, A_WIDTH), lambda i, j: (i, 0)),
                  pl.BlockSpec((tm, B_WIDTH), lambda i, j: (i, 0)),
                  pl.BlockSpec((tm, B_WIDTH), lambda i, j: (i, 0)),
                  pl.BlockSpec((tm, B_WIDTH), lambda i, j: (i, 2048 // B_WIDTH)),
                  pl.BlockSpec((1, B_DV), lambda i, j: (0, 0)),
                  pl.BlockSpec((A_WIDTH + B_WIDTH, tn), lambda i, j: (0, j)),
                  pl.BlockSpec((tm, tn), lambda i, j: (i, j)),
                  pl.BlockSpec((8, tn), lambda i, j: (0, 2 * (d // tn) + j))],
        out_specs=pl.BlockSpec((tm, tn), lambda i, j: (i, j)),
        out_shape=jax.ShapeDtypeStruct((t, d), F32),
        scratch_shapes=[pltpu.VMEM((tm, A_WIDTH + B_WIDTH), BF16)],
        compiler_params=_params(("parallel", "arbitrary")),
    )(oa, o_f, o_b, p, g_gla, w_out, x, ada_l)


def _ffn_kernel(x_ref, g_ref, sh_ref, sc_ref, gt_ref, wg_ref, wu_ref, wd_ref, o_ref, hn_ref, acc_ref, *, row):
    f = pl.program_id(1)

    @pl.when(f == 0)
    def _():
        hn_ref[...] = _norm_mod(x_ref[...], g_ref[...], sh_ref[row:row + 1, :], sc_ref[row:row + 1, :]).astype(BF16)
        acc_ref[...] = jnp.zeros_like(acc_ref)

    h = hn_ref[...]
    a = (_silu(_dot(h, wg_ref[...])) * _dot(h, wu_ref[...])).astype(BF16)
    acc_ref[...] += _dot(a, wd_ref[...])

    @pl.when(f == pl.num_programs(1) - 1)
    def _():
        o_ref[...] = x_ref[...] + gt_ref[row:row + 1, :] * acc_ref[...]


def _ffn(x, ada_l, row, g, wg, wu, wd):
    t, d = x.shape
    ff = wg.shape[1]
    tm = min(512, t)
    tf = 512
    return pl.pallas_call(
        functools.partial(_ffn_kernel, row=row),
        grid=(t // tm, ff // tf),
        in_specs=[pl.BlockSpec((tm, d), lambda i, f: (i, 0)),
                  pl.BlockSpec((1, d), lambda i, f: (0, 0)),
                  _ada_spec(3), _ada_spec(4), _ada_spec(5),
                  pl.BlockSpec((d, tf), lambda i, f: (0, f)),
                  pl.BlockSpec((d, tf), lambda i, f: (0, f)),
                  pl.BlockSpec((tf, d), lambda i, f: (f, 0))],
        out_specs=pl.BlockSpec((tm, d), lambda i, f: (i, 0)),
        out_shape=jax.ShapeDtypeStruct((t, d), F32),
        scratch_shapes=[pltpu.VMEM((tm, d), BF16), pltpu.VMEM((tm, d), F32)],
        compiler_params=_params(("parallel", "arbitrary")),
    )(x, g, ada_l, ada_l, ada_l, wg, wu, wd)


def _pool_kernel(x_ref, xp_ref, xn_ref, g_ref, sh_ref, sc_ref, gt_ref, w_ref, ps_ref, o_ref, hb_ref, *, row, t):
    i = pl.program_id(0)
    tm = x_ref.shape[0]
    g = g_ref[...]
    sh = sh_ref[row:row + 1, :]
    sc = sc_ref[row:row + 1, :]
    hb_ref[0:POOL_HALO, :] = jnp.where(i > 0, _norm_mod(xp_ref[...], g, sh, sc), 0.0)
    hb_ref[POOL_HALO:POOL_HALO + tm, :] = _norm_mod(x_ref[...], g, sh, sc)
    hb_ref[POOL_HALO + tm:, :] = jnp.where(i < pl.num_programs(0) - 1, _norm_mod(xn_ref[...], g, sh, sc), 0.0)
    tpos = i * tm + lax.broadcasted_iota(jnp.int32, (tm, 1), 0)
    gc = D_MODEL // POOL_GROUPS
    for grp, w in enumerate(POOL_WINDOWS):
        cols = slice(grp * gc, (grp + 1) * gc)
        acc = None
        for dlt in range(-(w // 2), w - w // 2):
            piece = hb_ref[POOL_HALO + dlt:POOL_HALO + dlt + tm, cols]
            acc = piece if acc is None else acc + piece
        cnt = jnp.minimum(tpos + (w - w // 2), t) - jnp.maximum(tpos - w // 2, 0)
        pooled = acc / cnt.astype(F32) - hb_ref[POOL_HALO:POOL_HALO + tm, cols]
        y = _dot(pooled.astype(BF16), w_ref[grp])
        o_ref[:, cols] = x_ref[:, cols] + gt_ref[row:row + 1, cols] * (y * ps_ref[:, cols])


def _pool(x, ada_l, row, g, w_pool, pool_scale):
    t, d = x.shape
    tm = min(256, t)
    nh = t // POOL_HALO
    gc = d // POOL_GROUPS
    return pl.pallas_call(
        functools.partial(_pool_kernel, row=row, t=t),
        grid=(t // tm,),
        in_specs=[pl.BlockSpec((tm, d), lambda i: (i, 0)),
                  pl.BlockSpec((POOL_HALO, d), lambda i: (jnp.maximum(i * (tm // POOL_HALO) - 1, 0), 0)),
                  pl.BlockSpec((POOL_HALO, d), lambda i: (jnp.minimum((i + 1) * (tm // POOL_HALO), nh - 1), 0)),
                  pl.BlockSpec((1, d), lambda i: (0, 0)),
                  _ada_spec(0), _ada_spec(1), _ada_spec(2),
                  pl.BlockSpec((POOL_GROUPS, gc, gc), lambda i: (0, 0, 0)),
                  pl.BlockSpec((1, d), lambda i: (0, 0))],
        out_specs=pl.BlockSpec((tm, d), lambda i: (i, 0)),
        out_shape=jax.ShapeDtypeStruct((t, d), F32),
        scratch_shapes=[pltpu.VMEM((tm + 2 * POOL_HALO, d), F32)],
        compiler_params=_params(("parallel",)),
    )(x, x, x, g, ada_l, ada_l, ada_l, w_pool, pool_scale)


def _router_kernel(x_ref, g_ref, sh_ref, sc_ref, wr_ref, hn_ref, sel_ref, gate_ref, *, row):
    h = _norm_mod(x_ref[...], g_ref[...], sh_ref[row:row + 1, :], sc_ref[row:row + 1, :])
    hn_ref[...] = h
    lane = lax.broadcasted_iota(jnp.int32, (h.shape[0], LANES), 1)
    lane_f = lane.astype(F32)
    logits = jnp.where(lane < N_EXPERTS, _dot(h, wr_ref[...], HIGHEST), -jnp.inf)
    m1 = jnp.max(logits, axis=-1, keepdims=True)
    i1 = jnp.min(jnp.where(logits == m1, lane_f, float(LANES)), axis=-1, keepdims=True)
    rest = jnp.where(lane_f == i1, -jnp.inf, logits)
    m2 = jnp.max(rest, axis=-1, keepdims=True)
    i2 = jnp.min(jnp.where(rest == m2, lane_f, float(LANES)), axis=-1, keepdims=True)
    e2 = jnp.exp(m2 - m1)
    den = 1.0 + e2
    sel_ref[...] = jnp.where(lane == 0, i1, jnp.where(lane == 1, i2, 0.0)).astype(jnp.int32)
    gate_ref[...] = jnp.where(lane == 0, 1.0 / den, jnp.where(lane == 1, e2 / den, 0.0))


def _router(x, ada_l, row, g, w_router_pad):
    t, d = x.shape
    tm = min(256, t)
    return pl.pallas_call(
        functools.partial(_router_kernel, row=row),
        grid=(t // tm,),
        in_specs=[pl.BlockSpec((tm, d), lambda i: (i, 0)),
                  pl.BlockSpec((1, d), lambda i: (0, 0)),
                  _ada_spec(3), _ada_spec(4),
                  pl.BlockSpec((d, LANES), lambda i: (0, 0))],
        out_specs=[pl.BlockSpec((tm, d), lambda i: (i, 0)),
                   pl.BlockSpec((tm, LANES), lambda i: (i, 0)),
                   pl.BlockSpec((tm, LANES), lambda i: (i, 0))],
        out_shape=[jax.ShapeDtypeStruct((t, d), F32),
                   jax.ShapeDtypeStruct((t, LANES), jnp.int32),
                   jax.ShapeDtypeStruct((t, LANES), F32)],
        compiler_params=_params(("parallel",)),
    )(x, g, ada_l, ada_l, w_router_pad)


GATHER_CHUNK = 128


def _gather_kernel(idx_ref, src_ref, dst_ref, sem, *, n):
    nchunk = n // GATHER_CHUNK

    def copy(p, slot):
        return pltpu.make_async_copy(src_ref.at[idx_ref[p]], dst_ref.at[p], sem.at[slot])

    def start_chunk(c):
        def body(r, carry):
            copy(c * GATHER_CHUNK + r, c % 2).start()
            return carry
        lax.fori_loop(0, GATHER_CHUNK, body, 0)

    def wait_chunk(c):
        def body(r, carry):
            copy(c * GATHER_CHUNK + r, c % 2).wait()
            return carry
        lax.fori_loop(0, GATHER_CHUNK, body, 0)

    start_chunk(0)

    def step(c, carry):
        start_chunk(c)
        wait_chunk(c - 1)
        return carry

    lax.fori_loop(1, nchunk, step, 0)
    wait_chunk(nchunk - 1)


def _gather_rows(src, idx):
    ns, d = src.shape
    n = idx.shape[0]
    out = pl.pallas_call(
        functools.partial(_gather_kernel, n=n),
        grid_spec=pltpu.PrefetchScalarGridSpec(
            num_scalar_prefetch=1,
            grid=(1,),
            in_specs=[pl.BlockSpec(memory_space=pl.ANY)],
            out_specs=pl.BlockSpec(memory_space=pl.ANY),
            scratch_shapes=[pltpu.SemaphoreType.DMA((2,))]),
        out_shape=jax.ShapeDtypeStruct((n, 1, d), src.dtype),
        compiler_params=_params(("arbitrary",)),
    )(idx, src.reshape(ns, 1, d))
    return out.reshape(n, d)


def _moe_ffn_kernel(te_ref, tv_ref, xs_ref, wg_ref, wu_ref, wd_ref, y_ref, xb_ref):
    i = pl.program_id(0)
    f = pl.program_id(1)
    valid = tv_ref[i] > 0

    @pl.when(f == 0)
    def _():
        xb_ref[...] = xs_ref[...].astype(BF16)
        y_ref[...] = jnp.zeros_like(y_ref)

    @pl.when(valid)
    def _():
        h = xb_ref[...]
        a = (_silu(_dot(h, wg_ref[0])) * _dot(h, wu_ref[0])).astype(BF16)
        y_ref[...] += _dot(a, wd_ref[0])


def _moe_ffn(xs, tile_expert, tile_valid, wg, wu, wd, tm):
    n, d = xs.shape
    ff = wg.shape[2]
    tf = 512
    nf = ff // tf
    nt = n // tm
    fidx = lambda i, f, te, tv: jnp.where(tv[i] > 0, f, nf - 1)
    return pl.pallas_call(
        _moe_ffn_kernel,
        grid_spec=pltpu.PrefetchScalarGridSpec(
            num_scalar_prefetch=2,
            grid=(nt, nf),
            in_specs=[pl.BlockSpec((tm, d), lambda i, f, te, tv: (i, 0)),
                      pl.BlockSpec((1, d, tf), lambda i, f, te, tv: (te[i], 0, fidx(i, f, te, tv))),
                      pl.BlockSpec((1, d, tf), lambda i, f, te, tv: (te[i], 0, fidx(i, f, te, tv))),
                      pl.BlockSpec((1, tf, d), lambda i, f, te, tv: (te[i], fidx(i, f, te, tv), 0))],
            out_specs=pl.BlockSpec((tm, d), lambda i, f, te, tv: (i, 0)),
            scratch_shapes=[pltpu.VMEM((tm, d), BF16)]),
        out_shape=jax.ShapeDtypeStruct((n, d), F32),
        compiler_params=_params(("parallel", "arbitrary")),
    )(tile_expert, tile_valid, xs, wg, wu, wd)


def _combine_kernel(x_ref, y1_ref, y2_ref, gate_ref, gt_ref, o_ref, *, row):
    g1 = gate_ref[:, 0:1]
    g2 = gate_ref[:, 1:2]
    o_ref[...] = x_ref[...] + gt_ref[row:row + 1, :] * (g1 * y1_ref[...] + g2 * y2_ref[...])


def _combine(x, y12, off1, off2, gates, ada_l, row):
    t, d = x.shape
    tm = min(256, t)
    b1, b2 = off1 // tm, off2 // tm
    return pl.pallas_call(
        functools.partial(_combine_kernel, row=row),
        grid=(t // tm,),
        in_specs=[pl.BlockSpec((tm, d), lambda i: (i, 0)),
                  pl.BlockSpec((tm, d), lambda i: (b1 + i, 0)),
                  pl.BlockSpec((tm, d), lambda i: (b2 + i, 0)),
                  pl.BlockSpec((tm, LANES), lambda i: (i, 0)),
                  _ada_spec(5)],
        out_specs=pl.BlockSpec((tm, d), lambda i: (i, 0)),
        out_shape=jax.ShapeDtypeStruct((t, d), F32),
        compiler_params=_params(("parallel",)),
    )(x, y12, y12, gates, ada_l)


MOE_TILE = 512


def _route(sel, tm):
    n_tok = sel.shape[0]
    n_asg = 2 * n_tok
    nt = -(-n_asg // tm) + N_EXPERTS
    e_flat = sel.reshape(-1)
    onehot = (e_flat[:, None] == jnp.arange(N_EXPERTS, dtype=jnp.int32)[None, :]).astype(jnp.int32)
    csum = jnp.cumsum(onehot, axis=0)
    count = csum[-1]
    rank = jnp.sum((csum - onehot) * onehot, axis=1)
    tiles_e = (count + tm - 1) // tm
    tile_end = jnp.cumsum(tiles_e)
    tile_start = tile_end - tiles_e
    pos = jnp.sum(onehot * tile_start[None, :], axis=1) * tm + rank
    src = jnp.zeros((nt * tm,), jnp.int32).at[pos].set(jnp.arange(n_asg, dtype=jnp.int32) // 2)
    tile_id = jnp.arange(nt, dtype=jnp.int32)
    used = tile_end[-1]
    owner = lambda i: jnp.minimum(jnp.sum((i[:, None] >= tile_end[None, :]).astype(jnp.int32), axis=1), N_EXPERTS - 1)
    tile_valid = (tile_id < used).astype(jnp.int32)
    tile_expert = owner(jnp.minimum(tile_id, used - 1))
    return src, pos.reshape(n_tok, 2), tile_expert, tile_valid


def _moe(streams, ada_l, g, w_router_pad, wg, wu, wd):
    hs, sels, gates = [], [], []
    for x, row in streams:
        h, s, gt = _router(x, ada_l, row, g, w_router_pad)
        hs.append(h), sels.append(s[:, :2]), gates.append(gt)
    h_all = hs[0] if len(hs) == 1 else jnp.concatenate(hs, axis=0)
    sel_all = sels[0] if len(sels) == 1 else jnp.concatenate(sels, axis=0)
    src, pos, tile_expert, tile_valid = _route(sel_all, MOE_TILE)
    xs = _gather_rows(h_all, src)
    y = _moe_ffn(xs, tile_expert, tile_valid, wg, wu, wd, MOE_TILE)
    n_all = h_all.shape[0]
    y12 = _gather_rows(y, jnp.concatenate([pos[:, 0], pos[:, 1]]))
    outs, off = [], 0
    for (x, row), gt in zip(streams, gates):
        n = x.shape[0]
        outs.append(_combine(x, y12, off, n_all + off, gt, ada_l, row))
        off += n
    return outs


def _even_layer(x, ctx, ada_l, g1, g2, w_in, g_qn, g_kn, sink, w_gate_up, b_gate_up, g_gla, w_out,
                w_ffn_gate, w_ffn_up, w_ffn_down, rope_tabs, ctx_out):
    d = D_MODEL
    w_main = jnp.concatenate([w_in[:, 2080:3104], w_in[:, 1024:2048], w_in[:, 3616:4640], w_in[:, 512:1024],
                              w_in[:, 3104:3616], w_in[:, 0:256], w_in[:, 256:512]], axis=1).astype(BF16)
    w_gate = jnp.pad(w_in[:, 2048:2080], ((0, 0), (0, LANES - 2 * GATE_RANK))).astype(BF16)
    nk = B_HEADS * B_DK
    w_up = jnp.zeros((LANES, 2 * nk), F32)
    w_up = w_up.at[0:GATE_RANK, 0:nk].set(w_gate_up[0]).at[GATE_RANK:2 * GATE_RANK, nk:].set(w_gate_up[1])
    b_up = b_gate_up.reshape(1, 2 * nk)
    w_out_b = w_out.astype(BF16)
    wg_b, wu_b, wd_b = w_ffn_gate.astype(BF16), w_ffn_up.astype(BF16), w_ffn_down.astype(BF16)
    gq, gk, gg = g_qn.reshape(1, -1), g_kn.reshape(1, -1), g_gla.reshape(1, -1)
    sink_tab = jnp.broadcast_to(sink[:, None], (A_HEADS, LANES))
    cos, sin = rope_tabs

    pc, la_c = _inproj(ctx, ada_l, 1, g1, w_main, w_gate, w_up, b_up)
    qc, kc = _qkprep(pc, cos, sin, gq, gk, rope=False)
    vc = pc[:, 4352:4608].astype(BF16)
    s0 = jnp.zeros((B_HEADS, B_DV, B_DK), F32)
    oc_f, s_fwd = _gla_scan(pc, la_c, s0, rev=False)
    oc_b, s_bwd = _gla_scan(pc, la_c, s0, rev=True)

    px, la_x = _inproj(x, ada_l, 0, g1, w_main, w_gate, w_up, b_up)
    qx, kx = _qkprep(px, cos, sin, gq, gk, rope=True)
    vx = px[:, 4352:4608].astype(BF16)
    oa = _attention(qx, kx, vx, kc, vc, sink_tab, local=True)
    ox_f, _ = _gla_scan(px, la_x, s_fwd, rev=False)
    ox_b, _ = _gla_scan(px, la_x, s_bwd, rev=True)
    x = _outproj(oa, ox_f, ox_b, px, gg, w_out_b, x, ada_l, 0)
    x = _ffn(x, ada_l, 0, g2, wg_b, wu_b, wd_b)
    if ctx_out:
        oa_c = _attention(qc, None, None, kc, vc, sink_tab, local=False)
        ctx = _outproj(oa_c, oc_f, oc_b, pc, gg, w_out_b, ctx, ada_l, 1)
        ctx = _ffn(ctx, ada_l, 1, g2, wg_b, wu_b, wd_b)
    return x, ctx


def _odd_layer(x, ctx, ada_l, g1, g2, w_pool, pool_scale, w_router, w_exp_gate, w_exp_up, w_exp_down, ctx_out):
    w_pool_b = w_pool.astype(BF16)
    ps = pool_scale.reshape(1, -1)
    wr = jnp.pad(w_router, ((0, 0), (0, LANES - N_EXPERTS)))
    wg_b, wu_b, wd_b = w_exp_gate.astype(BF16), w_exp_up.astype(BF16), w_exp_down.astype(BF16)
    x = _pool(x, ada_l, 0, g1, w_pool_b, ps)
    if ctx_out:
        ctx = _pool(ctx, ada_l, 1, g1, w_pool_b, ps)
        x, ctx = _moe([(x, 0), (ctx, 1)], ada_l, g2, wr, wg_b, wu_b, wd_b)
    else:
        (x,) = _moe([(x, 0)], ada_l, g2, wr, wg_b, wu_b, wd_b)
    return x, ctx


def kernel(x, c, ctx, c_ctx, w_ada, b_ada, norm_g, w_in, g_qn, g_kn, attn_sink, w_gate_up, b_gate_up, g_gla, w_out,
           w_ffn_gate, w_ffn_up, w_ffn_down, w_pool, pool_scale, w_router, w_exp_gate, w_exp_up, w_exp_down):
    depth = w_ada.shape[0]
    xs = x[0]
    cs = ctx[0]
    t = xs.shape[0]
    cond = jnp.zeros((8, D_MODEL), F32).at[0].set(c[0]).at[1].set(c_ctx)
    ada = _ada_all(cond, w_ada, b_ada)
    rope_tabs = _rope_tables(t)
    for l in range(depth):
        ctx_later = any(j % 2 == 0 for j in range(l + 1, depth))
        g1 = norm_g[l, 0].reshape(1, -1)
        g2 = norm_g[l, 1].reshape(1, -1)
        if l % 2 == 0:
            e = l // 2
            xs, cs = _even_layer(xs, cs, ada[l], g1, g2, w_in[e], g_qn[e], g_kn[e], attn_sink[e], w_gate_up[e],
                                 b_gate_up[e], g_gla[e], w_out[e], w_ffn_gate[e], w_ffn_up[e], w_ffn_down[e],
                                 rope_tabs, ctx_later)
        else:
            o = l // 2
            xs, cs = _odd_layer(xs, cs, ada[l], g1, g2, w_pool[o], pool_scale[o], w_router[o],
                                w_exp_gate[o], w_exp_up[o], w_exp_down[o], ctx_later)
    return xs[None]
```

```python
import functools

import jax
import jax.numpy as jnp
import numpy as np
from jax import lax
from jax.experimental import pallas as pl
from jax.experimental.pallas import tpu as pltpu

F32 = jnp.float32
BF16 = jnp.bfloat16
HIGHEST = lax.Precision.HIGHEST

D_MODEL = 2048
GRID_W = 64
A_HEADS = 8
A_KV_HEADS = 2
A_GROUP = 4
A_HEAD_DIM = 128
A_WIDTH = A_HEADS * A_HEAD_DIM
ATTN_BLOCK = 128
ROPE_BASE = 10000.0
B_HEADS = 4
B_DV = 256
B_DK = 128
B_WIDTH = B_HEADS * B_DV
GATE_RANK = 16
GATE_TAU = 16.0
GLA_CHUNK = 64
POOL_GROUPS = 4
POOL_WINDOWS = (2, 4, 8, 16)
POOL_HALO = 8
N_EXPERTS = 8
NORM_EPS = 1e-6
NEG_INF = -1e30
LANES = 128

P_COLS = 4608
VMEM_LIMIT = 56 * 1024 * 1024


def _params(sem, vmem=VMEM_LIMIT):
    return pltpu.CompilerParams(dimension_semantics=sem, vmem_limit_bytes=vmem)


def _dot(a, b, precision=None):
    return jnp.dot(a, b, preferred_element_type=F32, precision=precision)


def _dot_nt(a, b):
    return lax.dot_general(a, b, (((1,), (1,)), ((), ())), preferred_element_type=F32)


def _dot_tn(a, b):
    return lax.dot_general(a, b, (((0,), (0,)), ((), ())), preferred_element_type=F32)


def _silu(x):
    return x * jax.nn.sigmoid(x)


def _norm_mod(x, g, shift, scale):
    ms = jnp.mean(x * x, axis=-1, keepdims=True)
    return (x * lax.rsqrt(ms + NORM_EPS) * g) * (1.0 + scale) + shift


def _ada_kernel(cond_ref, w_ref, b_ref, o_ref):
    s = _silu(cond_ref[...])
    o_ref[0] = _dot(s, w_ref[0], HIGHEST) + b_ref[0]


def _ada_all(cond, w_ada, b_ada):
    depth, d, n = w_ada.shape
    tn = 1024
    return pl.pallas_call(
        _ada_kernel,
        grid=(depth, n // tn),
        in_specs=[pl.BlockSpec((8, d), lambda l, j: (0, 0)),
                  pl.BlockSpec((1, d, tn), lambda l, j: (l, 0, j)),
                  pl.BlockSpec((1, 1, tn), lambda l, j: (l, 0, j))],
        out_specs=pl.BlockSpec((1, 8, tn), lambda l, j: (l, 0, j)),
        out_shape=jax.ShapeDtypeStruct((depth, 8, n), F32),
        compiler_params=_params(("parallel", "parallel")),
    )(cond, w_ada, b_ada.reshape(depth, 1, n))


def _ada_spec(k):
    return pl.BlockSpec((8, D_MODEL), lambda *_: (0, k))


def _inproj_kernel(x_ref, g_ref, sh_ref, sc_ref, w_ref, wg_ref, wup_ref, bup_ref, p_ref, la_ref, hn_ref, *, row):
    @pl.when(pl.program_id(1) == 0)
    def _():
        h = _norm_mod(x_ref[...], g_ref[...], sh_ref[row:row + 1, :], sc_ref[row:row + 1, :]).astype(BF16)
        hn_ref[...] = h
        lr = _dot(h, wg_ref[...])
        z = _dot(lr, wup_ref[...], HIGHEST) + bup_ref[...]
        la_ref[...] = (jnp.minimum(z, 0.0) - jnp.log1p(jnp.exp(-jnp.abs(z)))) * (1.0 / GATE_TAU)

    p_ref[...] = _dot(hn_ref[...], w_ref[...])


def _inproj(x, ada_l, row, g, w_main, w_gate, w_up, b_up):
    t, d = x.shape
    tm = min(512, t)
    tn = 1536
    nla = 2 * B_HEADS * B_DK
    return pl.pallas_call(
        functools.partial(_inproj_kernel, row=row),
        grid=(t // tm, P_COLS // tn),
        in_specs=[pl.BlockSpec((tm, d), lambda i, j: (i, 0)),
                  pl.BlockSpec((1, d), lambda i, j: (0, 0)),
                  _ada_spec(0), _ada_spec(1),
                  pl.BlockSpec((d, tn), lambda i, j: (0, j)),
                  pl.BlockSpec((d, LANES), lambda i, j: (0, 0)),
                  pl.BlockSpec((LANES, nla), lambda i, j: (0, 0)),
                  pl.BlockSpec((1, nla), lambda i, j: (0, 0))],
        out_specs=[pl.BlockSpec((tm, tn), lambda i, j: (i, j)),
                   pl.BlockSpec((tm, nla), lambda i, j: (i, 0))],
        out_shape=[jax.ShapeDtypeStruct((t, P_COLS), F32),
                   jax.ShapeDtypeStruct((t, nla), F32)],
        scratch_shapes=[pltpu.VMEM((tm, d), BF16)],
        compiler_params=_params(("parallel", "arbitrary")),
    )(x, g, ada_l, ada_l, w_main, w_gate, w_up, b_up)


def _qkprep_kernel(q_ref, k_ref, cos_ref, sin_ref, gq_ref, gk_ref, qo_ref, ko_ref, *, rope):
    tm = q_ref.shape[0]
    lane = lax.broadcasted_iota(jnp.int32, (tm, A_HEAD_DIM), 1)
    first_half = (lane % 64) < 32

    def prep(xh, g, scale):
        ms = jnp.mean(xh * xh, axis=-1, keepdims=True)
        y = xh * lax.rsqrt(ms + NORM_EPS) * g
        if rope:
            partner = jnp.where(first_half, pltpu.roll(y, 96, 1), pltpu.roll(y, 32, 1))
            y = y * cos_ref[...] + partner * sin_ref[...]
        return (y * scale).astype(BF16)

    for h in range(A_HEADS):
        cols = slice(h * A_HEAD_DIM, (h + 1) * A_HEAD_DIM)
        qo_ref[:, cols] = prep(q_ref[:, cols], gq_ref[...], A_HEAD_DIM ** -0.5)
    for h in range(A_KV_HEADS):
        cols = slice(h * A_HEAD_DIM, (h + 1) * A_HEAD_DIM)
        ko_ref[:, cols] = prep(k_ref[:, cols], gk_ref[...], 1.0)


def _qkprep(p, cos, sin, g_qn, g_kn, rope):
    t = p.shape[0]
    tm = min(512, t)
    kw = A_KV_HEADS * A_HEAD_DIM
    return pl.pallas_call(
        functools.partial(_qkprep_kernel, rope=rope),
        grid=(t // tm,),
        in_specs=[pl.BlockSpec((tm, A_WIDTH), lambda i: (i, 0)),
                  pl.BlockSpec((tm, kw), lambda i: (i, 4096 // kw)),
                  pl.BlockSpec((tm, A_HEAD_DIM), lambda i: (i, 0)),
                  pl.BlockSpec((tm, A_HEAD_DIM), lambda i: (i, 0)),
                  pl.BlockSpec((1, A_HEAD_DIM), lambda i: (0, 0)),
                  pl.BlockSpec((1, A_HEAD_DIM), lambda i: (0, 0))],
        out_specs=[pl.BlockSpec((tm, A_WIDTH), lambda i: (i, 0)),
                   pl.BlockSpec((tm, kw), lambda i: (i, 0))],
        out_shape=[jax.ShapeDtypeStruct((t, A_WIDTH), BF16),
                   jax.ShapeDtypeStruct((t, kw), BF16)],
        compiler_params=_params(("parallel",)),
    )(p, p, cos, sin, g_qn, g_kn)


def _rope_tables(t):
    half = A_HEAD_DIM // 4
    freqs = ROPE_BASE ** (-jnp.arange(half, dtype=F32) / half)
    tok = jnp.arange(t)
    row = (tok // GRID_W).astype(F32)[:, None] * freqs
    col = (tok % GRID_W).astype(F32)[:, None] * freqs
    cos = jnp.concatenate([jnp.cos(row), jnp.cos(row), jnp.cos(col), jnp.cos(col)], axis=-1)
    sin = jnp.concatenate([-jnp.sin(row), jnp.sin(row), -jnp.sin(col), jnp.sin(col)], axis=-1)
    return cos, sin


def _softmax_pv(scores, values, sink_col):
    m = sink_col
    for s in scores:
        m = jnp.maximum(m, jnp.max(s, axis=-1, keepdims=True))
    denom = jnp.exp(sink_col - m)
    acc = None
    for s, v in zip(scores, values):
        p = jnp.exp(s - m)
        denom = denom + jnp.sum(p, axis=-1, keepdims=True)
        pv = _dot(p.astype(BF16), v)
        acc = pv if acc is None else acc + pv
    return acc / denom


def _attn_kernel(*refs, local):
    if local:
        q_ref, kp_ref, ko_ref, kn_ref, vp_ref, vo_ref, vn_ref, kc_ref, vc_ref, sink_ref, o_ref = refs
    else:
        q_ref, kc_ref, vc_ref, sink_ref, o_ref = refs
    n = pl.program_id(0)
    nb = pl.num_programs(0)
    rows = A_GROUP * ATTN_BLOCK
    if local:
        qi = lax.broadcasted_iota(jnp.int32, (rows, ATTN_BLOCK), 0) % ATTN_BLOCK
        kj = lax.broadcasted_iota(jnp.int32, (rows, ATTN_BLOCK), 1)
        mask_prev = (kj >= qi) & (n > 0)
        mask_next = (kj <= qi) & (n < nb - 1)
    for kv in range(A_KV_HEADS):
        kc = slice(kv * A_HEAD_DIM, (kv + 1) * A_HEAD_DIM)
        heads = [kv * A_GROUP + g for g in range(A_GROUP)]
        q4 = jnp.concatenate([q_ref[:, h * A_HEAD_DIM:(h + 1) * A_HEAD_DIM] for h in heads], axis=0)
        sink_col = jnp.concatenate(
            [jnp.broadcast_to(sink_ref[h:h + 1, 0:1], (ATTN_BLOCK, 1)) for h in heads], axis=0)
        scores, values = [], []
        if local:
            scores.append(jnp.where(mask_prev, _dot_nt(q4, kp_ref[:, kc]), NEG_INF))
            scores.append(_dot_nt(q4, ko_ref[:, kc]))
            scores.append(jnp.where(mask_next, _dot_nt(q4, kn_ref[:, kc]), NEG_INF))
            values += [vp_ref[:, kc], vo_ref[:, kc], vn_ref[:, kc]]
        scores.append(_dot_nt(q4, kc_ref[:, kc]))
        values.append(vc_ref[:, kc])
        o = _softmax_pv(scores, values, sink_col).astype(o_ref.dtype)
        for g, h in enumerate(heads):
            o_ref[:, h * A_HEAD_DIM:(h + 1) * A_HEAD_DIM] = o[g * ATTN_BLOCK:(g + 1) * ATTN_BLOCK, :]


def _attention(q, k, v, k_ctx, v_ctx, sink_tab, local):
    t = q.shape[0]
    nb = t // ATTN_BLOCK
    l = k_ctx.shape[0]
    kw = A_KV_HEADS * A_HEAD_DIM
    blk = lambda f: pl.BlockSpec((ATTN_BLOCK, kw), f)
    prev = lambda i: (jnp.maximum(i - 1, 0), 0)
    own = lambda i: (i, 0)
    nxt = lambda i: (jnp.minimum(i + 1, nb - 1), 0)
    full = pl.BlockSpec((l, kw), lambda i: (0, 0))
    in_specs = [pl.BlockSpec((ATTN_BLOCK, A_WIDTH), own)]
    args = [q]
    if local:
        in_specs += [blk(prev), blk(own), blk(nxt), blk(prev), blk(own), blk(nxt)]
        args += [k, k, k, v, v, v]
    in_specs += [full, full, pl.BlockSpec((A_HEADS, LANES), lambda i: (0, 0))]
    args += [k_ctx, v_ctx, sink_tab]
    return pl.pallas_call(
        functools.partial(_attn_kernel, local=local),
        grid=(nb,),
        in_specs=in_specs,
        out_specs=pl.BlockSpec((ATTN_BLOCK, A_WIDTH), own),
        out_shape=jax.ShapeDtypeStruct((t, A_WIDTH), BF16),
        compiler_params=_params(("parallel",)),
    )(*args)


def _gla_kernel(q_ref, k_ref, v_ref, la_ref, s0_ref, o_ref, sf_ref, st_ref, *, rev, nchunk):
    @pl.when(pl.program_id(1) == 0)
    def _():
        st_ref[...] = s0_ref[0]

    ii = lax.broadcasted_iota(jnp.int32, (GLA_CHUNK, GLA_CHUNK), 0)
    jj = lax.broadcasted_iota(jnp.int32, (GLA_CHUNK, GLA_CHUNK), 1)
    tri = (jj >= ii) if rev else (jj <= ii)
    tri_f = tri.astype(F32)
    order = range(nchunk - 1, -1, -1) if rev else range(nchunk)
    for c in order:
        rows = slice(c * GLA_CHUNK, (c + 1) * GLA_CHUNK)
        b = _dot(tri_f, la_ref[rows, :], HIGHEST)
        b_end = b[0:1, :] if rev else b[GLA_CHUNK - 1:GLA_CHUNK, :]
        k = k_ref[rows, :]
        qe = (q_ref[rows, :] * (B_DK ** -0.5) * jnp.exp(b)).astype(BF16)
        ke = (k * jnp.exp(-b)).astype(BF16)
        kd = (k * jnp.exp(b_end - b)).astype(BF16)
        v = v_ref[rows, :].astype(BF16)
        a = jnp.where(tri, _dot_nt(qe, ke), 0.0).astype(BF16)
        st = st_ref[...]
        o_ref[0, rows, :] = _dot(a, v) + _dot_nt(qe, st.astype(BF16))
        st_ref[...] = st * jnp.exp(b_end) + _dot_tn(v, kd)
    sf_ref[0] = st_ref[...]


def _gla_scan(p, la, s0, rev):
    t = p.shape[0]
    r = min(512, t)
    nblk = t // r
    rb = (lambda h, c: nblk - 1 - c) if rev else (lambda h, c: c)
    la_off = B_HEADS if rev else 0
    o, sf = pl.pallas_call(
        functools.partial(_gla_kernel, rev=rev, nchunk=r // GLA_CHUNK),
        grid=(B_HEADS, nblk),
        in_specs=[pl.BlockSpec((r, B_DK), lambda h, c: (rb(h, c), 3584 // B_DK + h)),
                  pl.BlockSpec((r, B_DK), lambda h, c: (rb(h, c), 3072 // B_DK + h)),
                  pl.BlockSpec((r, B_DV), lambda h, c: (rb(h, c), 1024 // B_DV + h)),
                  pl.BlockSpec((r, B_DK), lambda h, c: (rb(h, c), la_off + h)),
                  pl.BlockSpec((1, B_DV, B_DK), lambda h, c: (h, 0, 0))],
        out_specs=[pl.BlockSpec((1, r, B_DV), lambda h, c: (0, rb(h, c), h)),
                   pl.BlockSpec((1, B_DV, B_DK), lambda h, c: (h, 0, 0))],
        out_shape=[jax.ShapeDtypeStruct((1, t, B_WIDTH), F32),
                   jax.ShapeDtypeStruct((B_HEADS, B_DV, B_DK), F32)],
        scratch_shapes=[pltpu.VMEM((B_DV, B_DK), F32)],
        compiler_params=_params(("parallel", "arbitrary")),
    )(p, p, p, la, s0)
    return o[0], sf


def _outproj_kernel(oa_ref, of_ref, ob_ref, og_ref, gg_ref, w_ref, x_ref, gt_ref, o_ref, mix_ref, *, row):
    @pl.when(pl.program_id(1) == 0)
    def _():
        mix_ref[:, 0:A_WIDTH] = oa_ref[...]
        for h in range(B_HEADS):
            cols = slice(h * B_DV, (h + 1) * B_DV)
            o = of_ref[:, cols] + ob_ref[:, cols]
            ms = jnp.mean(o * o, axis=-1, keepdims=True)
            y = o * lax.rsqrt(ms + NORM_EPS) * gg_ref[...]
            mix_ref[:, A_WIDTH + h * B_DV:A_WIDTH + (h + 1) * B_DV] = (y * _silu(og_ref[:, cols])).astype(BF16)

    o_ref[...] = x_ref[...] + gt_ref[row:row + 1, :] * _dot(mix_ref[...], w_ref[...])


def _outproj(oa, o_f, o_b, p, g_gla, w_out, x, ada_l, row):
    t, d = x.shape
    tm = min(512, t)
    tn = 1024
    return pl.pallas_call(
        functools.partial(_outproj_kernel, row=row),
        grid=(t // tm, d // tn),
        in_specs=[pl.BlockSpec((tm, A_WIDTH), lambda i, j: (i, 0)),
                  pl.BlockSpec((tm, B_WIDTH), lambda i, j: (i, 0)),
                  pl.BlockSpec((tm, B_WIDTH), lambda i, j: (i, 0)),
                  pl.BlockSpec((tm, B_WIDTH), lambda i, j: (i, 2048 // B_WIDTH)),
                  pl.BlockSpec((1, B_DV), lambda i, j: (0, 0)),
                  pl.BlockSpec((A_WIDTH + B_WIDTH, tn), lambda i, j: (0, j)),
                  pl.BlockSpec((tm, tn), lambda i, j: (i, j)),
                  pl.BlockSpec((8, tn), lambda i, j: (0, 2 * (d // tn) + j))],
        out_specs=pl.BlockSpec((tm, tn), lambda i, j: (i, j)),
        out_shape=jax.ShapeDtypeStruct((t, d), F32),
        scratch_shapes=[pltpu.VMEM((tm, A_WIDTH + B_WIDTH), BF16)],
        compiler_params=_params(("parallel", "arbitrary")),
    )(oa, o_f, o_b, p, g_gla, w_out, x, ada_l)


def _ffn_kernel(x_ref, g_ref, sh_ref, sc_ref, gt_ref, wg_ref, wu_ref, wd_ref, o_ref, hn_ref, acc_ref, *, row):
    f = pl.program_id(1)

    @pl.when(f == 0)
    def _():
        hn_ref[...] = _norm_mod(x_ref[...], g_ref[...], sh_ref[row:row + 1, :], sc_ref[row:row + 1, :]).astype(BF16)
        acc_ref[...] = jnp.zeros_like(acc_ref)

    h = hn_ref[...]
    a = (_silu(_dot(h, wg_ref[...])) * _dot(h, wu_ref[...])).astype(BF16)
    acc_ref[...] += _dot(a, wd_ref[...])

    @pl.when(f == pl.num_programs(1) - 1)
    def _():
        o_ref[...] = x_ref[...] + gt_ref[row:row + 1, :] * acc_ref[...]


def _ffn(x, ada_l, row, g, wg, wu, wd):
    t, d = x.shape
    ff = wg.shape[1]
    tm = min(512, t)
    tf = 512
    return pl.pallas_call(
        functools.partial(_ffn_kernel, row=row),
        grid=(t // tm, ff // tf),
        in_specs=[pl.BlockSpec((tm, d), lambda i, f: (i, 0)),
                  pl.BlockSpec((1, d), lambda i, f: (0, 0)),
                  _ada_spec(3), _ada_spec(4), _ada_spec(5),
                  pl.BlockSpec((d, tf), lambda i, f: (0, f)),
                  pl.BlockSpec((d, tf), lambda i, f: (0, f)),
                  pl.BlockSpec((tf, d), lambda i, f: (f, 0))],
        out_specs=pl.BlockSpec((tm, d), lambda i, f: (i, 0)),
        out_shape=jax.ShapeDtypeStruct((t, d), F32),
        scratch_shapes=[pltpu.VMEM((tm, d), BF16), pltpu.VMEM((tm, d), F32)],
        compiler_params=_params(("parallel", "arbitrary")),
    )(x, g, ada_l, ada_l, ada_l, wg, wu, wd)


def _pool_kernel(x_ref, xp_ref, xn_ref, g_ref, sh_ref, sc_ref, gt_ref, w_ref, ps_ref, o_ref, hb_ref, *, row, t):
    i = pl.program_id(0)
    tm = x_ref.shape[0]
    g = g_ref[...]
    sh = sh_ref[row:row + 1, :]
    sc = sc_ref[row:row + 1, :]
    hb_ref[0:POOL_HALO, :] = jnp.where(i > 0, _norm_mod(xp_ref[...], g, sh, sc), 0.0)
    hb_ref[POOL_HALO:POOL_HALO + tm, :] = _norm_mod(x_ref[...], g, sh, sc)
    hb_ref[POOL_HALO + tm:, :] = jnp.where(i < pl.num_programs(0) - 1, _norm_mod(xn_ref[...], g, sh, sc), 0.0)
    tpos = i * tm + lax.broadcasted_iota(jnp.int32, (tm, 1), 0)
    gc = D_MODEL // POOL_GROUPS
    for grp, w in enumerate(POOL_WINDOWS):
        cols = slice(grp * gc, (grp + 1) * gc)
        acc = None
        for dlt in range(-(w // 2), w - w // 2):
            piece = hb_ref[POOL_HALO + dlt:POOL_HALO + dlt + tm, cols]
            acc = piece if acc is None else acc + piece
        cnt = jnp.minimum(tpos + (w - w // 2), t) - jnp.maximum(tpos - w // 2, 0)
        pooled = acc / cnt.astype(F32) - hb_ref[POOL_HALO:POOL_HALO + tm, cols]
        y = _dot(pooled.astype(BF16), w_ref[grp])
        o_ref[:, cols] = x_ref[:, cols] + gt_ref[row:row + 1, cols] * (y * ps_ref[:, cols])


def _pool(x, ada_l, row, g, w_pool, pool_scale):
    t, d = x.shape
    tm = min(256, t)
    nh = t // POOL_HALO
    gc = d // POOL_GROUPS
    return pl.pallas_call(
        functools.partial(_pool_kernel, row=row, t=t),
        grid=(t // tm,),
        in_specs=[pl.BlockSpec((tm, d), lambda i: (i, 0)),
                  pl.BlockSpec((POOL_HALO, d), lambda i: (jnp.maximum(i * (tm // POOL_HALO) - 1, 0), 0)),
                  pl.BlockSpec((POOL_HALO, d), lambda i: (jnp.minimum((i + 1) * (tm // POOL_HALO), nh - 1), 0)),
                  pl.BlockSpec((1, d), lambda i: (0, 0)),
                  _ada_spec(0), _ada_spec(1), _ada_spec(2),
                  pl.BlockSpec((POOL_GROUPS, gc, gc), lambda i: (0, 0, 0)),
                  pl.BlockSpec((1, d), lambda i: (0, 0))],
        out_specs=pl.BlockSpec((tm, d), lambda i: (i, 0)),
        out_shape=jax.ShapeDtypeStruct((t, d), F32),
        scratch_shapes=[pltpu.VMEM((tm + 2 * POOL_HALO, d), F32)],
        compiler_params=_params(("parallel",)),
    )(x, x, x, g, ada_l, ada_l, ada_l, w_pool, pool_scale)


def _router_kernel(x_ref, g_ref, sh_ref, sc_ref, wr_ref, hn_ref, sel_ref, gate_ref, *, row):
    h = _norm_mod(x_ref[...], g_ref[...], sh_ref[row:row + 1, :], sc_ref[row:row + 1, :])
    hn_ref[...] = h.astype(hn_ref.dtype)
    lane = lax.broadcasted_iota(jnp.int32, (h.shape[0], LANES), 1)
    lane_f = lane.astype(F32)
    logits = jnp.where(lane < N_EXPERTS, _dot(h, wr_ref[...], HIGHEST), -jnp.inf)
    m1 = jnp.max(logits, axis=-1, keepdims=True)
    i1 = jnp.min(jnp.where(logits == m1, lane_f, float(LANES)), axis=-1, keepdims=True)
    rest = jnp.where(lane_f == i1, -jnp.inf, logits)
    m2 = jnp.max(rest, axis=-1, keepdims=True)
    i2 = jnp.min(jnp.where(rest == m2, lane_f, float(LANES)), axis=-1, keepdims=True)
    e2 = jnp.exp(m2 - m1)
    den = 1.0 + e2
    sel_ref[...] = jnp.where(lane == 0, i1, jnp.where(lane == 1, i2, 0.0)).astype(jnp.int32)
    gate_ref[...] = jnp.where(lane == 0, 1.0 / den, jnp.where(lane == 1, e2 / den, 0.0))


def _router(x, ada_l, row, g, w_router_pad):
    t, d = x.shape
    tm = min(256, t)
    return pl.pallas_call(
        functools.partial(_router_kernel, row=row),
        grid=(t // tm,),
        in_specs=[pl.BlockSpec((tm, d), lambda i: (i, 0)),
                  pl.BlockSpec((1, d), lambda i: (0, 0)),
                  _ada_spec(3), _ada_spec(4),
                  pl.BlockSpec((d, LANES), lambda i: (0, 0))],
        out_specs=[pl.BlockSpec((tm, d), lambda i: (i, 0)),
                   pl.BlockSpec((tm, LANES), lambda i: (i, 0)),
                   pl.BlockSpec((tm, LANES), lambda i: (i, 0))],
        out_shape=[jax.ShapeDtypeStruct((t, d), BF16),
                   jax.ShapeDtypeStruct((t, LANES), jnp.int32),
                   jax.ShapeDtypeStruct((t, LANES), F32)],
        compiler_params=_params(("parallel",)),
    )(x, g, ada_l, ada_l, w_router_pad)


MOE_TILE = 512
SRC_BLK = 512
Y_BLK = 256
CMB_TILE = 256


def _moe_ffn_kernel(te_ref, tv_ref, blo_ref, bn_ref, src_ref, hn_hbm, wg_ref, wu_ref, wd_ref, y_ref,
                    xb_ref, acc_ref, hbuf_ref, sem):
    i = pl.program_id(0)
    f = pl.program_id(1)
    tm = xb_ref.shape[0]

    @pl.when(f == 0)
    def _():
        xb_ref[...] = jnp.zeros_like(xb_ref)
        acc_ref[...] = jnp.zeros_like(acc_ref)
        n = bn_ref[i]
        lo = blo_ref[i]

        def blk_copy(b, slot):
            return pltpu.make_async_copy(hn_hbm.at[pl.ds((lo + b) * SRC_BLK, SRC_BLK), :], hbuf_ref.at[slot],
                                         sem.at[slot])

        @pl.when(n > 0)
        def _():
            blk_copy(0, 0).start()

        src = src_ref[...]
        col = lax.broadcasted_iota(jnp.int32, (tm, SRC_BLK), 1)

        def body(b, carry):
            slot = b % 2
            blk_copy(b, slot).wait()

            @pl.when(b + 1 < n)
            def _():
                blk_copy(b + 1, 1 - slot).start()

            onehot = jnp.where(src - (lo + b) * SRC_BLK == col, 1.0, 0.0).astype(BF16)
            xb_ref[...] += _dot(onehot, hbuf_ref[slot]).astype(BF16)
            return carry

        lax.fori_loop(0, n, body, 0)

    @pl.when(tv_ref[i] > 0)
    def _():
        h = xb_ref[...]
        a = (_silu(_dot(h, wg_ref[0])) * _dot(h, wu_ref[0])).astype(BF16)
        acc_ref[...] += _dot(a, wd_ref[0])

    @pl.when(f == pl.num_programs(1) - 1)
    def _():
        y_ref[...] = acc_ref[...].astype(y_ref.dtype)


def _moe_ffn(hn, src, tile_expert, tile_valid, blk_lo, blk_n, wg, wu, wd):
    d = hn.shape[1]
    tm = MOE_TILE
    ff = wg.shape[2]
    tf = 512
    nf = ff // tf
    nt = src.shape[0] // tm
    fidx = lambda i, f, tv: jnp.where(tv[i] > 0, f, nf - 1)
    return pl.pallas_call(
        _moe_ffn_kernel,
        grid_spec=pltpu.PrefetchScalarGridSpec(
            num_scalar_prefetch=4,
            grid=(nt, nf),
            in_specs=[pl.BlockSpec((tm, 1), lambda i, f, te, tv, bl, bn: (i, 0)),
                      pl.BlockSpec(memory_space=pl.ANY),
                      pl.BlockSpec((1, d, tf), lambda i, f, te, tv, bl, bn: (te[i], 0, fidx(i, f, tv))),
                      pl.BlockSpec((1, d, tf), lambda i, f, te, tv, bl, bn: (te[i], 0, fidx(i, f, tv))),
                      pl.BlockSpec((1, tf, d), lambda i, f, te, tv, bl, bn: (te[i], fidx(i, f, tv), 0))],
            out_specs=pl.BlockSpec((tm, d), lambda i, f, te, tv, bl, bn: (i, 0)),
            scratch_shapes=[pltpu.VMEM((tm, d), BF16), pltpu.VMEM((tm, d), F32),
                            pltpu.VMEM((2, SRC_BLK, d), BF16), pltpu.SemaphoreType.DMA((2,))]),
        out_shape=jax.ShapeDtypeStruct((nt * tm, d), BF16),
        compiler_params=_params(("arbitrary", "arbitrary")),
    )(tile_expert, tile_valid, blk_lo, blk_n, src, hn, wg, wu, wd)


CMB_SLOTS = [(e, j) for e in range(N_EXPERTS) for j in range(2)]


def _combine_kernel(fb_ref, nb_ref, x_ref, pos_ref, gate_ref, gt_ref, y_hbm, o_ref, acc_ref, ybuf_ref, sem,
                    *, row, tile_off):
    base = (tile_off + pl.program_id(0)) * N_EXPERTS
    tm = x_ref.shape[0]
    pos1, pos2 = pos_ref[:, 0:1], pos_ref[:, 1:2]
    g1, g2 = gate_ref[:, 0:1], gate_ref[:, 1:2]
    col = lax.broadcasted_iota(jnp.int32, (tm, Y_BLK), 1)

    def blk(s):
        e, j = CMB_SLOTS[s]
        return fb_ref[base + e] + j

    def used(s):
        e, j = CMB_SLOTS[s]
        return nb_ref[base + e] > j

    def blk_copy(s):
        return pltpu.make_async_copy(y_hbm.at[pl.ds(blk(s) * Y_BLK, Y_BLK), :], ybuf_ref.at[s % 2], sem.at[s % 2])

    acc_ref[...] = jnp.zeros_like(acc_ref)

    @pl.when(used(0))
    def _():
        blk_copy(0).start()

    for s in range(len(CMB_SLOTS)):
        if s + 1 < len(CMB_SLOTS):
            @pl.when(used(s + 1))
            def _():
                blk_copy(s + 1).start()

        @pl.when(used(s))
        def _():
            blk_copy(s).wait()
            off = blk(s) * Y_BLK
            w = jnp.where(pos1 - off == col, g1, 0.0) + jnp.where(pos2 - off == col, g2, 0.0)
            acc_ref[...] += _dot(w.astype(BF16), ybuf_ref[s % 2])

    o_ref[...] = x_ref[...] + gt_ref[row:row + 1, :] * acc_ref[...]


def _combine(x, y, pos, gates, first_blk, num_blk, tok_off, ada_l, row):
    t, d = x.shape
    tm = CMB_TILE
    tile_off = tok_off // tm
    return pl.pallas_call(
        functools.partial(_combine_kernel, row=row, tile_off=tile_off),
        grid_spec=pltpu.PrefetchScalarGridSpec(
            num_scalar_prefetch=2,
            grid=(t // tm,),
            in_specs=[pl.BlockSpec((tm, d), lambda i, fb, nb: (i, 0)),
                      pl.BlockSpec((tm, 2), lambda i, fb, nb: (tile_off + i, 0)),
                      pl.BlockSpec((tm, LANES), lambda i, fb, nb: (i, 0)),
                      _ada_spec(5),
                      pl.BlockSpec(memory_space=pl.ANY)],
            out_specs=pl.BlockSpec((tm, d), lambda i, fb, nb: (i, 0)),
            scratch_shapes=[pltpu.VMEM((tm, d), F32), pltpu.VMEM((2, Y_BLK, d), BF16),
                            pltpu.SemaphoreType.DMA((2,))]),
        out_shape=jax.ShapeDtypeStruct((t, d), F32),
        compiler_params=_params(("arbitrary",)),
    )(first_blk, num_blk, x, pos, gates, ada_l, y)


def _route(sel):
    tm = MOE_TILE
    n_tok = sel.shape[0]
    n_asg = 2 * n_tok
    nt = n_asg // tm + N_EXPERTS
    e_flat = sel.reshape(-1)
    onehot = (e_flat[:, None] == jnp.arange(N_EXPERTS, dtype=jnp.int32)[None, :]).astype(jnp.int32)
    csum = jnp.cumsum(onehot, axis=0)
    count = csum[-1]
    rank = jnp.sum((csum - onehot) * onehot, axis=1)
    tiles_e = (count + tm - 1) // tm
    tile_end = jnp.cumsum(tiles_e)
    base = (tile_end - tiles_e) * tm
    pos = jnp.sum(onehot * base[None, :], axis=1) + rank
    src = jnp.full((nt * tm,), -1, jnp.int32).at[pos].set(jnp.arange(n_asg, dtype=jnp.int32) // 2)
    tile_id = jnp.arange(nt, dtype=jnp.int32)
    used = tile_end[-1]
    owner = lambda i: jnp.minimum(jnp.sum((i[:, None] >= tile_end[None, :]).astype(jnp.int32), axis=1), N_EXPERTS - 1)
    tile_valid = (tile_id < used).astype(jnp.int32)
    tile_expert = owner(jnp.minimum(tile_id, used - 1))
    srcm = src.reshape(nt, tm)
    tmax = jnp.max(srcm, axis=1)
    tmin = jnp.min(jnp.where(srcm >= 0, srcm, n_tok), axis=1)
    blk_lo = jnp.where(tmax >= 0, tmin // SRC_BLK, 0)
    blk_n = jnp.where(tmax >= 0, tmax // SRC_BLK - blk_lo + 1, 0)
    ntt = n_tok // CMB_TILE
    through = csum.reshape(ntt, 2 * CMB_TILE, N_EXPERTS)[:, -1, :]
    before = jnp.concatenate([jnp.zeros((1, N_EXPERTS), jnp.int32), through[:-1]], axis=0)
    lo = base[None, :] + before
    hi = base[None, :] + through - 1
    first_blk = lo // Y_BLK
    num_blk = jnp.where(through > before, hi // Y_BLK - first_blk + 1, 0)
    return (src.reshape(nt * tm, 1), pos.reshape(n_tok, 2), tile_expert, tile_valid, blk_lo, blk_n,
            first_blk.reshape(-1), num_blk.reshape(-1))


def _moe(streams, ada_l, g, w_router_pad, wg, wu, wd):
    hs, sels, gates = [], [], []
    for x, row in streams:
        h, s, gt = _router(x, ada_l, row, g, w_router_pad)
        hs.append(h), sels.append(s[:, :2]), gates.append(gt)
    n_tok = sum(h.shape[0] for h in hs)
    pad = -n_tok % SRC_BLK
    if pad:
        hs.append(jnp.zeros((pad, D_MODEL), BF16))
    h_all = hs[0] if len(hs) == 1 else jnp.concatenate(hs, axis=0)
    sel_all = sels[0] if len(sels) == 1 else jnp.concatenate(sels, axis=0)
    src, pos, tile_expert, tile_valid, blk_lo, blk_n, first_blk, num_blk = _route(sel_all)
    y = _moe_ffn(h_all, src, tile_expert, tile_valid, blk_lo, blk_n, wg, wu, wd)
    outs, off = [], 0
    for (x, row), gt in zip(streams, gates):
        outs.append(_combine(x, y, pos, gt, first_blk, num_blk, off, ada_l, row))
        off += x.shape[0]
    return outs


def _even_layer(x, ctx, ada_l, g1, g2, w_in, g_qn, g_kn, sink, w_gate_up, b_gate_up, g_gla, w_out,
                w_ffn_gate, w_ffn_up, w_ffn_down, rope_tabs, ctx_out):
    d = D_MODEL
    w_main = jnp.concatenate([w_in[:, 2080:3104], w_in[:, 1024:2048], w_in[:, 3616:4640], w_in[:, 512:1024],
                              w_in[:, 3104:3616], w_in[:, 0:256], w_in[:, 256:512]], axis=1).astype(BF16)
    w_gate = jnp.pad(w_in[:, 2048:2080], ((0, 0), (0, LANES - 2 * GATE_RANK))).astype(BF16)
    nk = B_HEADS * B_DK
    w_up = jnp.zeros((LANES, 2 * nk), F32)
    w_up = w_up.at[0:GATE_RANK, 0:nk].set(w_gate_up[0]).at[GATE_RANK:2 * GATE_RANK, nk:].set(w_gate_up[1])
    b_up = b_gate_up.reshape(1, 2 * nk)
    w_out_b = w_out.astype(BF16)
    wg_b, wu_b, wd_b = w_ffn_gate.astype(BF16), w_ffn_up.astype(BF16), w_ffn_down.astype(BF16)
    gq, gk, gg = g_qn.reshape(1, -1), g_kn.reshape(1, -1), g_gla.reshape(1, -1)
    sink_tab = jnp.broadcast_to(sink[:, None], (A_HEADS, LANES))
    cos, sin = rope_tabs

    pc, la_c = _inproj(ctx, ada_l, 1, g1, w_main, w_gate, w_up, b_up)
    qc, kc = _qkprep(pc, cos, sin, gq, gk, rope=False)
    vc = pc[:, 4352:4608].astype(BF16)
    s0 = jnp.zeros((B_HEADS, B_DV, B_DK), F32)
    oc_f, s_fwd = _gla_scan(pc, la_c, s0, rev=False)
    oc_b, s_bwd = _gla_scan(pc, la_c, s0, rev=True)

    px, la_x = _inproj(x, ada_l, 0, g1, w_main, w_gate, w_up, b_up)
    qx, kx = _qkprep(px, cos, sin, gq, gk, rope=True)
    vx = px[:, 4352:4608].astype(BF16)
    oa = _attention(qx, kx, vx, kc, vc, sink_tab, local=True)
    ox_f, _ = _gla_scan(px, la_x, s_fwd, rev=False)
    ox_b, _ = _gla_scan(px, la_x, s_bwd, rev=True)
    x = _outproj(oa, ox_f, ox_b, px, gg, w_out_b, x, ada_l, 0)
    x = _ffn(x, ada_l, 0, g2, wg_b, wu_b, wd_b)
    if ctx_out:
        oa_c = _attention(qc, None, None, kc, vc, sink_tab, local=False)
        ctx = _outproj(oa_c, oc_f, oc_b, pc, gg, w_out_b, ctx, ada_l, 1)
        ctx = _ffn(ctx, ada_l, 1, g2, wg_b, wu_b, wd_b)
    return x, ctx


def _odd_layer(x, ctx, ada_l, g1, g2, w_pool, pool_scale, w_router, w_exp_gate, w_exp_up, w_exp_down, ctx_out):
    w_pool_b = w_pool.astype(BF16)
    ps = pool_scale.reshape(1, -1)
    wr = jnp.pad(w_router, ((0, 0), (0, LANES - N_EXPERTS)))
    wg_b, wu_b, wd_b = w_exp_gate.astype(BF16), w_exp_up.astype(BF16), w_exp_down.astype(BF16)
    x = _pool(x, ada_l, 0, g1, w_pool_b, ps)
    if ctx_out:
        ctx = _pool(ctx, ada_l, 1, g1, w_pool_b, ps)
        x, ctx = _moe([(x, 0), (ctx, 1)], ada_l, g2, wr, wg_b, wu_b, wd_b)
    else:
        (x,) = _moe([(x, 0)], ada_l, g2, wr, wg_b, wu_b, wd_b)
    return x, ctx


def kernel(x, c, ctx, c_ctx, w_ada, b_ada, norm_g, w_in, g_qn, g_kn, attn_sink, w_gate_up, b_gate_up, g_gla, w_out,
           w_ffn_gate, w_ffn_up, w_ffn_down, w_pool, pool_scale, w_router, w_exp_gate, w_exp_up, w_exp_down):
    depth = w_ada.shape[0]
    xs = x[0]
    cs = ctx[0]
    t = xs.shape[0]
    cond = jnp.zeros((8, D_MODEL), F32).at[0].set(c[0]).at[1].set(c_ctx)
    ada = _ada_all(cond, w_ada, b_ada)
    rope_tabs = _rope_tables(t)
    for l in range(depth):
        ctx_later = any(j % 2 == 0 for j in range(l + 1, depth))
        g1 = norm_g[l, 0].reshape(1, -1)
        g2 = norm_g[l, 1].reshape(1, -1)
        if l % 2 == 0:
            e = l // 2
            xs, cs = _even_layer(xs, cs, ada[l], g1, g2, w_in[e], g_qn[e], g_kn[e], attn_sink[e], w_gate_up[e],
                                 b_gate_up[e], g_gla[e], w_out[e], w_ffn_gate[e], w_ffn_up[e], w_ffn_down[e],
                                 rope_tabs, ctx_later)
        else:
            o = l // 2
            xs, cs = _odd_layer(xs, cs, ada[l], g1, g2, w_pool[o], pool_scale[o], w_router[o],
                                w_exp_gate[o], w_exp_up[o], w_exp_down[o], ctx_later)
    return xs[None]
```

```python
import functools

import jax
import jax.numpy as jnp
import numpy as np
from jax import lax
from jax.experimental import pallas as pl
from jax.experimental.pallas import tpu as pltpu

F32 = jnp.float32
BF16 = jnp.bfloat16
HIGHEST = lax.Precision.HIGHEST

D_MODEL = 2048
GRID_W = 64
A_HEADS = 8
A_KV_HEADS = 2
A_GROUP = 4
A_HEAD_DIM = 128
A_WIDTH = A_HEADS * A_HEAD_DIM
ATTN_BLOCK = 128
ROPE_BASE = 10000.0
B_HEADS = 4
B_DV = 256
B_DK = 128
B_WIDTH = B_HEADS * B_DV
GATE_RANK = 16
GATE_TAU = 16.0
GLA_CHUNK = 64
POOL_GROUPS = 4
POOL_WINDOWS = (2, 4, 8, 16)
POOL_HALO = 8
N_EXPERTS = 8
NORM_EPS = 1e-6
NEG_INF = -1e30
LANES = 128

P_COLS = 4608
VMEM_LIMIT = 56 * 1024 * 1024


def _params(sem, vmem=VMEM_LIMIT):
    return pltpu.CompilerParams(dimension_semantics=sem, vmem_limit_bytes=vmem)


def _dot(a, b, precision=None):
    return jnp.dot(a, b, preferred_element_type=F32, precision=precision)


def _dot_nt(a, b):
    return lax.dot_general(a, b, (((1,), (1,)), ((), ())), preferred_element_type=F32)


def _dot_tn(a, b):
    return lax.dot_general(a, b, (((0,), (0,)), ((), ())), preferred_element_type=F32)


def _silu(x):
    return x * jax.nn.sigmoid(x)


def _norm_mod(x, g, shift, scale):
    ms = jnp.mean(x * x, axis=-1, keepdims=True)
    return (x * lax.rsqrt(ms + NORM_EPS) * g) * (1.0 + scale) + shift


def _ada_kernel(cond_ref, w_ref, b_ref, o_ref):
    s = _silu(cond_ref[...])
    o_ref[0] = _dot(s, w_ref[0], HIGHEST) + b_ref[0]


def _ada_all(cond, w_ada, b_ada):
    depth, d, n = w_ada.shape
    tn = 1024
    return pl.pallas_call(
        _ada_kernel,
        grid=(depth, n // tn),
        in_specs=[pl.BlockSpec((8, d), lambda l, j: (0, 0)),
                  pl.BlockSpec((1, d, tn), lambda l, j: (l, 0, j)),
                  pl.BlockSpec((1, 1, tn), lambda l, j: (l, 0, j))],
        out_specs=pl.BlockSpec((1, 8, tn), lambda l, j: (l, 0, j)),
        out_shape=jax.ShapeDtypeStruct((depth, 8, n), F32),
        compiler_params=_params(("parallel", "parallel")),
    )(cond, w_ada, b_ada.reshape(depth, 1, n))


def _ada_spec(k):
    return pl.BlockSpec((8, D_MODEL), lambda *_: (0, k))


def _inproj_kernel(x_ref, g_ref, sh_ref, sc_ref, w_ref, wg_ref, wup_ref, bup_ref, p_ref, la_ref, hn_ref, *, row):
    @pl.when(pl.program_id(1) == 0)
    def _():
        h = _norm_mod(x_ref[...], g_ref[...], sh_ref[row:row + 1, :], sc_ref[row:row + 1, :]).astype(BF16)
        hn_ref[...] = h
        lr = _dot(h, wg_ref[...])
        z = _dot(lr, wup_ref[...], HIGHEST) + bup_ref[...]
        la_ref[...] = (jnp.minimum(z, 0.0) - jnp.log1p(jnp.exp(-jnp.abs(z)))) * (1.0 / GATE_TAU)

    p_ref[...] = _dot(hn_ref[...], w_ref[...])


def _inproj(x, ada_l, row, g, w_main, w_gate, w_up, b_up):
    t, d = x.shape
    tm = min(512, t)
    tn = 1536
    nla = 2 * B_HEADS * B_DK
    return pl.pallas_call(
        functools.partial(_inproj_kernel, row=row),
        grid=(t // tm, P_COLS // tn),
        in_specs=[pl.BlockSpec((tm, d), lambda i, j: (i, 0)),
                  pl.BlockSpec((1, d), lambda i, j: (0, 0)),
                  _ada_spec(0), _ada_spec(1),
                  pl.BlockSpec((d, tn), lambda i, j: (0, j)),
                  pl.BlockSpec((d, LANES), lambda i, j: (0, 0)),
                  pl.BlockSpec((LANES, nla), lambda i, j: (0, 0)),
                  pl.BlockSpec((1, nla), lambda i, j: (0, 0))],
        out_specs=[pl.BlockSpec((tm, tn), lambda i, j: (i, j)),
                   pl.BlockSpec((tm, nla), lambda i, j: (i, 0))],
        out_shape=[jax.ShapeDtypeStruct((t, P_COLS), F32),
                   jax.ShapeDtypeStruct((t, nla), F32)],
        scratch_shapes=[pltpu.VMEM((tm, d), BF16)],
        compiler_params=_params(("parallel", "arbitrary")),
    )(x, g, ada_l, ada_l, w_main, w_gate, w_up, b_up)


def _qkprep_kernel(q_ref, k_ref, cos_ref, sin_ref, gq_ref, gk_ref, qo_ref, ko_ref, *, rope):
    tm = q_ref.shape[0]
    lane = lax.broadcasted_iota(jnp.int32, (tm, A_HEAD_DIM), 1)
    first_half = (lane % 64) < 32

    def prep(xh, g, scale):
        ms = jnp.mean(xh * xh, axis=-1, keepdims=True)
        y = xh * lax.rsqrt(ms + NORM_EPS) * g
        if rope:
            partner = jnp.where(first_half, pltpu.roll(y, 96, 1), pltpu.roll(y, 32, 1))
            y = y * cos_ref[...] + partner * sin_ref[...]
        return (y * scale).astype(BF16)

    for h in range(A_HEADS):
        cols = slice(h * A_HEAD_DIM, (h + 1) * A_HEAD_DIM)
        qo_ref[:, cols] = prep(q_ref[:, cols], gq_ref[...], A_HEAD_DIM ** -0.5)
    for h in range(A_KV_HEADS):
        cols = slice(h * A_HEAD_DIM, (h + 1) * A_HEAD_DIM)
        ko_ref[:, cols] = prep(k_ref[:, cols], gk_ref[...], 1.0)


def _qkprep(p, cos, sin, g_qn, g_kn, rope):
    t = p.shape[0]
    tm = min(512, t)
    kw = A_KV_HEADS * A_HEAD_DIM
    return pl.pallas_call(
        functools.partial(_qkprep_kernel, rope=rope),
        grid=(t // tm,),
        in_specs=[pl.BlockSpec((tm, A_WIDTH), lambda i: (i, 0)),
                  pl.BlockSpec((tm, kw), lambda i: (i, 4096 // kw)),
                  pl.BlockSpec((tm, A_HEAD_DIM), lambda i: (i, 0)),
                  pl.BlockSpec((tm, A_HEAD_DIM), lambda i: (i, 0)),
                  pl.BlockSpec((1, A_HEAD_DIM), lambda i: (0, 0)),
                  pl.BlockSpec((1, A_HEAD_DIM), lambda i: (0, 0))],
        out_specs=[pl.BlockSpec((tm, A_WIDTH), lambda i: (i, 0)),
                   pl.BlockSpec((tm, kw), lambda i: (i, 0))],
        out_shape=[jax.ShapeDtypeStruct((t, A_WIDTH), BF16),
                   jax.ShapeDtypeStruct((t, kw), BF16)],
        compiler_params=_params(("parallel",)),
    )(p, p, cos, sin, g_qn, g_kn)


def _rope_tables(t):
    half = A_HEAD_DIM // 4
    freqs = ROPE_BASE ** (-jnp.arange(half, dtype=F32) / half)
    tok = jnp.arange(t)
    row = (tok // GRID_W).astype(F32)[:, None] * freqs
    col = (tok % GRID_W).astype(F32)[:, None] * freqs
    cos = jnp.concatenate([jnp.cos(row), jnp.cos(row), jnp.cos(col), jnp.cos(col)], axis=-1)
    sin = jnp.concatenate([-jnp.sin(row), jnp.sin(row), -jnp.sin(col), jnp.sin(col)], axis=-1)
    return cos, sin


def _softmax_pv(scores, values, sink_col):
    m = sink_col
    for s in scores:
        m = jnp.maximum(m, jnp.max(s, axis=-1, keepdims=True))
    denom = jnp.exp(sink_col - m)
    acc = None
    for s, v in zip(scores, values):
        p = jnp.exp(s - m)
        denom = denom + jnp.sum(p, axis=-1, keepdims=True)
        pv = _dot(p.astype(BF16), v)
        acc = pv if acc is None else acc + pv
    return acc / denom


def _attn_kernel(*refs, local):
    if local:
        q_ref, kp_ref, ko_ref, kn_ref, vp_ref, vo_ref, vn_ref, kc_ref, vc_ref, sink_ref, o_ref = refs
    else:
        q_ref, kc_ref, vc_ref, sink_ref, o_ref = refs
    n = pl.program_id(0)
    nb = pl.num_programs(0)
    rows = A_GROUP * ATTN_BLOCK
    if local:
        qi = lax.broadcasted_iota(jnp.int32, (rows, ATTN_BLOCK), 0) % ATTN_BLOCK
        kj = lax.broadcasted_iota(jnp.int32, (rows, ATTN_BLOCK), 1)
        mask_prev = (kj >= qi) & (n > 0)
        mask_next = (kj <= qi) & (n < nb - 1)
    for kv in range(A_KV_HEADS):
        kc = slice(kv * A_HEAD_DIM, (kv + 1) * A_HEAD_DIM)
        heads = [kv * A_GROUP + g for g in range(A_GROUP)]
        q4 = jnp.concatenate([q_ref[:, h * A_HEAD_DIM:(h + 1) * A_HEAD_DIM] for h in heads], axis=0)
        sink_col = jnp.concatenate(
            [jnp.broadcast_to(sink_ref[h:h + 1, 0:1], (ATTN_BLOCK, 1)) for h in heads], axis=0)
        scores, values = [], []
        if local:
            scores.append(jnp.where(mask_prev, _dot_nt(q4, kp_ref[:, kc]), NEG_INF))
            scores.append(_dot_nt(q4, ko_ref[:, kc]))
            scores.append(jnp.where(mask_next, _dot_nt(q4, kn_ref[:, kc]), NEG_INF))
            values += [vp_ref[:, kc], vo_ref[:, kc], vn_ref[:, kc]]
        scores.append(_dot_nt(q4, kc_ref[:, kc]))
        values.append(vc_ref[:, kc])
        o = _softmax_pv(scores, values, sink_col).astype(o_ref.dtype)
        for g, h in enumerate(heads):
            o_ref[:, h * A_HEAD_DIM:(h + 1) * A_HEAD_DIM] = o[g * ATTN_BLOCK:(g + 1) * ATTN_BLOCK, :]


def _attention(q, k, v, k_ctx, v_ctx, sink_tab, local):
    t = q.shape[0]
    nb = t // ATTN_BLOCK
    l = k_ctx.shape[0]
    kw = A_KV_HEADS * A_HEAD_DIM
    blk = lambda f: pl.BlockSpec((ATTN_BLOCK, kw), f)
    prev = lambda i: (jnp.maximum(i - 1, 0), 0)
    own = lambda i: (i, 0)
    nxt = lambda i: (jnp.minimum(i + 1, nb - 1), 0)
    full = pl.BlockSpec((l, kw), lambda i: (0, 0))
    in_specs = [pl.BlockSpec((ATTN_BLOCK, A_WIDTH), own)]
    args = [q]
    if local:
        in_specs += [blk(prev), blk(own), blk(nxt), blk(prev), blk(own), blk(nxt)]
        args += [k, k, k, v, v, v]
    in_specs += [full, full, pl.BlockSpec((A_HEADS, LANES), lambda i: (0, 0))]
    args += [k_ctx, v_ctx, sink_tab]
    return pl.pallas_call(
        functools.partial(_attn_kernel, local=local),
        grid=(nb,),
        in_specs=in_specs,
        out_specs=pl.BlockSpec((ATTN_BLOCK, A_WIDTH), own),
        out_shape=jax.ShapeDtypeStruct((t, A_WIDTH), BF16),
        compiler_params=_params(("parallel",)),
    )(*args)


def _gla_kernel(q_ref, k_ref, v_ref, la_ref, s0_ref, o_ref, sf_ref, st_ref, *, rev, nchunk):
    @pl.when(pl.program_id(1) == 0)
    def _():
        st_ref[...] = s0_ref[0]

    ii = lax.broadcasted_iota(jnp.int32, (GLA_CHUNK, GLA_CHUNK), 0)
    jj = lax.broadcasted_iota(jnp.int32, (GLA_CHUNK, GLA_CHUNK), 1)
    tri = (jj >= ii) if rev else (jj <= ii)
    tri_f = tri.astype(F32)
    order = range(nchunk - 1, -1, -1) if rev else range(nchunk)
    for c in order:
        rows = slice(c * GLA_CHUNK, (c + 1) * GLA_CHUNK)
        b = _dot(tri_f, la_ref[rows, :], HIGHEST)
        b_end = b[0:1, :] if rev else b[GLA_CHUNK - 1:GLA_CHUNK, :]
        k = k_ref[rows, :]
        qe = (q_ref[rows, :] * (B_DK ** -0.5) * jnp.exp(b)).astype(BF16)
        ke = (k * jnp.exp(-b)).astype(BF16)
        kd = (k * jnp.exp(b_end - b)).astype(BF16)
        v = v_ref[rows, :].astype(BF16)
        a = jnp.where(tri, _dot_nt(qe, ke), 0.0).astype(BF16)
        st = st_ref[...]
        o_ref[0, rows, :] = _dot(a, v) + _dot_nt(qe, st.astype(BF16))
        st_ref[...] = st * jnp.exp(b_end) + _dot_tn(v, kd)
    sf_ref[0] = st_ref[...]


def _gla_scan(p, la, s0, rev):
    t = p.shape[0]
    r = min(512, t)
    nblk = t // r
    rb = (lambda h, c: nblk - 1 - c) if rev else (lambda h, c: c)
    la_off = B_HEADS if rev else 0
    o, sf = pl.pallas_call(
        functools.partial(_gla_kernel, rev=rev, nchunk=r // GLA_CHUNK),
        grid=(B_HEADS, nblk),
        in_specs=[pl.BlockSpec((r, B_DK), lambda h, c: (rb(h, c), 3584 // B_DK + h)),
                  pl.BlockSpec((r, B_DK), lambda h, c: (rb(h, c), 3072 // B_DK + h)),
                  pl.BlockSpec((r, B_DV), lambda h, c: (rb(h, c), 1024 // B_DV + h)),
                  pl.BlockSpec((r, B_DK), lambda h, c: (rb(h, c), la_off + h)),
                  pl.BlockSpec((1, B_DV, B_DK), lambda h, c: (h, 0, 0))],
        out_specs=[pl.BlockSpec((1, r, B_DV), lambda h, c: (0, rb(h, c), h)),
                   pl.BlockSpec((1, B_DV, B_DK), lambda h, c: (h, 0, 0))],
        out_shape=[jax.ShapeDtypeStruct((1, t, B_WIDTH), F32),
                   jax.ShapeDtypeStruct((B_HEADS, B_DV, B_DK), F32)],
        scratch_shapes=[pltpu.VMEM((B_DV, B_DK), F32)],
        compiler_params=_params(("parallel", "arbitrary")),
    )(p, p, p, la, s0)
    return o[0], sf


def _outproj_kernel(oa_ref, of_ref, ob_ref, og_ref, gg_ref, w_ref, x_ref, gt_ref, o_ref, mix_ref, *, row):
    @pl.when(pl.program_id(1) == 0)
    def _():
        mix_ref[:, 0:A_WIDTH] = oa_ref[...]
        for h in range(B_HEADS):
            cols = slice(h * B_DV, (h + 1) * B_DV)
            o = of_ref[:, cols] + ob_ref[:, cols]
            ms = jnp.mean(o * o, axis=-1, keepdims=True)
            y = o * lax.rsqrt(ms + NORM_EPS) * gg_ref[...]
            mix_ref[:, A_WIDTH + h * B_DV:A_WIDTH + (h + 1) * B_DV] = (y * _silu(og_ref[:, cols])).astype(BF16)

    o_ref[...] = x_ref[...] + gt_ref[row:row + 1, :] * _dot(mix_ref[...], w_ref[...])


def _outproj(oa, o_f, o_b, p, g_gla, w_out, x, ada_l, row):
    t, d = x.shape
    tm = min(512, t)
    tn = 1024
    return pl.pallas_call(
        functools.partial(_outproj_kernel, row=row),
        grid=(t // tm, d // tn),
        in_specs=[pl.BlockSpec((tm, A_WIDTH), lambda i, j: (i, 0)),
                  pl.BlockSpec((tm, B_WIDTH), lambda i, j: (i, 0)),
                  pl.BlockSpec((tm, B_WIDTH), lambda i, j: (i, 0)),
                  pl.BlockSpec((tm, B_WIDTH), lambda i, j: (i, 2048 // B_WIDTH)),
                  pl.BlockSpec((1, B_DV), lambda i, j: (0, 0)),
                  pl.BlockSpec((A_WIDTH + B_WIDTH, tn), lambda i, j: (0, j)),
                  pl.BlockSpec((tm, tn), lambda i, j: (i, j)),
                  pl.BlockSpec((8, tn), lambda i, j: (0, 2 * (d // tn) + j))],
        out_specs=pl.BlockSpec((tm, tn), lambda i, j: (i, j)),
        out_shape=jax.ShapeDtypeStruct((t, d), F32),
        scratch_shapes=[pltpu.VMEM((tm, A_WIDTH + B_WIDTH), BF16)],
        compiler_params=_params(("parallel", "arbitrary")),
    )(oa, o_f, o_b, p, g_gla, w_out, x, ada_l)


def _ffn_kernel(x_ref, g_ref, sh_ref, sc_ref, gt_ref, wg_ref, wu_ref, wd_ref, o_ref, hn_ref, acc_ref, *, row):
    f = pl.program_id(1)

    @pl.when(f == 0)
    def _():
        hn_ref[...] = _norm_mod(x_ref[...], g_ref[...], sh_ref[row:row + 1, :], sc_ref[row:row + 1, :]).astype(BF16)
        acc_ref[...] = jnp.zeros_like(acc_ref)

    h = hn_ref[...]
    a = (_silu(_dot(h, wg_ref[...])) * _dot(h, wu_ref[...])).astype(BF16)
    acc_ref[...] += _dot(a, wd_ref[...])

    @pl.when(f == pl.num_programs(1) - 1)
    def _():
        o_ref[...] = x_ref[...] + gt_ref[row:row + 1, :] * acc_ref[...]


def _ffn(x, ada_l, row, g, wg, wu, wd):
    t, d = x.shape
    ff = wg.shape[1]
    tm = min(512, t)
    tf = 512
    return pl.pallas_call(
        functools.partial(_ffn_kernel, row=row),
        grid=(t // tm, ff // tf),
        in_specs=[pl.BlockSpec((tm, d), lambda i, f: (i, 0)),
                  pl.BlockSpec((1, d), lambda i, f: (0, 0)),
                  _ada_spec(3), _ada_spec(4), _ada_spec(5),
                  pl.BlockSpec((d, tf), lambda i, f: (0, f)),
                  pl.BlockSpec((d, tf), lambda i, f: (0, f)),
                  pl.BlockSpec((tf, d), lambda i, f: (f, 0))],
        out_specs=pl.BlockSpec((tm, d), lambda i, f: (i, 0)),
        out_shape=jax.ShapeDtypeStruct((t, d), F32),
        scratch_shapes=[pltpu.VMEM((tm, d), BF16), pltpu.VMEM((tm, d), F32)],
        compiler_params=_params(("parallel", "arbitrary")),
    )(x, g, ada_l, ada_l, ada_l, wg, wu, wd)


def _pool_kernel(x_ref, xp_ref, xn_ref, g_ref, sh_ref, sc_ref, gt_ref, w_ref, ps_ref, o_ref, hb_ref, *, row, t):
    i = pl.program_id(0)
    tm = x_ref.shape[0]
    g = g_ref[...]
    sh = sh_ref[row:row + 1, :]
    sc = sc_ref[row:row + 1, :]
    hb_ref[0:POOL_HALO, :] = jnp.where(i > 0, _norm_mod(xp_ref[...], g, sh, sc), 0.0)
    hb_ref[POOL_HALO:POOL_HALO + tm, :] = _norm_mod(x_ref[...], g, sh, sc)
    hb_ref[POOL_HALO + tm:, :] = jnp.where(i < pl.num_programs(0) - 1, _norm_mod(xn_ref[...], g, sh, sc), 0.0)
    tpos = i * tm + lax.broadcasted_iota(jnp.int32, (tm, 1), 0)
    gc = D_MODEL // POOL_GROUPS
    for grp, w in enumerate(POOL_WINDOWS):
        cols = slice(grp * gc, (grp + 1) * gc)
        acc = None
        for dlt in range(-(w // 2), w - w // 2):
            piece = hb_ref[POOL_HALO + dlt:POOL_HALO + dlt + tm, cols]
            acc = piece if acc is None else acc + piece
        cnt = jnp.minimum(tpos + (w - w // 2), t) - jnp.maximum(tpos - w // 2, 0)
        pooled = acc / cnt.astype(F32) - hb_ref[POOL_HALO:POOL_HALO + tm, cols]
        y = _dot(pooled.astype(BF16), w_ref[grp])
        o_ref[:, cols] = x_ref[:, cols] + gt_ref[row:row + 1, cols] * (y * ps_ref[:, cols])


def _pool(x, ada_l, row, g, w_pool, pool_scale):
    t, d = x.shape
    tm = min(256, t)
    nh = t // POOL_HALO
    gc = d // POOL_GROUPS
    return pl.pallas_call(
        functools.partial(_pool_kernel, row=row, t=t),
        grid=(t // tm,),
        in_specs=[pl.BlockSpec((tm, d), lambda i: (i, 0)),
                  pl.BlockSpec((POOL_HALO, d), lambda i: (jnp.maximum(i * (tm // POOL_HALO) - 1, 0), 0)),
                  pl.BlockSpec((POOL_HALO, d), lambda i: (jnp.minimum((i + 1) * (tm // POOL_HALO), nh - 1), 0)),
                  pl.BlockSpec((1, d), lambda i: (0, 0)),
                  _ada_spec(0), _ada_spec(1), _ada_spec(2),
                  pl.BlockSpec((POOL_GROUPS, gc, gc), lambda i: (0, 0, 0)),
                  pl.BlockSpec((1, d), lambda i: (0, 0))],
        out_specs=pl.BlockSpec((tm, d), lambda i: (i, 0)),
        out_shape=jax.ShapeDtypeStruct((t, d), F32),
        scratch_shapes=[pltpu.VMEM((tm + 2 * POOL_HALO, d), F32)],
        compiler_params=_params(("parallel",)),
    )(x, x, x, g, ada_l, ada_l, ada_l, w_pool, pool_scale)


def _router_kernel(x_ref, g_ref, sh_ref, sc_ref, wr_ref, hn_ref, sel_ref, gate_ref, *, row):
    h = _norm_mod(x_ref[...], g_ref[...], sh_ref[row:row + 1, :], sc_ref[row:row + 1, :])
    hn_ref[...] = h.astype(hn_ref.dtype)
    lane = lax.broadcasted_iota(jnp.int32, (h.shape[0], LANES), 1)
    lane_f = lane.astype(F32)
    logits = jnp.where(lane < N_EXPERTS, _dot(h, wr_ref[...], HIGHEST), -jnp.inf)
    m1 = jnp.max(logits, axis=-1, keepdims=True)
    i1 = jnp.min(jnp.where(logits == m1, lane_f, float(LANES)), axis=-1, keepdims=True)
    rest = jnp.where(lane_f == i1, -jnp.inf, logits)
    m2 = jnp.max(rest, axis=-1, keepdims=True)
    i2 = jnp.min(jnp.where(rest == m2, lane_f, float(LANES)), axis=-1, keepdims=True)
    e2 = jnp.exp(m2 - m1)
    den = 1.0 + e2
    sel_ref[...] = jnp.where(lane == 0, i1, jnp.where(lane == 1, i2, 0.0)).astype(jnp.int32)
    gate_ref[...] = jnp.where(lane == 0, 1.0 / den, jnp.where(lane == 1, e2 / den, 0.0))


def _router(x, ada_l, row, g, w_router_pad):
    t, d = x.shape
    tm = min(256, t)
    return pl.pallas_call(
        functools.partial(_router_kernel, row=row),
        grid=(t // tm,),
        in_specs=[pl.BlockSpec((tm, d), lambda i: (i, 0)),
                  pl.BlockSpec((1, d), lambda i: (0, 0)),
                  _ada_spec(3), _ada_spec(4),
                  pl.BlockSpec((d, LANES), lambda i: (0, 0))],
        out_specs=[pl.BlockSpec((tm, d), lambda i: (i, 0)),
                   pl.BlockSpec((tm, LANES), lambda i: (i, 0)),
                   pl.BlockSpec((tm, LANES), lambda i: (i, 0))],
        out_shape=[jax.ShapeDtypeStruct((t, d), BF16),
                   jax.ShapeDtypeStruct((t, LANES), jnp.int32),
                   jax.ShapeDtypeStruct((t, LANES), F32)],
        compiler_params=_params(("parallel",)),
    )(x, g, ada_l, ada_l, w_router_pad)


MOE_TILE = 1024
MOE_SUB = 256
MOE_SUBS = MOE_TILE // MOE_SUB
SRC_BLK = 512
Y_BLK = 256
CMB_TILE = 256
MOE_TF = 256


def _moe_ffn_kernel(te_ref, tn_ref, blo_ref, bn_ref, src_ref, hn_hbm, wg_ref, wu_ref, wd_ref, y_ref,
                    xb_ref, acc_ref, wgb_ref, wub_ref, wdb_ref, hbuf_ref, sem):
    i = pl.program_id(0)
    f = pl.program_id(1)
    nsub = tn_ref[i]

    @pl.when(f == 0)
    def _():
        xb_ref[...] = jnp.zeros_like(xb_ref)
        acc_ref[...] = jnp.zeros_like(acc_ref)
        col = lax.broadcasted_iota(jnp.int32, (MOE_SUB, SRC_BLK), 1)

        def gather_sub(q, carry):
            rows = pl.ds(pl.multiple_of(q * MOE_SUB, MOE_SUB), MOE_SUB)
            n = bn_ref[i * MOE_SUBS + q]
            lo = blo_ref[i * MOE_SUBS + q]
            src = src_ref[rows, :]

            def blk_copy(b, slot):
                return pltpu.make_async_copy(hn_hbm.at[pl.ds((lo + b) * SRC_BLK, SRC_BLK), :], hbuf_ref.at[slot],
                                             sem.at[slot])

            @pl.when(n > 0)
            def _():
                blk_copy(0, 0).start()

            def body(b, c):
                slot = b % 2
                blk_copy(b, slot).wait()

                @pl.when(b + 1 < n)
                def _():
                    blk_copy(b + 1, 1 - slot).start()

                onehot = jnp.where(src - (lo + b) * SRC_BLK == col, 1.0, 0.0).astype(BF16)
                xb_ref[rows, :] += _dot(onehot, hbuf_ref[slot]).astype(BF16)
                return c

            lax.fori_loop(0, n, body, 0)
            return carry

        lax.fori_loop(0, nsub, gather_sub, 0)

    @pl.when(nsub > 0)
    def _():
        wgb_ref[...] = wg_ref[0, 0].astype(BF16)
        wub_ref[...] = wu_ref[0, 0].astype(BF16)
        wdb_ref[...] = wd_ref[0, 0].astype(BF16)

    for k in range(1, MOE_SUBS + 1):
        @pl.when(nsub == k)
        def _():
            rows = slice(0, k * MOE_SUB)
            h = xb_ref[rows, :]
            a = (_silu(_dot(h, wgb_ref[...])) * _dot(h, wub_ref[...])).astype(BF16)
            acc_ref[rows, :] += _dot(a, wdb_ref[...])

    @pl.when(f == pl.num_programs(1) - 1)
    def _():
        y_ref[...] = acc_ref[...].astype(y_ref.dtype)


def _moe_ffn(hn, src, tile_expert, tile_subs, blk_lo, blk_n, wg, wu, wd, lyr):
    d = hn.shape[1]
    tm = MOE_TILE
    ff = wg.shape[3]
    tf = MOE_TF
    nf = ff // tf
    nt = src.shape[0] // tm
    fidx = lambda i, f, tn: jnp.where(tn[i] > 0, f, nf - 1)
    return pl.pallas_call(
        _moe_ffn_kernel,
        grid_spec=pltpu.PrefetchScalarGridSpec(
            num_scalar_prefetch=4,
            grid=(nt, nf),
            in_specs=[pl.BlockSpec((tm, 1), lambda i, f, te, tn, bl, bn: (i, 0)),
                      pl.BlockSpec(memory_space=pl.ANY),
                      pl.BlockSpec((1, 1, d, tf), lambda i, f, te, tn, bl, bn: (lyr, te[i], 0, fidx(i, f, tn))),
                      pl.BlockSpec((1, 1, d, tf), lambda i, f, te, tn, bl, bn: (lyr, te[i], 0, fidx(i, f, tn))),
                      pl.BlockSpec((1, 1, tf, d), lambda i, f, te, tn, bl, bn: (lyr, te[i], fidx(i, f, tn), 0))],
            out_specs=pl.BlockSpec((tm, d), lambda i, f, te, tn, bl, bn: (i, 0)),
            scratch_shapes=[pltpu.VMEM((tm, d), BF16), pltpu.VMEM((tm, d), F32),
                            pltpu.VMEM((d, tf), BF16), pltpu.VMEM((d, tf), BF16), pltpu.VMEM((tf, d), BF16),
                            pltpu.VMEM((2, SRC_BLK, d), BF16), pltpu.SemaphoreType.DMA((2,))]),
        out_shape=jax.ShapeDtypeStruct((nt * tm, d), BF16),
        compiler_params=_params(("arbitrary", "arbitrary")),
    )(tile_expert, tile_subs, blk_lo, blk_n, src, hn, wg, wu, wd)


CMB_SLOTS = [(e, j) for e in range(N_EXPERTS) for j in range(2)]


def _combine_kernel(fb_ref, nb_ref, x_ref, pos_ref, gate_ref, gt_ref, y_hbm, o_ref, acc_ref, ybuf_ref, sem,
                    *, row, tile_off):
    base = (tile_off + pl.program_id(0)) * N_EXPERTS
    tm = x_ref.shape[0]
    pos1, pos2 = pos_ref[:, 0:1], pos_ref[:, 1:2]
    g1, g2 = gate_ref[:, 0:1], gate_ref[:, 1:2]
    col = lax.broadcasted_iota(jnp.int32, (tm, Y_BLK), 1)

    def blk(s):
        e, j = CMB_SLOTS[s]
        return fb_ref[base + e] + j

    def used(s):
        e, j = CMB_SLOTS[s]
        return nb_ref[base + e] > j

    def blk_copy(s):
        return pltpu.make_async_copy(y_hbm.at[pl.ds(blk(s) * Y_BLK, Y_BLK), :], ybuf_ref.at[s], sem.at[s])

    for s in range(len(CMB_SLOTS)):
        @pl.when(used(s))
        def _():
            blk_copy(s).start()

    acc_ref[...] = jnp.zeros_like(acc_ref)
    for s in range(len(CMB_SLOTS)):
        @pl.when(used(s))
        def _():
            blk_copy(s).wait()
            off = blk(s) * Y_BLK
            w = jnp.where(pos1 - off == col, g1, 0.0) + jnp.where(pos2 - off == col, g2, 0.0)
            acc_ref[...] += _dot(w.astype(BF16), ybuf_ref[s])

    o_ref[...] = x_ref[...] + gt_ref[row:row + 1, :] * acc_ref[...]


def _combine(x, y, pos, gates, first_blk, num_blk, tok_off, ada_l, row):
    t, d = x.shape
    tm = CMB_TILE
    tile_off = tok_off // tm
    return pl.pallas_call(
        functools.partial(_combine_kernel, row=row, tile_off=tile_off),
        grid_spec=pltpu.PrefetchScalarGridSpec(
            num_scalar_prefetch=2,
            grid=(t // tm,),
            in_specs=[pl.BlockSpec((tm, d), lambda i, fb, nb: (i, 0)),
                      pl.BlockSpec((tm, 2), lambda i, fb, nb: (tile_off + i, 0)),
                      pl.BlockSpec((tm, LANES), lambda i, fb, nb: (i, 0)),
                      _ada_spec(5),
                      pl.BlockSpec(memory_space=pl.ANY)],
            out_specs=pl.BlockSpec((tm, d), lambda i, fb, nb: (i, 0)),
            scratch_shapes=[pltpu.VMEM((tm, d), F32), pltpu.VMEM((len(CMB_SLOTS), Y_BLK, d), BF16),
                            pltpu.SemaphoreType.DMA((len(CMB_SLOTS),))]),
        out_shape=jax.ShapeDtypeStruct((t, d), F32),
        compiler_params=_params(("arbitrary",)),
    )(first_blk, num_blk, x, pos, gates, ada_l, y)


def _route(sel):
    tm = MOE_TILE
    n_tok = sel.shape[0]
    n_asg = 2 * n_tok
    nt = -(-n_asg // tm) + N_EXPERTS
    e_flat = sel.reshape(-1)
    onehot = (e_flat[:, None] == jnp.arange(N_EXPERTS, dtype=jnp.int32)[None, :]).astype(jnp.int32)
    csum = jnp.cumsum(onehot, axis=0)
    count = csum[-1]
    rank = jnp.sum((csum - onehot) * onehot, axis=1)
    tiles_e = (count + tm - 1) // tm
    tile_end = jnp.cumsum(tiles_e)
    base = (tile_end - tiles_e) * tm
    pos = jnp.sum(onehot * base[None, :], axis=1) + rank
    src = jnp.full((nt * tm,), -1, jnp.int32).at[pos].set(jnp.arange(n_asg, dtype=jnp.int32) // 2)
    tile_id = jnp.arange(nt, dtype=jnp.int32)
    used = tile_end[-1]
    owner = lambda i: jnp.minimum(jnp.sum((i[:, None] >= tile_end[None, :]).astype(jnp.int32), axis=1), N_EXPERTS - 1)
    tile_expert = owner(jnp.minimum(tile_id, used - 1))
    rows_in_tile = jnp.clip((base + count)[tile_expert] - tile_id * tm, 0, tm)
    tile_subs = jnp.where(tile_id < used, (rows_in_tile + MOE_SUB - 1) // MOE_SUB, 0)
    srcm = src.reshape(nt * MOE_SUBS, MOE_SUB)
    tmax = jnp.max(srcm, axis=1)
    tmin = jnp.min(jnp.where(srcm >= 0, srcm, n_tok), axis=1)
    blk_lo = jnp.where(tmax >= 0, tmin // SRC_BLK, 0)
    blk_n = jnp.where(tmax >= 0, tmax // SRC_BLK - blk_lo + 1, 0)
    ntt = n_tok // CMB_TILE
    through = csum.reshape(ntt, 2 * CMB_TILE, N_EXPERTS)[:, -1, :]
    before = jnp.concatenate([jnp.zeros((1, N_EXPERTS), jnp.int32), through[:-1]], axis=0)
    lo = base[None, :] + before
    hi = base[None, :] + through - 1
    first_blk = lo // Y_BLK
    num_blk = jnp.where(through > before, hi // Y_BLK - first_blk + 1, 0)
    return (src.reshape(nt * tm, 1), pos.reshape(n_tok, 2), tile_expert, tile_subs, blk_lo, blk_n,
            first_blk.reshape(-1), num_blk.reshape(-1))


def _moe(streams, ada_l, g, w_router_pad, wg, wu, wd, lyr):
    hs, sels, gates = [], [], []
    for x, row in streams:
        h, s, gt = _router(x, ada_l, row, g, w_router_pad)
        hs.append(h), sels.append(s[:, :2]), gates.append(gt)
    n_tok = sum(h.shape[0] for h in hs)
    pad = -n_tok % SRC_BLK
    if pad:
        hs.append(jnp.zeros((pad, D_MODEL), BF16))
    h_all = hs[0] if len(hs) == 1 else jnp.concatenate(hs, axis=0)
    sel_all = sels[0] if len(sels) == 1 else jnp.concatenate(sels, axis=0)
    src, pos, tile_expert, tile_subs, blk_lo, blk_n, first_blk, num_blk = _route(sel_all)
    y = _moe_ffn(h_all, src, tile_expert, tile_subs, blk_lo, blk_n, wg, wu, wd, lyr)
    outs, off = [], 0
    for (x, row), gt in zip(streams, gates):
        outs.append(_combine(x, y, pos, gt, first_blk, num_blk, off, ada_l, row))
        off += x.shape[0]
    return outs


def _even_layer(x, ctx, ada_l, g1, g2, w_in, g_qn, g_kn, sink, w_gate_up, b_gate_up, g_gla, w_out,
                w_ffn_gate, w_ffn_up, w_ffn_down, rope_tabs, ctx_out):
    d = D_MODEL
    w_main = jnp.concatenate([w_in[:, 2080:3104], w_in[:, 1024:2048], w_in[:, 3616:4640], w_in[:, 512:1024],
                              w_in[:, 3104:3616], w_in[:, 0:256], w_in[:, 256:512]], axis=1).astype(BF16)
    w_gate = jnp.pad(w_in[:, 2048:2080], ((0, 0), (0, LANES - 2 * GATE_RANK))).astype(BF16)
    nk = B_HEADS * B_DK
    w_up = jnp.zeros((LANES, 2 * nk), F32)
    w_up = w_up.at[0:GATE_RANK, 0:nk].set(w_gate_up[0]).at[GATE_RANK:2 * GATE_RANK, nk:].set(w_gate_up[1])
    b_up = b_gate_up.reshape(1, 2 * nk)
    w_out_b = w_out.astype(BF16)
    wg_b, wu_b, wd_b = w_ffn_gate.astype(BF16), w_ffn_up.astype(BF16), w_ffn_down.astype(BF16)
    gq, gk, gg = g_qn.reshape(1, -1), g_kn.reshape(1, -1), g_gla.reshape(1, -1)
    sink_tab = jnp.broadcast_to(sink[:, None], (A_HEADS, LANES))
    cos, sin = rope_tabs

    pc, la_c = _inproj(ctx, ada_l, 1, g1, w_main, w_gate, w_up, b_up)
    qc, kc = _qkprep(pc, cos, sin, gq, gk, rope=False)
    vc = pc[:, 4352:4608].astype(BF16)
    s0 = jnp.zeros((B_HEADS, B_DV, B_DK), F32)
    oc_f, s_fwd = _gla_scan(pc, la_c, s0, rev=False)
    oc_b, s_bwd = _gla_scan(pc, la_c, s0, rev=True)

    px, la_x = _inproj(x, ada_l, 0, g1, w_main, w_gate, w_up, b_up)
    qx, kx = _qkprep(px, cos, sin, gq, gk, rope=True)
    vx = px[:, 4352:4608].astype(BF16)
    oa = _attention(qx, kx, vx, kc, vc, sink_tab, local=True)
    ox_f, _ = _gla_scan(px, la_x, s_fwd, rev=False)
    ox_b, _ = _gla_scan(px, la_x, s_bwd, rev=True)
    x = _outproj(oa, ox_f, ox_b, px, gg, w_out_b, x, ada_l, 0)
    x = _ffn(x, ada_l, 0, g2, wg_b, wu_b, wd_b)
    if ctx_out:
        oa_c = _attention(qc, None, None, kc, vc, sink_tab, local=False)
        ctx = _outproj(oa_c, oc_f, oc_b, pc, gg, w_out_b, ctx, ada_l, 1)
        ctx = _ffn(ctx, ada_l, 1, g2, wg_b, wu_b, wd_b)
    return x, ctx


def _odd_layer(x, ctx, ada_l, g1, g2, w_pool, pool_scale, w_router, w_exp_gate, w_exp_up, w_exp_down, lyr, ctx_out):
    w_pool_b = w_pool.astype(BF16)
    ps = pool_scale.reshape(1, -1)
    wr = jnp.pad(w_router, ((0, 0), (0, LANES - N_EXPERTS)))
    x = _pool(x, ada_l, 0, g1, w_pool_b, ps)
    if ctx_out:
        ctx = _pool(ctx, ada_l, 1, g1, w_pool_b, ps)
        x, ctx = _moe([(x, 0), (ctx, 1)], ada_l, g2, wr, w_exp_gate, w_exp_up, w_exp_down, lyr)
    else:
        (x,) = _moe([(x, 0)], ada_l, g2, wr, w_exp_gate, w_exp_up, w_exp_down, lyr)
    return x, ctx


def kernel(x, c, ctx, c_ctx, w_ada, b_ada, norm_g, w_in, g_qn, g_kn, attn_sink, w_gate_up, b_gate_up, g_gla, w_out,
           w_ffn_gate, w_ffn_up, w_ffn_down, w_pool, pool_scale, w_router, w_exp_gate, w_exp_up, w_exp_down):
    depth = w_ada.shape[0]
    xs = x[0]
    cs = ctx[0]
    t = xs.shape[0]
    cond = jnp.zeros((8, D_MODEL), F32).at[0].set(c[0]).at[1].set(c_ctx)
    ada = _ada_all(cond, w_ada, b_ada)
    rope_tabs = _rope_tables(t)
    for l in range(depth):
        ctx_later = any(j % 2 == 0 for j in range(l + 1, depth))
        g1 = norm_g[l, 0].reshape(1, -1)
        g2 = norm_g[l, 1].reshape(1, -1)
        if l % 2 == 0:
            e = l // 2
            xs, cs = _even_layer(xs, cs, ada[l], g1, g2, w_in[e], g_qn[e], g_kn[e], attn_sink[e], w_gate_up[e],
                                 b_gate_up[e], g_gla[e], w_out[e], w_ffn_gate[e], w_ffn_up[e], w_ffn_down[e],
                                 rope_tabs, ctx_later)
        else:
            o = l // 2
            xs, cs = _odd_layer(xs, cs, ada[l], g1, g2, w_pool[o], pool_scale[o], w_router[o],
                                w_exp_gate, w_exp_up, w_exp_down, o, ctx_later)
    return xs[None]
```

```python
import functools

import jax
import jax.numpy as jnp
import numpy as np
from jax import lax
from jax.experimental import pallas as pl
from jax.experimental.pallas import tpu as pltpu

F32 = jnp.float32
BF16 = jnp.bfloat16
HIGHEST = lax.Precision.HIGHEST

D_MODEL = 2048
GRID_W = 64
A_HEADS = 8
A_KV_HEADS = 2
A_GROUP = 4
A_HEAD_DIM = 128
A_WIDTH = A_HEADS * A_HEAD_DIM
ATTN_BLOCK = 128
ROPE_BASE = 10000.0
B_HEADS = 4
B_DV = 256
B_DK = 128
B_WIDTH = B_HEADS * B_DV
GATE_RANK = 16
GATE_TAU = 16.0
GLA_CHUNK = 64
GLA_SUB = 256
POOL_GROUPS = 4
POOL_WINDOWS = (2, 4, 8, 16)
POOL_HALO = 8
N_EXPERTS = 8
NORM_EPS = 1e-6
NEG_INF = -1e30
LANES = 128

P_COLS = 4608
VMEM_LIMIT = 56 * 1024 * 1024
MOE_VMEM_LIMIT = 62 * 1024 * 1024


def _params(sem, vmem=VMEM_LIMIT):
    return pltpu.CompilerParams(dimension_semantics=sem, vmem_limit_bytes=vmem)


def _dot(a, b, precision=None):
    return jnp.dot(a, b, preferred_element_type=F32, precision=precision)


def _dot_nt(a, b):
    return lax.dot_general(a, b, (((1,), (1,)), ((), ())), preferred_element_type=F32)


def _dot_tn(a, b):
    return lax.dot_general(a, b, (((0,), (0,)), ((), ())), preferred_element_type=F32)


def _silu(x):
    return x * jax.nn.sigmoid(x)


def _norm_mod(x, g, shift, scale):
    ms = jnp.mean(x * x, axis=-1, keepdims=True)
    return (x * lax.rsqrt(ms + NORM_EPS) * g) * (1.0 + scale) + shift


def _ada_kernel(cond_ref, w_ref, b_ref, o_ref):
    s = _silu(cond_ref[...])
    s_hi = s.astype(BF16)
    s_lo = (s - s_hi.astype(F32)).astype(BF16)
    w = w_ref[0].astype(BF16)
    o_ref[0] = _dot(s_hi, w) + _dot(s_lo, w) + b_ref[0]


def _ada_all(cond, w_ada, b_ada):
    depth, d, n = w_ada.shape
    tn = 1024
    return pl.pallas_call(
        _ada_kernel,
        grid=(depth, n // tn),
        in_specs=[pl.BlockSpec((8, d), lambda l, j: (0, 0)),
                  pl.BlockSpec((1, d, tn), lambda l, j: (l, 0, j)),
                  pl.BlockSpec((1, 1, tn), lambda l, j: (l, 0, j))],
        out_specs=pl.BlockSpec((1, 8, tn), lambda l, j: (l, 0, j)),
        out_shape=jax.ShapeDtypeStruct((depth, 8, n), F32),
        compiler_params=_params(("parallel", "parallel")),
    )(cond, w_ada, b_ada.reshape(depth, 1, n))


def _ada_spec(k):
    return pl.BlockSpec((8, D_MODEL), lambda *_: (0, k))


def _inproj_kernel(x_ref, g_ref, sh_ref, sc_ref, w_ref, wg_ref, wup_ref, bup_ref, p_ref, la_ref, hn_ref, *, row):
    @pl.when(pl.program_id(1) == 0)
    def _():
        h = _norm_mod(x_ref[...], g_ref[...], sh_ref[row:row + 1, :], sc_ref[row:row + 1, :]).astype(BF16)
        hn_ref[...] = h
        lr = _dot(h, wg_ref[...])
        z = _dot(lr, wup_ref[...], HIGHEST) + bup_ref[...]
        la_ref[...] = (jnp.minimum(z, 0.0) - jnp.log1p(jnp.exp(-jnp.abs(z)))) * (1.0 / GATE_TAU)

    p_ref[...] = _dot(hn_ref[...], w_ref[...])


def _inproj(x, ada_l, row, g, w_main, w_gate, w_up, b_up):
    t, d = x.shape
    tm = min(512, t)
    tn = 1536
    nla = 2 * B_HEADS * B_DK
    return pl.pallas_call(
        functools.partial(_inproj_kernel, row=row),
        grid=(t // tm, P_COLS // tn),
        in_specs=[pl.BlockSpec((tm, d), lambda i, j: (i, 0)),
                  pl.BlockSpec((1, d), lambda i, j: (0, 0)),
                  _ada_spec(0), _ada_spec(1),
                  pl.BlockSpec((d, tn), lambda i, j: (0, j)),
                  pl.BlockSpec((d, LANES), lambda i, j: (0, 0)),
                  pl.BlockSpec((LANES, nla), lambda i, j: (0, 0)),
                  pl.BlockSpec((1, nla), lambda i, j: (0, 0))],
        out_specs=[pl.BlockSpec((tm, tn), lambda i, j: (i, j)),
                   pl.BlockSpec((tm, nla), lambda i, j: (i, 0))],
        out_shape=[jax.ShapeDtypeStruct((t, P_COLS), F32),
                   jax.ShapeDtypeStruct((t, nla), F32)],
        scratch_shapes=[pltpu.VMEM((tm, d), BF16)],
        compiler_params=_params(("parallel", "arbitrary")),
    )(x, g, ada_l, ada_l, w_main, w_gate, w_up, b_up)


def _qkprep_kernel(q_ref, k_ref, cos_ref, sin_ref, gq_ref, gk_ref, qo_ref, ko_ref, *, rope):
    tm = q_ref.shape[0]
    lane = lax.broadcasted_iota(jnp.int32, (tm, A_HEAD_DIM), 1)
    first_half = (lane % 64) < 32

    def prep(xh, g, scale):
        ms = jnp.mean(xh * xh, axis=-1, keepdims=True)
        y = xh * lax.rsqrt(ms + NORM_EPS) * g
        if rope:
            partner = jnp.where(first_half, pltpu.roll(y, 96, 1), pltpu.roll(y, 32, 1))
            y = y * cos_ref[...] + partner * sin_ref[...]
        return (y * scale).astype(BF16)

    for h in range(A_HEADS):
        cols = slice(h * A_HEAD_DIM, (h + 1) * A_HEAD_DIM)
        qo_ref[:, cols] = prep(q_ref[:, cols], gq_ref[...], A_HEAD_DIM ** -0.5)
    for h in range(A_KV_HEADS):
        cols = slice(h * A_HEAD_DIM, (h + 1) * A_HEAD_DIM)
        ko_ref[:, cols] = prep(k_ref[:, cols], gk_ref[...], 1.0)


def _qkprep(p, cos, sin, g_qn, g_kn, rope):
    t = p.shape[0]
    tm = min(512, t)
    kw = A_KV_HEADS * A_HEAD_DIM
    return pl.pallas_call(
        functools.partial(_qkprep_kernel, rope=rope),
        grid=(t // tm,),
        in_specs=[pl.BlockSpec((tm, A_WIDTH), lambda i: (i, 0)),
                  pl.BlockSpec((tm, kw), lambda i: (i, 4096 // kw)),
                  pl.BlockSpec((tm, A_HEAD_DIM), lambda i: (i, 0)),
                  pl.BlockSpec((tm, A_HEAD_DIM), lambda i: (i, 0)),
                  pl.BlockSpec((1, A_HEAD_DIM), lambda i: (0, 0)),
                  pl.BlockSpec((1, A_HEAD_DIM), lambda i: (0, 0))],
        out_specs=[pl.BlockSpec((tm, A_WIDTH), lambda i: (i, 0)),
                   pl.BlockSpec((tm, kw), lambda i: (i, 0))],
        out_shape=[jax.ShapeDtypeStruct((t, A_WIDTH), BF16),
                   jax.ShapeDtypeStruct((t, kw), BF16)],
        compiler_params=_params(("parallel",)),
    )(p, p, cos, sin, g_qn, g_kn)


def _rope_tables(t):
    half = A_HEAD_DIM // 4
    freqs = ROPE_BASE ** (-jnp.arange(half, dtype=F32) / half)
    tok = jnp.arange(t)
    row = (tok // GRID_W).astype(F32)[:, None] * freqs
    col = (tok % GRID_W).astype(F32)[:, None] * freqs
    cos = jnp.concatenate([jnp.cos(row), jnp.cos(row), jnp.cos(col), jnp.cos(col)], axis=-1)
    sin = jnp.concatenate([-jnp.sin(row), jnp.sin(row), -jnp.sin(col), jnp.sin(col)], axis=-1)
    return cos, sin


def _softmax_pv(scores, values, sink_col):
    def lane_blocks(xs):
        return [x[:, i * LANES:(i + 1) * LANES] for x in xs for i in range(x.shape[1] // LANES)]

    m = jnp.maximum(jnp.max(functools.reduce(jnp.maximum, lane_blocks(scores)), axis=-1, keepdims=True), sink_col)
    probs = [jnp.exp(s - m) for s in scores]
    denom = jnp.exp(sink_col - m) + jnp.sum(functools.reduce(jnp.add, lane_blocks(probs)), axis=-1, keepdims=True)
    acc = None
    for p, v in zip(probs, values):
        pv = _dot(p.astype(BF16), v)
        acc = pv if acc is None else acc + pv
    return acc / denom


def _attn_kernel(*refs, local):
    if local:
        q_ref, kp_ref, ko_ref, kn_ref, vp_ref, vo_ref, vn_ref, kc_ref, vc_ref, sink_ref, o_ref = refs
    else:
        q_ref, kc_ref, vc_ref, sink_ref, o_ref = refs
    n = pl.program_id(0)
    nb = pl.num_programs(0)
    rows = A_GROUP * ATTN_BLOCK
    if local:
        qi = lax.broadcasted_iota(jnp.int32, (rows, ATTN_BLOCK), 0) % ATTN_BLOCK
        kj = lax.broadcasted_iota(jnp.int32, (rows, ATTN_BLOCK), 1)
        mask_prev = (kj >= qi) & (n > 0)
        mask_next = (kj <= qi) & (n < nb - 1)
    for kv in range(A_KV_HEADS):
        kc = slice(kv * A_HEAD_DIM, (kv + 1) * A_HEAD_DIM)
        heads = [kv * A_GROUP + g for g in range(A_GROUP)]
        q4 = jnp.concatenate([q_ref[:, h * A_HEAD_DIM:(h + 1) * A_HEAD_DIM] for h in heads], axis=0)
        sink_col = jnp.concatenate(
            [jnp.broadcast_to(sink_ref[h:h + 1, 0:1], (ATTN_BLOCK, 1)) for h in heads], axis=0)
        scores, values = [], []
        if local:
            scores.append(jnp.where(mask_prev, _dot_nt(q4, kp_ref[:, kc]), NEG_INF))
            scores.append(_dot_nt(q4, ko_ref[:, kc]))
            scores.append(jnp.where(mask_next, _dot_nt(q4, kn_ref[:, kc]), NEG_INF))
            values += [vp_ref[:, kc], vo_ref[:, kc], vn_ref[:, kc]]
        scores.append(_dot_nt(q4, kc_ref[:, kc]))
        values.append(vc_ref[:, kc])
        o = _softmax_pv(scores, values, sink_col).astype(o_ref.dtype)
        for g, h in enumerate(heads):
            o_ref[:, h * A_HEAD_DIM:(h + 1) * A_HEAD_DIM] = o[g * ATTN_BLOCK:(g + 1) * ATTN_BLOCK, :]


def _attention(q, k, v, k_ctx, v_ctx, sink_tab, local):
    t = q.shape[0]
    nb = t // ATTN_BLOCK
    l = k_ctx.shape[0]
    kw = A_KV_HEADS * A_HEAD_DIM
    blk = lambda f: pl.BlockSpec((ATTN_BLOCK, kw), f)
    prev = lambda i: (jnp.maximum(i - 1, 0), 0)
    own = lambda i: (i, 0)
    nxt = lambda i: (jnp.minimum(i + 1, nb - 1), 0)
    full = pl.BlockSpec((l, kw), lambda i: (0, 0))
    in_specs = [pl.BlockSpec((ATTN_BLOCK, A_WIDTH), own)]
    args = [q]
    if local:
        in_specs += [blk(prev), blk(own), blk(nxt), blk(prev), blk(own), blk(nxt)]
        args += [k, k, k, v, v, v]
    in_specs += [full, full, pl.BlockSpec((A_HEADS, LANES), lambda i: (0, 0))]
    args += [k_ctx, v_ctx, sink_tab]
    return pl.pallas_call(
        functools.partial(_attn_kernel, local=local),
        grid=(nb,),
        in_specs=in_specs,
        out_specs=pl.BlockSpec((ATTN_BLOCK, A_WIDTH), own),
        out_shape=jax.ShapeDtypeStruct((t, A_WIDTH), BF16),
        compiler_params=_params(("parallel",)),
    )(*args)


def _gla_kernel(q_ref, k_ref, v_ref, la_ref, s0_ref, o_ref, sf_ref, st_ref, *, rev, nchunk):
    @pl.when(pl.program_id(0) == 0)
    def _():
        st_ref[...] = s0_ref[...]

    r = nchunk * GLA_CHUNK
    sub = min(r, GLA_SUB)
    ii = lax.broadcasted_iota(jnp.int32, (sub, sub), 0)
    jj = lax.broadcasted_iota(jnp.int32, (sub, sub), 1)
    same_chunk = (ii // GLA_CHUNK) == (jj // GLA_CHUNK)
    tri = same_chunk & ((jj >= ii) if rev else (jj <= ii))
    b_all = la_ref[...]
    pos = lax.broadcasted_iota(jnp.int32, b_all.shape, 0) % GLA_CHUNK
    step = 1
    while step < GLA_CHUNK:
        if rev:
            b_all = b_all + jnp.where(pos < GLA_CHUNK - step, pltpu.roll(b_all, r - step, 0), 0.0)
        else:
            b_all = b_all + jnp.where(pos >= step, pltpu.roll(b_all, step, 0), 0.0)
        step *= 2
    heads = []
    for h in range(B_HEADS):
        kc = slice(h * B_DK, (h + 1) * B_DK)
        b = b_all[:, kc]
        b3 = b.reshape(nchunk, GLA_CHUNK, B_DK)
        b_end = b3[:, 0:1, :] if rev else b3[:, GLA_CHUNK - 1:GLA_CHUNK, :]
        k = k_ref[:, kc]
        qe = (q_ref[:, kc] * (B_DK ** -0.5) * jnp.exp(b)).astype(BF16)
        ke = (k * jnp.exp(-b)).astype(BF16)
        kd = (k.reshape(nchunk, GLA_CHUNK, B_DK) * jnp.exp(b_end - b3)).astype(BF16)
        v = v_ref[:, h * B_DV:(h + 1) * B_DV].astype(BF16)
        o_intra = []
        for s0 in range(0, r, sub):
            rs = slice(s0, s0 + sub)
            a = jnp.where(tri, _dot_nt(qe[rs, :], ke[rs, :]), 0.0).astype(BF16)
            o_intra.append(_dot(a, v[rs, :]))
        heads.append((qe, kd, v, jnp.concatenate(o_intra, axis=0), jnp.exp(b_end)))
    st = [st_ref[h] for h in range(B_HEADS)]
    for c in (range(nchunk - 1, -1, -1) if rev else range(nchunk)):
        rows = slice(c * GLA_CHUNK, (c + 1) * GLA_CHUNK)
        for h, (qe, kd, v, o_intra, decay) in enumerate(heads):
            o_ref[0, rows, h * B_DV:(h + 1) * B_DV] = o_intra[rows, :] + _dot_nt(qe[rows, :], st[h].astype(BF16))
            st[h] = st[h] * decay[c] + _dot_tn(v[rows, :], kd[c])
    for h in range(B_HEADS):
        st_ref[h] = st[h]
        sf_ref[h] = st[h]


def _gla_scan(p, la, s0, rev):
    t = p.shape[0]
    r = min(512, t)
    nblk = t // r
    rb = (lambda c: nblk - 1 - c) if rev else (lambda c: c)
    nk = B_HEADS * B_DK
    o, sf = pl.pallas_call(
        functools.partial(_gla_kernel, rev=rev, nchunk=r // GLA_CHUNK),
        grid=(nblk,),
        in_specs=[pl.BlockSpec((r, nk), lambda c: (rb(c), 3584 // nk)),
                  pl.BlockSpec((r, nk), lambda c: (rb(c), 3072 // nk)),
                  pl.BlockSpec((r, B_WIDTH), lambda c: (rb(c), 1024 // B_WIDTH)),
                  pl.BlockSpec((r, nk), lambda c: (rb(c), 1 if rev else 0)),
                  pl.BlockSpec((B_HEADS, B_DV, B_DK), lambda c: (0, 0, 0))],
        out_specs=[pl.BlockSpec((1, r, B_WIDTH), lambda c: (0, rb(c), 0)),
                   pl.BlockSpec((B_HEADS, B_DV, B_DK), lambda c: (0, 0, 0))],
        out_shape=[jax.ShapeDtypeStruct((1, t, B_WIDTH), F32),
                   jax.ShapeDtypeStruct((B_HEADS, B_DV, B_DK), F32)],
        scratch_shapes=[pltpu.VMEM((B_HEADS, B_DV, B_DK), F32)],
        compiler_params=_params(("arbitrary",)),
    )(p, p, p, la, s0)
    return o[0], sf


def _outproj_kernel(oa_ref, of_ref, ob_ref, og_ref, gg_ref, w_ref, x_ref, gt_ref, o_ref, mix_ref, *, row):
    @pl.when(pl.program_id(1) == 0)
    def _():
        mix_ref[:, 0:A_WIDTH] = oa_ref[...]
        for h in range(B_HEADS):
            cols = slice(h * B_DV, (h + 1) * B_DV)
            o = of_ref[:, cols] + ob_ref[:, cols]
            ms = jnp.mean(o * o, axis=-1, keepdims=True)
            y = o * lax.rsqrt(ms + NORM_EPS) * gg_ref[...]
            mix_ref[:, A_WIDTH + h * B_DV:A_WIDTH + (h + 1) * B_DV] = (y * _silu(og_ref[:, cols])).astype(BF16)

    o_ref[...] = x_ref[...] + gt_ref[row:row + 1, :] * _dot(mix_ref[...], w_ref[...])


def _outproj(oa, o_f, o_b, p, g_gla, w_out, x, ada_l, row):
    t, d = x.shape
    tm = min(512, t)
    tn = 1024
    return pl.pallas_call(
        functools.partial(_outproj_kernel, row=row),
        grid=(t // tm, d // tn),
        in_specs=[pl.BlockSpec((tm, A_WIDTH), lambda i, j: (i, 0)),
                  pl.BlockSpec((tm, B_WIDTH), lambda i, j: (i, 0)),
                  pl.BlockSpec((tm, B_WIDTH), lambda i, j: (i, 0)),
                  pl.BlockSpec((tm, B_WIDTH), lambda i, j: (i, 2048 // B_WIDTH)),
                  pl.BlockSpec((1, B_DV), lambda i, j: (0, 0)),
                  pl.BlockSpec((A_WIDTH + B_WIDTH, tn), lambda i, j: (0, j)),
                  pl.BlockSpec((tm, tn), lambda i, j: (i, j)),
                  pl.BlockSpec((8, tn), lambda i, j: (0, 2 * (d // tn) + j))],
        out_specs=pl.BlockSpec((tm, tn), lambda i, j: (i, j)),
        out_shape=jax.ShapeDtypeStruct((t, d), F32),
        scratch_shapes=[pltpu.VMEM((tm, A_WIDTH + B_WIDTH), BF16)],
        compiler_params=_params(("parallel", "arbitrary")),
    )(oa, o_f, o_b, p, g_gla, w_out, x, ada_l)


def _ffn_kernel(x_ref, g_ref, sh_ref, sc_ref, gt_ref, wg_ref, wu_ref, wd_ref, o_ref, hn_ref, acc_ref, *, row):
    f = pl.program_id(1)

    @pl.when(f == 0)
    def _():
        hn_ref[...] = _norm_mod(x_ref[...], g_ref[...], sh_ref[row:row + 1, :], sc_ref[row:row + 1, :]).astype(BF16)
        acc_ref[...] = jnp.zeros_like(acc_ref)

    h = hn_ref[...]
    a = (_silu(_dot(h, wg_ref[...])) * _dot(h, wu_ref[...])).astype(BF16)
    acc_ref[...] += _dot(a, wd_ref[...])

    @pl.when(f == pl.num_programs(1) - 1)
    def _():
        o_ref[...] = x_ref[...] + gt_ref[row:row + 1, :] * acc_ref[...]


def _ffn(x, ada_l, row, g, wg, wu, wd):
    t, d = x.shape
    ff = wg.shape[1]
    tm = min(512, t)
    tf = 512
    return pl.pallas_call(
        functools.partial(_ffn_kernel, row=row),
        grid=(t // tm, ff // tf),
        in_specs=[pl.BlockSpec((tm, d), lambda i, f: (i, 0)),
                  pl.BlockSpec((1, d), lambda i, f: (0, 0)),
                  _ada_spec(3), _ada_spec(4), _ada_spec(5),
                  pl.BlockSpec((d, tf), lambda i, f: (0, f)),
                  pl.BlockSpec((d, tf), lambda i, f: (0, f)),
                  pl.BlockSpec((tf, d), lambda i, f: (f, 0))],
        out_specs=pl.BlockSpec((tm, d), lambda i, f: (i, 0)),
        out_shape=jax.ShapeDtypeStruct((t, d), F32),
        scratch_shapes=[pltpu.VMEM((tm, d), BF16), pltpu.VMEM((tm, d), F32)],
        compiler_params=_params(("parallel", "arbitrary")),
    )(x, g, ada_l, ada_l, ada_l, wg, wu, wd)


def _pool_kernel(x_ref, xp_ref, xn_ref, g_ref, sh_ref, sc_ref, gt_ref, w_ref, ps_ref, o_ref, hb_ref, *, row, t):
    i = pl.program_id(0)
    tm = x_ref.shape[0]
    g = g_ref[...]
    sh = sh_ref[row:row + 1, :]
    sc = sc_ref[row:row + 1, :]
    hb_ref[0:POOL_HALO, :] = jnp.where(i > 0, _norm_mod(xp_ref[...], g, sh, sc), 0.0)
    hb_ref[POOL_HALO:POOL_HALO + tm, :] = _norm_mod(x_ref[...], g, sh, sc)
    hb_ref[POOL_HALO + tm:, :] = jnp.where(i < pl.num_programs(0) - 1, _norm_mod(xn_ref[...], g, sh, sc), 0.0)
    tpos = i * tm + lax.broadcasted_iota(jnp.int32, (tm, 1), 0)
    gc = D_MODEL // POOL_GROUPS
    for grp, w in enumerate(POOL_WINDOWS):
        cols = slice(grp * gc, (grp + 1) * gc)
        acc = None
        for dlt in range(-(w // 2), w - w // 2):
            piece = hb_ref[POOL_HALO + dlt:POOL_HALO + dlt + tm, cols]
            acc = piece if acc is None else acc + piece
        cnt = jnp.minimum(tpos + (w - w // 2), t) - jnp.maximum(tpos - w // 2, 0)
        pooled = acc / cnt.astype(F32) - hb_ref[POOL_HALO:POOL_HALO + tm, cols]
        y = _dot(pooled.astype(BF16), w_ref[grp])
        o_ref[:, cols] = x_ref[:, cols] + gt_ref[row:row + 1, cols] * (y * ps_ref[:, cols])


def _pool(x, ada_l, row, g, w_pool, pool_scale):
    t, d = x.shape
    tm = min(256, t)
    nh = t // POOL_HALO
    gc = d // POOL_GROUPS
    return pl.pallas_call(
        functools.partial(_pool_kernel, row=row, t=t),
        grid=(t // tm,),
        in_specs=[pl.BlockSpec((tm, d), lambda i: (i, 0)),
                  pl.BlockSpec((POOL_HALO, d), lambda i: (jnp.maximum(i * (tm // POOL_HALO) - 1, 0), 0)),
                  pl.BlockSpec((POOL_HALO, d), lambda i: (jnp.minimum((i + 1) * (tm // POOL_HALO), nh - 1), 0)),
                  pl.BlockSpec((1, d), lambda i: (0, 0)),
                  _ada_spec(0), _ada_spec(1), _ada_spec(2),
                  pl.BlockSpec((POOL_GROUPS, gc, gc), lambda i: (0, 0, 0)),
                  pl.BlockSpec((1, d), lambda i: (0, 0))],
        out_specs=pl.BlockSpec((tm, d), lambda i: (i, 0)),
        out_shape=jax.ShapeDtypeStruct((t, d), F32),
        scratch_shapes=[pltpu.VMEM((tm + 2 * POOL_HALO, d), F32)],
        compiler_params=_params(("parallel",)),
    )(x, x, x, g, ada_l, ada_l, ada_l, w_pool, pool_scale)


def _router_kernel(x_ref, g_ref, sh_ref, sc_ref, wr_ref, hn_ref, sel_ref, gate_ref, *, row):
    h = _norm_mod(x_ref[...], g_ref[...], sh_ref[row:row + 1, :], sc_ref[row:row + 1, :])
    hn_ref[...] = h.astype(hn_ref.dtype)
    lane = lax.broadcasted_iota(jnp.int32, (h.shape[0], LANES), 1)
    lane_f = lane.astype(F32)
    logits = jnp.where(lane < N_EXPERTS, _dot(h, wr_ref[...], HIGHEST), -jnp.inf)
    m1 = jnp.max(logits, axis=-1, keepdims=True)
    i1 = jnp.min(jnp.where(logits == m1, lane_f, float(LANES)), axis=-1, keepdims=True)
    rest = jnp.where(lane_f == i1, -jnp.inf, logits)
    m2 = jnp.max(rest, axis=-1, keepdims=True)
    i2 = jnp.min(jnp.where(rest == m2, lane_f, float(LANES)), axis=-1, keepdims=True)
    e2 = jnp.exp(m2 - m1)
    den = 1.0 + e2
    sel_ref[...] = jnp.where(lane == 0, i1, jnp.where(lane == 1, i2, 0.0)).astype(jnp.int32)
    gate_ref[...] = jnp.where(lane == 0, 1.0 / den, jnp.where(lane == 1, e2 / den, 0.0))


def _router(x, ada_l, row, g, w_router_pad):
    t, d = x.shape
    tm = min(256, t)
    return pl.pallas_call(
        functools.partial(_router_kernel, row=row),
        grid=(t // tm,),
        in_specs=[pl.BlockSpec((tm, d), lambda i: (i, 0)),
                  pl.BlockSpec((1, d), lambda i: (0, 0)),
                  _ada_spec(3), _ada_spec(4),
                  pl.BlockSpec((d, LANES), lambda i: (0, 0))],
        out_specs=[pl.BlockSpec((tm, d), lambda i: (i, 0)),
                   pl.BlockSpec((tm, LANES), lambda i: (i, 0)),
                   pl.BlockSpec((tm, LANES), lambda i: (i, 0))],
        out_shape=[jax.ShapeDtypeStruct((t, d), BF16),
                   jax.ShapeDtypeStruct((t, LANES), jnp.int32),
                   jax.ShapeDtypeStruct((t, LANES), F32)],
        compiler_params=_params(("parallel",)),
    )(x, g, ada_l, ada_l, w_router_pad)


MOE_TILE = 1024
MOE_SUB = 256
MOE_SUBS = MOE_TILE // MOE_SUB
SRC_BLK = 256
Y_BLK = 256
CMB_TILE = 256
MOE_TF = 512


def _moe_ffn_kernel(te_ref, tn_ref, blo_ref, bn_ref, src_ref, hn_hbm, wg_ref, wu_ref, wd_ref, y_ref,
                    xb_ref, acc_ref, hbuf_ref, sem):
    i = pl.program_id(0)
    f = pl.program_id(1)
    nsub = tn_ref[i]

    @pl.when(f == 0)
    def _():
        xb_ref[...] = jnp.zeros_like(xb_ref)
        acc_ref[...] = jnp.zeros_like(acc_ref)
        col = lax.broadcasted_iota(jnp.int32, (MOE_SUB, SRC_BLK), 1)

        def gather_sub(q, carry):
            rows = pl.ds(pl.multiple_of(q * MOE_SUB, MOE_SUB), MOE_SUB)
            n = bn_ref[i * MOE_SUBS + q]
            lo = blo_ref[i * MOE_SUBS + q]
            src = src_ref[rows, :]

            def blk_copy(b, slot):
                return pltpu.make_async_copy(hn_hbm.at[pl.ds((lo + b) * SRC_BLK, SRC_BLK), :], hbuf_ref.at[slot],
                                             sem.at[slot])

            @pl.when(n > 0)
            def _():
                blk_copy(0, 0).start()

            def body(b, c):
                slot = b % 2
                blk_copy(b, slot).wait()

                @pl.when(b + 1 < n)
                def _():
                    blk_copy(b + 1, 1 - slot).start()

                onehot = jnp.where(src - (lo + b) * SRC_BLK == col, 1.0, 0.0).astype(BF16)
                xb_ref[rows, :] += _dot(onehot, hbuf_ref[slot]).astype(BF16)
                return c

            lax.fori_loop(0, n, body, 0)
            return carry

        lax.fori_loop(0, nsub, gather_sub, 0)

    for k in range(1, MOE_SUBS + 1):
        @pl.when(nsub == k)
        def _():
            rows = slice(0, k * MOE_SUB)
            h = xb_ref[rows, :]
            a = (_silu(_dot(h, wg_ref[0, 0].astype(BF16))) * _dot(h, wu_ref[0, 0].astype(BF16))).astype(BF16)
            acc_ref[rows, :] += _dot(a, wd_ref[0, 0].astype(BF16))

    @pl.when(f == pl.num_programs(1) - 1)
    def _():
        y_ref[...] = acc_ref[...].astype(y_ref.dtype)


def _moe_ffn(hn, src, tile_expert, tile_subs, blk_lo, blk_n, wg, wu, wd, lyr):
    d = hn.shape[1]
    tm = MOE_TILE
    ff = wg.shape[3]
    tf = MOE_TF
    nf = ff // tf
    nt = src.shape[0] // tm
    fidx = lambda i, f, tn: jnp.where(tn[i] > 0, f, nf - 1)
    return pl.pallas_call(
        _moe_ffn_kernel,
        grid_spec=pltpu.PrefetchScalarGridSpec(
            num_scalar_prefetch=4,
            grid=(nt, nf),
            in_specs=[pl.BlockSpec((tm, 1), lambda i, f, te, tn, bl, bn: (i, 0)),
                      pl.BlockSpec(memory_space=pl.ANY),
                      pl.BlockSpec((1, 1, d, tf), lambda i, f, te, tn, bl, bn: (lyr, te[i], 0, fidx(i, f, tn))),
                      pl.BlockSpec((1, 1, d, tf), lambda i, f, te, tn, bl, bn: (lyr, te[i], 0, fidx(i, f, tn))),
                      pl.BlockSpec((1, 1, tf, d), lambda i, f, te, tn, bl, bn: (lyr, te[i], fidx(i, f, tn), 0))],
            out_specs=pl.BlockSpec((tm, d), lambda i, f, te, tn, bl, bn: (i, 0)),
            scratch_shapes=[pltpu.VMEM((tm, d), BF16), pltpu.VMEM((tm, d), F32),
                            pltpu.VMEM((2, SRC_BLK, d), BF16), pltpu.SemaphoreType.DMA((2,))]),
        out_shape=jax.ShapeDtypeStruct((nt * tm, d), BF16),
        compiler_params=_params(("arbitrary", "arbitrary"), MOE_VMEM_LIMIT),
    )(tile_expert, tile_subs, blk_lo, blk_n, src, hn, wg, wu, wd)


CMB_SLOTS = [(e, j) for e in range(N_EXPERTS) for j in range(2)]


def _combine_kernel(fb_ref, nb_ref, x_ref, pos_ref, gate_ref, gt_ref, y_hbm, o_ref, acc_ref, ybuf_ref, sem,
                    *, row, tile_off):
    base = (tile_off + pl.program_id(0)) * N_EXPERTS
    tm = x_ref.shape[0]
    pos1, pos2 = pos_ref[:, 0:1], pos_ref[:, 1:2]
    g1, g2 = gate_ref[:, 0:1], gate_ref[:, 1:2]
    col = lax.broadcasted_iota(jnp.int32, (tm, Y_BLK), 1)

    def blk(s):
        e, j = CMB_SLOTS[s]
        return fb_ref[base + e] + j

    def used(s):
        e, j = CMB_SLOTS[s]
        return nb_ref[base + e] > j

    def blk_copy(s):
        return pltpu.make_async_copy(y_hbm.at[pl.ds(blk(s) * Y_BLK, Y_BLK), :], ybuf_ref.at[s], sem.at[s])

    for s in range(len(CMB_SLOTS)):
        @pl.when(used(s))
        def _():
            blk_copy(s).start()

    acc_ref[...] = jnp.zeros_like(acc_ref)
    for s in range(len(CMB_SLOTS)):
        @pl.when(used(s))
        def _():
            blk_copy(s).wait()
            off = blk(s) * Y_BLK
            w = jnp.where(pos1 - off == col, g1, 0.0) + jnp.where(pos2 - off == col, g2, 0.0)
            acc_ref[...] += _dot(w.astype(BF16), ybuf_ref[s])

    o_ref[...] = x_ref[...] + gt_ref[row:row + 1, :] * acc_ref[...]


def _combine(x, y, pos, gates, first_blk, num_blk, tok_off, ada_l, row):
    t, d = x.shape
    tm = CMB_TILE
    tile_off = tok_off // tm
    return pl.pallas_call(
        functools.partial(_combine_kernel, row=row, tile_off=tile_off),
        grid_spec=pltpu.PrefetchScalarGridSpec(
            num_scalar_prefetch=2,
            grid=(t // tm,),
            in_specs=[pl.BlockSpec((tm, d), lambda i, fb, nb: (i, 0)),
                      pl.BlockSpec((tm, 2), lambda i, fb, nb: (tile_off + i, 0)),
                      pl.BlockSpec((tm, LANES), lambda i, fb, nb: (i, 0)),
                      _ada_spec(5),
                      pl.BlockSpec(memory_space=pl.ANY)],
            out_specs=pl.BlockSpec((tm, d), lambda i, fb, nb: (i, 0)),
            scratch_shapes=[pltpu.VMEM((tm, d), F32), pltpu.VMEM((len(CMB_SLOTS), Y_BLK, d), BF16),
                            pltpu.SemaphoreType.DMA((len(CMB_SLOTS),))]),
        out_shape=jax.ShapeDtypeStruct((t, d), F32),
        compiler_params=_params(("arbitrary",)),
    )(first_blk, num_blk, x, pos, gates, ada_l, y)


def _route(sel):
    tm = MOE_TILE
    n_tok = sel.shape[0]
    n_asg = 2 * n_tok
    nt = -(-n_asg // tm) + N_EXPERTS
    e_flat = sel.reshape(-1)
    onehot = (e_flat[:, None] == jnp.arange(N_EXPERTS, dtype=jnp.int32)[None, :]).astype(jnp.int32)
    csum = jnp.cumsum(onehot, axis=0)
    count = csum[-1]
    rank = jnp.sum((csum - onehot) * onehot, axis=1)
    tiles_e = (count + tm - 1) // tm
    tile_end = jnp.cumsum(tiles_e)
    base = (tile_end - tiles_e) * tm
    pos = jnp.sum(onehot * base[None, :], axis=1) + rank
    src = jnp.full((nt * tm,), -1, jnp.int32).at[pos].set(jnp.arange(n_asg, dtype=jnp.int32) // 2)
    tile_id = jnp.arange(nt, dtype=jnp.int32)
    used = tile_end[-1]
    owner = lambda i: jnp.minimum(jnp.sum((i[:, None] >= tile_end[None, :]).astype(jnp.int32), axis=1), N_EXPERTS - 1)
    tile_expert = owner(jnp.minimum(tile_id, used - 1))
    rows_in_tile = jnp.clip((base + count)[tile_expert] - tile_id * tm, 0, tm)
    tile_subs = jnp.where(tile_id < used, (rows_in_tile + MOE_SUB - 1) // MOE_SUB, 0)
    srcm = src.reshape(nt * MOE_SUBS, MOE_SUB)
    tmax = jnp.max(srcm, axis=1)
    tmin = jnp.min(jnp.where(srcm >= 0, srcm, n_tok), axis=1)
    blk_lo = jnp.where(tmax >= 0, tmin // SRC_BLK, 0)
    blk_n = jnp.where(tmax >= 0, tmax // SRC_BLK - blk_lo + 1, 0)
    ntt = n_tok // CMB_TILE
    through = csum.reshape(ntt, 2 * CMB_TILE, N_EXPERTS)[:, -1, :]
    before = jnp.concatenate([jnp.zeros((1, N_EXPERTS), jnp.int32), through[:-1]], axis=0)
    lo = base[None, :] + before
    hi = base[None, :] + through - 1
    first_blk = lo // Y_BLK
    num_blk = jnp.where(through > before, hi // Y_BLK - first_blk + 1, 0)
    return (src.reshape(nt * tm, 1), pos.reshape(n_tok, 2), tile_expert, tile_subs, blk_lo, blk_n,
            first_blk.reshape(-1), num_blk.reshape(-1))


def _moe(streams, ada_l, g, w_router_pad, wg, wu, wd, lyr):
    hs, sels, gates = [], [], []
    for x, row in streams:
        h, s, gt = _router(x, ada_l, row, g, w_router_pad)
        hs.append(h), sels.append(s[:, :2]), gates.append(gt)
    n_tok = sum(h.shape[0] for h in hs)
    pad = -n_tok % SRC_BLK
    if pad:
        hs.append(jnp.zeros((pad, D_MODEL), BF16))
    h_all = hs[0] if len(hs) == 1 else jnp.concatenate(hs, axis=0)
    sel_all = sels[0] if len(sels) == 1 else jnp.concatenate(sels, axis=0)
    src, pos, tile_expert, tile_subs, blk_lo, blk_n, first_blk, num_blk = _route(sel_all)
    y = _moe_ffn(h_all, src, tile_expert, tile_subs, blk_lo, blk_n, wg, wu, wd, lyr)
    outs, off = [], 0
    for (x, row), gt in zip(streams, gates):
        outs.append(_combine(x, y, pos, gt, first_blk, num_blk, off, ada_l, row))
        off += x.shape[0]
    return outs


def _even_layer(x, ctx, ada_l, g1, g2, w_in, g_qn, g_kn, sink, w_gate_up, b_gate_up, g_gla, w_out,
                w_ffn_gate, w_ffn_up, w_ffn_down, rope_tabs, ctx_out):
    d = D_MODEL
    w_main = jnp.concatenate([w_in[:, 2080:3104], w_in[:, 1024:2048], w_in[:, 3616:4640], w_in[:, 512:1024],
                              w_in[:, 3104:3616], w_in[:, 0:256], w_in[:, 256:512]], axis=1).astype(BF16)
    w_gate = jnp.pad(w_in[:, 2048:2080], ((0, 0), (0, LANES - 2 * GATE_RANK))).astype(BF16)
    nk = B_HEADS * B_DK
    w_up = jnp.zeros((LANES, 2 * nk), F32)
    w_up = w_up.at[0:GATE_RANK, 0:nk].set(w_gate_up[0]).at[GATE_RANK:2 * GATE_RANK, nk:].set(w_gate_up[1])
    b_up = b_gate_up.reshape(1, 2 * nk)
    w_out_b = w_out.astype(BF16)
    wg_b, wu_b, wd_b = w_ffn_gate.astype(BF16), w_ffn_up.astype(BF16), w_ffn_down.astype(BF16)
    gq, gk, gg = g_qn.reshape(1, -1), g_kn.reshape(1, -1), g_gla.reshape(1, -1)
    sink_tab = jnp.broadcast_to(sink[:, None], (A_HEADS, LANES))
    cos, sin = rope_tabs

    pc, la_c = _inproj(ctx, ada_l, 1, g1, w_main, w_gate, w_up, b_up)
    qc, kc = _qkprep(pc, cos, sin, gq, gk, rope=False)
    vc = pc[:, 4352:4608].astype(BF16)
    s0 = jnp.zeros((B_HEADS, B_DV, B_DK), F32)
    oc_f, s_fwd = _gla_scan(pc, la_c, s0, rev=False)
    oc_b, s_bwd = _gla_scan(pc, la_c, s0, rev=True)

    px, la_x = _inproj(x, ada_l, 0, g1, w_main, w_gate, w_up, b_up)
    qx, kx = _qkprep(px, cos, sin, gq, gk, rope=True)
    vx = px[:, 4352:4608].astype(BF16)
    oa = _attention(qx, kx, vx, kc, vc, sink_tab, local=True)
    ox_f, _ = _gla_scan(px, la_x, s_fwd, rev=False)
    ox_b, _ = _gla_scan(px, la_x, s_bwd, rev=True)
    x = _outproj(oa, ox_f, ox_b, px, gg, w_out_b, x, ada_l, 0)
    x = _ffn(x, ada_l, 0, g2, wg_b, wu_b, wd_b)
    if ctx_out:
        oa_c = _attention(qc, None, None, kc, vc, sink_tab, local=False)
        ctx = _outproj(oa_c, oc_f, oc_b, pc, gg, w_out_b, ctx, ada_l, 1)
        ctx = _ffn(ctx, ada_l, 1, g2, wg_b, wu_b, wd_b)
    return x, ctx


def _odd_layer(x, ctx, ada_l, g1, g2, w_pool, pool_scale, w_router, w_exp_gate, w_exp_up, w_exp_down, lyr, ctx_out):
    w_pool_b = w_pool.astype(BF16)
    ps = pool_scale.reshape(1, -1)
    wr = jnp.pad(w_router, ((0, 0), (0, LANES - N_EXPERTS)))
    x = _pool(x, ada_l, 0, g1, w_pool_b, ps)
    if ctx_out:
        ctx = _pool(ctx, ada_l, 1, g1, w_pool_b, ps)
        x, ctx = _moe([(x, 0), (ctx, 1)], ada_l, g2, wr, w_exp_gate, w_exp_up, w_exp_down, lyr)
    else:
        (x,) = _moe([(x, 0)], ada_l, g2, wr, w_exp_gate, w_exp_up, w_exp_down, lyr)
    return x, ctx


def kernel(x, c, ctx, c_ctx, w_ada, b_ada, norm_g, w_in, g_qn, g_kn, attn_sink, w_gate_up, b_gate_up, g_gla, w_out,
           w_ffn_gate, w_ffn_up, w_ffn_down, w_pool, pool_scale, w_router, w_exp_gate, w_exp_up, w_exp_down):
    depth = w_ada.shape[0]
    xs = x[0]
    cs = ctx[0]
    t = xs.shape[0]
    cond = jnp.zeros((8, D_MODEL), F32).at[0].set(c[0]).at[1].set(c_ctx)
    ada = _ada_all(cond, w_ada, b_ada)
    rope_tabs = _rope_tables(t)
    for l in range(depth):
        ctx_later = any(j % 2 == 0 for j in range(l + 1, depth))
        g1 = norm_g[l, 0].reshape(1, -1)
        g2 = norm_g[l, 1].reshape(1, -1)
        if l % 2 == 0:
            e = l // 2
            xs, cs = _even_layer(xs, cs, ada[l], g1, g2, w_in[e], g_qn[e], g_kn[e], attn_sink[e], w_gate_up[e],
                                 b_gate_up[e], g_gla[e], w_out[e], w_ffn_gate[e], w_ffn_up[e], w_ffn_down[e],
                                 rope_tabs, ctx_later)
        else:
            o = l // 2
            xs, cs = _odd_layer(xs, cs, ada[l], g1, g2, w_pool[o], pool_scale[o], w_router[o],
                                w_exp_gate, w_exp_up, w_exp_down, o, ctx_later)
    return xs[None]
```

```python
import functools

import jax
import jax.numpy as jnp
import numpy as np
from jax import lax
from jax.experimental import pallas as pl
from jax.experimental.pallas import tpu as pltpu

F32 = jnp.float32
BF16 = jnp.bfloat16
HIGHEST = lax.Precision.HIGHEST

D_MODEL = 2048
GRID_W = 64
A_HEADS = 8
A_KV_HEADS = 2
A_GROUP = 4
A_HEAD_DIM = 128
A_WIDTH = A_HEADS * A_HEAD_DIM
ATTN_BLOCK = 128
ROPE_BASE = 10000.0
B_HEADS = 4
B_DV = 256
B_DK = 128
B_WIDTH = B_HEADS * B_DV
GATE_RANK = 16
GATE_TAU = 16.0
GLA_CHUNK = 64
GLA_SUB = 256
POOL_GROUPS = 4
POOL_WINDOWS = (2, 4, 8, 16)
POOL_HALO = 8
N_EXPERTS = 8
NORM_EPS = 1e-6
NEG_INF = -1e30
LANES = 128

P_COLS = 4608
VMEM_LIMIT = 56 * 1024 * 1024
MOE_VMEM_LIMIT = 62 * 1024 * 1024


def _params(sem, vmem=VMEM_LIMIT):
    return pltpu.CompilerParams(dimension_semantics=sem, vmem_limit_bytes=vmem)


def _dot(a, b, precision=None):
    return jnp.dot(a, b, preferred_element_type=F32, precision=precision)


def _dot_nt(a, b):
    return lax.dot_general(a, b, (((1,), (1,)), ((), ())), preferred_element_type=F32)


def _dot_tn(a, b):
    return lax.dot_general(a, b, (((0,), (0,)), ((), ())), preferred_element_type=F32)


def _silu(x):
    return x * jax.nn.sigmoid(x)


def _norm_mod(x, g, shift, scale):
    ms = jnp.mean(x * x, axis=-1, keepdims=True)
    return (x * lax.rsqrt(ms + NORM_EPS) * g) * (1.0 + scale) + shift


def _ada_kernel(cond_ref, w_ref, b_ref, o_ref):
    s = _silu(cond_ref[...])
    s_hi = s.astype(BF16)
    s_lo = (s - s_hi.astype(F32)).astype(BF16)
    w = w_ref[0].astype(BF16)
    o_ref[0] = _dot(s_hi, w) + _dot(s_lo, w) + b_ref[0]


def _ada_all(cond, w_ada, b_ada):
    depth, d, n = w_ada.shape
    tn = 1024
    return pl.pallas_call(
        _ada_kernel,
        grid=(depth, n // tn),
        in_specs=[pl.BlockSpec((8, d), lambda l, j: (0, 0)),
                  pl.BlockSpec((1, d, tn), lambda l, j: (l, 0, j)),
                  pl.BlockSpec((1, 1, tn), lambda l, j: (l, 0, j))],
        out_specs=pl.BlockSpec((1, 8, tn), lambda l, j: (l, 0, j)),
        out_shape=jax.ShapeDtypeStruct((depth, 8, n), F32),
        compiler_params=_params(("parallel", "parallel")),
    )(cond, w_ada, b_ada.reshape(depth, 1, n))


def _ada_spec(k):
    return pl.BlockSpec((8, D_MODEL), lambda *_: (0, k))


def _inproj_kernel(x_ref, g_ref, sh_ref, sc_ref, w_ref, wg_ref, wup_ref, bup_ref, p_ref, la_ref, hn_ref, *, row):
    @pl.when(pl.program_id(1) == 0)
    def _():
        h = _norm_mod(x_ref[...], g_ref[...], sh_ref[row:row + 1, :], sc_ref[row:row + 1, :]).astype(BF16)
        hn_ref[...] = h
        lr = _dot(h, wg_ref[...])
        z = _dot(lr, wup_ref[...], HIGHEST) + bup_ref[...]
        la_ref[...] = (jnp.minimum(z, 0.0) - jnp.log1p(jnp.exp(-jnp.abs(z)))) * (1.0 / GATE_TAU)

    p_ref[...] = _dot(hn_ref[...], w_ref[...])


def _inproj(x, ada_l, row, g, w_main, w_gate, w_up, b_up):
    t, d = x.shape
    tm = min(512, t)
    tn = 1536
    nla = 2 * B_HEADS * B_DK
    return pl.pallas_call(
        functools.partial(_inproj_kernel, row=row),
        grid=(t // tm, P_COLS // tn),
        in_specs=[pl.BlockSpec((tm, d), lambda i, j: (i, 0)),
                  pl.BlockSpec((1, d), lambda i, j: (0, 0)),
                  _ada_spec(0), _ada_spec(1),
                  pl.BlockSpec((d, tn), lambda i, j: (0, j)),
                  pl.BlockSpec((d, LANES), lambda i, j: (0, 0)),
                  pl.BlockSpec((LANES, nla), lambda i, j: (0, 0)),
                  pl.BlockSpec((1, nla), lambda i, j: (0, 0))],
        out_specs=[pl.BlockSpec((tm, tn), lambda i, j: (i, j)),
                   pl.BlockSpec((tm, nla), lambda i, j: (i, 0))],
        out_shape=[jax.ShapeDtypeStruct((t, P_COLS), F32),
                   jax.ShapeDtypeStruct((t, nla), F32)],
        scratch_shapes=[pltpu.VMEM((tm, d), BF16)],
        compiler_params=_params(("parallel", "arbitrary")),
    )(x, g, ada_l, ada_l, w_main, w_gate, w_up, b_up)


def _qkprep_kernel(q_ref, k_ref, cos_ref, sin_ref, gq_ref, gk_ref, qo_ref, ko_ref, *, rope):
    tm = q_ref.shape[0]
    lane = lax.broadcasted_iota(jnp.int32, (tm, A_HEAD_DIM), 1)
    first_half = (lane % 64) < 32

    def prep(xh, g, scale):
        ms = jnp.mean(xh * xh, axis=-1, keepdims=True)
        y = xh * lax.rsqrt(ms + NORM_EPS) * g
        if rope:
            partner = jnp.where(first_half, pltpu.roll(y, 96, 1), pltpu.roll(y, 32, 1))
            y = y * cos_ref[...] + partner * sin_ref[...]
        return (y * scale).astype(BF16)

    for h in range(A_HEADS):
        cols = slice(h * A_HEAD_DIM, (h + 1) * A_HEAD_DIM)
        qo_ref[:, cols] = prep(q_ref[:, cols], gq_ref[...], A_HEAD_DIM ** -0.5)
    for h in range(A_KV_HEADS):
        cols = slice(h * A_HEAD_DIM, (h + 1) * A_HEAD_DIM)
        ko_ref[:, cols] = prep(k_ref[:, cols], gk_ref[...], 1.0)


def _qkprep(p, cos, sin, g_qn, g_kn, rope):
    t = p.shape[0]
    tm = min(512, t)
    kw = A_KV_HEADS * A_HEAD_DIM
    return pl.pallas_call(
        functools.partial(_qkprep_kernel, rope=rope),
        grid=(t // tm,),
        in_specs=[pl.BlockSpec((tm, A_WIDTH), lambda i: (i, 0)),
                  pl.BlockSpec((tm, kw), lambda i: (i, 4096 // kw)),
                  pl.BlockSpec((tm, A_HEAD_DIM), lambda i: (i, 0)),
                  pl.BlockSpec((tm, A_HEAD_DIM), lambda i: (i, 0)),
                  pl.BlockSpec((1, A_HEAD_DIM), lambda i: (0, 0)),
                  pl.BlockSpec((1, A_HEAD_DIM), lambda i: (0, 0))],
        out_specs=[pl.BlockSpec((tm, A_WIDTH), lambda i: (i, 0)),
                   pl.BlockSpec((tm, kw), lambda i: (i, 0))],
        out_shape=[jax.ShapeDtypeStruct((t, A_WIDTH), BF16),
                   jax.ShapeDtypeStruct((t, kw), BF16)],
        compiler_params=_params(("parallel",)),
    )(p, p, cos, sin, g_qn, g_kn)


def _rope_tables(t):
    half = A_HEAD_DIM // 4
    freqs = ROPE_BASE ** (-jnp.arange(half, dtype=F32) / half)
    tok = jnp.arange(t)
    row = (tok // GRID_W).astype(F32)[:, None] * freqs
    col = (tok % GRID_W).astype(F32)[:, None] * freqs
    cos = jnp.concatenate([jnp.cos(row), jnp.cos(row), jnp.cos(col), jnp.cos(col)], axis=-1)
    sin = jnp.concatenate([-jnp.sin(row), jnp.sin(row), -jnp.sin(col), jnp.sin(col)], axis=-1)
    return cos, sin


def _softmax_pv(scores, values, sink_col):
    def lane_blocks(xs):
        return [x[:, i * LANES:(i + 1) * LANES] for x in xs for i in range(x.shape[1] // LANES)]

    m = jnp.maximum(jnp.max(functools.reduce(jnp.maximum, lane_blocks(scores)), axis=-1, keepdims=True), sink_col)
    probs = [jnp.exp(s - m) for s in scores]
    denom = jnp.exp(sink_col - m) + jnp.sum(functools.reduce(jnp.add, lane_blocks(probs)), axis=-1, keepdims=True)
    acc = None
    for p, v in zip(probs, values):
        pv = _dot(p.astype(BF16), v)
        acc = pv if acc is None else acc + pv
    return acc / denom


def _attn_kernel(*refs, local):
    if local:
        q_ref, kp_ref, ko_ref, kn_ref, vp_ref, vo_ref, vn_ref, kc_ref, vc_ref, sink_ref, o_ref = refs
    else:
        q_ref, kc_ref, vc_ref, sink_ref, o_ref = refs
    n = pl.program_id(0)
    nb = pl.num_programs(0)
    rows = A_GROUP * ATTN_BLOCK
    if local:
        qi = lax.broadcasted_iota(jnp.int32, (rows, ATTN_BLOCK), 0) % ATTN_BLOCK
        kj = lax.broadcasted_iota(jnp.int32, (rows, ATTN_BLOCK), 1)
        mask_prev = (kj >= qi) & (n > 0)
        mask_next = (kj <= qi) & (n < nb - 1)
    for kv in range(A_KV_HEADS):
        kc = slice(kv * A_HEAD_DIM, (kv + 1) * A_HEAD_DIM)
        heads = [kv * A_GROUP + g for g in range(A_GROUP)]
        q4 = jnp.concatenate([q_ref[:, h * A_HEAD_DIM:(h + 1) * A_HEAD_DIM] for h in heads], axis=0)
        sink_col = jnp.concatenate(
            [jnp.broadcast_to(sink_ref[h:h + 1, 0:1], (ATTN_BLOCK, 1)) for h in heads], axis=0)
        scores, values = [], []
        if local:
            scores.append(jnp.where(mask_prev, _dot_nt(q4, kp_ref[:, kc]), NEG_INF))
            scores.append(_dot_nt(q4, ko_ref[:, kc]))
            scores.append(jnp.where(mask_next, _dot_nt(q4, kn_ref[:, kc]), NEG_INF))
            values += [vp_ref[:, kc], vo_ref[:, kc], vn_ref[:, kc]]
        scores.append(_dot_nt(q4, kc_ref[:, kc]))
        values.append(vc_ref[:, kc])
        o = _softmax_pv(scores, values, sink_col).astype(o_ref.dtype)
        for g, h in enumerate(heads):
            o_ref[:, h * A_HEAD_DIM:(h + 1) * A_HEAD_DIM] = o[g * ATTN_BLOCK:(g + 1) * ATTN_BLOCK, :]


def _attention(q, k, v, k_ctx, v_ctx, sink_tab, local):
    t = q.shape[0]
    nb = t // ATTN_BLOCK
    l = k_ctx.shape[0]
    kw = A_KV_HEADS * A_HEAD_DIM
    blk = lambda f: pl.BlockSpec((ATTN_BLOCK, kw), f)
    prev = lambda i: (jnp.maximum(i - 1, 0), 0)
    own = lambda i: (i, 0)
    nxt = lambda i: (jnp.minimum(i + 1, nb - 1), 0)
    full = pl.BlockSpec((l, kw), lambda i: (0, 0))
    in_specs = [pl.BlockSpec((ATTN_BLOCK, A_WIDTH), own)]
    args = [q]
    if local:
        in_specs += [blk(prev), blk(own), blk(nxt), blk(prev), blk(own), blk(nxt)]
        args += [k, k, k, v, v, v]
    in_specs += [full, full, pl.BlockSpec((A_HEADS, LANES), lambda i: (0, 0))]
    args += [k_ctx, v_ctx, sink_tab]
    return pl.pallas_call(
        functools.partial(_attn_kernel, local=local),
        grid=(nb,),
        in_specs=in_specs,
        out_specs=pl.BlockSpec((ATTN_BLOCK, A_WIDTH), own),
        out_shape=jax.ShapeDtypeStruct((t, A_WIDTH), BF16),
        compiler_params=_params(("parallel",)),
    )(*args)


def _gla_kernel(q_ref, k_ref, v_ref, la_ref, s0_ref, o_ref, sf_ref, st_ref, *, rev, nchunk):
    @pl.when(pl.program_id(0) == 0)
    def _():
        st_ref[...] = s0_ref[...]

    r = nchunk * GLA_CHUNK
    sub = min(r, GLA_SUB)
    ii = lax.broadcasted_iota(jnp.int32, (sub, sub), 0)
    jj = lax.broadcasted_iota(jnp.int32, (sub, sub), 1)
    same_chunk = (ii // GLA_CHUNK) == (jj // GLA_CHUNK)
    tri = same_chunk & ((jj >= ii) if rev else (jj <= ii))
    b_all = la_ref[...]
    pos = lax.broadcasted_iota(jnp.int32, b_all.shape, 0) % GLA_CHUNK
    step = 1
    while step < GLA_CHUNK:
        if rev:
            b_all = b_all + jnp.where(pos < GLA_CHUNK - step, pltpu.roll(b_all, r - step, 0), 0.0)
        else:
            b_all = b_all + jnp.where(pos >= step, pltpu.roll(b_all, step, 0), 0.0)
        step *= 2
    heads = []
    for h in range(B_HEADS):
        kc = slice(h * B_DK, (h + 1) * B_DK)
        b = b_all[:, kc]
        b3 = b.reshape(nchunk, GLA_CHUNK, B_DK)
        b_end = b3[:, 0:1, :] if rev else b3[:, GLA_CHUNK - 1:GLA_CHUNK, :]
        k = k_ref[:, kc]
        qe = (q_ref[:, kc] * (B_DK ** -0.5) * jnp.exp(b)).astype(BF16)
        ke = (k * jnp.exp(-b)).astype(BF16)
        kd = (k.reshape(nchunk, GLA_CHUNK, B_DK) * jnp.exp(b_end - b3)).astype(BF16)
        v = v_ref[:, h * B_DV:(h + 1) * B_DV].astype(BF16)
        o_intra = []
        for s0 in range(0, r, sub):
            rs = slice(s0, s0 + sub)
            a = jnp.where(tri, _dot_nt(qe[rs, :], ke[rs, :]), 0.0).astype(BF16)
            o_intra.append(_dot(a, v[rs, :]))
        heads.append((qe, kd, v, jnp.concatenate(o_intra, axis=0), jnp.exp(b_end)))
    st = [st_ref[h] for h in range(B_HEADS)]
    for c in (range(nchunk - 1, -1, -1) if rev else range(nchunk)):
        rows = slice(c * GLA_CHUNK, (c + 1) * GLA_CHUNK)
        for h, (qe, kd, v, o_intra, decay) in enumerate(heads):
            o_ref[0, rows, h * B_DV:(h + 1) * B_DV] = o_intra[rows, :] + _dot_nt(qe[rows, :], st[h].astype(BF16))
            st[h] = st[h] * decay[c] + _dot_tn(v[rows, :], kd[c])
    for h in range(B_HEADS):
        st_ref[h] = st[h]
        sf_ref[h] = st[h]


def _gla_scan(p, la, s0, rev):
    t = p.shape[0]
    r = min(512, t)
    nblk = t // r
    rb = (lambda c: nblk - 1 - c) if rev else (lambda c: c)
    nk = B_HEADS * B_DK
    o, sf = pl.pallas_call(
        functools.partial(_gla_kernel, rev=rev, nchunk=r // GLA_CHUNK),
        grid=(nblk,),
        in_specs=[pl.BlockSpec((r, nk), lambda c: (rb(c), 3584 // nk)),
                  pl.BlockSpec((r, nk), lambda c: (rb(c), 3072 // nk)),
                  pl.BlockSpec((r, B_WIDTH), lambda c: (rb(c), 1024 // B_WIDTH)),
                  pl.BlockSpec((r, nk), lambda c: (rb(c), 1 if rev else 0)),
                  pl.BlockSpec((B_HEADS, B_DV, B_DK), lambda c: (0, 0, 0))],
        out_specs=[pl.BlockSpec((1, r, B_WIDTH), lambda c: (0, rb(c), 0)),
                   pl.BlockSpec((B_HEADS, B_DV, B_DK), lambda c: (0, 0, 0))],
        out_shape=[jax.ShapeDtypeStruct((1, t, B_WIDTH), F32),
                   jax.ShapeDtypeStruct((B_HEADS, B_DV, B_DK), F32)],
        scratch_shapes=[pltpu.VMEM((B_HEADS, B_DV, B_DK), F32)],
        compiler_params=_params(("arbitrary",)),
    )(p, p, p, la, s0)
    return o[0], sf


def _outproj_kernel(oa_ref, of_ref, ob_ref, og_ref, gg_ref, w_ref, x_ref, gt_ref, o_ref, mix_ref, *, row):
    @pl.when(pl.program_id(1) == 0)
    def _():
        mix_ref[:, 0:A_WIDTH] = oa_ref[...]
        for h in range(B_HEADS):
            cols = slice(h * B_DV, (h + 1) * B_DV)
            o = of_ref[:, cols] + ob_ref[:, cols]
            ms = jnp.mean(o * o, axis=-1, keepdims=True)
            y = o * lax.rsqrt(ms + NORM_EPS) * gg_ref[...]
            mix_ref[:, A_WIDTH + h * B_DV:A_WIDTH + (h + 1) * B_DV] = (y * _silu(og_ref[:, cols])).astype(BF16)

    o_ref[...] = x_ref[...] + gt_ref[row:row + 1, :] * _dot(mix_ref[...], w_ref[...])


def _outproj(oa, o_f, o_b, p, g_gla, w_out, x, ada_l, row):
    t, d = x.shape
    tm = min(512, t)
    tn = 1024
    return pl.pallas_call(
        functools.partial(_outproj_kernel, row=row),
        grid=(t // tm, d // tn),
        in_specs=[pl.BlockSpec((tm, A_WIDTH), lambda i, j: (i, 0)),
                  pl.BlockSpec((tm, B_WIDTH), lambda i, j: (i, 0)),
                  pl.BlockSpec((tm, B_WIDTH), lambda i, j: (i, 0)),
                  pl.BlockSpec((tm, B_WIDTH), lambda i, j: (i, 2048 // B_WIDTH)),
                  pl.BlockSpec((1, B_DV), lambda i, j: (0, 0)),
                  pl.BlockSpec((A_WIDTH + B_WIDTH, tn), lambda i, j: (0, j)),
                  pl.BlockSpec((tm, tn), lambda i, j: (i, j)),
                  pl.BlockSpec((8, tn), lambda i, j: (0, 2 * (d // tn) + j))],
        out_specs=pl.BlockSpec((tm, tn), lambda i, j: (i, j)),
        out_shape=jax.ShapeDtypeStruct((t, d), F32),
        scratch_shapes=[pltpu.VMEM((tm, A_WIDTH + B_WIDTH), BF16)],
        compiler_params=_params(("parallel", "arbitrary")),
    )(oa, o_f, o_b, p, g_gla, w_out, x, ada_l)


def _ffn_kernel(x_ref, g_ref, sh_ref, sc_ref, gt_ref, wg_ref, wu_ref, wd_ref, o_ref, hn_ref, acc_ref, *, row):
    f = pl.program_id(1)

    @pl.when(f == 0)
    def _():
        hn_ref[...] = _norm_mod(x_ref[...], g_ref[...], sh_ref[row:row + 1, :], sc_ref[row:row + 1, :]).astype(BF16)
        acc_ref[...] = jnp.zeros_like(acc_ref)

    h = hn_ref[...]
    a = (_silu(_dot(h, wg_ref[...])) * _dot(h, wu_ref[...])).astype(BF16)
    acc_ref[...] += _dot(a, wd_ref[...])

    @pl.when(f == pl.num_programs(1) - 1)
    def _():
        o_ref[...] = x_ref[...] + gt_ref[row:row + 1, :] * acc_ref[...]


def _ffn(x, ada_l, row, g, wg, wu, wd):
    t, d = x.shape
    ff = wg.shape[1]
    tm = min(512, t)
    tf = 512
    return pl.pallas_call(
        functools.partial(_ffn_kernel, row=row),
        grid=(t // tm, ff // tf),
        in_specs=[pl.BlockSpec((tm, d), lambda i, f: (i, 0)),
                  pl.BlockSpec((1, d), lambda i, f: (0, 0)),
                  _ada_spec(3), _ada_spec(4), _ada_spec(5),
                  pl.BlockSpec((d, tf), lambda i, f: (0, f)),
                  pl.BlockSpec((d, tf), lambda i, f: (0, f)),
                  pl.BlockSpec((tf, d), lambda i, f: (f, 0))],
        out_specs=pl.BlockSpec((tm, d), lambda i, f: (i, 0)),
        out_shape=jax.ShapeDtypeStruct((t, d), F32),
        scratch_shapes=[pltpu.VMEM((tm, d), BF16), pltpu.VMEM((tm, d), F32)],
        compiler_params=_params(("parallel", "arbitrary")),
    )(x, g, ada_l, ada_l, ada_l, wg, wu, wd)


def _pool_kernel(x_ref, xp_ref, xn_ref, g_ref, sh_ref, sc_ref, gt_ref, w_ref, ps_ref, o_ref, hb_ref, *, row, t):
    i = pl.program_id(0)
    tm = x_ref.shape[0]
    g = g_ref[...]
    sh = sh_ref[row:row + 1, :]
    sc = sc_ref[row:row + 1, :]
    hb_ref[0:POOL_HALO, :] = jnp.where(i > 0, _norm_mod(xp_ref[...], g, sh, sc), 0.0)
    hb_ref[POOL_HALO:POOL_HALO + tm, :] = _norm_mod(x_ref[...], g, sh, sc)
    hb_ref[POOL_HALO + tm:, :] = jnp.where(i < pl.num_programs(0) - 1, _norm_mod(xn_ref[...], g, sh, sc), 0.0)
    tpos = i * tm + lax.broadcasted_iota(jnp.int32, (tm, 1), 0)
    gc = D_MODEL // POOL_GROUPS
    for grp, w in enumerate(POOL_WINDOWS):
        cols = slice(grp * gc, (grp + 1) * gc)
        acc = None
        for dlt in range(-(w // 2), w - w // 2):
            piece = hb_ref[POOL_HALO + dlt:POOL_HALO + dlt + tm, cols]
            acc = piece if acc is None else acc + piece
        cnt = jnp.minimum(tpos + (w - w // 2), t) - jnp.maximum(tpos - w // 2, 0)
        pooled = acc / cnt.astype(F32) - hb_ref[POOL_HALO:POOL_HALO + tm, cols]
        y = _dot(pooled.astype(BF16), w_ref[grp])
        o_ref[:, cols] = x_ref[:, cols] + gt_ref[row:row + 1, cols] * (y * ps_ref[:, cols])


def _pool(x, ada_l, row, g, w_pool, pool_scale):
    t, d = x.shape
    tm = min(256, t)
    nh = t // POOL_HALO
    gc = d // POOL_GROUPS
    return pl.pallas_call(
        functools.partial(_pool_kernel, row=row, t=t),
        grid=(t // tm,),
        in_specs=[pl.BlockSpec((tm, d), lambda i: (i, 0)),
                  pl.BlockSpec((POOL_HALO, d), lambda i: (jnp.maximum(i * (tm // POOL_HALO) - 1, 0), 0)),
                  pl.BlockSpec((POOL_HALO, d), lambda i: (jnp.minimum((i + 1) * (tm // POOL_HALO), nh - 1), 0)),
                  pl.BlockSpec((1, d), lambda i: (0, 0)),
                  _ada_spec(0), _ada_spec(1), _ada_spec(2),
                  pl.BlockSpec((POOL_GROUPS, gc, gc), lambda i: (0, 0, 0)),
                  pl.BlockSpec((1, d), lambda i: (0, 0))],
        out_specs=pl.BlockSpec((tm, d), lambda i: (i, 0)),
        out_shape=jax.ShapeDtypeStruct((t, d), F32),
        scratch_shapes=[pltpu.VMEM((tm + 2 * POOL_HALO, d), F32)],
        compiler_params=_params(("parallel",)),
    )(x, x, x, g, ada_l, ada_l, ada_l, w_pool, pool_scale)


def _router_kernel(x_ref, g_ref, sh_ref, sc_ref, wr_ref, hn_ref, sel_ref, gate_ref, *, row):
    h = _norm_mod(x_ref[...], g_ref[...], sh_ref[row:row + 1, :], sc_ref[row:row + 1, :])
    hn_ref[...] = h.astype(hn_ref.dtype)
    lane = lax.broadcasted_iota(jnp.int32, (h.shape[0], LANES), 1)
    lane_f = lane.astype(F32)
    logits = jnp.where(lane < N_EXPERTS, _dot(h, wr_ref[...], HIGHEST), -jnp.inf)
    m1 = jnp.max(logits, axis=-1, keepdims=True)
    i1 = jnp.min(jnp.where(logits == m1, lane_f, float(LANES)), axis=-1, keepdims=True)
    rest = jnp.where(lane_f == i1, -jnp.inf, logits)
    m2 = jnp.max(rest, axis=-1, keepdims=True)
    i2 = jnp.min(jnp.where(rest == m2, lane_f, float(LANES)), axis=-1, keepdims=True)
    e2 = jnp.exp(m2 - m1)
    den = 1.0 + e2
    sel_ref[...] = jnp.where(lane == 0, i1, jnp.where(lane == 1, i2, 0.0)).astype(jnp.int32)
    gate_ref[...] = jnp.where(lane == 0, 1.0 / den, jnp.where(lane == 1, e2 / den, 0.0))


def _router(x, ada_l, row, g, w_router_pad):
    t, d = x.shape
    tm = min(256, t)
    return pl.pallas_call(
        functools.partial(_router_kernel, row=row),
        grid=(t // tm,),
        in_specs=[pl.BlockSpec((tm, d), lambda i: (i, 0)),
                  pl.BlockSpec((1, d), lambda i: (0, 0)),
                  _ada_spec(3), _ada_spec(4),
                  pl.BlockSpec((d, LANES), lambda i: (0, 0))],
        out_specs=[pl.BlockSpec((tm, d), lambda i: (i, 0)),
                   pl.BlockSpec((tm, LANES), lambda i: (i, 0)),
                   pl.BlockSpec((tm, LANES), lambda i: (i, 0))],
        out_shape=[jax.ShapeDtypeStruct((t, d), BF16),
                   jax.ShapeDtypeStruct((t, LANES), jnp.int32),
                   jax.ShapeDtypeStruct((t, LANES), F32)],
        compiler_params=_params(("parallel",)),
    )(x, g, ada_l, ada_l, w_router_pad)


MOE_TILE = 1280
MOE_SUB = 256
MOE_SUBS = MOE_TILE // MOE_SUB
SRC_BLK = 256
Y_BLK = 256
CMB_TILE = 256
MOE_TF = 512
GATHER_BUFS = 3


def _moe_ffn_kernel(te_ref, tn_ref, blo_ref, bn_ref, src_ref, hn_hbm, wg_ref, wu_ref, wd_ref, y_ref,
                    xb_ref, acc_ref, hbuf_ref, sem):
    i = pl.program_id(0)
    f = pl.program_id(1)
    nsub = tn_ref[i]

    @pl.when(f == 0)
    def _():
        xb_ref[...] = jnp.zeros_like(xb_ref)
        acc_ref[...] = jnp.zeros_like(acc_ref)
        col = lax.broadcasted_iota(jnp.int32, (MOE_SUB, SRC_BLK), 1)
        first = i * MOE_SUBS
        last = first + jnp.maximum(nsub, 1) - 1
        lo = blo_ref[first]
        n = jnp.where(nsub > 0, blo_ref[last] + bn_ref[last] - lo, 0)

        def blk_copy(b, slot):
            return pltpu.make_async_copy(hn_hbm.at[pl.ds((lo + b) * SRC_BLK, SRC_BLK), :], hbuf_ref.at[slot],
                                         sem.at[slot])

        for j in range(GATHER_BUFS - 1):
            @pl.when(j < n)
            def _():
                blk_copy(j, j).start()

        def body(b, carry):
            slot = b % GATHER_BUFS
            blk_copy(b, slot).wait()
            ahead = b + GATHER_BUFS - 1

            @pl.when(ahead < n)
            def _():
                blk_copy(ahead, ahead % GATHER_BUFS).start()

            blk = lo + b
            for q in range(MOE_SUBS):
                q_lo = blo_ref[first + q]

                @pl.when((q < nsub) & (q_lo <= blk) & (blk < q_lo + bn_ref[first + q]))
                def _():
                    rows = slice(q * MOE_SUB, (q + 1) * MOE_SUB)
                    onehot = jnp.where(src_ref[rows, :] - blk * SRC_BLK == col, 1.0, 0.0).astype(BF16)
                    xb_ref[rows, :] += _dot(onehot, hbuf_ref[slot]).astype(BF16)
            return carry

        lax.fori_loop(0, n, body, 0)

    for k in range(1, MOE_SUBS + 1):
        @pl.when(nsub == k)
        def _():
            rows = slice(0, k * MOE_SUB)
            h = xb_ref[rows, :]
            a = (_silu(_dot(h, wg_ref[0, 0].astype(BF16))) * _dot(h, wu_ref[0, 0].astype(BF16))).astype(BF16)
            acc_ref[rows, :] += _dot(a, wd_ref[0, 0].astype(BF16))

    @pl.when(f == pl.num_programs(1) - 1)
    def _():
        y_ref[...] = acc_ref[...].astype(y_ref.dtype)


def _moe_ffn(hn, src, tile_expert, tile_subs, blk_lo, blk_n, wg, wu, wd, lyr):
    d = hn.shape[1]
    tm = MOE_TILE
    ff = wg.shape[3]
    tf = MOE_TF
    nf = ff // tf
    nt = src.shape[0] // tm
    fidx = lambda i, f, tn: jnp.where(tn[i] > 0, f, nf - 1)
    return pl.pallas_call(
        _moe_ffn_kernel,
        grid_spec=pltpu.PrefetchScalarGridSpec(
            num_scalar_prefetch=4,
            grid=(nt, nf),
            in_specs=[pl.BlockSpec((tm, 1), lambda i, f, te, tn, bl, bn: (i, 0)),
                      pl.BlockSpec(memory_space=pl.ANY),
                      pl.BlockSpec((1, 1, d, tf), lambda i, f, te, tn, bl, bn: (lyr, te[i], 0, fidx(i, f, tn))),
                      pl.BlockSpec((1, 1, d, tf), lambda i, f, te, tn, bl, bn: (lyr, te[i], 0, fidx(i, f, tn))),
                      pl.BlockSpec((1, 1, tf, d), lambda i, f, te, tn, bl, bn: (lyr, te[i], fidx(i, f, tn), 0))],
            out_specs=pl.BlockSpec((tm, d), lambda i, f, te, tn, bl, bn: (i, 0)),
            scratch_shapes=[pltpu.VMEM((tm, d), BF16), pltpu.VMEM((tm, d), F32),
                            pltpu.VMEM((GATHER_BUFS, SRC_BLK, d), BF16),
                            pltpu.SemaphoreType.DMA((GATHER_BUFS,))]),
        out_shape=jax.ShapeDtypeStruct((nt * tm, d), BF16),
        compiler_params=_params(("arbitrary", "arbitrary"), MOE_VMEM_LIMIT),
    )(tile_expert, tile_subs, blk_lo, blk_n, src, hn, wg, wu, wd)


CMB_SLOTS = [(e, j) for e in range(N_EXPERTS) for j in range(2)]


def _combine_kernel(fb_ref, nb_ref, x_ref, pos_ref, gate_ref, gt_ref, y_hbm, o_ref, acc_ref, ybuf_ref, sem,
                    *, row, tile_off):
    base = (tile_off + pl.program_id(0)) * N_EXPERTS
    tm = x_ref.shape[0]
    pos1, pos2 = pos_ref[:, 0:1], pos_ref[:, 1:2]
    g1, g2 = gate_ref[:, 0:1], gate_ref[:, 1:2]
    col = lax.broadcasted_iota(jnp.int32, (tm, Y_BLK), 1)

    def blk(s):
        e, j = CMB_SLOTS[s]
        return fb_ref[base + e] + j

    def used(s):
        e, j = CMB_SLOTS[s]
        return nb_ref[base + e] > j

    def blk_copy(s):
        return pltpu.make_async_copy(y_hbm.at[pl.ds(blk(s) * Y_BLK, Y_BLK), :], ybuf_ref.at[s], sem.at[s])

    for s in range(len(CMB_SLOTS)):
        @pl.when(used(s))
        def _():
            blk_copy(s).start()

    acc_ref[...] = jnp.zeros_like(acc_ref)
    for s in range(len(CMB_SLOTS)):
        @pl.when(used(s))
        def _():
            blk_copy(s).wait()
            off = blk(s) * Y_BLK
            w = jnp.where(pos1 - off == col, g1, 0.0) + jnp.where(pos2 - off == col, g2, 0.0)
            acc_ref[...] += _dot(w.astype(BF16), ybuf_ref[s])

    o_ref[...] = x_ref[...] + gt_ref[row:row + 1, :] * acc_ref[...]


def _combine(x, y, pos, gates, first_blk, num_blk, tok_off, ada_l, row):
    t, d = x.shape
    tm = CMB_TILE
    tile_off = tok_off // tm
    return pl.pallas_call(
        functools.partial(_combine_kernel, row=row, tile_off=tile_off),
        grid_spec=pltpu.PrefetchScalarGridSpec(
            num_scalar_prefetch=2,
            grid=(t // tm,),
            in_specs=[pl.BlockSpec((tm, d), lambda i, fb, nb: (i, 0)),
                      pl.BlockSpec((tm, 2), lambda i, fb, nb: (tile_off + i, 0)),
                      pl.BlockSpec((tm, LANES), lambda i, fb, nb: (i, 0)),
                      _ada_spec(5),
                      pl.BlockSpec(memory_space=pl.ANY)],
            out_specs=pl.BlockSpec((tm, d), lambda i, fb, nb: (i, 0)),
            scratch_shapes=[pltpu.VMEM((tm, d), F32), pltpu.VMEM((len(CMB_SLOTS), Y_BLK, d), BF16),
                            pltpu.SemaphoreType.DMA((len(CMB_SLOTS),))]),
        out_shape=jax.ShapeDtypeStruct((t, d), F32),
        compiler_params=_params(("arbitrary",)),
    )(first_blk, num_blk, x, pos, gates, ada_l, y)


def _route(sel):
    tm = MOE_TILE
    n_tok = sel.shape[0]
    n_asg = 2 * n_tok
    nt = -(-n_asg // tm) + N_EXPERTS
    e_flat = sel.reshape(-1)
    onehot = (e_flat[:, None] == jnp.arange(N_EXPERTS, dtype=jnp.int32)[None, :]).astype(jnp.int32)
    csum = jnp.cumsum(onehot, axis=0)
    count = csum[-1]
    rank = jnp.sum((csum - onehot) * onehot, axis=1)
    tiles_e = (count + tm - 1) // tm
    tile_end = jnp.cumsum(tiles_e)
    base = (tile_end - tiles_e) * tm
    pos = jnp.sum(onehot * base[None, :], axis=1) + rank
    src = jnp.full((nt * tm,), -1, jnp.int32).at[pos].set(jnp.arange(n_asg, dtype=jnp.int32) // 2)
    tile_id = jnp.arange(nt, dtype=jnp.int32)
    used = tile_end[-1]
    owner = lambda i: jnp.minimum(jnp.sum((i[:, None] >= tile_end[None, :]).astype(jnp.int32), axis=1), N_EXPERTS - 1)
    tile_expert = owner(jnp.minimum(tile_id, used - 1))
    rows_in_tile = jnp.clip((base + count)[tile_expert] - tile_id * tm, 0, tm)
    tile_subs = jnp.where(tile_id < used, (rows_in_tile + MOE_SUB - 1) // MOE_SUB, 0)
    srcm = src.reshape(nt * MOE_SUBS, MOE_SUB)
    tmax = jnp.max(srcm, axis=1)
    tmin = jnp.min(jnp.where(srcm >= 0, srcm, n_tok), axis=1)
    blk_lo = jnp.where(tmax >= 0, tmin // SRC_BLK, 0)
    blk_n = jnp.where(tmax >= 0, tmax // SRC_BLK - blk_lo + 1, 0)
    ntt = n_tok // CMB_TILE
    through = csum.reshape(ntt, 2 * CMB_TILE, N_EXPERTS)[:, -1, :]
    before = jnp.concatenate([jnp.zeros((1, N_EXPERTS), jnp.int32), through[:-1]], axis=0)
    lo = base[None, :] + before
    hi = base[None, :] + through - 1
    first_blk = lo // Y_BLK
    num_blk = jnp.where(through > before, hi // Y_BLK - first_blk + 1, 0)
    return (src.reshape(nt * tm, 1), pos.reshape(n_tok, 2), tile_expert, tile_subs, blk_lo, blk_n,
            first_blk.reshape(-1), num_blk.reshape(-1))


def _moe(streams, ada_l, g, w_router_pad, wg, wu, wd, lyr):
    hs, sels, gates = [], [], []
    for x, row in streams:
        h, s, gt = _router(x, ada_l, row, g, w_router_pad)
        hs.append(h), sels.append(s[:, :2]), gates.append(gt)
    n_tok = sum(h.shape[0] for h in hs)
    pad = -n_tok % SRC_BLK
    if pad:
        hs.append(jnp.zeros((pad, D_MODEL), BF16))
    h_all = hs[0] if len(hs) == 1 else jnp.concatenate(hs, axis=0)
    sel_all = sels[0] if len(sels) == 1 else jnp.concatenate(sels, axis=0)
    src, pos, tile_expert, tile_subs, blk_lo, blk_n, first_blk, num_blk = _route(sel_all)
    y = _moe_ffn(h_all, src, tile_expert, tile_subs, blk_lo, blk_n, wg, wu, wd, lyr)
    outs, off = [], 0
    for (x, row), gt in zip(streams, gates):
        outs.append(_combine(x, y, pos, gt, first_blk, num_blk, off, ada_l, row))
        off += x.shape[0]
    return outs


def _even_layer(x, ctx, ada_l, g1, g2, w_in, g_qn, g_kn, sink, w_gate_up, b_gate_up, g_gla, w_out,
                w_ffn_gate, w_ffn_up, w_ffn_down, rope_tabs, ctx_out):
    d = D_MODEL
    w_main = jnp.concatenate([w_in[:, 2080:3104], w_in[:, 1024:2048], w_in[:, 3616:4640], w_in[:, 512:1024],
                              w_in[:, 3104:3616], w_in[:, 0:256], w_in[:, 256:512]], axis=1).astype(BF16)
    w_gate = jnp.pad(w_in[:, 2048:2080], ((0, 0), (0, LANES - 2 * GATE_RANK))).astype(BF16)
    nk = B_HEADS * B_DK
    w_up = jnp.zeros((LANES, 2 * nk), F32)
    w_up = w_up.at[0:GATE_RANK, 0:nk].set(w_gate_up[0]).at[GATE_RANK:2 * GATE_RANK, nk:].set(w_gate_up[1])
    b_up = b_gate_up.reshape(1, 2 * nk)
    w_out_b = w_out.astype(BF16)
    wg_b, wu_b, wd_b = w_ffn_gate.astype(BF16), w_ffn_up.astype(BF16), w_ffn_down.astype(BF16)
    gq, gk, gg = g_qn.reshape(1, -1), g_kn.reshape(1, -1), g_gla.reshape(1, -1)
    sink_tab = jnp.broadcast_to(sink[:, None], (A_HEADS, LANES))
    cos, sin = rope_tabs

    pc, la_c = _inproj(ctx, ada_l, 1, g1, w_main, w_gate, w_up, b_up)
    qc, kc = _qkprep(pc, cos, sin, gq, gk, rope=False)
    vc = pc[:, 4352:4608].astype(BF16)
    s0 = jnp.zeros((B_HEADS, B_DV, B_DK), F32)
    oc_f, s_fwd = _gla_scan(pc, la_c, s0, rev=False)
    oc_b, s_bwd = _gla_scan(pc, la_c, s0, rev=True)

    px, la_x = _inproj(x, ada_l, 0, g1, w_main, w_gate, w_up, b_up)
    qx, kx = _qkprep(px, cos, sin, gq, gk, rope=True)
    vx = px[:, 4352:4608].astype(BF16)
    oa = _attention(qx, kx, vx, kc, vc, sink_tab, local=True)
    ox_f, _ = _gla_scan(px, la_x, s_fwd, rev=False)
    ox_b, _ = _gla_scan(px, la_x, s_bwd, rev=True)
    x = _outproj(oa, ox_f, ox_b, px, gg, w_out_b, x, ada_l, 0)
    x = _ffn(x, ada_l, 0, g2, wg_b, wu_b, wd_b)
    if ctx_out:
        oa_c = _attention(qc, None, None, kc, vc, sink_tab, local=False)
        ctx = _outproj(oa_c, oc_f, oc_b, pc, gg, w_out_b, ctx, ada_l, 1)
        ctx = _ffn(ctx, ada_l, 1, g2, wg_b, wu_b, wd_b)
    return x, ctx


def _odd_layer(x, ctx, ada_l, g1, g2, w_pool, pool_scale, w_router, w_exp_gate, w_exp_up, w_exp_down, lyr, ctx_out):
    w_pool_b = w_pool.astype(BF16)
    ps = pool_scale.reshape(1, -1)
    wr = jnp.pad(w_router, ((0, 0), (0, LANES - N_EXPERTS)))
    x = _pool(x, ada_l, 0, g1, w_pool_b, ps)
    if ctx_out:
        ctx = _pool(ctx, ada_l, 1, g1, w_pool_b, ps)
        x, ctx = _moe([(x, 0), (ctx, 1)], ada_l, g2, wr, w_exp_gate, w_exp_up, w_exp_down, lyr)
    else:
        (x,) = _moe([(x, 0)], ada_l, g2, wr, w_exp_gate, w_exp_up, w_exp_down, lyr)
    return x, ctx


def kernel(x, c, ctx, c_ctx, w_ada, b_ada, norm_g, w_in, g_qn, g_kn, attn_sink, w_gate_up, b_gate_up, g_gla, w_out,
           w_ffn_gate, w_ffn_up, w_ffn_down, w_pool, pool_scale, w_router, w_exp_gate, w_exp_up, w_exp_down):
    depth = w_ada.shape[0]
    xs = x[0]
    cs = ctx[0]
    t = xs.shape[0]
    cond = jnp.zeros((8, D_MODEL), F32).at[0].set(c[0]).at[1].set(c_ctx)
    ada = _ada_all(cond, w_ada, b_ada)
    rope_tabs = _rope_tables(t)
    for l in range(depth):
        ctx_later = any(j % 2 == 0 for j in range(l + 1, depth))
        g1 = norm_g[l, 0].reshape(1, -1)
        g2 = norm_g[l, 1].reshape(1, -1)
        if l % 2 == 0:
            e = l // 2
            xs, cs = _even_layer(xs, cs, ada[l], g1, g2, w_in[e], g_qn[e], g_kn[e], attn_sink[e], w_gate_up[e],
                                 b_gate_up[e], g_gla[e], w_out[e], w_ffn_gate[e], w_ffn_up[e], w_ffn_down[e],
                                 rope_tabs, ctx_later)
        else:
            o = l // 2
            xs, cs = _odd_layer(xs, cs, ada[l], g1, g2, w_pool[o], pool_scale[o], w_router[o],
                                w_exp_gate, w_exp_up, w_exp_down, o, ctx_later)
    return xs[None]
```

```python
import functools

import jax
import jax.numpy as jnp
import numpy as np
from jax import lax
from jax.experimental import pallas as pl
from jax.experimental.pallas import tpu as pltpu

F32 = jnp.float32
BF16 = jnp.bfloat16
HIGHEST = lax.Precision.HIGHEST

D_MODEL = 2048
GRID_W = 64
A_HEADS = 8
A_KV_HEADS = 2
A_GROUP = 4
A_HEAD_DIM = 128
A_WIDTH = A_HEADS * A_HEAD_DIM
ATTN_BLOCK = 128
ROPE_BASE = 10000.0
B_HEADS = 4
B_DV = 256
B_DK = 128
B_WIDTH = B_HEADS * B_DV
GATE_RANK = 16
GATE_TAU = 16.0
GLA_CHUNK = 64
GLA_SUB = 256
POOL_GROUPS = 4
POOL_WINDOWS = (2, 4, 8, 16)
POOL_HALO = 8
N_EXPERTS = 8
NORM_EPS = 1e-6
NEG_INF = -1e30
LANES = 128

P_COLS = 4608
VMEM_LIMIT = 56 * 1024 * 1024
MOE_VMEM_LIMIT = 62 * 1024 * 1024


def _params(sem, vmem=VMEM_LIMIT):
    return pltpu.CompilerParams(dimension_semantics=sem, vmem_limit_bytes=vmem)


def _dot(a, b, precision=None):
    return jnp.dot(a, b, preferred_element_type=F32, precision=precision)


def _dot_nt(a, b):
    return lax.dot_general(a, b, (((1,), (1,)), ((), ())), preferred_element_type=F32)


def _dot_tn(a, b):
    return lax.dot_general(a, b, (((0,), (0,)), ((), ())), preferred_element_type=F32)


def _silu(x):
    return x * jax.nn.sigmoid(x)


def _norm_mod(x, g, shift, scale):
    ms = jnp.mean(x * x, axis=-1, keepdims=True)
    return (x * lax.rsqrt(ms + NORM_EPS) * g) * (1.0 + scale) + shift


def _ada_kernel(cond_ref, w_ref, b_ref, o_ref):
    s = _silu(cond_ref[...])
    s_hi = s.astype(BF16)
    s_lo = (s - s_hi.astype(F32)).astype(BF16)
    w = w_ref[0].astype(BF16)
    o_ref[0] = _dot(s_hi, w) + _dot(s_lo, w) + b_ref[0]


def _ada_all(cond, w_ada, b_ada):
    depth, d, n = w_ada.shape
    tn = 1024
    return pl.pallas_call(
        _ada_kernel,
        grid=(depth, n // tn),
        in_specs=[pl.BlockSpec((8, d), lambda l, j: (0, 0)),
                  pl.BlockSpec((1, d, tn), lambda l, j: (l, 0, j)),
                  pl.BlockSpec((1, 1, tn), lambda l, j: (l, 0, j))],
        out_specs=pl.BlockSpec((1, 8, tn), lambda l, j: (l, 0, j)),
        out_shape=jax.ShapeDtypeStruct((depth, 8, n), F32),
        compiler_params=_params(("parallel", "parallel")),
    )(cond, w_ada, b_ada.reshape(depth, 1, n))


def _ada_spec(k):
    return pl.BlockSpec((8, D_MODEL), lambda *_: (0, k))


def _inproj_kernel(x_ref, g_ref, sh_ref, sc_ref, w_ref, wg_ref, wup_ref, bup_ref, p_ref, la_ref, hn_ref, *, row):
    @pl.when(pl.program_id(1) == 0)
    def _():
        h = _norm_mod(x_ref[...], g_ref[...], sh_ref[row:row + 1, :], sc_ref[row:row + 1, :]).astype(BF16)
        hn_ref[...] = h
        lr = _dot(h, wg_ref[...])
        z = _dot(lr, wup_ref[...], HIGHEST) + bup_ref[...]
        la_ref[...] = (jnp.minimum(z, 0.0) - jnp.log1p(jnp.exp(-jnp.abs(z)))) * (1.0 / GATE_TAU)

    p_ref[...] = _dot(hn_ref[...], w_ref[...]).astype(p_ref.dtype)


def _inproj(x, ada_l, row, g, w_main, w_gate, w_up, b_up):
    t, d = x.shape
    tm = min(512, t)
    tn = 1536
    nla = 2 * B_HEADS * B_DK
    return pl.pallas_call(
        functools.partial(_inproj_kernel, row=row),
        grid=(t // tm, P_COLS // tn),
        in_specs=[pl.BlockSpec((tm, d), lambda i, j: (i, 0)),
                  pl.BlockSpec((1, d), lambda i, j: (0, 0)),
                  _ada_spec(0), _ada_spec(1),
                  pl.BlockSpec((d, tn), lambda i, j: (0, j)),
                  pl.BlockSpec((d, LANES), lambda i, j: (0, 0)),
                  pl.BlockSpec((LANES, nla), lambda i, j: (0, 0)),
                  pl.BlockSpec((1, nla), lambda i, j: (0, 0))],
        out_specs=[pl.BlockSpec((tm, tn), lambda i, j: (i, j)),
                   pl.BlockSpec((tm, nla), lambda i, j: (i, 0))],
        out_shape=[jax.ShapeDtypeStruct((t, P_COLS), BF16),
                   jax.ShapeDtypeStruct((t, nla), F32)],
        scratch_shapes=[pltpu.VMEM((tm, d), BF16)],
        compiler_params=_params(("parallel", "arbitrary")),
    )(x, g, ada_l, ada_l, w_main, w_gate, w_up, b_up)


def _qkprep_kernel(q_ref, k_ref, cos_ref, sin_ref, gq_ref, gk_ref, qo_ref, ko_ref, *, rope):
    tm = q_ref.shape[0]
    lane = lax.broadcasted_iota(jnp.int32, (tm, A_HEAD_DIM), 1)
    first_half = (lane % 64) < 32

    def prep(xh, g, scale):
        ms = jnp.mean(xh * xh, axis=-1, keepdims=True)
        y = xh * lax.rsqrt(ms + NORM_EPS) * g
        if rope:
            partner = jnp.where(first_half, pltpu.roll(y, 96, 1), pltpu.roll(y, 32, 1))
            y = y * cos_ref[...] + partner * sin_ref[...]
        return (y * scale).astype(BF16)

    for h in range(A_HEADS):
        cols = slice(h * A_HEAD_DIM, (h + 1) * A_HEAD_DIM)
        qo_ref[:, cols] = prep(q_ref[:, cols].astype(F32), gq_ref[...], A_HEAD_DIM ** -0.5)
    for h in range(A_KV_HEADS):
        cols = slice(h * A_HEAD_DIM, (h + 1) * A_HEAD_DIM)
        ko_ref[:, cols] = prep(k_ref[:, cols].astype(F32), gk_ref[...], 1.0)


def _qkprep(p, cos, sin, g_qn, g_kn, rope):
    t = p.shape[0]
    tm = min(512, t)
    kw = A_KV_HEADS * A_HEAD_DIM
    return pl.pallas_call(
        functools.partial(_qkprep_kernel, rope=rope),
        grid=(t // tm,),
        in_specs=[pl.BlockSpec((tm, A_WIDTH), lambda i: (i, 0)),
                  pl.BlockSpec((tm, kw), lambda i: (i, 4096 // kw)),
                  pl.BlockSpec((tm, A_HEAD_DIM), lambda i: (i, 0)),
                  pl.BlockSpec((tm, A_HEAD_DIM), lambda i: (i, 0)),
                  pl.BlockSpec((1, A_HEAD_DIM), lambda i: (0, 0)),
                  pl.BlockSpec((1, A_HEAD_DIM), lambda i: (0, 0))],
        out_specs=[pl.BlockSpec((tm, A_WIDTH), lambda i: (i, 0)),
                   pl.BlockSpec((tm, kw), lambda i: (i, 0))],
        out_shape=[jax.ShapeDtypeStruct((t, A_WIDTH), BF16),
                   jax.ShapeDtypeStruct((t, kw), BF16)],
        compiler_params=_params(("parallel",)),
    )(p, p, cos, sin, g_qn, g_kn)


def _rope_tables(t):
    half = A_HEAD_DIM // 4
    freqs = ROPE_BASE ** (-jnp.arange(half, dtype=F32) / half)
    tok = jnp.arange(t)
    row = (tok // GRID_W).astype(F32)[:, None] * freqs
    col = (tok % GRID_W).astype(F32)[:, None] * freqs
    cos = jnp.concatenate([jnp.cos(row), jnp.cos(row), jnp.cos(col), jnp.cos(col)], axis=-1)
    sin = jnp.concatenate([-jnp.sin(row), jnp.sin(row), -jnp.sin(col), jnp.sin(col)], axis=-1)
    return cos, sin


def _softmax_pv(scores, values, sink_col):
    def lane_blocks(xs):
        return [x[:, i * LANES:(i + 1) * LANES] for x in xs for i in range(x.shape[1] // LANES)]

    m = jnp.maximum(jnp.max(functools.reduce(jnp.maximum, lane_blocks(scores)), axis=-1, keepdims=True), sink_col)
    probs = [jnp.exp(s - m) for s in scores]
    denom = jnp.exp(sink_col - m) + jnp.sum(functools.reduce(jnp.add, lane_blocks(probs)), axis=-1, keepdims=True)
    acc = None
    for p, v in zip(probs, values):
        pv = _dot(p.astype(BF16), v)
        acc = pv if acc is None else acc + pv
    return acc / denom


def _attn_kernel(*refs, local):
    if local:
        q_ref, kp_ref, ko_ref, kn_ref, vp_ref, vo_ref, vn_ref, kc_ref, vc_ref, sink_ref, o_ref = refs
    else:
        q_ref, kc_ref, vc_ref, sink_ref, o_ref = refs
    n = pl.program_id(0)
    nb = pl.num_programs(0)
    rows = A_GROUP * ATTN_BLOCK
    if local:
        qi = lax.broadcasted_iota(jnp.int32, (rows, ATTN_BLOCK), 0) % ATTN_BLOCK
        kj = lax.broadcasted_iota(jnp.int32, (rows, ATTN_BLOCK), 1)
        mask_prev = (kj >= qi) & (n > 0)
        mask_next = (kj <= qi) & (n < nb - 1)
    for kv in range(A_KV_HEADS):
        kc = slice(kv * A_HEAD_DIM, (kv + 1) * A_HEAD_DIM)
        heads = [kv * A_GROUP + g for g in range(A_GROUP)]
        q4 = jnp.concatenate([q_ref[:, h * A_HEAD_DIM:(h + 1) * A_HEAD_DIM] for h in heads], axis=0)
        sink_col = jnp.concatenate(
            [jnp.broadcast_to(sink_ref[h:h + 1, 0:1], (ATTN_BLOCK, 1)) for h in heads], axis=0)
        scores, values = [], []
        if local:
            scores.append(jnp.where(mask_prev, _dot_nt(q4, kp_ref[:, kc]), NEG_INF))
            scores.append(_dot_nt(q4, ko_ref[:, kc]))
            scores.append(jnp.where(mask_next, _dot_nt(q4, kn_ref[:, kc]), NEG_INF))
            values += [vp_ref[:, kc], vo_ref[:, kc], vn_ref[:, kc]]
        scores.append(_dot_nt(q4, kc_ref[:, kc]))
        values.append(vc_ref[:, kc])
        o = _softmax_pv(scores, values, sink_col).astype(o_ref.dtype)
        for g, h in enumerate(heads):
            o_ref[:, h * A_HEAD_DIM:(h + 1) * A_HEAD_DIM] = o[g * ATTN_BLOCK:(g + 1) * ATTN_BLOCK, :]


def _attention(q, k, p, k_ctx, p_ctx, sink_tab, local):
    t = q.shape[0]
    nb = t // ATTN_BLOCK
    l = k_ctx.shape[0]
    kw = A_KV_HEADS * A_HEAD_DIM
    vcol = 4352 // kw
    blk = lambda f: pl.BlockSpec((ATTN_BLOCK, kw), f)
    blkv = lambda f: pl.BlockSpec((ATTN_BLOCK, kw), lambda i: (f(i)[0], vcol))
    prev = lambda i: (jnp.maximum(i - 1, 0), 0)
    own = lambda i: (i, 0)
    nxt = lambda i: (jnp.minimum(i + 1, nb - 1), 0)
    in_specs = [pl.BlockSpec((ATTN_BLOCK, A_WIDTH), own)]
    args = [q]
    if local:
        in_specs += [blk(prev), blk(own), blk(nxt), blkv(prev), blkv(own), blkv(nxt)]
        args += [k, k, k, p, p, p]
    in_specs += [pl.BlockSpec((l, kw), lambda i: (0, 0)), pl.BlockSpec((l, kw), lambda i: (0, vcol)),
                 pl.BlockSpec((A_HEADS, LANES), lambda i: (0, 0))]
    args += [k_ctx, p_ctx, sink_tab]
    return pl.pallas_call(
        functools.partial(_attn_kernel, local=local),
        grid=(nb,),
        in_specs=in_specs,
        out_specs=pl.BlockSpec((ATTN_BLOCK, A_WIDTH), own),
        out_shape=jax.ShapeDtypeStruct((t, A_WIDTH), BF16),
        compiler_params=_params(("parallel",)),
    )(*args)


def _gla_kernel(q_ref, k_ref, v_ref, la_ref, s0_ref, o_ref, sf_ref, st_ref, *, rev, nchunk):
    @pl.when(pl.program_id(0) == 0)
    def _():
        st_ref[...] = s0_ref[...]

    r = nchunk * GLA_CHUNK
    sub = min(r, GLA_SUB)
    ii = lax.broadcasted_iota(jnp.int32, (sub, sub), 0)
    jj = lax.broadcasted_iota(jnp.int32, (sub, sub), 1)
    same_chunk = (ii // GLA_CHUNK) == (jj // GLA_CHUNK)
    tri = same_chunk & ((jj >= ii) if rev else (jj <= ii))
    b_all = la_ref[...]
    pos = lax.broadcasted_iota(jnp.int32, b_all.shape, 0) % GLA_CHUNK
    step = 1
    while step < GLA_CHUNK:
        if rev:
            b_all = b_all + jnp.where(pos < GLA_CHUNK - step, pltpu.roll(b_all, r - step, 0), 0.0)
        else:
            b_all = b_all + jnp.where(pos >= step, pltpu.roll(b_all, step, 0), 0.0)
        step *= 2
    heads = []
    for h in range(B_HEADS):
        kc = slice(h * B_DK, (h + 1) * B_DK)
        b = b_all[:, kc]
        b3 = b.reshape(nchunk, GLA_CHUNK, B_DK)
        b_end = b3[:, 0:1, :] if rev else b3[:, GLA_CHUNK - 1:GLA_CHUNK, :]
        k = k_ref[:, kc].astype(F32)
        qe = (q_ref[:, kc].astype(F32) * (B_DK ** -0.5) * jnp.exp(b)).astype(BF16)
        ke = (k * jnp.exp(-b)).astype(BF16)
        kd = (k.reshape(nchunk, GLA_CHUNK, B_DK) * jnp.exp(b_end - b3)).astype(BF16)
        v = v_ref[:, h * B_DV:(h + 1) * B_DV].astype(BF16)
        o_intra = []
        for s0 in range(0, r, sub):
            rs = slice(s0, s0 + sub)
            a = jnp.where(tri, _dot_nt(qe[rs, :], ke[rs, :]), 0.0).astype(BF16)
            o_intra.append(_dot(a, v[rs, :]))
        heads.append((qe, kd, v, jnp.concatenate(o_intra, axis=0), jnp.exp(b_end)))
    st = [st_ref[h] for h in range(B_HEADS)]
    for c in (range(nchunk - 1, -1, -1) if rev else range(nchunk)):
        rows = slice(c * GLA_CHUNK, (c + 1) * GLA_CHUNK)
        for h, (qe, kd, v, o_intra, decay) in enumerate(heads):
            o = o_intra[rows, :] + _dot_nt(qe[rows, :], st[h].astype(BF16))
            o_ref[0, rows, h * B_DV:(h + 1) * B_DV] = o.astype(o_ref.dtype)
            st[h] = st[h] * decay[c] + _dot_tn(v[rows, :], kd[c])
    for h in range(B_HEADS):
        st_ref[h] = st[h]
        sf_ref[h] = st[h]


def _gla_scan(p, la, s0, rev):
    t = p.shape[0]
    r = min(512, t)
    nblk = t // r
    rb = (lambda c: nblk - 1 - c) if rev else (lambda c: c)
    nk = B_HEADS * B_DK
    o, sf = pl.pallas_call(
        functools.partial(_gla_kernel, rev=rev, nchunk=r // GLA_CHUNK),
        grid=(nblk,),
        in_specs=[pl.BlockSpec((r, nk), lambda c: (rb(c), 3584 // nk)),
                  pl.BlockSpec((r, nk), lambda c: (rb(c), 3072 // nk)),
                  pl.BlockSpec((r, B_WIDTH), lambda c: (rb(c), 1024 // B_WIDTH)),
                  pl.BlockSpec((r, nk), lambda c: (rb(c), 1 if rev else 0)),
                  pl.BlockSpec((B_HEADS, B_DV, B_DK), lambda c: (0, 0, 0))],
        out_specs=[pl.BlockSpec((1, r, B_WIDTH), lambda c: (0, rb(c), 0)),
                   pl.BlockSpec((B_HEADS, B_DV, B_DK), lambda c: (0, 0, 0))],
        out_shape=[jax.ShapeDtypeStruct((1, t, B_WIDTH), BF16),
                   jax.ShapeDtypeStruct((B_HEADS, B_DV, B_DK), F32)],
        scratch_shapes=[pltpu.VMEM((B_HEADS, B_DV, B_DK), F32)],
        compiler_params=_params(("arbitrary",)),
    )(p, p, p, la, s0)
    return o[0], sf


def _outproj_kernel(oa_ref, of_ref, ob_ref, og_ref, gg_ref, w_ref, x_ref, gt_ref, o_ref, mix_ref, *, row):
    @pl.when(pl.program_id(1) == 0)
    def _():
        mix_ref[:, 0:A_WIDTH] = oa_ref[...]
        for h in range(B_HEADS):
            cols = slice(h * B_DV, (h + 1) * B_DV)
            o = of_ref[:, cols].astype(F32) + ob_ref[:, cols].astype(F32)
            ms = jnp.mean(o * o, axis=-1, keepdims=True)
            y = o * lax.rsqrt(ms + NORM_EPS) * gg_ref[...]
            mix_ref[:, A_WIDTH + h * B_DV:A_WIDTH + (h + 1) * B_DV] = (y * _silu(og_ref[:, cols].astype(F32))).astype(BF16)

    o_ref[...] = x_ref[...] + gt_ref[row:row + 1, :] * _dot(mix_ref[...], w_ref[...])


def _outproj(oa, o_f, o_b, p, g_gla, w_out, x, ada_l, row):
    t, d = x.shape
    tm = min(512, t)
    tn = 1024
    return pl.pallas_call(
        functools.partial(_outproj_kernel, row=row),
        grid=(t // tm, d // tn),
        in_specs=[pl.BlockSpec((tm, A_WIDTH), lambda i, j: (i, 0)),
                  pl.BlockSpec((tm, B_WIDTH), lambda i, j: (i, 0)),
                  pl.BlockSpec((tm, B_WIDTH), lambda i, j: (i, 0)),
                  pl.BlockSpec((tm, B_WIDTH), lambda i, j: (i, 2048 // B_WIDTH)),
                  pl.BlockSpec((1, B_DV), lambda i, j: (0, 0)),
                  pl.BlockSpec((A_WIDTH + B_WIDTH, tn), lambda i, j: (0, j)),
                  pl.BlockSpec((tm, tn), lambda i, j: (i, j)),
                  pl.BlockSpec((8, tn), lambda i, j: (0, 2 * (d // tn) + j))],
        out_specs=pl.BlockSpec((tm, tn), lambda i, j: (i, j)),
        out_shape=jax.ShapeDtypeStruct((t, d), F32),
        scratch_shapes=[pltpu.VMEM((tm, A_WIDTH + B_WIDTH), BF16)],
        compiler_params=_params(("parallel", "arbitrary")),
    )(oa, o_f, o_b, p, g_gla, w_out, x, ada_l)


def _ffn_kernel(x_ref, g_ref, sh_ref, sc_ref, gt_ref, wg_ref, wu_ref, wd_ref, o_ref, hn_ref, *, row):
    f = pl.program_id(1)

    @pl.when(f == 0)
    def _():
        hn_ref[...] = _norm_mod(x_ref[...], g_ref[...], sh_ref[row:row + 1, :], sc_ref[row:row + 1, :]).astype(BF16)
        o_ref[...] = jnp.zeros_like(o_ref)

    h = hn_ref[...]
    a = (_silu(_dot(h, wg_ref[0].astype(BF16))) * _dot(h, wu_ref[0].astype(BF16))).astype(BF16)
    o_ref[...] += _dot(a, wd_ref[0].astype(BF16))

    @pl.when(f == pl.num_programs(1) - 1)
    def _():
        o_ref[...] = x_ref[...] + gt_ref[row:row + 1, :] * o_ref[...]


def _ffn(x, ada_l, row, g, wg, wu, wd, lyr):
    t, d = x.shape
    ff = wg.shape[2]
    tm = min(1024, t)
    tf = 256
    return pl.pallas_call(
        functools.partial(_ffn_kernel, row=row),
        grid=(t // tm, ff // tf),
        in_specs=[pl.BlockSpec((tm, d), lambda i, f: (i, 0)),
                  pl.BlockSpec((1, d), lambda i, f: (0, 0)),
                  _ada_spec(3), _ada_spec(4), _ada_spec(5),
                  pl.BlockSpec((1, d, tf), lambda i, f: (lyr, 0, f)),
                  pl.BlockSpec((1, d, tf), lambda i, f: (lyr, 0, f)),
                  pl.BlockSpec((1, tf, d), lambda i, f: (lyr, f, 0))],
        out_specs=pl.BlockSpec((tm, d), lambda i, f: (i, 0)),
        out_shape=jax.ShapeDtypeStruct((t, d), F32),
        scratch_shapes=[pltpu.VMEM((tm, d), BF16)],
        compiler_params=_params(("parallel", "arbitrary"), MOE_VMEM_LIMIT),
    )(x, g, ada_l, ada_l, ada_l, wg, wu, wd)


def _pool_kernel(x_ref, xp_ref, xn_ref, g_ref, sh_ref, sc_ref, gt_ref, w_ref, ps_ref, o_ref, hb_ref, *, row, t):
    i = pl.program_id(0)
    tm = x_ref.shape[0]
    g = g_ref[...]
    sh = sh_ref[row:row + 1, :]
    sc = sc_ref[row:row + 1, :]
    hb_ref[0:POOL_HALO, :] = jnp.where(i > 0, _norm_mod(xp_ref[...], g, sh, sc), 0.0)
    hb_ref[POOL_HALO:POOL_HALO + tm, :] = _norm_mod(x_ref[...], g, sh, sc)
    hb_ref[POOL_HALO + tm:, :] = jnp.where(i < pl.num_programs(0) - 1, _norm_mod(xn_ref[...], g, sh, sc), 0.0)
    tpos = i * tm + lax.broadcasted_iota(jnp.int32, (tm, 1), 0)
    gc = D_MODEL // POOL_GROUPS
    for grp, w in enumerate(POOL_WINDOWS):
        cols = slice(grp * gc, (grp + 1) * gc)
        acc = None
        for dlt in range(-(w // 2), w - w // 2):
            piece = hb_ref[POOL_HALO + dlt:POOL_HALO + dlt + tm, cols]
            acc = piece if acc is None else acc + piece
        cnt = jnp.minimum(tpos + (w - w // 2), t) - jnp.maximum(tpos - w // 2, 0)
        pooled = acc / cnt.astype(F32) - hb_ref[POOL_HALO:POOL_HALO + tm, cols]
        y = _dot(pooled.astype(BF16), w_ref[grp])
        o_ref[:, cols] = x_ref[:, cols] + gt_ref[row:row + 1, cols] * (y * ps_ref[:, cols])


def _pool(x, ada_l, row, g, w_pool, pool_scale):
    t, d = x.shape
    tm = min(256, t)
    nh = t // POOL_HALO
    gc = d // POOL_GROUPS
    return pl.pallas_call(
        functools.partial(_pool_kernel, row=row, t=t),
        grid=(t // tm,),
        in_specs=[pl.BlockSpec((tm, d), lambda i: (i, 0)),
                  pl.BlockSpec((POOL_HALO, d), lambda i: (jnp.maximum(i * (tm // POOL_HALO) - 1, 0), 0)),
                  pl.BlockSpec((POOL_HALO, d), lambda i: (jnp.minimum((i + 1) * (tm // POOL_HALO), nh - 1), 0)),
                  pl.BlockSpec((1, d), lambda i: (0, 0)),
                  _ada_spec(0), _ada_spec(1), _ada_spec(2),
                  pl.BlockSpec((POOL_GROUPS, gc, gc), lambda i: (0, 0, 0)),
                  pl.BlockSpec((1, d), lambda i: (0, 0))],
        out_specs=pl.BlockSpec((tm, d), lambda i: (i, 0)),
        out_shape=jax.ShapeDtypeStruct((t, d), F32),
        scratch_shapes=[pltpu.VMEM((tm + 2 * POOL_HALO, d), F32)],
        compiler_params=_params(("parallel",)),
    )(x, x, x, g, ada_l, ada_l, ada_l, w_pool, pool_scale)


def _router_kernel(x_ref, g_ref, sh_ref, sc_ref, wr_ref, hn_ref, sel_ref, gate_ref, *, row):
    h = _norm_mod(x_ref[...], g_ref[...], sh_ref[row:row + 1, :], sc_ref[row:row + 1, :])
    hn_ref[...] = h.astype(hn_ref.dtype)
    lane = lax.broadcasted_iota(jnp.int32, (h.shape[0], LANES), 1)
    lane_f = lane.astype(F32)
    logits = jnp.where(lane < N_EXPERTS, _dot(h, wr_ref[...], HIGHEST), -jnp.inf)
    m1 = jnp.max(logits, axis=-1, keepdims=True)
    i1 = jnp.min(jnp.where(logits == m1, lane_f, float(LANES)), axis=-1, keepdims=True)
    rest = jnp.where(lane_f == i1, -jnp.inf, logits)
    m2 = jnp.max(rest, axis=-1, keepdims=True)
    i2 = jnp.min(jnp.where(rest == m2, lane_f, float(LANES)), axis=-1, keepdims=True)
    e2 = jnp.exp(m2 - m1)
    den = 1.0 + e2
    sel_ref[...] = jnp.where(lane == 0, i1, jnp.where(lane == 1, i2, 0.0)).astype(jnp.int32)
    gate_ref[...] = jnp.where(lane == 0, 1.0 / den, jnp.where(lane == 1, e2 / den, 0.0))


def _router(x, ada_l, row, g, w_router_pad):
    t, d = x.shape
    tm = min(256, t)
    return pl.pallas_call(
        functools.partial(_router_kernel, row=row),
        grid=(t // tm,),
        in_specs=[pl.BlockSpec((tm, d), lambda i: (i, 0)),
                  pl.BlockSpec((1, d), lambda i: (0, 0)),
                  _ada_spec(3), _ada_spec(4),
                  pl.BlockSpec((d, LANES), lambda i: (0, 0))],
        out_specs=[pl.BlockSpec((tm, d), lambda i: (i, 0)),
                   pl.BlockSpec((tm, LANES), lambda i: (i, 0)),
                   pl.BlockSpec((tm, LANES), lambda i: (i, 0))],
        out_shape=[jax.ShapeDtypeStruct((t, d), BF16),
                   jax.ShapeDtypeStruct((t, LANES), jnp.int32),
                   jax.ShapeDtypeStruct((t, LANES), F32)],
        compiler_params=_params(("parallel",)),
    )(x, g, ada_l, ada_l, w_router_pad)


MOE_TILE = 1280
MOE_SUB = 256
MOE_SUBS = MOE_TILE // MOE_SUB
SRC_BLK = 256
Y_BLK = 256
CMB_TILE = 256
MOE_TF = 512
GATHER_BUFS = 3


def _moe_ffn_kernel(te_ref, tn_ref, tlo_ref, tnb_ref, trk_ref, cb_ref, posb_ref, hn_hbm, wg_ref, wu_ref, wd_ref,
                    y_ref, xb_ref, acc_ref, hbuf_ref, sem):
    i = pl.program_id(0)
    f = pl.program_id(1)
    nsub = tn_ref[i]

    @pl.when(f == 0)
    def _():
        xb_ref[...] = jnp.zeros_like(xb_ref)
        acc_ref[...] = jnp.zeros_like(acc_ref)
        row_id = lax.broadcasted_iota(jnp.int32, (MOE_SUB, SRC_BLK), 0) + i * MOE_TILE
        lo = tlo_ref[i]
        n = tnb_ref[i]
        rank0 = trk_ref[i]
        cb_base = te_ref[i] * (posb_ref.shape[0] + 1)

        def blk_copy(b, slot):
            return pltpu.make_async_copy(hn_hbm.at[pl.ds((lo + b) * SRC_BLK, SRC_BLK), :], hbuf_ref.at[slot],
                                         sem.at[slot])

        for j in range(GATHER_BUFS - 1):
            @pl.when(j < n)
            def _():
                blk_copy(j, j).start()

        def body(b, carry):
            slot = b % GATHER_BUFS
            blk_copy(b, slot).wait()
            ahead = b + GATHER_BUFS - 1

            @pl.when(ahead < n)
            def _():
                blk_copy(ahead, ahead % GATHER_BUFS).start()

            blk = lo + b
            pos = posb_ref[blk]
            pos1, pos2 = pos[0:1, :], pos[1:2, :]
            before = cb_ref[cb_base + blk]
            through = cb_ref[cb_base + blk + 1]
            for q in range(MOE_SUBS):
                @pl.when((q < nsub) & (before < rank0 + (q + 1) * MOE_SUB) & (through > rank0 + q * MOE_SUB))
                def _():
                    rows = slice(q * MOE_SUB, (q + 1) * MOE_SUB)
                    rid = row_id + q * MOE_SUB
                    onehot = jnp.where((pos1 == rid) | (pos2 == rid), 1.0, 0.0).astype(BF16)
                    xb_ref[rows, :] += _dot(onehot, hbuf_ref[slot]).astype(BF16)
            return carry

        lax.fori_loop(0, n, body, 0)

    for k in range(1, MOE_SUBS + 1):
        @pl.when(nsub == k)
        def _():
            rows = slice(0, k * MOE_SUB)
            h = xb_ref[rows, :]
            a = (_silu(_dot(h, wg_ref[0, 0].astype(BF16))) * _dot(h, wu_ref[0, 0].astype(BF16))).astype(BF16)
            acc_ref[rows, :] += _dot(a, wd_ref[0, 0].astype(BF16))

    @pl.when(f == pl.num_programs(1) - 1)
    def _():
        y_ref[...] = acc_ref[...].astype(y_ref.dtype)


def _moe_ffn(hn, posb, tile_expert, tile_subs, tile_blk_lo, tile_blk_n, tile_rank, blk_counts, wg, wu, wd, lyr):
    d = hn.shape[1]
    tm = MOE_TILE
    ff = wg.shape[3]
    tf = MOE_TF
    nf = ff // tf
    nt = tile_expert.shape[0]
    fidx = lambda i, f, tn: jnp.where(tn[i] > 0, f, nf - 1)
    return pl.pallas_call(
        _moe_ffn_kernel,
        grid_spec=pltpu.PrefetchScalarGridSpec(
            num_scalar_prefetch=6,
            grid=(nt, nf),
            in_specs=[pl.BlockSpec(posb.shape, lambda i, f, te, tn, *_: (0, 0, 0)),
                      pl.BlockSpec(memory_space=pl.ANY),
                      pl.BlockSpec((1, 1, d, tf), lambda i, f, te, tn, *_: (lyr, te[i], 0, fidx(i, f, tn))),
                      pl.BlockSpec((1, 1, d, tf), lambda i, f, te, tn, *_: (lyr, te[i], 0, fidx(i, f, tn))),
                      pl.BlockSpec((1, 1, tf, d), lambda i, f, te, tn, *_: (lyr, te[i], fidx(i, f, tn), 0))],
            out_specs=pl.BlockSpec((tm, d), lambda i, f, te, tn, *_: (i, 0)),
            scratch_shapes=[pltpu.VMEM((tm, d), BF16), pltpu.VMEM((tm, d), F32),
                            pltpu.VMEM((GATHER_BUFS, SRC_BLK, d), BF16),
                            pltpu.SemaphoreType.DMA((GATHER_BUFS,))]),
        out_shape=jax.ShapeDtypeStruct((nt * tm, d), BF16),
        compiler_params=_params(("arbitrary", "arbitrary"), MOE_VMEM_LIMIT),
    )(tile_expert, tile_subs, tile_blk_lo, tile_blk_n, tile_rank, blk_counts, posb, hn, wg, wu, wd)


CMB_SLOTS = [(e, j) for e in range(N_EXPERTS) for j in range(2)]


def _combine_kernel(fb_ref, nb_ref, x_ref, pos_ref, gate_ref, gt_ref, y_hbm, o_ref, acc_ref, ybuf_ref, sem,
                    *, row, tile_off):
    base = (tile_off + pl.program_id(0)) * N_EXPERTS
    tm = x_ref.shape[0]
    pos1, pos2 = pos_ref[:, 0:1], pos_ref[:, 1:2]
    g1, g2 = gate_ref[:, 0:1], gate_ref[:, 1:2]
    col = lax.broadcasted_iota(jnp.int32, (tm, Y_BLK), 1)

    def blk(s):
        e, j = CMB_SLOTS[s]
        return fb_ref[base + e] + j

    def used(s):
        e, j = CMB_SLOTS[s]
        return nb_ref[base + e] > j

    def blk_copy(s):
        return pltpu.make_async_copy(y_hbm.at[pl.ds(blk(s) * Y_BLK, Y_BLK), :], ybuf_ref.at[s], sem.at[s])

    for s in range(len(CMB_SLOTS)):
        @pl.when(used(s))
        def _():
            blk_copy(s).start()

    acc_ref[...] = jnp.zeros_like(acc_ref)
    for s in range(len(CMB_SLOTS)):
        @pl.when(used(s))
        def _():
            blk_copy(s).wait()
            off = blk(s) * Y_BLK
            w = jnp.where(pos1 - off == col, g1, 0.0) + jnp.where(pos2 - off == col, g2, 0.0)
            acc_ref[...] += _dot(w.astype(BF16), ybuf_ref[s])

    o_ref[...] = x_ref[...] + gt_ref[row:row + 1, :] * acc_ref[...]


def _combine(x, y, pos, gates, first_blk, num_blk, tok_off, ada_l, row):
    t, d = x.shape
    tm = CMB_TILE
    tile_off = tok_off // tm
    return pl.pallas_call(
        functools.partial(_combine_kernel, row=row, tile_off=tile_off),
        grid_spec=pltpu.PrefetchScalarGridSpec(
            num_scalar_prefetch=2,
            grid=(t // tm,),
            in_specs=[pl.BlockSpec((tm, d), lambda i, fb, nb: (i, 0)),
                      pl.BlockSpec((tm, 2), lambda i, fb, nb: (tile_off + i, 0)),
                      pl.BlockSpec((tm, LANES), lambda i, fb, nb: (i, 0)),
                      _ada_spec(5),
                      pl.BlockSpec(memory_space=pl.ANY)],
            out_specs=pl.BlockSpec((tm, d), lambda i, fb, nb: (i, 0)),
            scratch_shapes=[pltpu.VMEM((tm, d), F32), pltpu.VMEM((len(CMB_SLOTS), Y_BLK, d), BF16),
                            pltpu.SemaphoreType.DMA((len(CMB_SLOTS),))]),
        out_shape=jax.ShapeDtypeStruct((t, d), F32),
        compiler_params=_params(("arbitrary",)),
    )(first_blk, num_blk, x, pos, gates, ada_l, y)


def _route(sel):
    tm = MOE_TILE
    n_tok = sel.shape[0]
    n_asg = 2 * n_tok
    nt = -(-n_asg // tm) + N_EXPERTS
    e_flat = sel.reshape(-1)
    onehot = (e_flat[:, None] == jnp.arange(N_EXPERTS, dtype=jnp.int32)[None, :]).astype(jnp.int32)
    csum = jnp.cumsum(onehot, axis=0)
    count = csum[-1]
    rank = jnp.sum((csum - onehot) * onehot, axis=1)
    tiles_e = (count + tm - 1) // tm
    tile_end = jnp.cumsum(tiles_e)
    base = (tile_end - tiles_e) * tm
    pos = jnp.sum(onehot * base[None, :], axis=1) + rank
    tile_id = jnp.arange(nt, dtype=jnp.int32)
    used = tile_end[-1]
    owner = lambda i: jnp.minimum(jnp.sum((i[:, None] >= tile_end[None, :]).astype(jnp.int32), axis=1), N_EXPERTS - 1)
    tile_expert = owner(jnp.minimum(tile_id, used - 1))
    rows_in_tile = jnp.clip((base + count)[tile_expert] - tile_id * tm, 0, tm)
    tile_subs = jnp.where(tile_id < used, (rows_in_tile + MOE_SUB - 1) // MOE_SUB, 0)
    n_blk = n_tok // SRC_BLK
    blk_counts = jnp.concatenate([jnp.zeros((1, N_EXPERTS), jnp.int32),
                                  csum.reshape(n_blk, 2 * SRC_BLK, N_EXPERTS)[:, -1, :]], axis=0).T
    tile_rank = tile_id * tm - base[tile_expert]
    counts_t = blk_counts[tile_expert]
    tile_blk_lo = jnp.sum((counts_t[:, 1:] <= tile_rank[:, None]).astype(jnp.int32), axis=1)
    tile_blk_hi = jnp.sum((counts_t[:, :-1] < (tile_rank + rows_in_tile)[:, None]).astype(jnp.int32), axis=1)
    tile_blk_n = jnp.where(tile_subs > 0, jnp.maximum(tile_blk_hi - tile_blk_lo, 0), 0)
    tile_blk_lo = jnp.minimum(tile_blk_lo, n_blk - 1)
    posb = jnp.pad(pos.reshape(n_blk, SRC_BLK, 2).transpose(0, 2, 1), ((0, 0), (0, 6), (0, 0)), constant_values=-1)
    ntt = n_tok // CMB_TILE
    through = csum.reshape(ntt, 2 * CMB_TILE, N_EXPERTS)[:, -1, :]
    before = jnp.concatenate([jnp.zeros((1, N_EXPERTS), jnp.int32), through[:-1]], axis=0)
    lo = base[None, :] + before
    hi = base[None, :] + through - 1
    first_blk = lo // Y_BLK
    num_blk = jnp.where(through > before, hi // Y_BLK - first_blk + 1, 0)
    return (posb, pos.reshape(n_tok, 2), tile_expert, tile_subs, tile_blk_lo, tile_blk_n, tile_rank,
            blk_counts.reshape(-1), first_blk.reshape(-1), num_blk.reshape(-1))


def _moe(streams, ada_l, g, w_router_pad, wg, wu, wd, lyr):
    hs, sels, gates = [], [], []
    for x, row in streams:
        h, s, gt = _router(x, ada_l, row, g, w_router_pad)
        hs.append(h), sels.append(s[:, :2]), gates.append(gt)
    n_tok = sum(h.shape[0] for h in hs)
    pad = -n_tok % SRC_BLK
    if pad:
        hs.append(jnp.zeros((pad, D_MODEL), BF16))
    h_all = hs[0] if len(hs) == 1 else jnp.concatenate(hs, axis=0)
    sel_all = sels[0] if len(sels) == 1 else jnp.concatenate(sels, axis=0)
    (posb, pos, tile_expert, tile_subs, tile_blk_lo, tile_blk_n, tile_rank, blk_counts,
     first_blk, num_blk) = _route(sel_all)
    y = _moe_ffn(h_all, posb, tile_expert, tile_subs, tile_blk_lo, tile_blk_n, tile_rank, blk_counts,
                 wg, wu, wd, lyr)
    outs, off = [], 0
    for (x, row), gt in zip(streams, gates):
        outs.append(_combine(x, y, pos, gt, first_blk, num_blk, off, ada_l, row))
        off += x.shape[0]
    return outs


def _even_layer(x, ctx, ada_l, g1, g2, w_in, g_qn, g_kn, sink, w_gate_up, b_gate_up, g_gla, w_out,
                w_ffn_gate, w_ffn_up, w_ffn_down, lyr, rope_tabs, ctx_out):
    d = D_MODEL
    w_main = jnp.concatenate([w_in[:, 2080:3104], w_in[:, 1024:2048], w_in[:, 3616:4640], w_in[:, 512:1024],
                              w_in[:, 3104:3616], w_in[:, 0:256], w_in[:, 256:512]], axis=1).astype(BF16)
    w_gate = jnp.pad(w_in[:, 2048:2080], ((0, 0), (0, LANES - 2 * GATE_RANK))).astype(BF16)
    nk = B_HEADS * B_DK
    w_up = jnp.zeros((LANES, 2 * nk), F32)
    w_up = w_up.at[0:GATE_RANK, 0:nk].set(w_gate_up[0]).at[GATE_RANK:2 * GATE_RANK, nk:].set(w_gate_up[1])
    b_up = b_gate_up.reshape(1, 2 * nk)
    w_out_b = w_out.astype(BF16)
    gq, gk, gg = g_qn.reshape(1, -1), g_kn.reshape(1, -1), g_gla.reshape(1, -1)
    sink_tab = jnp.broadcast_to(sink[:, None], (A_HEADS, LANES))
    cos, sin = rope_tabs

    pc, la_c = _inproj(ctx, ada_l, 1, g1, w_main, w_gate, w_up, b_up)
    qc, kc = _qkprep(pc, cos, sin, gq, gk, rope=False)
    s0 = jnp.zeros((B_HEADS, B_DV, B_DK), F32)
    oc_f, s_fwd = _gla_scan(pc, la_c, s0, rev=False)
    oc_b, s_bwd = _gla_scan(pc, la_c, s0, rev=True)

    px, la_x = _inproj(x, ada_l, 0, g1, w_main, w_gate, w_up, b_up)
    qx, kx = _qkprep(px, cos, sin, gq, gk, rope=True)
    oa = _attention(qx, kx, px, kc, pc, sink_tab, local=True)
    ox_f, _ = _gla_scan(px, la_x, s_fwd, rev=False)
    ox_b, _ = _gla_scan(px, la_x, s_bwd, rev=True)
    x = _outproj(oa, ox_f, ox_b, px, gg, w_out_b, x, ada_l, 0)
    x = _ffn(x, ada_l, 0, g2, w_ffn_gate, w_ffn_up, w_ffn_down, lyr)
    if ctx_out:
        oa_c = _attention(qc, None, None, kc, pc, sink_tab, local=False)
        ctx = _outproj(oa_c, oc_f, oc_b, pc, gg, w_out_b, ctx, ada_l, 1)
        ctx = _ffn(ctx, ada_l, 1, g2, w_ffn_gate, w_ffn_up, w_ffn_down, lyr)
    return x, ctx


def _odd_layer(x, ctx, ada_l, g1, g2, w_pool, pool_scale, w_router, w_exp_gate, w_exp_up, w_exp_down, lyr, ctx_out):
    w_pool_b = w_pool.astype(BF16)
    ps = pool_scale.reshape(1, -1)
    wr = jnp.pad(w_router, ((0, 0), (0, LANES - N_EXPERTS)))
    x = _pool(x, ada_l, 0, g1, w_pool_b, ps)
    if ctx_out:
        ctx = _pool(ctx, ada_l, 1, g1, w_pool_b, ps)
        x, ctx = _moe([(x, 0), (ctx, 1)], ada_l, g2, wr, w_exp_gate, w_exp_up, w_exp_down, lyr)
    else:
        (x,) = _moe([(x, 0)], ada_l, g2, wr, w_exp_gate, w_exp_up, w_exp_down, lyr)
    return x, ctx


def kernel(x, c, ctx, c_ctx, w_ada, b_ada, norm_g, w_in, g_qn, g_kn, attn_sink, w_gate_up, b_gate_up, g_gla, w_out,
           w_ffn_gate, w_ffn_up, w_ffn_down, w_pool, pool_scale, w_router, w_exp_gate, w_exp_up, w_exp_down):
    depth = w_ada.shape[0]
    xs = x[0]
    cs = ctx[0]
    t = xs.shape[0]
    cond = jnp.zeros((8, D_MODEL), F32).at[0].set(c[0]).at[1].set(c_ctx)
    ada = _ada_all(cond, w_ada, b_ada)
    rope_tabs = _rope_tables(t)
    for l in range(depth):
        ctx_later = any(j % 2 == 0 for j in range(l + 1, depth))
        g1 = norm_g[l, 0].reshape(1, -1)
        g2 = norm_g[l, 1].reshape(1, -1)
        if l % 2 == 0:
            e = l // 2
            xs, cs = _even_layer(xs, cs, ada[l], g1, g2, w_in[e], g_qn[e], g_kn[e], attn_sink[e], w_gate_up[e],
                                 b_gate_up[e], g_gla[e], w_out[e], w_ffn_gate, w_ffn_up, w_ffn_down, e,
                                 rope_tabs, ctx_later)
        else:
            o = l // 2
            xs, cs = _odd_layer(xs, cs, ada[l], g1, g2, w_pool[o], pool_scale[o], w_router[o],
                                w_exp_gate, w_exp_up, w_exp_down, o, ctx_later)
    return xs[None]
```

```python
import functools

import jax
import jax.numpy as jnp
import numpy as np
from jax import lax
from jax.experimental import pallas as pl
from jax.experimental.pallas import tpu as pltpu

F32 = jnp.float32
BF16 = jnp.bfloat16
HIGHEST = lax.Precision.HIGHEST

D_MODEL = 2048
GRID_W = 64
A_HEADS = 8
A_KV_HEADS = 2
A_GROUP = 4
A_HEAD_DIM = 128
A_WIDTH = A_HEADS * A_HEAD_DIM
ATTN_BLOCK = 128
ROPE_BASE = 10000.0
B_HEADS = 4
B_DV = 256
B_DK = 128
B_WIDTH = B_HEADS * B_DV
GATE_RANK = 16
GATE_TAU = 16.0
GLA_CHUNK = 64
GLA_SUB = 256
POOL_GROUPS = 4
POOL_WINDOWS = (2, 4, 8, 16)
POOL_HALO = 8
N_EXPERTS = 8
NORM_EPS = 1e-6
NEG_INF = -1e30
LANES = 128

P_COLS = 4608
VMEM_LIMIT = 56 * 1024 * 1024
MOE_VMEM_LIMIT = 62 * 1024 * 1024


def _params(sem, vmem=VMEM_LIMIT):
    return pltpu.CompilerParams(dimension_semantics=sem, vmem_limit_bytes=vmem)


def _dot(a, b, precision=None):
    return jnp.dot(a, b, preferred_element_type=F32, precision=precision)


def _dot_nt(a, b):
    return lax.dot_general(a, b, (((1,), (1,)), ((), ())), preferred_element_type=F32)


def _dot_tn(a, b):
    return lax.dot_general(a, b, (((0,), (0,)), ((), ())), preferred_element_type=F32)


def _silu(x):
    return x * jax.nn.sigmoid(x)


def _norm_mod(x, g, shift, scale):
    ms = jnp.mean(x * x, axis=-1, keepdims=True)
    return (x * lax.rsqrt(ms + NORM_EPS) * g) * (1.0 + scale) + shift


def _ada_kernel(cond_ref, w_ref, b_ref, o_ref):
    s = _silu(cond_ref[...])
    s_hi = s.astype(BF16)
    s_lo = (s - s_hi.astype(F32)).astype(BF16)
    w = w_ref[0].astype(BF16)
    o_ref[0] = _dot(s_hi, w) + _dot(s_lo, w) + b_ref[0]


def _ada_all(cond, w_ada, b_ada):
    depth, d, n = w_ada.shape
    tn = 1024
    return pl.pallas_call(
        _ada_kernel,
        grid=(depth, n // tn),
        in_specs=[pl.BlockSpec((8, d), lambda l, j: (0, 0)),
                  pl.BlockSpec((1, d, tn), lambda l, j: (l, 0, j)),
                  pl.BlockSpec((1, 1, tn), lambda l, j: (l, 0, j))],
        out_specs=pl.BlockSpec((1, 8, tn), lambda l, j: (l, 0, j)),
        out_shape=jax.ShapeDtypeStruct((depth, 8, n), F32),
        compiler_params=_params(("parallel", "parallel")),
    )(cond, w_ada, b_ada.reshape(depth, 1, n))


def _ada_spec(k):
    return pl.BlockSpec((8, D_MODEL), lambda *_: (0, k))


def _inproj_kernel(x_ref, g_ref, sh_ref, sc_ref, w_ref, wg_ref, wup_ref, bup_ref, p_ref, la_ref, hn_ref, *, row):
    @pl.when(pl.program_id(1) == 0)
    def _():
        h = _norm_mod(x_ref[...], g_ref[...], sh_ref[row:row + 1, :], sc_ref[row:row + 1, :]).astype(BF16)
        hn_ref[...] = h
        lr = _dot(h, wg_ref[...])
        z = _dot(lr, wup_ref[...], HIGHEST) + bup_ref[...]
        la_ref[...] = (jnp.minimum(z, 0.0) - jnp.log1p(jnp.exp(-jnp.abs(z)))) * (1.0 / GATE_TAU)

    p_ref[...] = _dot(hn_ref[...], w_ref[...]).astype(p_ref.dtype)


def _inproj(x, ada_l, row, g, w_main, w_gate, w_up, b_up):
    t, d = x.shape
    tm = min(512, t)
    tn = 1536
    nla = 2 * B_HEADS * B_DK
    return pl.pallas_call(
        functools.partial(_inproj_kernel, row=row),
        grid=(t // tm, P_COLS // tn),
        in_specs=[pl.BlockSpec((tm, d), lambda i, j: (i, 0)),
                  pl.BlockSpec((1, d), lambda i, j: (0, 0)),
                  _ada_spec(0), _ada_spec(1),
                  pl.BlockSpec((d, tn), lambda i, j: (0, j)),
                  pl.BlockSpec((d, LANES), lambda i, j: (0, 0)),
                  pl.BlockSpec((LANES, nla), lambda i, j: (0, 0)),
                  pl.BlockSpec((1, nla), lambda i, j: (0, 0))],
        out_specs=[pl.BlockSpec((tm, tn), lambda i, j: (i, j)),
                   pl.BlockSpec((tm, nla), lambda i, j: (i, 0))],
        out_shape=[jax.ShapeDtypeStruct((t, P_COLS), BF16),
                   jax.ShapeDtypeStruct((t, nla), F32)],
        scratch_shapes=[pltpu.VMEM((tm, d), BF16)],
        compiler_params=_params(("parallel", "arbitrary")),
    )(x, g, ada_l, ada_l, w_main, w_gate, w_up, b_up)


def _qkprep_kernel(q_ref, k_ref, cos_ref, sin_ref, gq_ref, gk_ref, qo_ref, ko_ref, *, rope):
    tm = q_ref.shape[0]
    lane = lax.broadcasted_iota(jnp.int32, (tm, A_HEAD_DIM), 1)
    first_half = (lane % 64) < 32

    def prep(xh, g, scale):
        ms = jnp.mean(xh * xh, axis=-1, keepdims=True)
        y = xh * lax.rsqrt(ms + NORM_EPS) * g
        if rope:
            partner = jnp.where(first_half, pltpu.roll(y, 96, 1), pltpu.roll(y, 32, 1))
            y = y * cos_ref[...] + partner * sin_ref[...]
        return (y * scale).astype(BF16)

    for h in range(A_HEADS):
        cols = slice(h * A_HEAD_DIM, (h + 1) * A_HEAD_DIM)
        qo_ref[:, cols] = prep(q_ref[:, cols].astype(F32), gq_ref[...], A_HEAD_DIM ** -0.5)
    for h in range(A_KV_HEADS):
        cols = slice(h * A_HEAD_DIM, (h + 1) * A_HEAD_DIM)
        ko_ref[:, cols] = prep(k_ref[:, cols].astype(F32), gk_ref[...], 1.0)


def _qkprep(p, cos, sin, g_qn, g_kn, rope):
    t = p.shape[0]
    tm = min(512, t)
    kw = A_KV_HEADS * A_HEAD_DIM
    return pl.pallas_call(
        functools.partial(_qkprep_kernel, rope=rope),
        grid=(t // tm,),
        in_specs=[pl.BlockSpec((tm, A_WIDTH), lambda i: (i, 0)),
                  pl.BlockSpec((tm, kw), lambda i: (i, 4096 // kw)),
                  pl.BlockSpec((tm, A_HEAD_DIM), lambda i: (i, 0)),
                  pl.BlockSpec((tm, A_HEAD_DIM), lambda i: (i, 0)),
                  pl.BlockSpec((1, A_HEAD_DIM), lambda i: (0, 0)),
                  pl.BlockSpec((1, A_HEAD_DIM), lambda i: (0, 0))],
        out_specs=[pl.BlockSpec((tm, A_WIDTH), lambda i: (i, 0)),
                   pl.BlockSpec((tm, kw), lambda i: (i, 0))],
        out_shape=[jax.ShapeDtypeStruct((t, A_WIDTH), BF16),
                   jax.ShapeDtypeStruct((t, kw), BF16)],
        compiler_params=_params(("parallel",)),
    )(p, p, cos, sin, g_qn, g_kn)


def _rope_tables(t):
    half = A_HEAD_DIM // 4
    freqs = ROPE_BASE ** (-np.arange(half, dtype=np.float64) / half)
    tok = np.arange(t)
    row = (tok // GRID_W).astype(np.float64)[:, None] * freqs
    col = (tok % GRID_W).astype(np.float64)[:, None] * freqs
    cos = np.concatenate([np.cos(row), np.cos(row), np.cos(col), np.cos(col)], axis=-1)
    sin = np.concatenate([-np.sin(row), np.sin(row), -np.sin(col), np.sin(col)], axis=-1)
    return jnp.asarray(cos.astype(np.float32)), jnp.asarray(sin.astype(np.float32))


def _softmax_pv(scores, values, sink_col):
    def lane_blocks(xs):
        return [x[:, i * LANES:(i + 1) * LANES] for x in xs for i in range(x.shape[1] // LANES)]

    m = jnp.maximum(jnp.max(functools.reduce(jnp.maximum, lane_blocks(scores)), axis=-1, keepdims=True), sink_col)
    probs = [jnp.exp(s - m) for s in scores]
    denom = jnp.exp(sink_col - m) + jnp.sum(functools.reduce(jnp.add, lane_blocks(probs)), axis=-1, keepdims=True)
    acc = None
    for p, v in zip(probs, values):
        pv = _dot(p.astype(BF16), v)
        acc = pv if acc is None else acc + pv
    return acc / denom


def _attn_kernel(*refs, local):
    if local:
        q_ref, kp_ref, ko_ref, kn_ref, vp_ref, vo_ref, vn_ref, kc_ref, vc_ref, sink_ref, o_ref = refs
    else:
        q_ref, kc_ref, vc_ref, sink_ref, o_ref = refs
    n = pl.program_id(0)
    nb = pl.num_programs(0)
    rows = A_GROUP * ATTN_BLOCK
    if local:
        qi = lax.broadcasted_iota(jnp.int32, (rows, ATTN_BLOCK), 0) % ATTN_BLOCK
        kj = lax.broadcasted_iota(jnp.int32, (rows, ATTN_BLOCK), 1)
        mask_prev = (kj >= qi) & (n > 0)
        mask_next = (kj <= qi) & (n < nb - 1)
    for kv in range(A_KV_HEADS):
        kc = slice(kv * A_HEAD_DIM, (kv + 1) * A_HEAD_DIM)
        heads = [kv * A_GROUP + g for g in range(A_GROUP)]
        q4 = jnp.concatenate([q_ref[:, h * A_HEAD_DIM:(h + 1) * A_HEAD_DIM] for h in heads], axis=0)
        sink_col = jnp.concatenate(
            [jnp.broadcast_to(sink_ref[h:h + 1, 0:1], (ATTN_BLOCK, 1)) for h in heads], axis=0)
        scores, values = [], []
        if local:
            scores.append(jnp.where(mask_prev, _dot_nt(q4, kp_ref[:, kc]), NEG_INF))
            scores.append(_dot_nt(q4, ko_ref[:, kc]))
            scores.append(jnp.where(mask_next, _dot_nt(q4, kn_ref[:, kc]), NEG_INF))
            values += [vp_ref[:, kc], vo_ref[:, kc], vn_ref[:, kc]]
        scores.append(_dot_nt(q4, kc_ref[:, kc]))
        values.append(vc_ref[:, kc])
        o = _softmax_pv(scores, values, sink_col).astype(o_ref.dtype)
        for g, h in enumerate(heads):
            o_ref[:, h * A_HEAD_DIM:(h + 1) * A_HEAD_DIM] = o[g * ATTN_BLOCK:(g + 1) * ATTN_BLOCK, :]


def _attention(q, k, p, k_ctx, p_ctx, sink_tab, local):
    t = q.shape[0]
    nb = t // ATTN_BLOCK
    l = k_ctx.shape[0]
    kw = A_KV_HEADS * A_HEAD_DIM
    vcol = 4352 // kw
    blk = lambda f: pl.BlockSpec((ATTN_BLOCK, kw), f)
    blkv = lambda f: pl.BlockSpec((ATTN_BLOCK, kw), lambda i: (f(i)[0], vcol))
    prev = lambda i: (jnp.maximum(i - 1, 0), 0)
    own = lambda i: (i, 0)
    nxt = lambda i: (jnp.minimum(i + 1, nb - 1), 0)
    in_specs = [pl.BlockSpec((ATTN_BLOCK, A_WIDTH), own)]
    args = [q]
    if local:
        in_specs += [blk(prev), blk(own), blk(nxt), blkv(prev), blkv(own), blkv(nxt)]
        args += [k, k, k, p, p, p]
    in_specs += [pl.BlockSpec((l, kw), lambda i: (0, 0)), pl.BlockSpec((l, kw), lambda i: (0, vcol)),
                 pl.BlockSpec((A_HEADS, LANES), lambda i: (0, 0))]
    args += [k_ctx, p_ctx, sink_tab]
    return pl.pallas_call(
        functools.partial(_attn_kernel, local=local),
        grid=(nb,),
        in_specs=in_specs,
        out_specs=pl.BlockSpec((ATTN_BLOCK, A_WIDTH), own),
        out_shape=jax.ShapeDtypeStruct((t, A_WIDTH), BF16),
        compiler_params=_params(("parallel",)),
    )(*args)


def _gla_kernel(q_ref, k_ref, v_ref, la_ref, s0_ref, o_ref, sf_ref, st_ref, *, rev, nchunk):
    @pl.when(pl.program_id(0) == 0)
    def _():
        st_ref[...] = s0_ref[...]

    r = nchunk * GLA_CHUNK
    sub = min(r, GLA_SUB)
    ii = lax.broadcasted_iota(jnp.int32, (sub, sub), 0)
    jj = lax.broadcasted_iota(jnp.int32, (sub, sub), 1)
    same_chunk = (ii // GLA_CHUNK) == (jj // GLA_CHUNK)
    tri = same_chunk & ((jj >= ii) if rev else (jj <= ii))
    b_all = la_ref[...]
    pos = lax.broadcasted_iota(jnp.int32, b_all.shape, 0) % GLA_CHUNK
    step = 1
    while step < GLA_CHUNK:
        if rev:
            b_all = b_all + jnp.where(pos < GLA_CHUNK - step, pltpu.roll(b_all, r - step, 0), 0.0)
        else:
            b_all = b_all + jnp.where(pos >= step, pltpu.roll(b_all, step, 0), 0.0)
        step *= 2
    heads = []
    for h in range(B_HEADS):
        kc = slice(h * B_DK, (h + 1) * B_DK)
        b = b_all[:, kc]
        b3 = b.reshape(nchunk, GLA_CHUNK, B_DK)
        b_end = b3[:, 0:1, :] if rev else b3[:, GLA_CHUNK - 1:GLA_CHUNK, :]
        k = k_ref[:, kc].astype(F32)
        qe = (q_ref[:, kc].astype(F32) * (B_DK ** -0.5) * jnp.exp(b)).astype(BF16)
        ke = (k * jnp.exp(-b)).astype(BF16)
        kd = (k.reshape(nchunk, GLA_CHUNK, B_DK) * jnp.exp(b_end - b3)).astype(BF16)
        v = v_ref[:, h * B_DV:(h + 1) * B_DV].astype(BF16)
        o_intra = []
        for s0 in range(0, r, sub):
            rs = slice(s0, s0 + sub)
            a = jnp.where(tri, _dot_nt(qe[rs, :], ke[rs, :]), 0.0).astype(BF16)
            o_intra.append(_dot(a, v[rs, :]))
        heads.append((qe, kd, v, jnp.concatenate(o_intra, axis=0), jnp.exp(b_end)))
    st = [st_ref[h] for h in range(B_HEADS)]
    for c in (range(nchunk - 1, -1, -1) if rev else range(nchunk)):
        rows = slice(c * GLA_CHUNK, (c + 1) * GLA_CHUNK)
        for h, (qe, kd, v, o_intra, decay) in enumerate(heads):
            o = o_intra[rows, :] + _dot_nt(qe[rows, :], st[h].astype(BF16))
            o_ref[0, rows, h * B_DV:(h + 1) * B_DV] = o.astype(o_ref.dtype)
            st[h] = st[h] * decay[c] + _dot_tn(v[rows, :], kd[c])
    for h in range(B_HEADS):
        st_ref[h] = st[h]
        sf_ref[h] = st[h]


def _gla_scan(p, la, s0, rev):
    t = p.shape[0]
    r = min(512, t)
    nblk = t // r
    rb = (lambda c: nblk - 1 - c) if rev else (lambda c: c)
    nk = B_HEADS * B_DK
    o, sf = pl.pallas_call(
        functools.partial(_gla_kernel, rev=rev, nchunk=r // GLA_CHUNK),
        grid=(nblk,),
        in_specs=[pl.BlockSpec((r, nk), lambda c: (rb(c), 3584 // nk)),
                  pl.BlockSpec((r, nk), lambda c: (rb(c), 3072 // nk)),
                  pl.BlockSpec((r, B_WIDTH), lambda c: (rb(c), 1024 // B_WIDTH)),
                  pl.BlockSpec((r, nk), lambda c: (rb(c), 1 if rev else 0)),
                  pl.BlockSpec((B_HEADS, B_DV, B_DK), lambda c: (0, 0, 0))],
        out_specs=[pl.BlockSpec((1, r, B_WIDTH), lambda c: (0, rb(c), 0)),
                   pl.BlockSpec((B_HEADS, B_DV, B_DK), lambda c: (0, 0, 0))],
        out_shape=[jax.ShapeDtypeStruct((1, t, B_WIDTH), BF16),
                   jax.ShapeDtypeStruct((B_HEADS, B_DV, B_DK), F32)],
        scratch_shapes=[pltpu.VMEM((B_HEADS, B_DV, B_DK), F32)],
        compiler_params=_params(("arbitrary",)),
    )(p, p, p, la, s0)
    return o[0], sf


def _outproj_kernel(oa_ref, of_ref, ob_ref, og_ref, gg_ref, w_ref, x_ref, gt_ref, o_ref, mix_ref, *, row):
    @pl.when(pl.program_id(1) == 0)
    def _():
        mix_ref[:, 0:A_WIDTH] = oa_ref[...]
        for h in range(B_HEADS):
            cols = slice(h * B_DV, (h + 1) * B_DV)
            o = of_ref[:, cols].astype(F32) + ob_ref[:, cols].astype(F32)
            ms = jnp.mean(o * o, axis=-1, keepdims=True)
            y = o * lax.rsqrt(ms + NORM_EPS) * gg_ref[...]
            mix_ref[:, A_WIDTH + h * B_DV:A_WIDTH + (h + 1) * B_DV] = (y * _silu(og_ref[:, cols].astype(F32))).astype(BF16)

    o_ref[...] = x_ref[...] + gt_ref[row:row + 1, :] * _dot(mix_ref[...], w_ref[...])


def _outproj(oa, o_f, o_b, p, g_gla, w_out, x, ada_l, row):
    t, d = x.shape
    tm = min(512, t)
    tn = 1024
    return pl.pallas_call(
        functools.partial(_outproj_kernel, row=row),
        grid=(t // tm, d // tn),
        in_specs=[pl.BlockSpec((tm, A_WIDTH), lambda i, j: (i, 0)),
                  pl.BlockSpec((tm, B_WIDTH), lambda i, j: (i, 0)),
                  pl.BlockSpec((tm, B_WIDTH), lambda i, j: (i, 0)),
                  pl.BlockSpec((tm, B_WIDTH), lambda i, j: (i, 2048 // B_WIDTH)),
                  pl.BlockSpec((1, B_DV), lambda i, j: (0, 0)),
                  pl.BlockSpec((A_WIDTH + B_WIDTH, tn), lambda i, j: (0, j)),
                  pl.BlockSpec((tm, tn), lambda i, j: (i, j)),
                  pl.BlockSpec((8, tn), lambda i, j: (0, 2 * (d // tn) + j))],
        out_specs=pl.BlockSpec((tm, tn), lambda i, j: (i, j)),
        out_shape=jax.ShapeDtypeStruct((t, d), F32),
        scratch_shapes=[pltpu.VMEM((tm, A_WIDTH + B_WIDTH), BF16)],
        compiler_params=_params(("parallel", "arbitrary")),
    )(oa, o_f, o_b, p, g_gla, w_out, x, ada_l)


def _ffn_kernel(x_ref, g_ref, sh_ref, sc_ref, gt_ref, wg_ref, wu_ref, wd_ref, o_ref, hn_ref, *, row):
    f = pl.program_id(1)

    @pl.when(f == 0)
    def _():
        hn_ref[...] = _norm_mod(x_ref[...], g_ref[...], sh_ref[row:row + 1, :], sc_ref[row:row + 1, :]).astype(BF16)
        o_ref[...] = jnp.zeros_like(o_ref)

    h = hn_ref[...]
    a = (_silu(_dot(h, wg_ref[0].astype(BF16))) * _dot(h, wu_ref[0].astype(BF16))).astype(BF16)
    o_ref[...] += _dot(a, wd_ref[0].astype(BF16))

    @pl.when(f == pl.num_programs(1) - 1)
    def _():
        o_ref[...] = x_ref[...] + gt_ref[row:row + 1, :] * o_ref[...]


def _ffn(x, ada_l, row, g, wg, wu, wd, lyr):
    t, d = x.shape
    ff = wg.shape[2]
    tm = min(1024, t)
    tf = 256
    return pl.pallas_call(
        functools.partial(_ffn_kernel, row=row),
        grid=(t // tm, ff // tf),
        in_specs=[pl.BlockSpec((tm, d), lambda i, f: (i, 0)),
                  pl.BlockSpec((1, d), lambda i, f: (0, 0)),
                  _ada_spec(3), _ada_spec(4), _ada_spec(5),
                  pl.BlockSpec((1, d, tf), lambda i, f: (lyr, 0, f)),
                  pl.BlockSpec((1, d, tf), lambda i, f: (lyr, 0, f)),
                  pl.BlockSpec((1, tf, d), lambda i, f: (lyr, f, 0))],
        out_specs=pl.BlockSpec((tm, d), lambda i, f: (i, 0)),
        out_shape=jax.ShapeDtypeStruct((t, d), F32),
        scratch_shapes=[pltpu.VMEM((tm, d), BF16)],
        compiler_params=_params(("parallel", "arbitrary"), MOE_VMEM_LIMIT),
    )(x, g, ada_l, ada_l, ada_l, wg, wu, wd)


def _pool_router_kernel(x_ref, xp_ref, xn_ref, g_ref, sh_ref, sc_ref, gt_ref, w_ref, ps_ref,
                        g2_ref, sh2_ref, sc2_ref, wr_ref, o_ref, hn_ref, sel_ref, gate_ref, hb_ref, *, row, t):
    i = pl.program_id(0)
    tm = x_ref.shape[0]
    g = g_ref[...]
    sh = sh_ref[row:row + 1, :]
    sc = sc_ref[row:row + 1, :]
    hb_ref[0:POOL_HALO, :] = jnp.where(i > 0, _norm_mod(xp_ref[...], g, sh, sc), 0.0)
    hb_ref[POOL_HALO:POOL_HALO + tm, :] = _norm_mod(x_ref[...], g, sh, sc)
    hb_ref[POOL_HALO + tm:, :] = jnp.where(i < pl.num_programs(0) - 1, _norm_mod(xn_ref[...], g, sh, sc), 0.0)
    tpos = i * tm + lax.broadcasted_iota(jnp.int32, (tm, 1), 0)
    gc = D_MODEL // POOL_GROUPS
    for grp, w in enumerate(POOL_WINDOWS):
        cols = slice(grp * gc, (grp + 1) * gc)
        acc = None
        for dlt in range(-(w // 2), w - w // 2):
            piece = hb_ref[POOL_HALO + dlt:POOL_HALO + dlt + tm, cols]
            acc = piece if acc is None else acc + piece
        cnt = jnp.minimum(tpos + (w - w // 2), t) - jnp.maximum(tpos - w // 2, 0)
        pooled = acc / cnt.astype(F32) - hb_ref[POOL_HALO:POOL_HALO + tm, cols]
        y = _dot(pooled.astype(BF16), w_ref[grp])
        o_ref[:, cols] = x_ref[:, cols] + gt_ref[row:row + 1, cols] * (y * ps_ref[:, cols])

    h = _norm_mod(o_ref[...], g2_ref[...], sh2_ref[row:row + 1, :], sc2_ref[row:row + 1, :])
    hn_ref[...] = h.astype(hn_ref.dtype)
    lane = lax.broadcasted_iota(jnp.int32, (h.shape[0], LANES), 1)
    lane_f = lane.astype(F32)
    logits = jnp.where(lane < N_EXPERTS, _dot(h, wr_ref[...], HIGHEST), -jnp.inf)
    m1 = jnp.max(logits, axis=-1, keepdims=True)
    i1 = jnp.min(jnp.where(logits == m1, lane_f, float(LANES)), axis=-1, keepdims=True)
    rest = jnp.where(lane_f == i1, -jnp.inf, logits)
    m2 = jnp.max(rest, axis=-1, keepdims=True)
    i2 = jnp.min(jnp.where(rest == m2, lane_f, float(LANES)), axis=-1, keepdims=True)
    e2 = jnp.exp(m2 - m1)
    den = 1.0 + e2
    sel_ref[...] = jnp.where(lane == 0, i1, jnp.where(lane == 1, i2, 0.0)).astype(jnp.int32)
    gate_ref[...] = jnp.where(lane == 0, 1.0 / den, jnp.where(lane == 1, e2 / den, 0.0))


def _pool_router(x, ada_l, row, g1, g2, w_pool, pool_scale, w_router_pad):
    t, d = x.shape
    tm = min(256, t)
    nh = t // POOL_HALO
    gc = d // POOL_GROUPS
    rows = lambda i: (i, 0)
    const = lambda i: (0, 0)
    return pl.pallas_call(
        functools.partial(_pool_router_kernel, row=row, t=t),
        grid=(t // tm,),
        in_specs=[pl.BlockSpec((tm, d), rows),
                  pl.BlockSpec((POOL_HALO, d), lambda i: (jnp.maximum(i * (tm // POOL_HALO) - 1, 0), 0)),
                  pl.BlockSpec((POOL_HALO, d), lambda i: (jnp.minimum((i + 1) * (tm // POOL_HALO), nh - 1), 0)),
                  pl.BlockSpec((1, d), const),
                  _ada_spec(0), _ada_spec(1), _ada_spec(2),
                  pl.BlockSpec((POOL_GROUPS, gc, gc), lambda i: (0, 0, 0)),
                  pl.BlockSpec((1, d), const),
                  pl.BlockSpec((1, d), const),
                  _ada_spec(3), _ada_spec(4),
                  pl.BlockSpec((d, LANES), const)],
        out_specs=[pl.BlockSpec((tm, d), rows), pl.BlockSpec((tm, d), rows),
                   pl.BlockSpec((tm, LANES), rows), pl.BlockSpec((tm, LANES), rows)],
        out_shape=[jax.ShapeDtypeStruct((t, d), F32),
                   jax.ShapeDtypeStruct((t, d), BF16),
                   jax.ShapeDtypeStruct((t, LANES), jnp.int32),
                   jax.ShapeDtypeStruct((t, LANES), F32)],
        scratch_shapes=[pltpu.VMEM((tm + 2 * POOL_HALO, d), F32)],
        compiler_params=_params(("parallel",)),
    )(x, x, x, g1, ada_l, ada_l, ada_l, w_pool, pool_scale, g2, ada_l, ada_l, w_router_pad)


MOE_TILE = 1152
MOE_SUB = 128
MOE_SUBS = MOE_TILE // MOE_SUB
GATHER_SUB = 384
GATHER_SUBS = MOE_TILE // GATHER_SUB
SRC_BLK = 256
Y_BLK = 256
CMB_TILE = 256
MOE_TF = 512
GATHER_BUFS = 3


def _moe_ffn_kernel(te_ref, tn_ref, tlo_ref, tnb_ref, trk_ref, cb_ref, posb_ref, hn_hbm, wg_ref, wu_ref, wd_ref,
                    y_ref, xb_ref, acc_ref, hbuf_ref, sem):
    i = pl.program_id(0)
    f = pl.program_id(1)
    nsub = tn_ref[i]

    @pl.when(f == 0)
    def _():
        xb_ref[...] = jnp.zeros_like(xb_ref)
        acc_ref[...] = jnp.zeros_like(acc_ref)
        row_id = lax.broadcasted_iota(jnp.int32, (GATHER_SUB, SRC_BLK), 0) + i * MOE_TILE
        rows_used = nsub * MOE_SUB
        lo = tlo_ref[i]
        n = tnb_ref[i]
        rank0 = trk_ref[i]
        cb_base = te_ref[i] * (posb_ref.shape[0] + 1)

        def blk_copy(b, slot):
            return pltpu.make_async_copy(hn_hbm.at[pl.ds((lo + b) * SRC_BLK, SRC_BLK), :], hbuf_ref.at[slot],
                                         sem.at[slot])

        for j in range(GATHER_BUFS - 1):
            @pl.when(j < n)
            def _():
                blk_copy(j, j).start()

        def body(b, carry):
            slot = b % GATHER_BUFS
            blk_copy(b, slot).wait()
            ahead = b + GATHER_BUFS - 1

            @pl.when(ahead < n)
            def _():
                blk_copy(ahead, ahead % GATHER_BUFS).start()

            blk = lo + b
            pos = posb_ref[blk]
            pos1, pos2 = pos[0:1, :], pos[1:2, :]
            before = cb_ref[cb_base + blk]
            through = cb_ref[cb_base + blk + 1]
            for q in range(GATHER_SUBS):
                @pl.when((q * GATHER_SUB < rows_used) & (before < rank0 + (q + 1) * GATHER_SUB)
                         & (through > rank0 + q * GATHER_SUB))
                def _():
                    rows = slice(q * GATHER_SUB, (q + 1) * GATHER_SUB)
                    rid = row_id + q * GATHER_SUB
                    onehot = jnp.where((pos1 == rid) | (pos2 == rid), 1.0, 0.0).astype(BF16)
                    xb_ref[rows, :] += _dot(onehot, hbuf_ref[slot]).astype(BF16)
            return carry

        lax.fori_loop(0, n, body, 0)

    for k in range(1, MOE_SUBS + 1):
        @pl.when(nsub == k)
        def _():
            rows = slice(0, k * MOE_SUB)
            h = xb_ref[rows, :]
            a = (_silu(_dot(h, wg_ref[0, 0].astype(BF16))) * _dot(h, wu_ref[0, 0].astype(BF16))).astype(BF16)
            acc_ref[rows, :] += _dot(a, wd_ref[0, 0].astype(BF16))

    @pl.when(f == pl.num_programs(1) - 1)
    def _():
        y_ref[...] = acc_ref[...].astype(y_ref.dtype)


def _moe_ffn(hn, posb, tile_expert, tile_subs, tile_blk_lo, tile_blk_n, tile_rank, blk_counts, wg, wu, wd, lyr):
    d = hn.shape[1]
    tm = MOE_TILE
    ff = wg.shape[3]
    tf = MOE_TF
    nf = ff // tf
    nt = tile_expert.shape[0]
    fidx = lambda i, f, tn: jnp.where(tn[i] > 0, f, nf - 1)
    return pl.pallas_call(
        _moe_ffn_kernel,
        grid_spec=pltpu.PrefetchScalarGridSpec(
            num_scalar_prefetch=6,
            grid=(nt, nf),
            in_specs=[pl.BlockSpec(posb.shape, lambda i, f, te, tn, *_: (0, 0, 0)),
                      pl.BlockSpec(memory_space=pl.ANY),
                      pl.BlockSpec((1, 1, d, tf), lambda i, f, te, tn, *_: (lyr, te[i], 0, fidx(i, f, tn))),
                      pl.BlockSpec((1, 1, d, tf), lambda i, f, te, tn, *_: (lyr, te[i], 0, fidx(i, f, tn))),
                      pl.BlockSpec((1, 1, tf, d), lambda i, f, te, tn, *_: (lyr, te[i], fidx(i, f, tn), 0))],
            out_specs=pl.BlockSpec((tm, d), lambda i, f, te, tn, *_: (i, 0)),
            scratch_shapes=[pltpu.VMEM((tm, d), BF16), pltpu.VMEM((tm, d), F32),
                            pltpu.VMEM((GATHER_BUFS, SRC_BLK, d), BF16),
                            pltpu.SemaphoreType.DMA((GATHER_BUFS,))]),
        out_shape=jax.ShapeDtypeStruct((nt * tm, d), BF16),
        compiler_params=_params(("arbitrary", "arbitrary"), MOE_VMEM_LIMIT),
    )(tile_expert, tile_subs, tile_blk_lo, tile_blk_n, tile_rank, blk_counts, posb, hn, wg, wu, wd)


CMB_SLOTS = [(e, j) for e in range(N_EXPERTS) for j in range(2)]


def _combine_kernel(fb_ref, nb_ref, x_ref, pos_ref, gate_ref, gt_ref, y_hbm, o_ref, acc_ref, ybuf_ref, sem,
                    *, row, tile_off):
    base = (tile_off + pl.program_id(0)) * N_EXPERTS
    tm = x_ref.shape[0]
    pos1, pos2 = pos_ref[:, 0:1], pos_ref[:, 1:2]
    g1, g2 = gate_ref[:, 0:1], gate_ref[:, 1:2]
    col = lax.broadcasted_iota(jnp.int32, (tm, Y_BLK), 1)

    def blk(s):
        e, j = CMB_SLOTS[s]
        return fb_ref[base + e] + j

    def used(s):
        e, j = CMB_SLOTS[s]
        return nb_ref[base + e] > j

    def blk_copy(s):
        return pltpu.make_async_copy(y_hbm.at[pl.ds(blk(s) * Y_BLK, Y_BLK), :], ybuf_ref.at[s], sem.at[s])

    for s in range(len(CMB_SLOTS)):
        @pl.when(used(s))
        def _():
            blk_copy(s).start()

    acc_ref[...] = jnp.zeros_like(acc_ref)
    for s in range(len(CMB_SLOTS)):
        @pl.when(used(s))
        def _():
            blk_copy(s).wait()
            off = blk(s) * Y_BLK
            w = jnp.where(pos1 - off == col, g1, 0.0) + jnp.where(pos2 - off == col, g2, 0.0)
            acc_ref[...] += _dot(w.astype(BF16), ybuf_ref[s])

    o_ref[...] = x_ref[...] + gt_ref[row:row + 1, :] * acc_ref[...]


def _combine(x, y, pos, gates, first_blk, num_blk, tok_off, ada_l, row):
    t, d = x.shape
    tm = CMB_TILE
    tile_off = tok_off // tm
    return pl.pallas_call(
        functools.partial(_combine_kernel, row=row, tile_off=tile_off),
        grid_spec=pltpu.PrefetchScalarGridSpec(
            num_scalar_prefetch=2,
            grid=(t // tm,),
            in_specs=[pl.BlockSpec((tm, d), lambda i, fb, nb: (i, 0)),
                      pl.BlockSpec((tm, 2), lambda i, fb, nb: (tile_off + i, 0)),
                      pl.BlockSpec((tm, LANES), lambda i, fb, nb: (i, 0)),
                      _ada_spec(5),
                      pl.BlockSpec(memory_space=pl.ANY)],
            out_specs=pl.BlockSpec((tm, d), lambda i, fb, nb: (i, 0)),
            scratch_shapes=[pltpu.VMEM((tm, d), F32), pltpu.VMEM((len(CMB_SLOTS), Y_BLK, d), BF16),
                            pltpu.SemaphoreType.DMA((len(CMB_SLOTS),))]),
        out_shape=jax.ShapeDtypeStruct((t, d), F32),
        compiler_params=_params(("arbitrary",)),
    )(first_blk, num_blk, x, pos, gates, ada_l, y)


def _route(sel):
    tm = MOE_TILE
    n_tok = sel.shape[0]
    n_asg = 2 * n_tok
    nt = -(-n_asg // tm) + N_EXPERTS
    e_flat = sel.reshape(-1)
    onehot = (e_flat[:, None] == jnp.arange(N_EXPERTS, dtype=jnp.int32)[None, :]).astype(jnp.int32)
    csum = jnp.cumsum(onehot, axis=0)
    count = csum[-1]
    rank = jnp.sum((csum - onehot) * onehot, axis=1)
    tiles_e = (count + tm - 1) // tm
    tile_end = jnp.cumsum(tiles_e)
    base = (tile_end - tiles_e) * tm
    pos = jnp.sum(onehot * base[None, :], axis=1) + rank
    tile_id = jnp.arange(nt, dtype=jnp.int32)
    used = tile_end[-1]
    owner = lambda i: jnp.minimum(jnp.sum((i[:, None] >= tile_end[None, :]).astype(jnp.int32), axis=1), N_EXPERTS - 1)
    tile_expert = owner(jnp.minimum(tile_id, used - 1))
    rows_in_tile = jnp.clip((base + count)[tile_expert] - tile_id * tm, 0, tm)
    tile_subs = jnp.where(tile_id < used, (rows_in_tile + MOE_SUB - 1) // MOE_SUB, 0)
    n_blk = n_tok // SRC_BLK
    blk_counts = jnp.concatenate([jnp.zeros((1, N_EXPERTS), jnp.int32),
                                  csum.reshape(n_blk, 2 * SRC_BLK, N_EXPERTS)[:, -1, :]], axis=0).T
    tile_rank = tile_id * tm - base[tile_expert]
    counts_t = blk_counts[tile_expert]
    tile_blk_lo = jnp.sum((counts_t[:, 1:] <= tile_rank[:, None]).astype(jnp.int32), axis=1)
    tile_blk_hi = jnp.sum((counts_t[:, :-1] < (tile_rank + rows_in_tile)[:, None]).astype(jnp.int32), axis=1)
    tile_blk_n = jnp.where(tile_subs > 0, jnp.maximum(tile_blk_hi - tile_blk_lo, 0), 0)
    tile_blk_lo = jnp.minimum(tile_blk_lo, n_blk - 1)
    posb = jnp.pad(pos.reshape(n_blk, SRC_BLK, 2).transpose(0, 2, 1), ((0, 0), (0, 6), (0, 0)), constant_values=-1)
    ntt = n_tok // CMB_TILE
    through = csum.reshape(ntt, 2 * CMB_TILE, N_EXPERTS)[:, -1, :]
    before = jnp.concatenate([jnp.zeros((1, N_EXPERTS), jnp.int32), through[:-1]], axis=0)
    lo = base[None, :] + before
    hi = base[None, :] + through - 1
    first_blk = lo // Y_BLK
    num_blk = jnp.where(through > before, hi // Y_BLK - first_blk + 1, 0)
    return (posb, pos.reshape(n_tok, 2), tile_expert, tile_subs, tile_blk_lo, tile_blk_n, tile_rank,
            blk_counts.reshape(-1), first_blk.reshape(-1), num_blk.reshape(-1))


def _moe(streams, ada_l, wg, wu, wd, lyr):
    hs = [s[2] for s in streams]
    sels = [s[3][:, :2] for s in streams]
    n_tok = sum(h.shape[0] for h in hs)
    pad = -n_tok % SRC_BLK
    if pad:
        hs.append(jnp.zeros((pad, D_MODEL), BF16))
    h_all = hs[0] if len(hs) == 1 else jnp.concatenate(hs, axis=0)
    sel_all = sels[0] if len(sels) == 1 else jnp.concatenate(sels, axis=0)
    (posb, pos, tile_expert, tile_subs, tile_blk_lo, tile_blk_n, tile_rank, blk_counts,
     first_blk, num_blk) = _route(sel_all)
    y = _moe_ffn(h_all, posb, tile_expert, tile_subs, tile_blk_lo, tile_blk_n, tile_rank, blk_counts,
                 wg, wu, wd, lyr)
    outs, off = [], 0
    for x, row, _, _, gt in streams:
        outs.append(_combine(x, y, pos, gt, first_blk, num_blk, off, ada_l, row))
        off += x.shape[0]
    return outs


def _even_layer(x, ctx, ada_l, g1, g2, w_in, g_qn, g_kn, sink, w_gate_up, b_gate_up, g_gla, w_out,
                w_ffn_gate, w_ffn_up, w_ffn_down, lyr, rope_tabs, ctx_out):
    d = D_MODEL
    w_main = jnp.concatenate([w_in[:, 2080:3104], w_in[:, 1024:2048], w_in[:, 3616:4640], w_in[:, 512:1024],
                              w_in[:, 3104:3616], w_in[:, 0:256], w_in[:, 256:512]], axis=1).astype(BF16)
    w_gate = jnp.pad(w_in[:, 2048:2080], ((0, 0), (0, LANES - 2 * GATE_RANK))).astype(BF16)
    nk = B_HEADS * B_DK
    w_up = jnp.zeros((LANES, 2 * nk), F32)
    w_up = w_up.at[0:GATE_RANK, 0:nk].set(w_gate_up[0]).at[GATE_RANK:2 * GATE_RANK, nk:].set(w_gate_up[1])
    b_up = b_gate_up.reshape(1, 2 * nk)
    w_out_b = w_out.astype(BF16)
    gq, gk, gg = g_qn.reshape(1, -1), g_kn.reshape(1, -1), g_gla.reshape(1, -1)
    sink_tab = jnp.broadcast_to(sink[:, None], (A_HEADS, LANES))
    cos, sin = rope_tabs

    pc, la_c = _inproj(ctx, ada_l, 1, g1, w_main, w_gate, w_up, b_up)
    qc, kc = _qkprep(pc, cos, sin, gq, gk, rope=False)
    s0 = jnp.zeros((B_HEADS, B_DV, B_DK), F32)
    oc_f, s_fwd = _gla_scan(pc, la_c, s0, rev=False)
    oc_b, s_bwd = _gla_scan(pc, la_c, s0, rev=True)

    px, la_x = _inproj(x, ada_l, 0, g1, w_main, w_gate, w_up, b_up)
    qx, kx = _qkprep(px, cos, sin, gq, gk, rope=True)
    oa = _attention(qx, kx, px, kc, pc, sink_tab, local=True)
    ox_f, _ = _gla_scan(px, la_x, s_fwd, rev=False)
    ox_b, _ = _gla_scan(px, la_x, s_bwd, rev=True)
    x = _outproj(oa, ox_f, ox_b, px, gg, w_out_b, x, ada_l, 0)
    x = _ffn(x, ada_l, 0, g2, w_ffn_gate, w_ffn_up, w_ffn_down, lyr)
    if ctx_out:
        oa_c = _attention(qc, None, None, kc, pc, sink_tab, local=False)
        ctx = _outproj(oa_c, oc_f, oc_b, pc, gg, w_out_b, ctx, ada_l, 1)
        ctx = _ffn(ctx, ada_l, 1, g2, w_ffn_gate, w_ffn_up, w_ffn_down, lyr)
    return x, ctx


def _odd_layer(x, ctx, ada_l, g1, g2, w_pool, pool_scale, w_router, w_exp_gate, w_exp_up, w_exp_down, lyr, ctx_out):
    w_pool_b = w_pool.astype(BF16)
    ps = pool_scale.reshape(1, -1)
    wr = jnp.pad(w_router, ((0, 0), (0, LANES - N_EXPERTS)))
    streams = [(0,) + tuple(_pool_router(x, ada_l, 0, g1, g2, w_pool_b, ps, wr))]
    if ctx_out:
        streams.append((1,) + tuple(_pool_router(ctx, ada_l, 1, g1, g2, w_pool_b, ps, wr)))
    outs = _moe([(s[1], s[0], s[2], s[3], s[4]) for s in streams], ada_l, w_exp_gate, w_exp_up, w_exp_down, lyr)
    return outs[0], (outs[1] if ctx_out else ctx)


def kernel(x, c, ctx, c_ctx, w_ada, b_ada, norm_g, w_in, g_qn, g_kn, attn_sink, w_gate_up, b_gate_up, g_gla, w_out,
           w_ffn_gate, w_ffn_up, w_ffn_down, w_pool, pool_scale, w_router, w_exp_gate, w_exp_up, w_exp_down):
    depth = w_ada.shape[0]
    xs = x[0]
    cs = ctx[0]
    t = xs.shape[0]
    cond = jnp.zeros((8, D_MODEL), F32).at[0].set(c[0]).at[1].set(c_ctx)
    ada = _ada_all(cond, w_ada, b_ada)
    rope_tabs = _rope_tables(t)
    for l in range(depth):
        ctx_later = any(j % 2 == 0 for j in range(l + 1, depth))
        g1 = norm_g[l, 0].reshape(1, -1)
        g2 = norm_g[l, 1].reshape(1, -1)
        if l % 2 == 0:
            e = l // 2
            xs, cs = _even_layer(xs, cs, ada[l], g1, g2, w_in[e], g_qn[e], g_kn[e], attn_sink[e], w_gate_up[e],
                                 b_gate_up[e], g_gla[e], w_out[e], w_ffn_gate, w_ffn_up, w_ffn_down, e,
                                 rope_tabs, ctx_later)
        else:
            o = l // 2
            xs, cs = _odd_layer(xs, cs, ada[l], g1, g2, w_pool[o], pool_scale[o], w_router[o],
                                w_exp_gate, w_exp_up, w_exp_down, o, ctx_later)
    return xs[None]
```

```python
import functools

import jax
import jax.numpy as jnp
import numpy as np
from jax import lax
from jax.experimental import pallas as pl
from jax.experimental.pallas import tpu as pltpu

F32 = jnp.float32
BF16 = jnp.bfloat16
HIGHEST = lax.Precision.HIGHEST

D_MODEL = 2048
GRID_W = 64
A_HEADS = 8
A_KV_HEADS = 2
A_GROUP = 4
A_HEAD_DIM = 128
A_WIDTH = A_HEADS * A_HEAD_DIM
ATTN_BLOCK = 128
ROPE_BASE = 10000.0
B_HEADS = 4
B_DV = 256
B_DK = 128
B_WIDTH = B_HEADS * B_DV
GATE_RANK = 16
GATE_TAU = 16.0
GLA_CHUNK = 64
GLA_SUB = 256
POOL_GROUPS = 4
POOL_WINDOWS = (2, 4, 8, 16)
POOL_HALO = 8
N_EXPERTS = 8
NORM_EPS = 1e-6
NEG_INF = -1e30
LANES = 128

P_COLS = 4608
VMEM_LIMIT = 56 * 1024 * 1024
MOE_VMEM_LIMIT = 62 * 1024 * 1024


def _params(sem, vmem=VMEM_LIMIT):
    return pltpu.CompilerParams(dimension_semantics=sem, vmem_limit_bytes=vmem)


def _dot(a, b, precision=None):
    return jnp.dot(a, b, preferred_element_type=F32, precision=precision)


def _dot_nt(a, b):
    return lax.dot_general(a, b, (((1,), (1,)), ((), ())), preferred_element_type=F32)


def _dot_tn(a, b):
    return lax.dot_general(a, b, (((0,), (0,)), ((), ())), preferred_element_type=F32)


def _silu(x):
    return x * jax.nn.sigmoid(x)


def _norm_mod(x, g, shift, scale):
    ms = jnp.mean(x * x, axis=-1, keepdims=True)
    return (x * lax.rsqrt(ms + NORM_EPS) * g) * (1.0 + scale) + shift


def _ada_kernel(cond_ref, w_ref, b_ref, o_ref):
    s = _silu(cond_ref[...])
    s_hi = s.astype(BF16)
    s_lo = (s - s_hi.astype(F32)).astype(BF16)
    w = w_ref[0].astype(BF16)
    o_ref[0] = _dot(s_hi, w) + _dot(s_lo, w) + b_ref[0]


def _ada_all(cond, w_ada, b_ada):
    depth, d, n = w_ada.shape
    tn = 1024
    return pl.pallas_call(
        _ada_kernel,
        grid=(depth, n // tn),
        in_specs=[pl.BlockSpec((8, d), lambda l, j: (0, 0)),
                  pl.BlockSpec((1, d, tn), lambda l, j: (l, 0, j)),
                  pl.BlockSpec((1, 1, tn), lambda l, j: (l, 0, j))],
        out_specs=pl.BlockSpec((1, 8, tn), lambda l, j: (l, 0, j)),
        out_shape=jax.ShapeDtypeStruct((depth, 8, n), F32),
        compiler_params=_params(("parallel", "parallel")),
    )(cond, w_ada, b_ada.reshape(depth, 1, n))


def _ada_spec(k):
    return pl.BlockSpec((8, D_MODEL), lambda *_: (0, k))


def _inproj_kernel(x_ref, g_ref, sh_ref, sc_ref, w_ref, wg_ref, wup_ref, bup_ref, p_ref, la_ref, hn_ref, *, row):
    @pl.when(pl.program_id(1) == 0)
    def _():
        h = _norm_mod(x_ref[...], g_ref[...], sh_ref[row:row + 1, :], sc_ref[row:row + 1, :]).astype(BF16)
        hn_ref[...] = h
        lr = _dot(h, wg_ref[...])
        z = _dot(lr, wup_ref[...], HIGHEST) + bup_ref[...]
        la_ref[...] = (jnp.minimum(z, 0.0) - jnp.log1p(jnp.exp(-jnp.abs(z)))) * (1.0 / GATE_TAU)

    p_ref[...] = _dot(hn_ref[...], w_ref[...]).astype(p_ref.dtype)


def _inproj(x, ada_l, row, g, w_main, w_gate, w_up, b_up):
    t, d = x.shape
    tm = min(512, t)
    tn = 1536
    nla = 2 * B_HEADS * B_DK
    return pl.pallas_call(
        functools.partial(_inproj_kernel, row=row),
        grid=(t // tm, P_COLS // tn),
        in_specs=[pl.BlockSpec((tm, d), lambda i, j: (i, 0)),
                  pl.BlockSpec((1, d), lambda i, j: (0, 0)),
                  _ada_spec(0), _ada_spec(1),
                  pl.BlockSpec((d, tn), lambda i, j: (0, j)),
                  pl.BlockSpec((d, LANES), lambda i, j: (0, 0)),
                  pl.BlockSpec((LANES, nla), lambda i, j: (0, 0)),
                  pl.BlockSpec((1, nla), lambda i, j: (0, 0))],
        out_specs=[pl.BlockSpec((tm, tn), lambda i, j: (i, j)),
                   pl.BlockSpec((tm, nla), lambda i, j: (i, 0))],
        out_shape=[jax.ShapeDtypeStruct((t, P_COLS), BF16),
                   jax.ShapeDtypeStruct((t, nla), F32)],
        scratch_shapes=[pltpu.VMEM((tm, d), BF16)],
        compiler_params=_params(("parallel", "arbitrary")),
    )(x, g, ada_l, ada_l, w_main, w_gate, w_up, b_up)


def _qkprep_kernel(q_ref, k_ref, cos_ref, sin_ref, gq_ref, gk_ref, qo_ref, ko_ref, *, rope):
    tm = q_ref.shape[0]
    lane = lax.broadcasted_iota(jnp.int32, (tm, A_HEAD_DIM), 1)
    first_half = (lane % 64) < 32

    def prep(xh, g, scale):
        ms = jnp.mean(xh * xh, axis=-1, keepdims=True)
        y = xh * lax.rsqrt(ms + NORM_EPS) * g
        if rope:
            partner = jnp.where(first_half, pltpu.roll(y, 96, 1), pltpu.roll(y, 32, 1))
            y = y * cos_ref[...] + partner * sin_ref[...]
        return (y * scale).astype(BF16)

    for h in range(A_HEADS):
        cols = slice(h * A_HEAD_DIM, (h + 1) * A_HEAD_DIM)
        qo_ref[:, cols] = prep(q_ref[:, cols].astype(F32), gq_ref[...], A_HEAD_DIM ** -0.5)
    for h in range(A_KV_HEADS):
        cols = slice(h * A_HEAD_DIM, (h + 1) * A_HEAD_DIM)
        ko_ref[:, cols] = prep(k_ref[:, cols].astype(F32), gk_ref[...], 1.0)


def _qkprep(p, cos, sin, g_qn, g_kn, rope):
    t = p.shape[0]
    tm = min(512, t)
    kw = A_KV_HEADS * A_HEAD_DIM
    return pl.pallas_call(
        functools.partial(_qkprep_kernel, rope=rope),
        grid=(t // tm,),
        in_specs=[pl.BlockSpec((tm, A_WIDTH), lambda i: (i, 0)),
                  pl.BlockSpec((tm, kw), lambda i: (i, 4096 // kw)),
                  pl.BlockSpec((tm, A_HEAD_DIM), lambda i: (i, 0)),
                  pl.BlockSpec((tm, A_HEAD_DIM), lambda i: (i, 0)),
                  pl.BlockSpec((1, A_HEAD_DIM), lambda i: (0, 0)),
                  pl.BlockSpec((1, A_HEAD_DIM), lambda i: (0, 0))],
        out_specs=[pl.BlockSpec((tm, A_WIDTH), lambda i: (i, 0)),
                   pl.BlockSpec((tm, kw), lambda i: (i, 0))],
        out_shape=[jax.ShapeDtypeStruct((t, A_WIDTH), BF16),
                   jax.ShapeDtypeStruct((t, kw), BF16)],
        compiler_params=_params(("parallel",)),
    )(p, p, cos, sin, g_qn, g_kn)


def _rope_tables(t):
    half = A_HEAD_DIM // 4
    freqs = ROPE_BASE ** (-np.arange(half, dtype=np.float64) / half)
    tok = np.arange(t)
    row = (tok // GRID_W).astype(np.float64)[:, None] * freqs
    col = (tok % GRID_W).astype(np.float64)[:, None] * freqs
    cos = np.concatenate([np.cos(row), np.cos(row), np.cos(col), np.cos(col)], axis=-1)
    sin = np.concatenate([-np.sin(row), np.sin(row), -np.sin(col), np.sin(col)], axis=-1)
    return jnp.asarray(cos.astype(np.float32)), jnp.asarray(sin.astype(np.float32))


def _softmax_pv(scores, values, sink_col):
    def lane_blocks(xs):
        return [x[:, i * LANES:(i + 1) * LANES] for x in xs for i in range(x.shape[1] // LANES)]

    m = jnp.maximum(jnp.max(functools.reduce(jnp.maximum, lane_blocks(scores)), axis=-1, keepdims=True), sink_col)
    probs = [jnp.exp(s - m) for s in scores]
    denom = jnp.exp(sink_col - m) + jnp.sum(functools.reduce(jnp.add, lane_blocks(probs)), axis=-1, keepdims=True)
    acc = None
    for p, v in zip(probs, values):
        pv = _dot(p.astype(BF16), v)
        acc = pv if acc is None else acc + pv
    return acc / denom


def _attn_kernel(*refs, local):
    if local:
        q_ref, kp_ref, ko_ref, kn_ref, vp_ref, vo_ref, vn_ref, kc_ref, vc_ref, sink_ref, o_ref = refs
    else:
        q_ref, kc_ref, vc_ref, sink_ref, o_ref = refs
    n = pl.program_id(0)
    nb = pl.num_programs(0)
    rows = A_GROUP * ATTN_BLOCK
    if local:
        qi = lax.broadcasted_iota(jnp.int32, (rows, ATTN_BLOCK), 0) % ATTN_BLOCK
        kj = lax.broadcasted_iota(jnp.int32, (rows, ATTN_BLOCK), 1)
        mask_prev = (kj >= qi) & (n > 0)
        mask_next = (kj <= qi) & (n < nb - 1)
    for kv in range(A_KV_HEADS):
        kc = slice(kv * A_HEAD_DIM, (kv + 1) * A_HEAD_DIM)
        heads = [kv * A_GROUP + g for g in range(A_GROUP)]
        q4 = jnp.concatenate([q_ref[:, h * A_HEAD_DIM:(h + 1) * A_HEAD_DIM] for h in heads], axis=0)
        sink_col = jnp.concatenate(
            [jnp.broadcast_to(sink_ref[h:h + 1, 0:1], (ATTN_BLOCK, 1)) for h in heads], axis=0)
        scores, values = [], []
        if local:
            scores.append(jnp.where(mask_prev, _dot_nt(q4, kp_ref[:, kc]), NEG_INF))
            scores.append(_dot_nt(q4, ko_ref[:, kc]))
            scores.append(jnp.where(mask_next, _dot_nt(q4, kn_ref[:, kc]), NEG_INF))
            values += [vp_ref[:, kc], vo_ref[:, kc], vn_ref[:, kc]]
        scores.append(_dot_nt(q4, kc_ref[:, kc]))
        values.append(vc_ref[:, kc])
        o = _softmax_pv(scores, values, sink_col).astype(o_ref.dtype)
        for g, h in enumerate(heads):
            o_ref[:, h * A_HEAD_DIM:(h + 1) * A_HEAD_DIM] = o[g * ATTN_BLOCK:(g + 1) * ATTN_BLOCK, :]


def _attention(q, k, p, k_ctx, p_ctx, sink_tab, local):
    t = q.shape[0]
    nb = t // ATTN_BLOCK
    l = k_ctx.shape[0]
    kw = A_KV_HEADS * A_HEAD_DIM
    vcol = 4352 // kw
    blk = lambda f: pl.BlockSpec((ATTN_BLOCK, kw), f)
    blkv = lambda f: pl.BlockSpec((ATTN_BLOCK, kw), lambda i: (f(i)[0], vcol))
    prev = lambda i: (jnp.maximum(i - 1, 0), 0)
    own = lambda i: (i, 0)
    nxt = lambda i: (jnp.minimum(i + 1, nb - 1), 0)
    in_specs = [pl.BlockSpec((ATTN_BLOCK, A_WIDTH), own)]
    args = [q]
    if local:
        in_specs += [blk(prev), blk(own), blk(nxt), blkv(prev), blkv(own), blkv(nxt)]
        args += [k, k, k, p, p, p]
    in_specs += [pl.BlockSpec((l, kw), lambda i: (0, 0)), pl.BlockSpec((l, kw), lambda i: (0, vcol)),
                 pl.BlockSpec((A_HEADS, LANES), lambda i: (0, 0))]
    args += [k_ctx, p_ctx, sink_tab]
    return pl.pallas_call(
        functools.partial(_attn_kernel, local=local),
        grid=(nb,),
        in_specs=in_specs,
        out_specs=pl.BlockSpec((ATTN_BLOCK, A_WIDTH), own),
        out_shape=jax.ShapeDtypeStruct((t, A_WIDTH), BF16),
        compiler_params=_params(("parallel",)),
    )(*args)


def _gla_kernel(q_ref, k_ref, v_ref, la_ref, s0_ref, o_ref, sf_ref, st_ref, *, rev, nchunk):
    @pl.when(pl.program_id(0) == 0)
    def _():
        st_ref[...] = s0_ref[...]

    r = nchunk * GLA_CHUNK
    sub = min(r, GLA_SUB)
    ii = lax.broadcasted_iota(jnp.int32, (sub, sub), 0)
    jj = lax.broadcasted_iota(jnp.int32, (sub, sub), 1)
    same_chunk = (ii // GLA_CHUNK) == (jj // GLA_CHUNK)
    tri = same_chunk & ((jj >= ii) if rev else (jj <= ii))
    b_all = la_ref[...]
    pos = lax.broadcasted_iota(jnp.int32, b_all.shape, 0) % GLA_CHUNK
    step = 1
    while step < GLA_CHUNK:
        if rev:
            b_all = b_all + jnp.where(pos < GLA_CHUNK - step, pltpu.roll(b_all, r - step, 0), 0.0)
        else:
            b_all = b_all + jnp.where(pos >= step, pltpu.roll(b_all, step, 0), 0.0)
        step *= 2
    heads = []
    for h in range(B_HEADS):
        kc = slice(h * B_DK, (h + 1) * B_DK)
        b = b_all[:, kc]
        b3 = b.reshape(nchunk, GLA_CHUNK, B_DK)
        b_end = b3[:, 0:1, :] if rev else b3[:, GLA_CHUNK - 1:GLA_CHUNK, :]
        k = k_ref[:, kc].astype(F32)
        qe = (q_ref[:, kc].astype(F32) * (B_DK ** -0.5) * jnp.exp(b)).astype(BF16)
        ke = (k * jnp.exp(-b)).astype(BF16)
        kd = (k.reshape(nchunk, GLA_CHUNK, B_DK) * jnp.exp(b_end - b3)).astype(BF16)
        v = v_ref[:, h * B_DV:(h + 1) * B_DV].astype(BF16)
        o_intra = []
        for s0 in range(0, r, sub):
            rs = slice(s0, s0 + sub)
            a = jnp.where(tri, _dot_nt(qe[rs, :], ke[rs, :]), 0.0).astype(BF16)
            o_intra.append(_dot(a, v[rs, :]))
        heads.append((qe, kd, v, jnp.concatenate(o_intra, axis=0), jnp.exp(b_end)))
    st = [st_ref[h] for h in range(B_HEADS)]
    for c in (range(nchunk - 1, -1, -1) if rev else range(nchunk)):
        rows = slice(c * GLA_CHUNK, (c + 1) * GLA_CHUNK)
        for h, (qe, kd, v, o_intra, decay) in enumerate(heads):
            o = o_intra[rows, :] + _dot_nt(qe[rows, :], st[h].astype(BF16))
            o_ref[0, rows, h * B_DV:(h + 1) * B_DV] = o.astype(o_ref.dtype)
            st[h] = st[h] * decay[c] + _dot_tn(v[rows, :], kd[c])
    for h in range(B_HEADS):
        st_ref[h] = st[h]
        sf_ref[h] = st[h]


def _gla_scan(p, la, s0, rev):
    t = p.shape[0]
    r = min(512, t)
    nblk = t // r
    rb = (lambda c: nblk - 1 - c) if rev else (lambda c: c)
    nk = B_HEADS * B_DK
    o, sf = pl.pallas_call(
        functools.partial(_gla_kernel, rev=rev, nchunk=r // GLA_CHUNK),
        grid=(nblk,),
        in_specs=[pl.BlockSpec((r, nk), lambda c: (rb(c), 3584 // nk)),
                  pl.BlockSpec((r, nk), lambda c: (rb(c), 3072 // nk)),
                  pl.BlockSpec((r, B_WIDTH), lambda c: (rb(c), 1024 // B_WIDTH)),
                  pl.BlockSpec((r, nk), lambda c: (rb(c), 1 if rev else 0)),
                  pl.BlockSpec((B_HEADS, B_DV, B_DK), lambda c: (0, 0, 0))],
        out_specs=[pl.BlockSpec((1, r, B_WIDTH), lambda c: (0, rb(c), 0)),
                   pl.BlockSpec((B_HEADS, B_DV, B_DK), lambda c: (0, 0, 0))],
        out_shape=[jax.ShapeDtypeStruct((1, t, B_WIDTH), BF16),
                   jax.ShapeDtypeStruct((B_HEADS, B_DV, B_DK), F32)],
        scratch_shapes=[pltpu.VMEM((B_HEADS, B_DV, B_DK), F32)],
        compiler_params=_params(("arbitrary",)),
    )(p, p, p, la, s0)
    return o[0], sf


def _outproj_kernel(oa_ref, of_ref, ob_ref, og_ref, gg_ref, w_ref, x_ref, gt_ref, o_ref, mix_ref, *, row):
    @pl.when(pl.program_id(1) == 0)
    def _():
        mix_ref[:, 0:A_WIDTH] = oa_ref[...]
        for h in range(B_HEADS):
            cols = slice(h * B_DV, (h + 1) * B_DV)
            o = of_ref[:, cols].astype(F32) + ob_ref[:, cols].astype(F32)
            ms = jnp.mean(o * o, axis=-1, keepdims=True)
            y = o * lax.rsqrt(ms + NORM_EPS) * gg_ref[...]
            mix_ref[:, A_WIDTH + h * B_DV:A_WIDTH + (h + 1) * B_DV] = (y * _silu(og_ref[:, cols].astype(F32))).astype(BF16)

    o_ref[...] = x_ref[...] + gt_ref[row:row + 1, :] * _dot(mix_ref[...], w_ref[...])


def _outproj(oa, o_f, o_b, p, g_gla, w_out, x, ada_l, row):
    t, d = x.shape
    tm = min(512, t)
    tn = 1024
    return pl.pallas_call(
        functools.partial(_outproj_kernel, row=row),
        grid=(t // tm, d // tn),
        in_specs=[pl.BlockSpec((tm, A_WIDTH), lambda i, j: (i, 0)),
                  pl.BlockSpec((tm, B_WIDTH), lambda i, j: (i, 0)),
                  pl.BlockSpec((tm, B_WIDTH), lambda i, j: (i, 0)),
                  pl.BlockSpec((tm, B_WIDTH), lambda i, j: (i, 2048 // B_WIDTH)),
                  pl.BlockSpec((1, B_DV), lambda i, j: (0, 0)),
                  pl.BlockSpec((A_WIDTH + B_WIDTH, tn), lambda i, j: (0, j)),
                  pl.BlockSpec((tm, tn), lambda i, j: (i, j)),
                  pl.BlockSpec((8, tn), lambda i, j: (0, 2 * (d // tn) + j))],
        out_specs=pl.BlockSpec((tm, tn), lambda i, j: (i, j)),
        out_shape=jax.ShapeDtypeStruct((t, d), F32),
        scratch_shapes=[pltpu.VMEM((tm, A_WIDTH + B_WIDTH), BF16)],
        compiler_params=_params(("parallel", "arbitrary")),
    )(oa, o_f, o_b, p, g_gla, w_out, x, ada_l)


def _ffn_kernel(x_ref, g_ref, sh_ref, sc_ref, gt_ref, wg_ref, wu_ref, wd_ref, o_ref, hn_ref, *, row):
    f = pl.program_id(1)

    @pl.when(f == 0)
    def _():
        hn_ref[...] = _norm_mod(x_ref[...], g_ref[...], sh_ref[row:row + 1, :], sc_ref[row:row + 1, :]).astype(BF16)
        o_ref[...] = jnp.zeros_like(o_ref)

    h = hn_ref[...]
    a = (_silu(_dot(h, wg_ref[0].astype(BF16))) * _dot(h, wu_ref[0].astype(BF16))).astype(BF16)
    o_ref[...] += _dot(a, wd_ref[0].astype(BF16))

    @pl.when(f == pl.num_programs(1) - 1)
    def _():
        o_ref[...] = x_ref[...] + gt_ref[row:row + 1, :] * o_ref[...]


def _ffn(x, ada_l, row, g, wg, wu, wd, lyr):
    t, d = x.shape
    ff = wg.shape[2]
    tm = min(1024, t)
    tf = 256
    return pl.pallas_call(
        functools.partial(_ffn_kernel, row=row),
        grid=(t // tm, ff // tf),
        in_specs=[pl.BlockSpec((tm, d), lambda i, f: (i, 0)),
                  pl.BlockSpec((1, d), lambda i, f: (0, 0)),
                  _ada_spec(3), _ada_spec(4), _ada_spec(5),
                  pl.BlockSpec((1, d, tf), lambda i, f: (lyr, 0, f)),
                  pl.BlockSpec((1, d, tf), lambda i, f: (lyr, 0, f)),
                  pl.BlockSpec((1, tf, d), lambda i, f: (lyr, f, 0))],
        out_specs=pl.BlockSpec((tm, d), lambda i, f: (i, 0)),
        out_shape=jax.ShapeDtypeStruct((t, d), F32),
        scratch_shapes=[pltpu.VMEM((tm, d), BF16)],
        compiler_params=_params(("parallel", "arbitrary"), MOE_VMEM_LIMIT),
    )(x, g, ada_l, ada_l, ada_l, wg, wu, wd)


def _pool_router_kernel(x_ref, xp_ref, xn_ref, g_ref, sh_ref, sc_ref, gt_ref, w_ref, ps_ref,
                        g2_ref, sh2_ref, sc2_ref, wr_ref, o_ref, hn_ref, sel_ref, gate_ref, hb_ref, *, row, t):
    i = pl.program_id(0)
    tm = x_ref.shape[0]
    g = g_ref[...]
    sh = sh_ref[row:row + 1, :]
    sc = sc_ref[row:row + 1, :]
    hb_ref[0:POOL_HALO, :] = jnp.where(i > 0, _norm_mod(xp_ref[...], g, sh, sc), 0.0)
    hb_ref[POOL_HALO:POOL_HALO + tm, :] = _norm_mod(x_ref[...], g, sh, sc)
    hb_ref[POOL_HALO + tm:, :] = jnp.where(i < pl.num_programs(0) - 1, _norm_mod(xn_ref[...], g, sh, sc), 0.0)
    tpos = i * tm + lax.broadcasted_iota(jnp.int32, (tm, 1), 0)
    gc = D_MODEL // POOL_GROUPS
    for grp, w in enumerate(POOL_WINDOWS):
        cols = slice(grp * gc, (grp + 1) * gc)
        acc = None
        for dlt in range(-(w // 2), w - w // 2):
            piece = hb_ref[POOL_HALO + dlt:POOL_HALO + dlt + tm, cols]
            acc = piece if acc is None else acc + piece
        cnt = jnp.minimum(tpos + (w - w // 2), t) - jnp.maximum(tpos - w // 2, 0)
        pooled = acc / cnt.astype(F32) - hb_ref[POOL_HALO:POOL_HALO + tm, cols]
        y = _dot(pooled.astype(BF16), w_ref[grp])
        o_ref[:, cols] = x_ref[:, cols] + gt_ref[row:row + 1, cols] * (y * ps_ref[:, cols])

    h = _norm_mod(o_ref[...], g2_ref[...], sh2_ref[row:row + 1, :], sc2_ref[row:row + 1, :])
    hn_ref[...] = h.astype(hn_ref.dtype)
    lane = lax.broadcasted_iota(jnp.int32, (h.shape[0], LANES), 1)
    lane_f = lane.astype(F32)
    logits = jnp.where(lane < N_EXPERTS, _dot(h, wr_ref[...], HIGHEST), -jnp.inf)
    m1 = jnp.max(logits, axis=-1, keepdims=True)
    i1 = jnp.min(jnp.where(logits == m1, lane_f, float(LANES)), axis=-1, keepdims=True)
    rest = jnp.where(lane_f == i1, -jnp.inf, logits)
    m2 = jnp.max(rest, axis=-1, keepdims=True)
    i2 = jnp.min(jnp.where(rest == m2, lane_f, float(LANES)), axis=-1, keepdims=True)
    e2 = jnp.exp(m2 - m1)
    den = 1.0 + e2
    sel_ref[...] = jnp.where(lane == 0, i1, jnp.where(lane == 1, i2, 0.0)).astype(jnp.int32)
    gate_ref[...] = jnp.where(lane == 0, 1.0 / den, jnp.where(lane == 1, e2 / den, 0.0))


def _pool_router(x, ada_l, row, g1, g2, w_pool, pool_scale, w_router_pad):
    t, d = x.shape
    tm = min(256, t)
    nh = t // POOL_HALO
    gc = d // POOL_GROUPS
    rows = lambda i: (i, 0)
    const = lambda i: (0, 0)
    return pl.pallas_call(
        functools.partial(_pool_router_kernel, row=row, t=t),
        grid=(t // tm,),
        in_specs=[pl.BlockSpec((tm, d), rows),
                  pl.BlockSpec((POOL_HALO, d), lambda i: (jnp.maximum(i * (tm // POOL_HALO) - 1, 0), 0)),
                  pl.BlockSpec((POOL_HALO, d), lambda i: (jnp.minimum((i + 1) * (tm // POOL_HALO), nh - 1), 0)),
                  pl.BlockSpec((1, d), const),
                  _ada_spec(0), _ada_spec(1), _ada_spec(2),
                  pl.BlockSpec((POOL_GROUPS, gc, gc), lambda i: (0, 0, 0)),
                  pl.BlockSpec((1, d), const),
                  pl.BlockSpec((1, d), const),
                  _ada_spec(3), _ada_spec(4),
                  pl.BlockSpec((d, LANES), const)],
        out_specs=[pl.BlockSpec((tm, d), rows), pl.BlockSpec((tm, d), rows),
                   pl.BlockSpec((tm, LANES), rows), pl.BlockSpec((tm, LANES), rows)],
        out_shape=[jax.ShapeDtypeStruct((t, d), F32),
                   jax.ShapeDtypeStruct((t, d), BF16),
                   jax.ShapeDtypeStruct((t, LANES), jnp.int32),
                   jax.ShapeDtypeStruct((t, LANES), F32)],
        scratch_shapes=[pltpu.VMEM((tm + 2 * POOL_HALO, d), F32)],
        compiler_params=_params(("parallel",)),
    )(x, x, x, g1, ada_l, ada_l, ada_l, w_pool, pool_scale, g2, ada_l, ada_l, w_router_pad)


MOE_TILE = 1280
MOE_SUB = 128
MOE_SUBS = MOE_TILE // MOE_SUB
GATHER_SUB = 256
GATHER_SUBS = MOE_TILE // GATHER_SUB
SRC_BLK = 256
Y_BLK = 256
CMB_TILE = 256
MOE_TF = 512
GATHER_BUFS = 3


def _moe_ffn_kernel(te_ref, tn_ref, tlo_ref, tnb_ref, trk_ref, cb_ref, posb_ref, hn_hbm, wg_ref, wu_ref, wd_ref,
                    y_ref, xb_ref, acc_ref, hbuf_ref, sem):
    i = pl.program_id(0)
    f = pl.program_id(1)
    nsub = tn_ref[i]

    @pl.when(f == 0)
    def _():
        xb_ref[...] = jnp.zeros_like(xb_ref)
        acc_ref[...] = jnp.zeros_like(acc_ref)
        row_id = lax.broadcasted_iota(jnp.int32, (GATHER_SUB, SRC_BLK), 0) + i * MOE_TILE
        rows_used = nsub * MOE_SUB
        lo = tlo_ref[i]
        n = tnb_ref[i]
        rank0 = trk_ref[i]
        cb_base = te_ref[i] * (posb_ref.shape[0] + 1)

        def blk_copy(b, slot):
            return pltpu.make_async_copy(hn_hbm.at[pl.ds((lo + b) * SRC_BLK, SRC_BLK), :], hbuf_ref.at[slot],
                                         sem.at[slot])

        for j in range(GATHER_BUFS - 1):
            @pl.when(j < n)
            def _():
                blk_copy(j, j).start()

        def body(b, carry):
            slot = b % GATHER_BUFS
            blk_copy(b, slot).wait()
            ahead = b + GATHER_BUFS - 1

            @pl.when(ahead < n)
            def _():
                blk_copy(ahead, ahead % GATHER_BUFS).start()

            blk = lo + b
            pos = posb_ref[blk]
            pos1, pos2 = pos[0:1, :], pos[1:2, :]
            before = cb_ref[cb_base + blk]
            through = cb_ref[cb_base + blk + 1]
            for q in range(GATHER_SUBS):
                @pl.when((q * GATHER_SUB < rows_used) & (before < rank0 + (q + 1) * GATHER_SUB)
                         & (through > rank0 + q * GATHER_SUB))
                def _():
                    rows = slice(q * GATHER_SUB, (q + 1) * GATHER_SUB)
                    rid = row_id + q * GATHER_SUB
                    onehot = jnp.where((pos1 == rid) | (pos2 == rid), 1.0, 0.0).astype(BF16)
                    xb_ref[rows, :] += _dot(onehot, hbuf_ref[slot]).astype(BF16)
            return carry

        lax.fori_loop(0, n, body, 0)

    for k in range(1, MOE_SUBS + 1):
        @pl.when(nsub == k)
        def _():
            rows = slice(0, k * MOE_SUB)
            h = xb_ref[rows, :]
            a = (_silu(_dot(h, wg_ref[0, 0].astype(BF16))) * _dot(h, wu_ref[0, 0].astype(BF16))).astype(BF16)
            acc_ref[rows, :] += _dot(a, wd_ref[0, 0].astype(BF16))

    @pl.when(f == pl.num_programs(1) - 1)
    def _():
        y_ref[...] = acc_ref[...].astype(y_ref.dtype)


def _moe_ffn(hn, posb, tile_expert, tile_subs, tile_blk_lo, tile_blk_n, tile_rank, blk_counts, wg, wu, wd, lyr):
    d = hn.shape[1]
    tm = MOE_TILE
    ff = wg.shape[3]
    tf = MOE_TF
    nf = ff // tf
    nt = tile_expert.shape[0]
    fidx = lambda i, f, tn: jnp.where(tn[i] > 0, f, nf - 1)
    return pl.pallas_call(
        _moe_ffn_kernel,
        grid_spec=pltpu.PrefetchScalarGridSpec(
            num_scalar_prefetch=6,
            grid=(nt, nf),
            in_specs=[pl.BlockSpec(posb.shape, lambda i, f, te, tn, *_: (0, 0, 0)),
                      pl.BlockSpec(memory_space=pl.ANY),
                      pl.BlockSpec((1, 1, d, tf), lambda i, f, te, tn, *_: (lyr, te[i], 0, fidx(i, f, tn))),
                      pl.BlockSpec((1, 1, d, tf), lambda i, f, te, tn, *_: (lyr, te[i], 0, fidx(i, f, tn))),
                      pl.BlockSpec((1, 1, tf, d), lambda i, f, te, tn, *_: (lyr, te[i], fidx(i, f, tn), 0))],
            out_specs=pl.BlockSpec((tm, d), lambda i, f, te, tn, *_: (i, 0)),
            scratch_shapes=[pltpu.VMEM((tm, d), BF16), pltpu.VMEM((tm, d), F32),
                            pltpu.VMEM((GATHER_BUFS, SRC_BLK, d), BF16),
                            pltpu.SemaphoreType.DMA((GATHER_BUFS,))]),
        out_shape=jax.ShapeDtypeStruct((nt * tm, d), BF16),
        compiler_params=_params(("arbitrary", "arbitrary"), MOE_VMEM_LIMIT),
    )(tile_expert, tile_subs, tile_blk_lo, tile_blk_n, tile_rank, blk_counts, posb, hn, wg, wu, wd)


CMB_SLOTS = [(e, j) for e in range(N_EXPERTS) for j in range(2)]


def _combine_kernel(fb_ref, nb_ref, x_ref, pos_ref, gate_ref, gt_ref, y_hbm, o_ref, acc_ref, ybuf_ref, sem,
                    *, row, tile_off):
    base = (tile_off + pl.program_id(0)) * N_EXPERTS
    tm = x_ref.shape[0]
    pos1, pos2 = pos_ref[:, 0:1], pos_ref[:, 1:2]
    g1, g2 = gate_ref[:, 0:1], gate_ref[:, 1:2]
    col = lax.broadcasted_iota(jnp.int32, (tm, Y_BLK), 1)

    def blk(s):
        e, j = CMB_SLOTS[s]
        return fb_ref[base + e] + j

    def used(s):
        e, j = CMB_SLOTS[s]
        return nb_ref[base + e] > j

    def blk_copy(s):
        return pltpu.make_async_copy(y_hbm.at[pl.ds(blk(s) * Y_BLK, Y_BLK), :], ybuf_ref.at[s], sem.at[s])

    for s in range(len(CMB_SLOTS)):
        @pl.when(used(s))
        def _():
            blk_copy(s).start()

    acc_ref[...] = jnp.zeros_like(acc_ref)
    for s in range(len(CMB_SLOTS)):
        @pl.when(used(s))
        def _():
            blk_copy(s).wait()
            off = blk(s) * Y_BLK
            w = jnp.where(pos1 - off == col, g1, 0.0) + jnp.where(pos2 - off == col, g2, 0.0)
            acc_ref[...] += _dot(w.astype(BF16), ybuf_ref[s])

    o_ref[...] = x_ref[...] + gt_ref[row:row + 1, :] * acc_ref[...]


def _combine(x, y, pos, gates, first_blk, num_blk, tok_off, ada_l, row):
    t, d = x.shape
    tm = CMB_TILE
    tile_off = tok_off // tm
    return pl.pallas_call(
        functools.partial(_combine_kernel, row=row, tile_off=tile_off),
        grid_spec=pltpu.PrefetchScalarGridSpec(
            num_scalar_prefetch=2,
            grid=(t // tm,),
            in_specs=[pl.BlockSpec((tm, d), lambda i, fb, nb: (i, 0)),
                      pl.BlockSpec((tm, 2), lambda i, fb, nb: (tile_off + i, 0)),
                      pl.BlockSpec((tm, LANES), lambda i, fb, nb: (i, 0)),
                      _ada_spec(5),
                      pl.BlockSpec(memory_space=pl.ANY)],
            out_specs=pl.BlockSpec((tm, d), lambda i, fb, nb: (i, 0)),
            scratch_shapes=[pltpu.VMEM((tm, d), F32), pltpu.VMEM((len(CMB_SLOTS), Y_BLK, d), BF16),
                            pltpu.SemaphoreType.DMA((len(CMB_SLOTS),))]),
        out_shape=jax.ShapeDtypeStruct((t, d), F32),
        compiler_params=_params(("arbitrary",)),
    )(first_blk, num_blk, x, pos, gates, ada_l, y)


def _route(sel):
    tm = MOE_TILE
    n_tok = sel.shape[0]
    n_asg = 2 * n_tok
    nt = -(-n_asg // tm) + N_EXPERTS
    e_flat = sel.reshape(-1)
    onehot = (e_flat[:, None] == jnp.arange(N_EXPERTS, dtype=jnp.int32)[None, :]).astype(jnp.int32)
    csum = jnp.cumsum(onehot, axis=0)
    count = csum[-1]
    rank = jnp.sum((csum - onehot) * onehot, axis=1)
    tiles_e = (count + tm - 1) // tm
    tile_end = jnp.cumsum(tiles_e)
    base = (tile_end - tiles_e) * tm
    pos = jnp.sum(onehot * base[None, :], axis=1) + rank
    tile_id = jnp.arange(nt, dtype=jnp.int32)
    used = tile_end[-1]
    owner = lambda i: jnp.minimum(jnp.sum((i[:, None] >= tile_end[None, :]).astype(jnp.int32), axis=1), N_EXPERTS - 1)
    tile_expert = owner(jnp.minimum(tile_id, used - 1))
    rows_in_tile = jnp.clip((base + count)[tile_expert] - tile_id * tm, 0, tm)
    tile_subs = jnp.where(tile_id < used, (rows_in_tile + MOE_SUB - 1) // MOE_SUB, 0)
    n_blk = n_tok // SRC_BLK
    blk_counts = jnp.concatenate([jnp.zeros((1, N_EXPERTS), jnp.int32),
                                  csum.reshape(n_blk, 2 * SRC_BLK, N_EXPERTS)[:, -1, :]], axis=0).T
    tile_rank = tile_id * tm - base[tile_expert]
    counts_t = blk_counts[tile_expert]
    tile_blk_lo = jnp.sum((counts_t[:, 1:] <= tile_rank[:, None]).astype(jnp.int32), axis=1)
    tile_blk_hi = jnp.sum((counts_t[:, :-1] < (tile_rank + rows_in_tile)[:, None]).astype(jnp.int32), axis=1)
    tile_blk_n = jnp.where(tile_subs > 0, jnp.maximum(tile_blk_hi - tile_blk_lo, 0), 0)
    tile_blk_lo = jnp.minimum(tile_blk_lo, n_blk - 1)
    posb = jnp.pad(pos.reshape(n_blk, SRC_BLK, 2).transpose(0, 2, 1), ((0, 0), (0, 6), (0, 0)), constant_values=-1)
    ntt = n_tok // CMB_TILE
    through = csum.reshape(ntt, 2 * CMB_TILE, N_EXPERTS)[:, -1, :]
    before = jnp.concatenate([jnp.zeros((1, N_EXPERTS), jnp.int32), through[:-1]], axis=0)
    lo = base[None, :] + before
    hi = base[None, :] + through - 1
    first_blk = lo // Y_BLK
    num_blk = jnp.where(through > before, hi // Y_BLK - first_blk + 1, 0)
    return (posb, pos.reshape(n_tok, 2), tile_expert, tile_subs, tile_blk_lo, tile_blk_n, tile_rank,
            blk_counts.reshape(-1), first_blk.reshape(-1), num_blk.reshape(-1))


def _moe(streams, ada_l, wg, wu, wd, lyr):
    hs = [s[2] for s in streams]
    sels = [s[3][:, :2] for s in streams]
    n_tok = sum(h.shape[0] for h in hs)
    pad = -n_tok % SRC_BLK
    if pad:
        hs.append(jnp.zeros((pad, D_MODEL), BF16))
    h_all = hs[0] if len(hs) == 1 else jnp.concatenate(hs, axis=0)
    sel_all = sels[0] if len(sels) == 1 else jnp.concatenate(sels, axis=0)
    (posb, pos, tile_expert, tile_subs, tile_blk_lo, tile_blk_n, tile_rank, blk_counts,
     first_blk, num_blk) = _route(sel_all)
    y = _moe_ffn(h_all, posb, tile_expert, tile_subs, tile_blk_lo, tile_blk_n, tile_rank, blk_counts,
                 wg, wu, wd, lyr)
    outs, off = [], 0
    for x, row, _, _, gt in streams:
        outs.append(_combine(x, y, pos, gt, first_blk, num_blk, off, ada_l, row))
        off += x.shape[0]
    return outs


def _even_layer(x, ctx, ada_l, g1, g2, w_in, g_qn, g_kn, sink, w_gate_up, b_gate_up, g_gla, w_out,
                w_ffn_gate, w_ffn_up, w_ffn_down, lyr, rope_tabs, ctx_out):
    d = D_MODEL
    w_main = jnp.concatenate([w_in[:, 2080:3104], w_in[:, 1024:2048], w_in[:, 3616:4640], w_in[:, 512:1024],
                              w_in[:, 3104:3616], w_in[:, 0:256], w_in[:, 256:512]], axis=1).astype(BF16)
    w_gate = jnp.pad(w_in[:, 2048:2080], ((0, 0), (0, LANES - 2 * GATE_RANK))).astype(BF16)
    nk = B_HEADS * B_DK
    w_up = jnp.zeros((LANES, 2 * nk), F32)
    w_up = w_up.at[0:GATE_RANK, 0:nk].set(w_gate_up[0]).at[GATE_RANK:2 * GATE_RANK, nk:].set(w_gate_up[1])
    b_up = b_gate_up.reshape(1, 2 * nk)
    w_out_b = w_out.astype(BF16)
    gq, gk, gg = g_qn.reshape(1, -1), g_kn.reshape(1, -1), g_gla.reshape(1, -1)
    sink_tab = jnp.broadcast_to(sink[:, None], (A_HEADS, LANES))
    cos, sin = rope_tabs

    pc, la_c = _inproj(ctx, ada_l, 1, g1, w_main, w_gate, w_up, b_up)
    qc, kc = _qkprep(pc, cos, sin, gq, gk, rope=False)
    s0 = jnp.zeros((B_HEADS, B_DV, B_DK), F32)
    oc_f, s_fwd = _gla_scan(pc, la_c, s0, rev=False)
    oc_b, s_bwd = _gla_scan(pc, la_c, s0, rev=True)

    px, la_x = _inproj(x, ada_l, 0, g1, w_main, w_gate, w_up, b_up)
    qx, kx = _qkprep(px, cos, sin, gq, gk, rope=True)
    oa = _attention(qx, kx, px, kc, pc, sink_tab, local=True)
    ox_f, _ = _gla_scan(px, la_x, s_fwd, rev=False)
    ox_b, _ = _gla_scan(px, la_x, s_bwd, rev=True)
    x = _outproj(oa, ox_f, ox_b, px, gg, w_out_b, x, ada_l, 0)
    x = _ffn(x, ada_l, 0, g2, w_ffn_gate, w_ffn_up, w_ffn_down, lyr)
    if ctx_out:
        oa_c = _attention(qc, None, None, kc, pc, sink_tab, local=False)
        ctx = _outproj(oa_c, oc_f, oc_b, pc, gg, w_out_b, ctx, ada_l, 1)
        ctx = _ffn(ctx, ada_l, 1, g2, w_ffn_gate, w_ffn_up, w_ffn_down, lyr)
    return x, ctx


def _odd_layer(x, ctx, ada_l, g1, g2, w_pool, pool_scale, w_router, w_exp_gate, w_exp_up, w_exp_down, lyr, ctx_out):
    w_pool_b = w_pool.astype(BF16)
    ps = pool_scale.reshape(1, -1)
    wr = jnp.pad(w_router, ((0, 0), (0, LANES - N_EXPERTS)))
    streams = [(0,) + tuple(_pool_router(x, ada_l, 0, g1, g2, w_pool_b, ps, wr))]
    if ctx_out:
        streams.append((1,) + tuple(_pool_router(ctx, ada_l, 1, g1, g2, w_pool_b, ps, wr)))
    outs = _moe([(s[1], s[0], s[2], s[3], s[4]) for s in streams], ada_l, w_exp_gate, w_exp_up, w_exp_down, lyr)
    return outs[0], (outs[1] if ctx_out else ctx)


def kernel(x, c, ctx, c_ctx, w_ada, b_ada, norm_g, w_in, g_qn, g_kn, attn_sink, w_gate_up, b_gate_up, g_gla, w_out,
           w_ffn_gate, w_ffn_up, w_ffn_down, w_pool, pool_scale, w_router, w_exp_gate, w_exp_up, w_exp_down):
    depth = w_ada.shape[0]
    xs = x[0]
    cs = ctx[0]
    t = xs.shape[0]
    cond = jnp.zeros((8, D_MODEL), F32).at[0].set(c[0]).at[1].set(c_ctx)
    ada = _ada_all(cond, w_ada, b_ada)
    rope_tabs = _rope_tables(t)
    for l in range(depth):
        ctx_later = any(j % 2 == 0 for j in range(l + 1, depth))
        g1 = norm_g[l, 0].reshape(1, -1)
        g2 = norm_g[l, 1].reshape(1, -1)
        if l % 2 == 0:
            e = l // 2
            xs, cs = _even_layer(xs, cs, ada[l], g1, g2, w_in[e], g_qn[e], g_kn[e], attn_sink[e], w_gate_up[e],
                                 b_gate_up[e], g_gla[e], w_out[e], w_ffn_gate, w_ffn_up, w_ffn_down, e,
                                 rope_tabs, ctx_later)
        else:
            o = l // 2
            xs, cs = _odd_layer(xs, cs, ada[l], g1, g2, w_pool[o], pool_scale[o], w_router[o],
                                w_exp_gate, w_exp_up, w_exp_down, o, ctx_later)
    return xs[None]
```

```python
import functools

import jax
import jax.numpy as jnp
import numpy as np
from jax import lax
from jax.experimental import pallas as pl
from jax.experimental.pallas import tpu as pltpu

F32 = jnp.float32
BF16 = jnp.bfloat16
HIGHEST = lax.Precision.HIGHEST

D_MODEL = 2048
GRID_W = 64
A_HEADS = 8
A_KV_HEADS = 2
A_GROUP = 4
A_HEAD_DIM = 128
A_WIDTH = A_HEADS * A_HEAD_DIM
ATTN_BLOCK = 128
ROPE_BASE = 10000.0
B_HEADS = 4
B_DV = 256
B_DK = 128
B_WIDTH = B_HEADS * B_DV
GATE_RANK = 16
GATE_TAU = 16.0
GLA_CHUNK = 64
GLA_SUB = 256
POOL_GROUPS = 4
POOL_WINDOWS = (2, 4, 8, 16)
POOL_HALO = 8
N_EXPERTS = 8
NORM_EPS = 1e-6
NEG_INF = -1e30
LANES = 128

P_COLS = 4608
VMEM_LIMIT = 56 * 1024 * 1024
MOE_VMEM_LIMIT = 62 * 1024 * 1024


def _params(sem, vmem=VMEM_LIMIT):
    return pltpu.CompilerParams(dimension_semantics=sem, vmem_limit_bytes=vmem)


def _dot(a, b, precision=None):
    return jnp.dot(a, b, preferred_element_type=F32, precision=precision)


def _dot_nt(a, b):
    return lax.dot_general(a, b, (((1,), (1,)), ((), ())), preferred_element_type=F32)


def _dot_tn(a, b):
    return lax.dot_general(a, b, (((0,), (0,)), ((), ())), preferred_element_type=F32)


def _silu(x):
    return x * jax.nn.sigmoid(x)


def _norm_mod(x, g, shift, scale):
    ms = jnp.mean(x * x, axis=-1, keepdims=True)
    return (x * lax.rsqrt(ms + NORM_EPS) * g) * (1.0 + scale) + shift


def _ada_kernel(cond_ref, w_ref, b_ref, o_ref):
    s = _silu(cond_ref[...])
    s_hi = s.astype(BF16)
    s_lo = (s - s_hi.astype(F32)).astype(BF16)
    w = w_ref[0].astype(BF16)
    o_ref[0] = _dot(s_hi, w) + _dot(s_lo, w) + b_ref[0]


def _ada_all(cond, w_ada, b_ada):
    depth, d, n = w_ada.shape
    tn = 1024
    return pl.pallas_call(
        _ada_kernel,
        grid=(depth, n // tn),
        in_specs=[pl.BlockSpec((8, d), lambda l, j: (0, 0)),
                  pl.BlockSpec((1, d, tn), lambda l, j: (l, 0, j)),
                  pl.BlockSpec((1, 1, tn), lambda l, j: (l, 0, j))],
        out_specs=pl.BlockSpec((1, 8, tn), lambda l, j: (l, 0, j)),
        out_shape=jax.ShapeDtypeStruct((depth, 8, n), F32),
        compiler_params=_params(("parallel", "parallel")),
    )(cond, w_ada, b_ada.reshape(depth, 1, n))


def _ada_spec(k):
    return pl.BlockSpec((8, D_MODEL), lambda *_: (0, k))


def _inproj_kernel(x_ref, g_ref, sh_ref, sc_ref, w_ref, wg_ref, wup_ref, bup_ref, p_ref, la_ref, hn_ref, *, row):
    @pl.when(pl.program_id(1) == 0)
    def _():
        h = _norm_mod(x_ref[...], g_ref[...], sh_ref[row:row + 1, :], sc_ref[row:row + 1, :]).astype(BF16)
        hn_ref[...] = h
        lr = _dot(h, wg_ref[...])
        z = _dot(lr, wup_ref[...], HIGHEST) + bup_ref[...]
        la_ref[...] = (jnp.minimum(z, 0.0) - jnp.log1p(jnp.exp(-jnp.abs(z)))) * (1.0 / GATE_TAU)

    p_ref[...] = _dot(hn_ref[...], w_ref[...]).astype(p_ref.dtype)


def _inproj(x, ada_l, row, g, w_main, w_gate, w_up, b_up):
    t, d = x.shape
    tm = min(512, t)
    tn = 1536
    nla = 2 * B_HEADS * B_DK
    return pl.pallas_call(
        functools.partial(_inproj_kernel, row=row),
        grid=(t // tm, P_COLS // tn),
        in_specs=[pl.BlockSpec((tm, d), lambda i, j: (i, 0)),
                  pl.BlockSpec((1, d), lambda i, j: (0, 0)),
                  _ada_spec(0), _ada_spec(1),
                  pl.BlockSpec((d, tn), lambda i, j: (0, j)),
                  pl.BlockSpec((d, LANES), lambda i, j: (0, 0)),
                  pl.BlockSpec((LANES, nla), lambda i, j: (0, 0)),
                  pl.BlockSpec((1, nla), lambda i, j: (0, 0))],
        out_specs=[pl.BlockSpec((tm, tn), lambda i, j: (i, j)),
                   pl.BlockSpec((tm, nla), lambda i, j: (i, 0))],
        out_shape=[jax.ShapeDtypeStruct((t, P_COLS), BF16),
                   jax.ShapeDtypeStruct((t, nla), F32)],
        scratch_shapes=[pltpu.VMEM((tm, d), BF16)],
        compiler_params=_params(("parallel", "arbitrary")),
    )(x, g, ada_l, ada_l, w_main, w_gate, w_up, b_up)


def _qkprep_kernel(q_ref, k_ref, cos_ref, sin_ref, gq_ref, gk_ref, qo_ref, ko_ref, *, rope):
    tm = q_ref.shape[0]
    lane = lax.broadcasted_iota(jnp.int32, (tm, A_HEAD_DIM), 1)
    first_half = (lane % 64) < 32

    def prep(xh, g, scale):
        ms = jnp.mean(xh * xh, axis=-1, keepdims=True)
        y = xh * lax.rsqrt(ms + NORM_EPS) * g
        if rope:
            partner = jnp.where(first_half, pltpu.roll(y, 96, 1), pltpu.roll(y, 32, 1))
            y = y * cos_ref[...] + partner * sin_ref[...]
        return (y * scale).astype(BF16)

    for h in range(A_HEADS):
        cols = slice(h * A_HEAD_DIM, (h + 1) * A_HEAD_DIM)
        qo_ref[:, cols] = prep(q_ref[:, cols].astype(F32), gq_ref[...], A_HEAD_DIM ** -0.5)
    for h in range(A_KV_HEADS):
        cols = slice(h * A_HEAD_DIM, (h + 1) * A_HEAD_DIM)
        ko_ref[:, cols] = prep(k_ref[:, cols].astype(F32), gk_ref[...], 1.0)


def _qkprep(p, cos, sin, g_qn, g_kn, rope):
    t = p.shape[0]
    tm = min(512, t)
    kw = A_KV_HEADS * A_HEAD_DIM
    return pl.pallas_call(
        functools.partial(_qkprep_kernel, rope=rope),
        grid=(t // tm,),
        in_specs=[pl.BlockSpec((tm, A_WIDTH), lambda i: (i, 0)),
                  pl.BlockSpec((tm, kw), lambda i: (i, 4096 // kw)),
                  pl.BlockSpec((tm, A_HEAD_DIM), lambda i: (i, 0)),
                  pl.BlockSpec((tm, A_HEAD_DIM), lambda i: (i, 0)),
                  pl.BlockSpec((1, A_HEAD_DIM), lambda i: (0, 0)),
                  pl.BlockSpec((1, A_HEAD_DIM), lambda i: (0, 0))],
        out_specs=[pl.BlockSpec((tm, A_WIDTH), lambda i: (i, 0)),
                   pl.BlockSpec((tm, kw), lambda i: (i, 0))],
        out_shape=[jax.ShapeDtypeStruct((t, A_WIDTH), BF16),
                   jax.ShapeDtypeStruct((t, kw), BF16)],
        compiler_params=_params(("parallel",)),
    )(p, p, cos, sin, g_qn, g_kn)


def _rope_tables(t):
    half = A_HEAD_DIM // 4
    freqs = ROPE_BASE ** (-np.arange(half, dtype=np.float64) / half)
    tok = np.arange(t)
    row = (tok // GRID_W).astype(np.float64)[:, None] * freqs
    col = (tok % GRID_W).astype(np.float64)[:, None] * freqs
    cos = np.concatenate([np.cos(row), np.cos(row), np.cos(col), np.cos(col)], axis=-1)
    sin = np.concatenate([-np.sin(row), np.sin(row), -np.sin(col), np.sin(col)], axis=-1)
    return jnp.asarray(cos.astype(np.float32)), jnp.asarray(sin.astype(np.float32))


def _softmax_pv(scores, values, sink_col):
    def lane_blocks(xs):
        return [x[:, i * LANES:(i + 1) * LANES] for x in xs for i in range(x.shape[1] // LANES)]

    m = jnp.maximum(jnp.max(functools.reduce(jnp.maximum, lane_blocks(scores)), axis=-1, keepdims=True), sink_col)
    probs = [jnp.exp(s - m) for s in scores]
    denom = jnp.exp(sink_col - m) + jnp.sum(functools.reduce(jnp.add, lane_blocks(probs)), axis=-1, keepdims=True)
    acc = None
    for p, v in zip(probs, values):
        pv = _dot(p.astype(BF16), v)
        acc = pv if acc is None else acc + pv
    return acc / denom


def _attn_kernel(*refs, local):
    if local:
        q_ref, kp_ref, ko_ref, kn_ref, vp_ref, vo_ref, vn_ref, kc_ref, vc_ref, sink_ref, o_ref = refs
    else:
        q_ref, kc_ref, vc_ref, sink_ref, o_ref = refs
    n = pl.program_id(0)
    nb = pl.num_programs(0)
    rows = A_GROUP * ATTN_BLOCK
    if local:
        qi = lax.broadcasted_iota(jnp.int32, (rows, ATTN_BLOCK), 0) % ATTN_BLOCK
        kj = lax.broadcasted_iota(jnp.int32, (rows, ATTN_BLOCK), 1)
        mask_prev = (kj >= qi) & (n > 0)
        mask_next = (kj <= qi) & (n < nb - 1)
    for kv in range(A_KV_HEADS):
        kc = slice(kv * A_HEAD_DIM, (kv + 1) * A_HEAD_DIM)
        heads = [kv * A_GROUP + g for g in range(A_GROUP)]
        q4 = jnp.concatenate([q_ref[:, h * A_HEAD_DIM:(h + 1) * A_HEAD_DIM] for h in heads], axis=0)
        sink_col = jnp.concatenate(
            [jnp.broadcast_to(sink_ref[h:h + 1, 0:1], (ATTN_BLOCK, 1)) for h in heads], axis=0)
        scores, values = [], []
        if local:
            scores.append(jnp.where(mask_prev, _dot_nt(q4, kp_ref[:, kc]), NEG_INF))
            scores.append(_dot_nt(q4, ko_ref[:, kc]))
            scores.append(jnp.where(mask_next, _dot_nt(q4, kn_ref[:, kc]), NEG_INF))
            values += [vp_ref[:, kc], vo_ref[:, kc], vn_ref[:, kc]]
        scores.append(_dot_nt(q4, kc_ref[:, kc]))
        values.append(vc_ref[:, kc])
        o = _softmax_pv(scores, values, sink_col).astype(o_ref.dtype)
        for g, h in enumerate(heads):
            o_ref[:, h * A_HEAD_DIM:(h + 1) * A_HEAD_DIM] = o[g * ATTN_BLOCK:(g + 1) * ATTN_BLOCK, :]


def _attention(q, k, p, k_ctx, p_ctx, sink_tab, local):
    t = q.shape[0]
    nb = t // ATTN_BLOCK
    l = k_ctx.shape[0]
    kw = A_KV_HEADS * A_HEAD_DIM
    vcol = 4352 // kw
    blk = lambda f: pl.BlockSpec((ATTN_BLOCK, kw), f)
    blkv = lambda f: pl.BlockSpec((ATTN_BLOCK, kw), lambda i: (f(i)[0], vcol))
    prev = lambda i: (jnp.maximum(i - 1, 0), 0)
    own = lambda i: (i, 0)
    nxt = lambda i: (jnp.minimum(i + 1, nb - 1), 0)
    in_specs = [pl.BlockSpec((ATTN_BLOCK, A_WIDTH), own)]
    args = [q]
    if local:
        in_specs += [blk(prev), blk(own), blk(nxt), blkv(prev), blkv(own), blkv(nxt)]
        args += [k, k, k, p, p, p]
    in_specs += [pl.BlockSpec((l, kw), lambda i: (0, 0)), pl.BlockSpec((l, kw), lambda i: (0, vcol)),
                 pl.BlockSpec((A_HEADS, LANES), lambda i: (0, 0))]
    args += [k_ctx, p_ctx, sink_tab]
    return pl.pallas_call(
        functools.partial(_attn_kernel, local=local),
        grid=(nb,),
        in_specs=in_specs,
        out_specs=pl.BlockSpec((ATTN_BLOCK, A_WIDTH), own),
        out_shape=jax.ShapeDtypeStruct((t, A_WIDTH), BF16),
        compiler_params=_params(("parallel",)),
    )(*args)


def _gla_kernel(q_ref, k_ref, v_ref, la_ref, s0_ref, o_ref, sf_ref, st_ref, *, rev, nchunk):
    @pl.when(pl.program_id(0) == 0)
    def _():
        st_ref[...] = s0_ref[...]

    r = nchunk * GLA_CHUNK
    sub = min(r, GLA_SUB)
    ii = lax.broadcasted_iota(jnp.int32, (sub, sub), 0)
    jj = lax.broadcasted_iota(jnp.int32, (sub, sub), 1)
    same_chunk = (ii // GLA_CHUNK) == (jj // GLA_CHUNK)
    tri = same_chunk & ((jj >= ii) if rev else (jj <= ii))
    b_all = la_ref[...]
    pos = lax.broadcasted_iota(jnp.int32, b_all.shape, 0) % GLA_CHUNK
    step = 1
    while step < GLA_CHUNK:
        if rev:
            b_all = b_all + jnp.where(pos < GLA_CHUNK - step, pltpu.roll(b_all, r - step, 0), 0.0)
        else:
            b_all = b_all + jnp.where(pos >= step, pltpu.roll(b_all, step, 0), 0.0)
        step *= 2
    heads = []
    for h in range(B_HEADS):
        kc = slice(h * B_DK, (h + 1) * B_DK)
        b = b_all[:, kc]
        b3 = b.reshape(nchunk, GLA_CHUNK, B_DK)
        b_end = b3[:, 0:1, :] if rev else b3[:, GLA_CHUNK - 1:GLA_CHUNK, :]
        k = k_ref[:, kc].astype(F32)
        qe = (q_ref[:, kc].astype(F32) * (B_DK ** -0.5) * jnp.exp(b)).astype(BF16)
        ke = (k * jnp.exp(-b)).astype(BF16)
        kd = (k.reshape(nchunk, GLA_CHUNK, B_DK) * jnp.exp(b_end - b3)).astype(BF16)
        v = v_ref[:, h * B_DV:(h + 1) * B_DV].astype(BF16)
        o_intra = []
        for s0 in range(0, r, sub):
            rs = slice(s0, s0 + sub)
            a = jnp.where(tri, _dot_nt(qe[rs, :], ke[rs, :]), 0.0).astype(BF16)
            o_intra.append(_dot(a, v[rs, :]))
        heads.append((qe, kd, v, jnp.concatenate(o_intra, axis=0), jnp.exp(b_end)))
    st = [st_ref[h] for h in range(B_HEADS)]
    for c in (range(nchunk - 1, -1, -1) if rev else range(nchunk)):
        rows = slice(c * GLA_CHUNK, (c + 1) * GLA_CHUNK)
        for h, (qe, kd, v, o_intra, decay) in enumerate(heads):
            o = o_intra[rows, :] + _dot_nt(qe[rows, :], st[h].astype(BF16))
            o_ref[0, rows, h * B_DV:(h + 1) * B_DV] = o.astype(o_ref.dtype)
            st[h] = st[h] * decay[c] + _dot_tn(v[rows, :], kd[c])
    for h in range(B_HEADS):
        st_ref[h] = st[h]
        sf_ref[h] = st[h]


def _gla_scan(p, la, s0, rev):
    t = p.shape[0]
    r = min(512, t)
    nblk = t // r
    rb = (lambda c: nblk - 1 - c) if rev else (lambda c: c)
    nk = B_HEADS * B_DK
    o, sf = pl.pallas_call(
        functools.partial(_gla_kernel, rev=rev, nchunk=r // GLA_CHUNK),
        grid=(nblk,),
        in_specs=[pl.BlockSpec((r, nk), lambda c: (rb(c), 3584 // nk)),
                  pl.BlockSpec((r, nk), lambda c: (rb(c), 3072 // nk)),
                  pl.BlockSpec((r, B_WIDTH), lambda c: (rb(c), 1024 // B_WIDTH)),
                  pl.BlockSpec((r, nk), lambda c: (rb(c), 1 if rev else 0)),
                  pl.BlockSpec((B_HEADS, B_DV, B_DK), lambda c: (0, 0, 0))],
        out_specs=[pl.BlockSpec((1, r, B_WIDTH), lambda c: (0, rb(c), 0)),
                   pl.BlockSpec((B_HEADS, B_DV, B_DK), lambda c: (0, 0, 0))],
        out_shape=[jax.ShapeDtypeStruct((1, t, B_WIDTH), BF16),
                   jax.ShapeDtypeStruct((B_HEADS, B_DV, B_DK), F32)],
        scratch_shapes=[pltpu.VMEM((B_HEADS, B_DV, B_DK), F32)],
        compiler_params=_params(("arbitrary",)),
    )(p, p, p, la, s0)
    return o[0], sf


def _outproj_kernel(oa_ref, of_ref, ob_ref, og_ref, gg_ref, w_ref, x_ref, gt_ref, o_ref, mix_ref, *, row):
    @pl.when(pl.program_id(1) == 0)
    def _():
        mix_ref[:, 0:A_WIDTH] = oa_ref[...]
        for h in range(B_HEADS):
            cols = slice(h * B_DV, (h + 1) * B_DV)
            o = of_ref[:, cols].astype(F32) + ob_ref[:, cols].astype(F32)
            ms = jnp.mean(o * o, axis=-1, keepdims=True)
            y = o * lax.rsqrt(ms + NORM_EPS) * gg_ref[...]
            mix_ref[:, A_WIDTH + h * B_DV:A_WIDTH + (h + 1) * B_DV] = (y * _silu(og_ref[:, cols].astype(F32))).astype(BF16)

    o_ref[...] = x_ref[...] + gt_ref[row:row + 1, :] * _dot(mix_ref[...], w_ref[...])


def _outproj(oa, o_f, o_b, p, g_gla, w_out, x, ada_l, row):
    t, d = x.shape
    tm = min(512, t)
    tn = 1024
    return pl.pallas_call(
        functools.partial(_outproj_kernel, row=row),
        grid=(t // tm, d // tn),
        in_specs=[pl.BlockSpec((tm, A_WIDTH), lambda i, j: (i, 0)),
                  pl.BlockSpec((tm, B_WIDTH), lambda i, j: (i, 0)),
                  pl.BlockSpec((tm, B_WIDTH), lambda i, j: (i, 0)),
                  pl.BlockSpec((tm, B_WIDTH), lambda i, j: (i, 2048 // B_WIDTH)),
                  pl.BlockSpec((1, B_DV), lambda i, j: (0, 0)),
                  pl.BlockSpec((A_WIDTH + B_WIDTH, tn), lambda i, j: (0, j)),
                  pl.BlockSpec((tm, tn), lambda i, j: (i, j)),
                  pl.BlockSpec((8, tn), lambda i, j: (0, 2 * (d // tn) + j))],
        out_specs=pl.BlockSpec((tm, tn), lambda i, j: (i, j)),
        out_shape=jax.ShapeDtypeStruct((t, d), F32),
        scratch_shapes=[pltpu.VMEM((tm, A_WIDTH + B_WIDTH), BF16)],
        compiler_params=_params(("parallel", "arbitrary")),
    )(oa, o_f, o_b, p, g_gla, w_out, x, ada_l)


def _ffn_kernel(x_ref, g_ref, sh_ref, sc_ref, gt_ref, wg_ref, wu_ref, wd_ref, o_ref, hn_ref, *, row):
    f = pl.program_id(1)

    @pl.when(f == 0)
    def _():
        hn_ref[...] = _norm_mod(x_ref[...], g_ref[...], sh_ref[row:row + 1, :], sc_ref[row:row + 1, :]).astype(BF16)
        o_ref[...] = jnp.zeros_like(o_ref)

    h = hn_ref[...]
    a = (_silu(_dot(h, wg_ref[0].astype(BF16))) * _dot(h, wu_ref[0].astype(BF16))).astype(BF16)
    o_ref[...] += _dot(a, wd_ref[0].astype(BF16))

    @pl.when(f == pl.num_programs(1) - 1)
    def _():
        o_ref[...] = x_ref[...] + gt_ref[row:row + 1, :] * o_ref[...]


def _ffn(x, ada_l, row, g, wg, wu, wd, lyr):
    t, d = x.shape
    ff = wg.shape[2]
    tm = min(1024, t)
    tf = 256
    return pl.pallas_call(
        functools.partial(_ffn_kernel, row=row),
        grid=(t // tm, ff // tf),
        in_specs=[pl.BlockSpec((tm, d), lambda i, f: (i, 0)),
                  pl.BlockSpec((1, d), lambda i, f: (0, 0)),
                  _ada_spec(3), _ada_spec(4), _ada_spec(5),
                  pl.BlockSpec((1, d, tf), lambda i, f: (lyr, 0, f)),
                  pl.BlockSpec((1, d, tf), lambda i, f: (lyr, 0, f)),
                  pl.BlockSpec((1, tf, d), lambda i, f: (lyr, f, 0))],
        out_specs=pl.BlockSpec((tm, d), lambda i, f: (i, 0)),
        out_shape=jax.ShapeDtypeStruct((t, d), F32),
        scratch_shapes=[pltpu.VMEM((tm, d), BF16)],
        compiler_params=_params(("parallel", "arbitrary"), MOE_VMEM_LIMIT),
    )(x, g, ada_l, ada_l, ada_l, wg, wu, wd)


def _pool_router_kernel(x_ref, xp_ref, xn_ref, g_ref, sh_ref, sc_ref, gt_ref, w_ref, ps_ref,
                        g2_ref, sh2_ref, sc2_ref, wr_ref, o_ref, hn_ref, sel_ref, gate_ref, hb_ref, *, row, t):
    i = pl.program_id(0)
    tm = x_ref.shape[0]
    g = g_ref[...]
    sh = sh_ref[row:row + 1, :]
    sc = sc_ref[row:row + 1, :]
    hb_ref[0:POOL_HALO, :] = jnp.where(i > 0, _norm_mod(xp_ref[...], g, sh, sc), 0.0)
    hb_ref[POOL_HALO:POOL_HALO + tm, :] = _norm_mod(x_ref[...], g, sh, sc)
    hb_ref[POOL_HALO + tm:, :] = jnp.where(i < pl.num_programs(0) - 1, _norm_mod(xn_ref[...], g, sh, sc), 0.0)
    tpos = i * tm + lax.broadcasted_iota(jnp.int32, (tm, 1), 0)
    gc = D_MODEL // POOL_GROUPS
    for grp, w in enumerate(POOL_WINDOWS):
        cols = slice(grp * gc, (grp + 1) * gc)
        acc = None
        for dlt in range(-(w // 2), w - w // 2):
            piece = hb_ref[POOL_HALO + dlt:POOL_HALO + dlt + tm, cols]
            acc = piece if acc is None else acc + piece
        cnt = jnp.minimum(tpos + (w - w // 2), t) - jnp.maximum(tpos - w // 2, 0)
        pooled = acc / cnt.astype(F32) - hb_ref[POOL_HALO:POOL_HALO + tm, cols]
        y = _dot(pooled.astype(BF16), w_ref[grp])
        o_ref[:, cols] = x_ref[:, cols] + gt_ref[row:row + 1, cols] * (y * ps_ref[:, cols])

    h = _norm_mod(o_ref[...], g2_ref[...], sh2_ref[row:row + 1, :], sc2_ref[row:row + 1, :])
    hn_ref[...] = h.astype(hn_ref.dtype)
    lane = lax.broadcasted_iota(jnp.int32, (h.shape[0], LANES), 1)
    lane_f = lane.astype(F32)
    logits = jnp.where(lane < N_EXPERTS, _dot(h, wr_ref[...], HIGHEST), -jnp.inf)
    m1 = jnp.max(logits, axis=-1, keepdims=True)
    i1 = jnp.min(jnp.where(logits == m1, lane_f, float(LANES)), axis=-1, keepdims=True)
    rest = jnp.where(lane_f == i1, -jnp.inf, logits)
    m2 = jnp.max(rest, axis=-1, keepdims=True)
    i2 = jnp.min(jnp.where(rest == m2, lane_f, float(LANES)), axis=-1, keepdims=True)
    e2 = jnp.exp(m2 - m1)
    den = 1.0 + e2
    sel_ref[...] = jnp.where(lane == 0, i1, jnp.where(lane == 1, i2, 0.0)).astype(jnp.int32)
    gate_ref[...] = jnp.where(lane == 0, 1.0 / den, jnp.where(lane == 1, e2 / den, 0.0))


def _pool_router(x, ada_l, row, g1, g2, w_pool, pool_scale, w_router_pad):
    t, d = x.shape
    tm = min(256, t)
    nh = t // POOL_HALO
    gc = d // POOL_GROUPS
    rows = lambda i: (i, 0)
    const = lambda i: (0, 0)
    return pl.pallas_call(
        functools.partial(_pool_router_kernel, row=row, t=t),
        grid=(t // tm,),
        in_specs=[pl.BlockSpec((tm, d), rows),
                  pl.BlockSpec((POOL_HALO, d), lambda i: (jnp.maximum(i * (tm // POOL_HALO) - 1, 0), 0)),
                  pl.BlockSpec((POOL_HALO, d), lambda i: (jnp.minimum((i + 1) * (tm // POOL_HALO), nh - 1), 0)),
                  pl.BlockSpec((1, d), const),
                  _ada_spec(0), _ada_spec(1), _ada_spec(2),
                  pl.BlockSpec((POOL_GROUPS, gc, gc), lambda i: (0, 0, 0)),
                  pl.BlockSpec((1, d), const),
                  pl.BlockSpec((1, d), const),
                  _ada_spec(3), _ada_spec(4),
                  pl.BlockSpec((d, LANES), const)],
        out_specs=[pl.BlockSpec((tm, d), rows), pl.BlockSpec((tm, d), rows),
                   pl.BlockSpec((tm, LANES), rows), pl.BlockSpec((tm, LANES), rows)],
        out_shape=[jax.ShapeDtypeStruct((t, d), F32),
                   jax.ShapeDtypeStruct((t, d), BF16),
                   jax.ShapeDtypeStruct((t, LANES), jnp.int32),
                   jax.ShapeDtypeStruct((t, LANES), F32)],
        scratch_shapes=[pltpu.VMEM((tm + 2 * POOL_HALO, d), F32)],
        compiler_params=_params(("parallel",)),
    )(x, x, x, g1, ada_l, ada_l, ada_l, w_pool, pool_scale, g2, ada_l, ada_l, w_router_pad)


MOE_TILE = 1280
MOE_SUB = 256
MOE_SUBS = MOE_TILE // MOE_SUB
GATHER_SUB = 256
GATHER_SUBS = MOE_TILE // GATHER_SUB
SRC_BLK = 256
Y_BLK = 256
CMB_TILE = 256
MOE_TF = 512
GATHER_BUFS = 3


def _moe_ffn_kernel(te_ref, tn_ref, tlo_ref, tnb_ref, trk_ref, cb_ref, posb_ref, hn_hbm, wg_ref, wu_ref, wd_ref,
                    y_ref, xb_ref, acc_ref, hbuf_ref, sem):
    i = pl.program_id(0)
    f = pl.program_id(1)
    nsub = tn_ref[i]

    @pl.when(f == 0)
    def _():
        xb_ref[...] = jnp.zeros_like(xb_ref)
        acc_ref[...] = jnp.zeros_like(acc_ref)
        row_id = lax.broadcasted_iota(jnp.int32, (GATHER_SUB, SRC_BLK), 0) + i * MOE_TILE
        rows_used = nsub * MOE_SUB
        lo = tlo_ref[i]
        n = tnb_ref[i]
        rank0 = trk_ref[i]
        cb_base = te_ref[i] * (posb_ref.shape[0] + 1)

        def blk_copy(b, slot):
            return pltpu.make_async_copy(hn_hbm.at[pl.ds((lo + b) * SRC_BLK, SRC_BLK), :], hbuf_ref.at[slot],
                                         sem.at[slot])

        for j in range(GATHER_BUFS - 1):
            @pl.when(j < n)
            def _():
                blk_copy(j, j).start()

        def body(b, carry):
            slot = b % GATHER_BUFS
            blk_copy(b, slot).wait()
            ahead = b + GATHER_BUFS - 1

            @pl.when(ahead < n)
            def _():
                blk_copy(ahead, ahead % GATHER_BUFS).start()

            blk = lo + b
            pos = posb_ref[blk]
            pos1, pos2 = pos[0:1, :], pos[1:2, :]
            before = cb_ref[cb_base + blk]
            through = cb_ref[cb_base + blk + 1]
            for q in range(GATHER_SUBS):
                @pl.when((q * GATHER_SUB < rows_used) & (before < rank0 + (q + 1) * GATHER_SUB)
                         & (through > rank0 + q * GATHER_SUB))
                def _():
                    rows = slice(q * GATHER_SUB, (q + 1) * GATHER_SUB)
                    rid = row_id + q * GATHER_SUB
                    onehot = jnp.where((pos1 == rid) | (pos2 == rid), 1.0, 0.0).astype(BF16)
                    xb_ref[rows, :] += _dot(onehot, hbuf_ref[slot]).astype(BF16)
            return carry

        lax.fori_loop(0, n, body, 0)

    for k, pred in ((MOE_SUBS, nsub == MOE_SUBS), (MOE_SUBS - 1, (nsub > 0) & (nsub < MOE_SUBS))):
        @pl.when(pred)
        def _():
            rows = slice(0, k * MOE_SUB)
            h = xb_ref[rows, :]
            a = (_silu(_dot(h, wg_ref[0, 0].astype(BF16))) * _dot(h, wu_ref[0, 0].astype(BF16))).astype(BF16)
            acc_ref[rows, :] += _dot(a, wd_ref[0, 0].astype(BF16))

    @pl.when(f == pl.num_programs(1) - 1)
    def _():
        y_ref[...] = acc_ref[...].astype(y_ref.dtype)


def _moe_ffn(hn, posb, tile_expert, tile_subs, tile_blk_lo, tile_blk_n, tile_rank, blk_counts, wg, wu, wd, lyr):
    d = hn.shape[1]
    tm = MOE_TILE
    ff = wg.shape[3]
    tf = MOE_TF
    nf = ff // tf
    nt = tile_expert.shape[0]
    fidx = lambda i, f, tn: jnp.where(tn[i] > 0, f, nf - 1)
    return pl.pallas_call(
        _moe_ffn_kernel,
        grid_spec=pltpu.PrefetchScalarGridSpec(
            num_scalar_prefetch=6,
            grid=(nt, nf),
            in_specs=[pl.BlockSpec(posb.shape, lambda i, f, te, tn, *_: (0, 0, 0)),
                      pl.BlockSpec(memory_space=pl.ANY),
                      pl.BlockSpec((1, 1, d, tf), lambda i, f, te, tn, *_: (lyr, te[i], 0, fidx(i, f, tn))),
                      pl.BlockSpec((1, 1, d, tf), lambda i, f, te, tn, *_: (lyr, te[i], 0, fidx(i, f, tn))),
                      pl.BlockSpec((1, 1, tf, d), lambda i, f, te, tn, *_: (lyr, te[i], fidx(i, f, tn), 0))],
            out_specs=pl.BlockSpec((tm, d), lambda i, f, te, tn, *_: (i, 0)),
            scratch_shapes=[pltpu.VMEM((tm, d), BF16), pltpu.VMEM((tm, d), F32),
                            pltpu.VMEM((GATHER_BUFS, SRC_BLK, d), BF16),
                            pltpu.SemaphoreType.DMA((GATHER_BUFS,))]),
        out_shape=jax.ShapeDtypeStruct((nt * tm, d), BF16),
        compiler_params=_params(("arbitrary", "arbitrary"), MOE_VMEM_LIMIT),
    )(tile_expert, tile_subs, tile_blk_lo, tile_blk_n, tile_rank, blk_counts, posb, hn, wg, wu, wd)


CMB_SLOTS = [(e, j) for e in range(N_EXPERTS) for j in range(2)]


def _combine_kernel(fb_ref, nb_ref, x_ref, pos_ref, gate_ref, gt_ref, y_hbm, o_ref, acc_ref, ybuf_ref, sem,
                    *, row, tile_off):
    base = (tile_off + pl.program_id(0)) * N_EXPERTS
    tm = x_ref.shape[0]
    pos1, pos2 = pos_ref[:, 0:1], pos_ref[:, 1:2]
    g1, g2 = gate_ref[:, 0:1], gate_ref[:, 1:2]
    col = lax.broadcasted_iota(jnp.int32, (tm, Y_BLK), 1)

    def blk(s):
        e, j = CMB_SLOTS[s]
        return fb_ref[base + e] + j

    def used(s):
        e, j = CMB_SLOTS[s]
        return nb_ref[base + e] > j

    def blk_copy(s):
        return pltpu.make_async_copy(y_hbm.at[pl.ds(blk(s) * Y_BLK, Y_BLK), :], ybuf_ref.at[s], sem.at[s])

    for s in range(len(CMB_SLOTS)):
        @pl.when(used(s))
        def _():
            blk_copy(s).start()

    acc_ref[...] = jnp.zeros_like(acc_ref)
    for s in range(len(CMB_SLOTS)):
        @pl.when(used(s))
        def _():
            blk_copy(s).wait()
            off = blk(s) * Y_BLK
            w = jnp.where(pos1 - off == col, g1, 0.0) + jnp.where(pos2 - off == col, g2, 0.0)
            acc_ref[...] += _dot(w.astype(BF16), ybuf_ref[s])

    o_ref[...] = x_ref[...] + gt_ref[row:row + 1, :] * acc_ref[...]


def _combine(x, y, pos, gates, first_blk, num_blk, tok_off, ada_l, row):
    t, d = x.shape
    tm = CMB_TILE
    tile_off = tok_off // tm
    return pl.pallas_call(
        functools.partial(_combine_kernel, row=row, tile_off=tile_off),
        grid_spec=pltpu.PrefetchScalarGridSpec(
            num_scalar_prefetch=2,
            grid=(t // tm,),
            in_specs=[pl.BlockSpec((tm, d), lambda i, fb, nb: (i, 0)),
                      pl.BlockSpec((tm, 2), lambda i, fb, nb: (tile_off + i, 0)),
                      pl.BlockSpec((tm, LANES), lambda i, fb, nb: (i, 0)),
                      _ada_spec(5),
                      pl.BlockSpec(memory_space=pl.ANY)],
            out_specs=pl.BlockSpec((tm, d), lambda i, fb, nb: (i, 0)),
            scratch_shapes=[pltpu.VMEM((tm, d), F32), pltpu.VMEM((len(CMB_SLOTS), Y_BLK, d), BF16),
                            pltpu.SemaphoreType.DMA((len(CMB_SLOTS),))]),
        out_shape=jax.ShapeDtypeStruct((t, d), F32),
        compiler_params=_params(("arbitrary",)),
    )(first_blk, num_blk, x, pos, gates, ada_l, y)


def _route(sel):
    tm = MOE_TILE
    n_tok = sel.shape[0]
    n_asg = 2 * n_tok
    nt = -(-n_asg // tm) + N_EXPERTS
    e_flat = sel.reshape(-1)
    onehot = (e_flat[:, None] == jnp.arange(N_EXPERTS, dtype=jnp.int32)[None, :]).astype(jnp.int32)
    csum = jnp.cumsum(onehot, axis=0)
    count = csum[-1]
    rank = jnp.sum((csum - onehot) * onehot, axis=1)
    tiles_e = (count + tm - 1) // tm
    tile_end = jnp.cumsum(tiles_e)
    base = (tile_end - tiles_e) * tm
    pos = jnp.sum(onehot * base[None, :], axis=1) + rank
    tile_id = jnp.arange(nt, dtype=jnp.int32)
    used = tile_end[-1]
    owner = lambda i: jnp.minimum(jnp.sum((i[:, None] >= tile_end[None, :]).astype(jnp.int32), axis=1), N_EXPERTS - 1)
    tile_expert = owner(jnp.minimum(tile_id, used - 1))
    rows_in_tile = jnp.clip((base + count)[tile_expert] - tile_id * tm, 0, tm)
    tile_subs = jnp.where(tile_id < used, (rows_in_tile + MOE_SUB - 1) // MOE_SUB, 0)
    n_blk = n_tok // SRC_BLK
    blk_counts = jnp.concatenate([jnp.zeros((1, N_EXPERTS), jnp.int32),
                                  csum.reshape(n_blk, 2 * SRC_BLK, N_EXPERTS)[:, -1, :]], axis=0).T
    tile_rank = tile_id * tm - base[tile_expert]
    counts_t = blk_counts[tile_expert]
    tile_blk_lo = jnp.sum((counts_t[:, 1:] <= tile_rank[:, None]).astype(jnp.int32), axis=1)
    tile_blk_hi = jnp.sum((counts_t[:, :-1] < (tile_rank + rows_in_tile)[:, None]).astype(jnp.int32), axis=1)
    tile_blk_n = jnp.where(tile_subs > 0, jnp.maximum(tile_blk_hi - tile_blk_lo, 0), 0)
    tile_blk_lo = jnp.minimum(tile_blk_lo, n_blk - 1)
    posb = jnp.pad(pos.reshape(n_blk, SRC_BLK, 2).transpose(0, 2, 1), ((0, 0), (0, 6), (0, 0)), constant_values=-1)
    ntt = n_tok // CMB_TILE
    through = csum.reshape(ntt, 2 * CMB_TILE, N_EXPERTS)[:, -1, :]
    before = jnp.concatenate([jnp.zeros((1, N_EXPERTS), jnp.int32), through[:-1]], axis=0)
    lo = base[None, :] + before
    hi = base[None, :] + through - 1
    first_blk = lo // Y_BLK
    num_blk = jnp.where(through > before, hi // Y_BLK - first_blk + 1, 0)
    return (posb, pos.reshape(n_tok, 2), tile_expert, tile_subs, tile_blk_lo, tile_blk_n, tile_rank,
            blk_counts.reshape(-1), first_blk.reshape(-1), num_blk.reshape(-1))


def _moe(streams, ada_l, wg, wu, wd, lyr):
    hs = [s[2] for s in streams]
    sels = [s[3][:, :2] for s in streams]
    n_tok = sum(h.shape[0] for h in hs)
    pad = -n_tok % SRC_BLK
    if pad:
        hs.append(jnp.zeros((pad, D_MODEL), BF16))
    h_all = hs[0] if len(hs) == 1 else jnp.concatenate(hs, axis=0)
    sel_all = sels[0] if len(sels) == 1 else jnp.concatenate(sels, axis=0)
    (posb, pos, tile_expert, tile_subs, tile_blk_lo, tile_blk_n, tile_rank, blk_counts,
     first_blk, num_blk) = _route(sel_all)
    y = _moe_ffn(h_all, posb, tile_expert, tile_subs, tile_blk_lo, tile_blk_n, tile_rank, blk_counts,
                 wg, wu, wd, lyr)
    outs, off = [], 0
    for x, row, _, _, gt in streams:
        outs.append(_combine(x, y, pos, gt, first_blk, num_blk, off, ada_l, row))
        off += x.shape[0]
    return outs


def _even_layer(x, ctx, ada_l, g1, g2, w_in, g_qn, g_kn, sink, w_gate_up, b_gate_up, g_gla, w_out,
                w_ffn_gate, w_ffn_up, w_ffn_down, lyr, rope_tabs, ctx_out):
    d = D_MODEL
    w_main = jnp.concatenate([w_in[:, 2080:3104], w_in[:, 1024:2048], w_in[:, 3616:4640], w_in[:, 512:1024],
                              w_in[:, 3104:3616], w_in[:, 0:256], w_in[:, 256:512]], axis=1).astype(BF16)
    w_gate = jnp.pad(w_in[:, 2048:2080], ((0, 0), (0, LANES - 2 * GATE_RANK))).astype(BF16)
    nk = B_HEADS * B_DK
    w_up = jnp.zeros((LANES, 2 * nk), F32)
    w_up = w_up.at[0:GATE_RANK, 0:nk].set(w_gate_up[0]).at[GATE_RANK:2 * GATE_RANK, nk:].set(w_gate_up[1])
    b_up = b_gate_up.reshape(1, 2 * nk)
    w_out_b = w_out.astype(BF16)
    gq, gk, gg = g_qn.reshape(1, -1), g_kn.reshape(1, -1), g_gla.reshape(1, -1)
    sink_tab = jnp.broadcast_to(sink[:, None], (A_HEADS, LANES))
    cos, sin = rope_tabs

    pc, la_c = _inproj(ctx, ada_l, 1, g1, w_main, w_gate, w_up, b_up)
    qc, kc = _qkprep(pc, cos, sin, gq, gk, rope=False)
    s0 = jnp.zeros((B_HEADS, B_DV, B_DK), F32)
    oc_f, s_fwd = _gla_scan(pc, la_c, s0, rev=False)
    oc_b, s_bwd = _gla_scan(pc, la_c, s0, rev=True)

    px, la_x = _inproj(x, ada_l, 0, g1, w_main, w_gate, w_up, b_up)
    qx, kx = _qkprep(px, cos, sin, gq, gk, rope=True)
    oa = _attention(qx, kx, px, kc, pc, sink_tab, local=True)
    ox_f, _ = _gla_scan(px, la_x, s_fwd, rev=False)
    ox_b, _ = _gla_scan(px, la_x, s_bwd, rev=True)
    x = _outproj(oa, ox_f, ox_b, px, gg, w_out_b, x, ada_l, 0)
    x = _ffn(x, ada_l, 0, g2, w_ffn_gate, w_ffn_up, w_ffn_down, lyr)
    if ctx_out:
        oa_c = _attention(qc, None, None, kc, pc, sink_tab, local=False)
        ctx = _outproj(oa_c, oc_f, oc_b, pc, gg, w_out_b, ctx, ada_l, 1)
        ctx = _ffn(ctx, ada_l, 1, g2, w_ffn_gate, w_ffn_up, w_ffn_down, lyr)
    return x, ctx


def _odd_layer(x, ctx, ada_l, g1, g2, w_pool, pool_scale, w_router, w_exp_gate, w_exp_up, w_exp_down, lyr, ctx_out):
    w_pool_b = w_pool.astype(BF16)
    ps = pool_scale.reshape(1, -1)
    wr = jnp.pad(w_router, ((0, 0), (0, LANES - N_EXPERTS)))
    streams = [(0,) + tuple(_pool_router(x, ada_l, 0, g1, g2, w_pool_b, ps, wr))]
    if ctx_out:
        streams.append((1,) + tuple(_pool_router(ctx, ada_l, 1, g1, g2, w_pool_b, ps, wr)))
    outs = _moe([(s[1], s[0], s[2], s[3], s[4]) for s in streams], ada_l, w_exp_gate, w_exp_up, w_exp_down, lyr)
    return outs[0], (outs[1] if ctx_out else ctx)


def kernel(x, c, ctx, c_ctx, w_ada, b_ada, norm_g, w_in, g_qn, g_kn, attn_sink, w_gate_up, b_gate_up, g_gla, w_out,
           w_ffn_gate, w_ffn_up, w_ffn_down, w_pool, pool_scale, w_router, w_exp_gate, w_exp_up, w_exp_down):
    depth = w_ada.shape[0]
    xs = x[0]
    cs = ctx[0]
    t = xs.shape[0]
    cond = jnp.zeros((8, D_MODEL), F32).at[0].set(c[0]).at[1].set(c_ctx)
    ada = _ada_all(cond, w_ada, b_ada)
    rope_tabs = _rope_tables(t)
    for l in range(depth):
        ctx_later = any(j % 2 == 0 for j in range(l + 1, depth))
        g1 = norm_g[l, 0].reshape(1, -1)
        g2 = norm_g[l, 1].reshape(1, -1)
        if l % 2 == 0:
            e = l // 2
            xs, cs = _even_layer(xs, cs, ada[l], g1, g2, w_in[e], g_qn[e], g_kn[e], attn_sink[e], w_gate_up[e],
                                 b_gate_up[e], g_gla[e], w_out[e], w_ffn_gate, w_ffn_up, w_ffn_down, e,
                                 rope_tabs, ctx_later)
        else:
            o = l // 2
            xs, cs = _odd_layer(xs, cs, ada[l], g1, g2, w_pool[o], pool_scale[o], w_router[o],
                                w_exp_gate, w_exp_up, w_exp_down, o, ctx_later)
    return xs[None]
```

```python
import functools

import jax
import jax.numpy as jnp
import numpy as np
from jax import lax
from jax.experimental import pallas as pl
from jax.experimental.pallas import tpu as pltpu

F32 = jnp.float32
BF16 = jnp.bfloat16
HIGHEST = lax.Precision.HIGHEST

D_MODEL = 2048
GRID_W = 64
A_HEADS = 8
A_KV_HEADS = 2
A_GROUP = 4
A_HEAD_DIM = 128
A_WIDTH = A_HEADS * A_HEAD_DIM
ATTN_BLOCK = 128
ROPE_BASE = 10000.0
B_HEADS = 4
B_DV = 256
B_DK = 128
B_WIDTH = B_HEADS * B_DV
GATE_RANK = 16
GATE_TAU = 16.0
GLA_CHUNK = 64
GLA_SUB = 256
POOL_GROUPS = 4
POOL_WINDOWS = (2, 4, 8, 16)
POOL_HALO = 8
N_EXPERTS = 8
NORM_EPS = 1e-6
NEG_INF = -1e30
LANES = 128

P_COLS = 4608
VMEM_LIMIT = 56 * 1024 * 1024
MOE_VMEM_LIMIT = 62 * 1024 * 1024


def _params(sem, vmem=VMEM_LIMIT):
    return pltpu.CompilerParams(dimension_semantics=sem, vmem_limit_bytes=vmem)


def _dot(a, b, precision=None):
    return jnp.dot(a, b, preferred_element_type=F32, precision=precision)


def _dot_nt(a, b):
    return lax.dot_general(a, b, (((1,), (1,)), ((), ())), preferred_element_type=F32)


def _dot_tn(a, b):
    return lax.dot_general(a, b, (((0,), (0,)), ((), ())), preferred_element_type=F32)


def _silu(x):
    return x * jax.nn.sigmoid(x)


def _norm_mod(x, g, shift, scale):
    ms = jnp.mean(x * x, axis=-1, keepdims=True)
    return (x * lax.rsqrt(ms + NORM_EPS) * g) * (1.0 + scale) + shift


def _ada_kernel(cond_ref, w_ref, b_ref, o_ref):
    s = _silu(cond_ref[...])
    s_hi = s.astype(BF16)
    s_lo = (s - s_hi.astype(F32)).astype(BF16)
    w = w_ref[0].astype(BF16)
    o_ref[0] = _dot(s_hi, w) + _dot(s_lo, w) + b_ref[0]


def _ada_all(cond, w_ada, b_ada):
    depth, d, n = w_ada.shape
    tn = 1024
    return pl.pallas_call(
        _ada_kernel,
        grid=(depth, n // tn),
        in_specs=[pl.BlockSpec((8, d), lambda l, j: (0, 0)),
                  pl.BlockSpec((1, d, tn), lambda l, j: (l, 0, j)),
                  pl.BlockSpec((1, 1, tn), lambda l, j: (l, 0, j))],
        out_specs=pl.BlockSpec((1, 8, tn), lambda l, j: (l, 0, j)),
        out_shape=jax.ShapeDtypeStruct((depth, 8, n), F32),
        compiler_params=_params(("parallel", "parallel")),
    )(cond, w_ada, b_ada.reshape(depth, 1, n))


def _ada_spec(k):
    return pl.BlockSpec((8, D_MODEL), lambda *_: (0, k))


def _inproj_kernel(x_ref, g_ref, sh_ref, sc_ref, w_ref, wg_ref, wup_ref, bup_ref, p_ref, la_ref, hn_ref, *, row):
    @pl.when(pl.program_id(1) == 0)
    def _():
        h = _norm_mod(x_ref[...], g_ref[...], sh_ref[row:row + 1, :], sc_ref[row:row + 1, :]).astype(BF16)
        hn_ref[...] = h
        lr = _dot(h, wg_ref[...])
        z = _dot(lr, wup_ref[...], HIGHEST) + bup_ref[...]
        la_ref[...] = (jnp.minimum(z, 0.0) - jnp.log1p(jnp.exp(-jnp.abs(z)))) * (1.0 / GATE_TAU)

    p_ref[...] = _dot(hn_ref[...], w_ref[...]).astype(p_ref.dtype)


def _inproj(x, ada_l, row, g, w_main, w_gate, w_up, b_up):
    t, d = x.shape
    tm = min(512, t)
    tn = 1536
    nla = 2 * B_HEADS * B_DK
    return pl.pallas_call(
        functools.partial(_inproj_kernel, row=row),
        grid=(t // tm, P_COLS // tn),
        in_specs=[pl.BlockSpec((tm, d), lambda i, j: (i, 0)),
                  pl.BlockSpec((1, d), lambda i, j: (0, 0)),
                  _ada_spec(0), _ada_spec(1),
                  pl.BlockSpec((d, tn), lambda i, j: (0, j)),
                  pl.BlockSpec((d, LANES), lambda i, j: (0, 0)),
                  pl.BlockSpec((LANES, nla), lambda i, j: (0, 0)),
                  pl.BlockSpec((1, nla), lambda i, j: (0, 0))],
        out_specs=[pl.BlockSpec((tm, tn), lambda i, j: (i, j)),
                   pl.BlockSpec((tm, nla), lambda i, j: (i, 0))],
        out_shape=[jax.ShapeDtypeStruct((t, P_COLS), BF16),
                   jax.ShapeDtypeStruct((t, nla), F32)],
        scratch_shapes=[pltpu.VMEM((tm, d), BF16)],
        compiler_params=_params(("parallel", "arbitrary")),
    )(x, g, ada_l, ada_l, w_main, w_gate, w_up, b_up)


def _qkprep_kernel(q_ref, k_ref, cos_ref, sin_ref, gq_ref, gk_ref, qo_ref, ko_ref, *, rope):
    tm = q_ref.shape[0]
    lane = lax.broadcasted_iota(jnp.int32, (tm, A_HEAD_DIM), 1)
    first_half = (lane % 64) < 32

    def prep(xh, g, scale):
        ms = jnp.mean(xh * xh, axis=-1, keepdims=True)
        y = xh * lax.rsqrt(ms + NORM_EPS) * g
        if rope:
            partner = jnp.where(first_half, pltpu.roll(y, 96, 1), pltpu.roll(y, 32, 1))
            y = y * cos_ref[...] + partner * sin_ref[...]
        return (y * scale).astype(BF16)

    for h in range(A_HEADS):
        cols = slice(h * A_HEAD_DIM, (h + 1) * A_HEAD_DIM)
        qo_ref[:, cols] = prep(q_ref[:, cols].astype(F32), gq_ref[...], A_HEAD_DIM ** -0.5)
    for h in range(A_KV_HEADS):
        cols = slice(h * A_HEAD_DIM, (h + 1) * A_HEAD_DIM)
        ko_ref[:, cols] = prep(k_ref[:, cols].astype(F32), gk_ref[...], 1.0)


def _qkprep(p, cos, sin, g_qn, g_kn, rope):
    t = p.shape[0]
    tm = min(512, t)
    kw = A_KV_HEADS * A_HEAD_DIM
    return pl.pallas_call(
        functools.partial(_qkprep_kernel, rope=rope),
        grid=(t // tm,),
        in_specs=[pl.BlockSpec((tm, A_WIDTH), lambda i: (i, 0)),
                  pl.BlockSpec((tm, kw), lambda i: (i, 4096 // kw)),
                  pl.BlockSpec((tm, A_HEAD_DIM), lambda i: (i, 0)),
                  pl.BlockSpec((tm, A_HEAD_DIM), lambda i: (i, 0)),
                  pl.BlockSpec((1, A_HEAD_DIM), lambda i: (0, 0)),
                  pl.BlockSpec((1, A_HEAD_DIM), lambda i: (0, 0))],
        out_specs=[pl.BlockSpec((tm, A_WIDTH), lambda i: (i, 0)),
                   pl.BlockSpec((tm, kw), lambda i: (i, 0))],
        out_shape=[jax.ShapeDtypeStruct((t, A_WIDTH), BF16),
                   jax.ShapeDtypeStruct((t, kw), BF16)],
        compiler_params=_params(("parallel",)),
    )(p, p, cos, sin, g_qn, g_kn)


def _rope_tables(t):
    half = A_HEAD_DIM // 4
    freqs = ROPE_BASE ** (-np.arange(half, dtype=np.float64) / half)
    tok = np.arange(t)
    row = (tok // GRID_W).astype(np.float64)[:, None] * freqs
    col = (tok % GRID_W).astype(np.float64)[:, None] * freqs
    cos = np.concatenate([np.cos(row), np.cos(row), np.cos(col), np.cos(col)], axis=-1)
    sin = np.concatenate([-np.sin(row), np.sin(row), -np.sin(col), np.sin(col)], axis=-1)
    return jnp.asarray(cos.astype(np.float32)), jnp.asarray(sin.astype(np.float32))


def _softmax_pv(scores, values, sink_col):
    def lane_blocks(xs):
        return [x[:, i * LANES:(i + 1) * LANES] for x in xs for i in range(x.shape[1] // LANES)]

    m = jnp.maximum(jnp.max(functools.reduce(jnp.maximum, lane_blocks(scores)), axis=-1, keepdims=True), sink_col)
    probs = [jnp.exp(s - m) for s in scores]
    denom = jnp.exp(sink_col - m) + jnp.sum(functools.reduce(jnp.add, lane_blocks(probs)), axis=-1, keepdims=True)
    acc = None
    for p, v in zip(probs, values):
        pv = _dot(p.astype(BF16), v)
        acc = pv if acc is None else acc + pv
    return acc / denom


def _attn_kernel(*refs, local):
    if local:
        q_ref, kp_ref, ko_ref, kn_ref, vp_ref, vo_ref, vn_ref, kc_ref, vc_ref, sink_ref, o_ref = refs
    else:
        q_ref, kc_ref, vc_ref, sink_ref, o_ref = refs
    n = pl.program_id(0)
    nb = pl.num_programs(0)
    rows = A_GROUP * ATTN_BLOCK
    if local:
        qi = lax.broadcasted_iota(jnp.int32, (rows, ATTN_BLOCK), 0) % ATTN_BLOCK
        kj = lax.broadcasted_iota(jnp.int32, (rows, ATTN_BLOCK), 1)
        mask_prev = (kj >= qi) & (n > 0)
        mask_next = (kj <= qi) & (n < nb - 1)
    for kv in range(A_KV_HEADS):
        kc = slice(kv * A_HEAD_DIM, (kv + 1) * A_HEAD_DIM)
        heads = [kv * A_GROUP + g for g in range(A_GROUP)]
        q4 = jnp.concatenate([q_ref[:, h * A_HEAD_DIM:(h + 1) * A_HEAD_DIM] for h in heads], axis=0)
        sink_col = jnp.concatenate(
            [jnp.broadcast_to(sink_ref[h:h + 1, 0:1], (ATTN_BLOCK, 1)) for h in heads], axis=0)
        scores, values = [], []
        if local:
            scores.append(jnp.where(mask_prev, _dot_nt(q4, kp_ref[:, kc]), NEG_INF))
            scores.append(_dot_nt(q4, ko_ref[:, kc]))
            scores.append(jnp.where(mask_next, _dot_nt(q4, kn_ref[:, kc]), NEG_INF))
            values += [vp_ref[:, kc], vo_ref[:, kc], vn_ref[:, kc]]
        scores.append(_dot_nt(q4, kc_ref[:, kc]))
        values.append(vc_ref[:, kc])
        o = _softmax_pv(scores, values, sink_col).astype(o_ref.dtype)
        for g, h in enumerate(heads):
            o_ref[:, h * A_HEAD_DIM:(h + 1) * A_HEAD_DIM] = o[g * ATTN_BLOCK:(g + 1) * ATTN_BLOCK, :]


def _attention(q, k, p, k_ctx, p_ctx, sink_tab, local):
    t = q.shape[0]
    nb = t // ATTN_BLOCK
    l = k_ctx.shape[0]
    kw = A_KV_HEADS * A_HEAD_DIM
    vcol = 4352 // kw
    blk = lambda f: pl.BlockSpec((ATTN_BLOCK, kw), f)
    blkv = lambda f: pl.BlockSpec((ATTN_BLOCK, kw), lambda i: (f(i)[0], vcol))
    prev = lambda i: (jnp.maximum(i - 1, 0), 0)
    own = lambda i: (i, 0)
    nxt = lambda i: (jnp.minimum(i + 1, nb - 1), 0)
    in_specs = [pl.BlockSpec((ATTN_BLOCK, A_WIDTH), own)]
    args = [q]
    if local:
        in_specs += [blk(prev), blk(own), blk(nxt), blkv(prev), blkv(own), blkv(nxt)]
        args += [k, k, k, p, p, p]
    in_specs += [pl.BlockSpec((l, kw), lambda i: (0, 0)), pl.BlockSpec((l, kw), lambda i: (0, vcol)),
                 pl.BlockSpec((A_HEADS, LANES), lambda i: (0, 0))]
    args += [k_ctx, p_ctx, sink_tab]
    return pl.pallas_call(
        functools.partial(_attn_kernel, local=local),
        grid=(nb,),
        in_specs=in_specs,
        out_specs=pl.BlockSpec((ATTN_BLOCK, A_WIDTH), own),
        out_shape=jax.ShapeDtypeStruct((t, A_WIDTH), BF16),
        compiler_params=_params(("parallel",)),
    )(*args)


def _gla_kernel(q_ref, k_ref, v_ref, la_ref, s0_ref, o_ref, sf_ref, st_ref, *, rev, nchunk):
    @pl.when(pl.program_id(0) == 0)
    def _():
        st_ref[...] = s0_ref[...]

    r = nchunk * GLA_CHUNK
    sub = min(r, GLA_SUB)
    ii = lax.broadcasted_iota(jnp.int32, (sub, sub), 0)
    jj = lax.broadcasted_iota(jnp.int32, (sub, sub), 1)
    same_chunk = (ii // GLA_CHUNK) == (jj // GLA_CHUNK)
    tri = same_chunk & ((jj >= ii) if rev else (jj <= ii))
    b_all = la_ref[...]
    pos = lax.broadcasted_iota(jnp.int32, b_all.shape, 0) % GLA_CHUNK
    step = 1
    while step < GLA_CHUNK:
        if rev:
            b_all = b_all + jnp.where(pos < GLA_CHUNK - step, pltpu.roll(b_all, r - step, 0), 0.0)
        else:
            b_all = b_all + jnp.where(pos >= step, pltpu.roll(b_all, step, 0), 0.0)
        step *= 2
    heads = []
    for h in range(B_HEADS):
        kc = slice(h * B_DK, (h + 1) * B_DK)
        b = b_all[:, kc]
        b3 = b.reshape(nchunk, GLA_CHUNK, B_DK)
        b_end = b3[:, 0:1, :] if rev else b3[:, GLA_CHUNK - 1:GLA_CHUNK, :]
        k = k_ref[:, kc].astype(F32)
        qe = (q_ref[:, kc].astype(F32) * (B_DK ** -0.5) * jnp.exp(b)).astype(BF16)
        ke = (k * jnp.exp(-b)).astype(BF16)
        kd = (k.reshape(nchunk, GLA_CHUNK, B_DK) * jnp.exp(b_end - b3)).astype(BF16)
        v = v_ref[:, h * B_DV:(h + 1) * B_DV].astype(BF16)
        o_intra = []
        for s0 in range(0, r, sub):
            rs = slice(s0, s0 + sub)
            a = jnp.where(tri, _dot_nt(qe[rs, :], ke[rs, :]), 0.0).astype(BF16)
            o_intra.append(_dot(a, v[rs, :]))
        heads.append((qe, kd, v, jnp.concatenate(o_intra, axis=0), jnp.exp(b_end)))
    st = [st_ref[h] for h in range(B_HEADS)]
    for c in (range(nchunk - 1, -1, -1) if rev else range(nchunk)):
        rows = slice(c * GLA_CHUNK, (c + 1) * GLA_CHUNK)
        for h, (qe, kd, v, o_intra, decay) in enumerate(heads):
            o = o_intra[rows, :] + _dot_nt(qe[rows, :], st[h].astype(BF16))
            o_ref[0, rows, h * B_DV:(h + 1) * B_DV] = o.astype(o_ref.dtype)
            st[h] = st[h] * decay[c] + _dot_tn(v[rows, :], kd[c])
    for h in range(B_HEADS):
        st_ref[h] = st[h]
        sf_ref[h] = st[h]


def _gla_scan(p, la, s0, rev):
    t = p.shape[0]
    r = min(512, t)
    nblk = t // r
    rb = (lambda c: nblk - 1 - c) if rev else (lambda c: c)
    nk = B_HEADS * B_DK
    o, sf = pl.pallas_call(
        functools.partial(_gla_kernel, rev=rev, nchunk=r // GLA_CHUNK),
        grid=(nblk,),
        in_specs=[pl.BlockSpec((r, nk), lambda c: (rb(c), 3584 // nk)),
                  pl.BlockSpec((r, nk), lambda c: (rb(c), 3072 // nk)),
                  pl.BlockSpec((r, B_WIDTH), lambda c: (rb(c), 1024 // B_WIDTH)),
                  pl.BlockSpec((r, nk), lambda c: (rb(c), 1 if rev else 0)),
                  pl.BlockSpec((B_HEADS, B_DV, B_DK), lambda c: (0, 0, 0))],
        out_specs=[pl.BlockSpec((1, r, B_WIDTH), lambda c: (0, rb(c), 0)),
                   pl.BlockSpec((B_HEADS, B_DV, B_DK), lambda c: (0, 0, 0))],
        out_shape=[jax.ShapeDtypeStruct((1, t, B_WIDTH), BF16),
                   jax.ShapeDtypeStruct((B_HEADS, B_DV, B_DK), F32)],
        scratch_shapes=[pltpu.VMEM((B_HEADS, B_DV, B_DK), F32)],
        compiler_params=_params(("arbitrary",)),
    )(p, p, p, la, s0)
    return o[0], sf


def _outproj_kernel(oa_ref, of_ref, ob_ref, og_ref, gg_ref, w_ref, x_ref, gt_ref, o_ref, mix_ref, *, row):
    @pl.when(pl.program_id(1) == 0)
    def _():
        mix_ref[:, 0:A_WIDTH] = oa_ref[...]
        for h in range(B_HEADS):
            cols = slice(h * B_DV, (h + 1) * B_DV)
            o = of_ref[:, cols].astype(F32) + ob_ref[:, cols].astype(F32)
            ms = jnp.mean(o * o, axis=-1, keepdims=True)
            y = o * lax.rsqrt(ms + NORM_EPS) * gg_ref[...]
            mix_ref[:, A_WIDTH + h * B_DV:A_WIDTH + (h + 1) * B_DV] = (y * _silu(og_ref[:, cols].astype(F32))).astype(BF16)

    o_ref[...] = x_ref[...] + gt_ref[row:row + 1, :] * _dot(mix_ref[...], w_ref[...])


def _outproj(oa, o_f, o_b, p, g_gla, w_out, x, ada_l, row):
    t, d = x.shape
    tm = min(512, t)
    tn = 1024
    return pl.pallas_call(
        functools.partial(_outproj_kernel, row=row),
        grid=(t // tm, d // tn),
        in_specs=[pl.BlockSpec((tm, A_WIDTH), lambda i, j: (i, 0)),
                  pl.BlockSpec((tm, B_WIDTH), lambda i, j: (i, 0)),
                  pl.BlockSpec((tm, B_WIDTH), lambda i, j: (i, 0)),
                  pl.BlockSpec((tm, B_WIDTH), lambda i, j: (i, 2048 // B_WIDTH)),
                  pl.BlockSpec((1, B_DV), lambda i, j: (0, 0)),
                  pl.BlockSpec((A_WIDTH + B_WIDTH, tn), lambda i, j: (0, j)),
                  pl.BlockSpec((tm, tn), lambda i, j: (i, j)),
                  pl.BlockSpec((8, tn), lambda i, j: (0, 2 * (d // tn) + j))],
        out_specs=pl.BlockSpec((tm, tn), lambda i, j: (i, j)),
        out_shape=jax.ShapeDtypeStruct((t, d), F32),
        scratch_shapes=[pltpu.VMEM((tm, A_WIDTH + B_WIDTH), BF16)],
        compiler_params=_params(("parallel", "arbitrary")),
    )(oa, o_f, o_b, p, g_gla, w_out, x, ada_l)


def _ffn_kernel(x_ref, g_ref, sh_ref, sc_ref, gt_ref, wg_ref, wu_ref, wd_ref, o_ref, hn_ref, *, row):
    f = pl.program_id(1)

    @pl.when(f == 0)
    def _():
        hn_ref[...] = _norm_mod(x_ref[...], g_ref[...], sh_ref[row:row + 1, :], sc_ref[row:row + 1, :]).astype(BF16)
        o_ref[...] = jnp.zeros_like(o_ref)

    h = hn_ref[...]
    a = (_silu(_dot(h, wg_ref[0].astype(BF16))) * _dot(h, wu_ref[0].astype(BF16))).astype(BF16)
    o_ref[...] += _dot(a, wd_ref[0].astype(BF16))

    @pl.when(f == pl.num_programs(1) - 1)
    def _():
        o_ref[...] = x_ref[...] + gt_ref[row:row + 1, :] * o_ref[...]


def _ffn(x, ada_l, row, g, wg, wu, wd, lyr):
    t, d = x.shape
    ff = wg.shape[2]
    tm = min(1024, t)
    tf = 256
    return pl.pallas_call(
        functools.partial(_ffn_kernel, row=row),
        grid=(t // tm, ff // tf),
        in_specs=[pl.BlockSpec((tm, d), lambda i, f: (i, 0)),
                  pl.BlockSpec((1, d), lambda i, f: (0, 0)),
                  _ada_spec(3), _ada_spec(4), _ada_spec(5),
                  pl.BlockSpec((1, d, tf), lambda i, f: (lyr, 0, f)),
                  pl.BlockSpec((1, d, tf), lambda i, f: (lyr, 0, f)),
                  pl.BlockSpec((1, tf, d), lambda i, f: (lyr, f, 0))],
        out_specs=pl.BlockSpec((tm, d), lambda i, f: (i, 0)),
        out_shape=jax.ShapeDtypeStruct((t, d), F32),
        scratch_shapes=[pltpu.VMEM((tm, d), BF16)],
        compiler_params=_params(("parallel", "arbitrary"), MOE_VMEM_LIMIT),
    )(x, g, ada_l, ada_l, ada_l, wg, wu, wd)


def _pool_router_kernel(x_ref, xp_ref, xn_ref, g_ref, sh_ref, sc_ref, gt_ref, w_ref, ps_ref,
                        g2_ref, sh2_ref, sc2_ref, wr_ref, o_ref, hn_ref, sel_ref, gate_ref, hb_ref, *, row, t):
    i = pl.program_id(0)
    tm = x_ref.shape[0]
    g = g_ref[...]
    sh = sh_ref[row:row + 1, :]
    sc = sc_ref[row:row + 1, :]
    hb_ref[0:POOL_HALO, :] = jnp.where(i > 0, _norm_mod(xp_ref[...], g, sh, sc), 0.0)
    hb_ref[POOL_HALO:POOL_HALO + tm, :] = _norm_mod(x_ref[...], g, sh, sc)
    hb_ref[POOL_HALO + tm:, :] = jnp.where(i < pl.num_programs(0) - 1, _norm_mod(xn_ref[...], g, sh, sc), 0.0)
    tpos = i * tm + lax.broadcasted_iota(jnp.int32, (tm, 1), 0)
    gc = D_MODEL // POOL_GROUPS
    for grp, w in enumerate(POOL_WINDOWS):
        cols = slice(grp * gc, (grp + 1) * gc)
        acc = None
        for dlt in range(-(w // 2), w - w // 2):
            piece = hb_ref[POOL_HALO + dlt:POOL_HALO + dlt + tm, cols]
            acc = piece if acc is None else acc + piece
        cnt = jnp.minimum(tpos + (w - w // 2), t) - jnp.maximum(tpos - w // 2, 0)
        pooled = acc / cnt.astype(F32) - hb_ref[POOL_HALO:POOL_HALO + tm, cols]
        y = _dot(pooled.astype(BF16), w_ref[grp])
        o_ref[:, cols] = x_ref[:, cols] + gt_ref[row:row + 1, cols] * (y * ps_ref[:, cols])

    h = _norm_mod(o_ref[...], g2_ref[...], sh2_ref[row:row + 1, :], sc2_ref[row:row + 1, :])
    hn_ref[...] = h.astype(hn_ref.dtype)
    lane = lax.broadcasted_iota(jnp.int32, (h.shape[0], LANES), 1)
    lane_f = lane.astype(F32)
    logits = jnp.where(lane < N_EXPERTS, _dot(h, wr_ref[...], HIGHEST), -jnp.inf)
    m1 = jnp.max(logits, axis=-1, keepdims=True)
    i1 = jnp.min(jnp.where(logits == m1, lane_f, float(LANES)), axis=-1, keepdims=True)
    rest = jnp.where(lane_f == i1, -jnp.inf, logits)
    m2 = jnp.max(rest, axis=-1, keepdims=True)
    i2 = jnp.min(jnp.where(rest == m2, lane_f, float(LANES)), axis=-1, keepdims=True)
    e2 = jnp.exp(m2 - m1)
    den = 1.0 + e2
    sel_ref[...] = jnp.where(lane == 0, i1, jnp.where(lane == 1, i2, 0.0)).astype(jnp.int32)
    gate_ref[...] = jnp.where(lane == 0, 1.0 / den, jnp.where(lane == 1, e2 / den, 0.0))


def _pool_router(x, ada_l, row, g1, g2, w_pool, pool_scale, w_router_pad):
    t, d = x.shape
    tm = min(256, t)
    nh = t // POOL_HALO
    gc = d // POOL_GROUPS
    rows = lambda i: (i, 0)
    const = lambda i: (0, 0)
    return pl.pallas_call(
        functools.partial(_pool_router_kernel, row=row, t=t),
        grid=(t // tm,),
        in_specs=[pl.BlockSpec((tm, d), rows),
                  pl.BlockSpec((POOL_HALO, d), lambda i: (jnp.maximum(i * (tm // POOL_HALO) - 1, 0), 0)),
                  pl.BlockSpec((POOL_HALO, d), lambda i: (jnp.minimum((i + 1) * (tm // POOL_HALO), nh - 1), 0)),
                  pl.BlockSpec((1, d), const),
                  _ada_spec(0), _ada_spec(1), _ada_spec(2),
                  pl.BlockSpec((POOL_GROUPS, gc, gc), lambda i: (0, 0, 0)),
                  pl.BlockSpec((1, d), const),
                  pl.BlockSpec((1, d), const),
                  _ada_spec(3), _ada_spec(4),
                  pl.BlockSpec((d, LANES), const)],
        out_specs=[pl.BlockSpec((tm, d), rows), pl.BlockSpec((tm, d), rows),
                   pl.BlockSpec((tm, LANES), rows), pl.BlockSpec((tm, LANES), rows)],
        out_shape=[jax.ShapeDtypeStruct((t, d), F32),
                   jax.ShapeDtypeStruct((t, d), BF16),
                   jax.ShapeDtypeStruct((t, LANES), jnp.int32),
                   jax.ShapeDtypeStruct((t, LANES), F32)],
        scratch_shapes=[pltpu.VMEM((tm + 2 * POOL_HALO, d), F32)],
        compiler_params=_params(("parallel",)),
    )(x, x, x, g1, ada_l, ada_l, ada_l, w_pool, pool_scale, g2, ada_l, ada_l, w_router_pad)


MOE_TILE = 1280
MOE_SUB = 256
MOE_SUBS = MOE_TILE // MOE_SUB
GATHER_SUB = 256
GATHER_SUBS = MOE_TILE // GATHER_SUB
SRC_BLK = 512
Y_BLK = 256
CMB_TILE = 256
MOE_TF = 512
GATHER_BUFS = 2


def _moe_ffn_kernel(te_ref, tn_ref, tlo_ref, tnb_ref, trk_ref, cb_ref, posb_ref, hn_hbm, wg_ref, wu_ref, wd_ref,
                    y_ref, xb_ref, acc_ref, hbuf_ref, sem):
    i = pl.program_id(0)
    f = pl.program_id(1)
    nsub = tn_ref[i]

    @pl.when(f == 0)
    def _():
        xb_ref[...] = jnp.zeros_like(xb_ref)
        acc_ref[...] = jnp.zeros_like(acc_ref)
        row_id = lax.broadcasted_iota(jnp.int32, (GATHER_SUB, SRC_BLK), 0) + i * MOE_TILE
        rows_used = nsub * MOE_SUB
        lo = tlo_ref[i]
        n = tnb_ref[i]
        rank0 = trk_ref[i]
        cb_base = te_ref[i] * (posb_ref.shape[0] + 1)

        def blk_copy(b, slot):
            return pltpu.make_async_copy(hn_hbm.at[pl.ds((lo + b) * SRC_BLK, SRC_BLK), :], hbuf_ref.at[slot],
                                         sem.at[slot])

        for j in range(GATHER_BUFS - 1):
            @pl.when(j < n)
            def _():
                blk_copy(j, j).start()

        def body(b, carry):
            slot = b % GATHER_BUFS
            blk_copy(b, slot).wait()
            ahead = b + GATHER_BUFS - 1

            @pl.when(ahead < n)
            def _():
                blk_copy(ahead, ahead % GATHER_BUFS).start()

            blk = lo + b
            pos = posb_ref[blk]
            pos1, pos2 = pos[0:1, :], pos[1:2, :]
            before = cb_ref[cb_base + blk]
            through = cb_ref[cb_base + blk + 1]
            for q in range(GATHER_SUBS):
                @pl.when((q * GATHER_SUB < rows_used) & (before < rank0 + (q + 1) * GATHER_SUB)
                         & (through > rank0 + q * GATHER_SUB))
                def _():
                    rows = slice(q * GATHER_SUB, (q + 1) * GATHER_SUB)
                    rid = row_id + q * GATHER_SUB
                    onehot = jnp.where((pos1 == rid) | (pos2 == rid), 1.0, 0.0).astype(BF16)
                    xb_ref[rows, :] += _dot(onehot, hbuf_ref[slot]).astype(BF16)
            return carry

        lax.fori_loop(0, n, body, 0)

    for k in range(1, MOE_SUBS + 1):
        @pl.when(nsub == k)
        def _():
            rows = slice(0, k * MOE_SUB)
            h = xb_ref[rows, :]
            a = (_silu(_dot(h, wg_ref[0, 0].astype(BF16))) * _dot(h, wu_ref[0, 0].astype(BF16))).astype(BF16)
            acc_ref[rows, :] += _dot(a, wd_ref[0, 0].astype(BF16))

    @pl.when(f == pl.num_programs(1) - 1)
    def _():
        y_ref[...] = acc_ref[...].astype(y_ref.dtype)


def _moe_ffn(hn, posb, tile_expert, tile_subs, tile_blk_lo, tile_blk_n, tile_rank, blk_counts, wg, wu, wd, lyr):
    d = hn.shape[1]
    tm = MOE_TILE
    ff = wg.shape[3]
    tf = MOE_TF
    nf = ff // tf
    nt = tile_expert.shape[0]
    fidx = lambda i, f, tn: jnp.where(tn[i] > 0, f, nf - 1)
    return pl.pallas_call(
        _moe_ffn_kernel,
        grid_spec=pltpu.PrefetchScalarGridSpec(
            num_scalar_prefetch=6,
            grid=(nt, nf),
            in_specs=[pl.BlockSpec(posb.shape, lambda i, f, te, tn, *_: (0, 0, 0)),
                      pl.BlockSpec(memory_space=pl.ANY),
                      pl.BlockSpec((1, 1, d, tf), lambda i, f, te, tn, *_: (lyr, te[i], 0, fidx(i, f, tn))),
                      pl.BlockSpec((1, 1, d, tf), lambda i, f, te, tn, *_: (lyr, te[i], 0, fidx(i, f, tn))),
                      pl.BlockSpec((1, 1, tf, d), lambda i, f, te, tn, *_: (lyr, te[i], fidx(i, f, tn), 0))],
            out_specs=pl.BlockSpec((tm, d), lambda i, f, te, tn, *_: (i, 0)),
            scratch_shapes=[pltpu.VMEM((tm, d), BF16), pltpu.VMEM((tm, d), F32),
                            pltpu.VMEM((GATHER_BUFS, SRC_BLK, d), BF16),
                            pltpu.SemaphoreType.DMA((GATHER_BUFS,))]),
        out_shape=jax.ShapeDtypeStruct((nt * tm, d), BF16),
        compiler_params=_params(("arbitrary", "arbitrary"), MOE_VMEM_LIMIT),
    )(tile_expert, tile_subs, tile_blk_lo, tile_blk_n, tile_rank, blk_counts, posb, hn, wg, wu, wd)


CMB_SLOTS = [(e, j) for e in range(N_EXPERTS) for j in range(2)]


def _combine_kernel(fb_ref, nb_ref, x_ref, pos_ref, gate_ref, gt_ref, y_hbm, o_ref, acc_ref, ybuf_ref, sem,
                    *, row, tile_off):
    base = (tile_off + pl.program_id(0)) * N_EXPERTS
    tm = x_ref.shape[0]
    pos1, pos2 = pos_ref[:, 0:1], pos_ref[:, 1:2]
    g1, g2 = gate_ref[:, 0:1], gate_ref[:, 1:2]
    col = lax.broadcasted_iota(jnp.int32, (tm, Y_BLK), 1)

    def blk(s):
        e, j = CMB_SLOTS[s]
        return fb_ref[base + e] + j

    def used(s):
        e, j = CMB_SLOTS[s]
        return nb_ref[base + e] > j

    def blk_copy(s):
        return pltpu.make_async_copy(y_hbm.at[pl.ds(blk(s) * Y_BLK, Y_BLK), :], ybuf_ref.at[s], sem.at[s])

    for s in range(len(CMB_SLOTS)):
        @pl.when(used(s))
        def _():
            blk_copy(s).start()

    acc_ref[...] = jnp.zeros_like(acc_ref)
    for s in range(len(CMB_SLOTS)):
        @pl.when(used(s))
        def _():
            blk_copy(s).wait()
            off = blk(s) * Y_BLK
            w = jnp.where(pos1 - off == col, g1, 0.0) + jnp.where(pos2 - off == col, g2, 0.0)
            acc_ref[...] += _dot(w.astype(BF16), ybuf_ref[s])

    o_ref[...] = x_ref[...] + gt_ref[row:row + 1, :] * acc_ref[...]


def _combine(x, y, pos, gates, first_blk, num_blk, tok_off, ada_l, row):
    t, d = x.shape
    tm = CMB_TILE
    tile_off = tok_off // tm
    return pl.pallas_call(
        functools.partial(_combine_kernel, row=row, tile_off=tile_off),
        grid_spec=pltpu.PrefetchScalarGridSpec(
            num_scalar_prefetch=2,
            grid=(t // tm,),
            in_specs=[pl.BlockSpec((tm, d), lambda i, fb, nb: (i, 0)),
                      pl.BlockSpec((tm, 2), lambda i, fb, nb: (tile_off + i, 0)),
                      pl.BlockSpec((tm, LANES), lambda i, fb, nb: (i, 0)),
                      _ada_spec(5),
                      pl.BlockSpec(memory_space=pl.ANY)],
            out_specs=pl.BlockSpec((tm, d), lambda i, fb, nb: (i, 0)),
            scratch_shapes=[pltpu.VMEM((tm, d), F32), pltpu.VMEM((len(CMB_SLOTS), Y_BLK, d), BF16),
                            pltpu.SemaphoreType.DMA((len(CMB_SLOTS),))]),
        out_shape=jax.ShapeDtypeStruct((t, d), F32),
        compiler_params=_params(("arbitrary",)),
    )(first_blk, num_blk, x, pos, gates, ada_l, y)


def _route(sel):
    tm = MOE_TILE
    n_tok = sel.shape[0]
    n_asg = 2 * n_tok
    nt = -(-n_asg // tm) + N_EXPERTS
    e_flat = sel.reshape(-1)
    onehot = (e_flat[:, None] == jnp.arange(N_EXPERTS, dtype=jnp.int32)[None, :]).astype(jnp.int32)
    csum = jnp.cumsum(onehot, axis=0)
    count = csum[-1]
    rank = jnp.sum((csum - onehot) * onehot, axis=1)
    tiles_e = (count + tm - 1) // tm
    tile_end = jnp.cumsum(tiles_e)
    base = (tile_end - tiles_e) * tm
    pos = jnp.sum(onehot * base[None, :], axis=1) + rank
    pos = jnp.where(e_flat >= 0, pos, -1)
    tile_id = jnp.arange(nt, dtype=jnp.int32)
    used = tile_end[-1]
    owner = lambda i: jnp.minimum(jnp.sum((i[:, None] >= tile_end[None, :]).astype(jnp.int32), axis=1), N_EXPERTS - 1)
    tile_expert = owner(jnp.minimum(tile_id, used - 1))
    rows_in_tile = jnp.clip((base + count)[tile_expert] - tile_id * tm, 0, tm)
    tile_subs = jnp.where(tile_id < used, (rows_in_tile + MOE_SUB - 1) // MOE_SUB, 0)
    n_blk = n_tok // SRC_BLK
    blk_counts = jnp.concatenate([jnp.zeros((1, N_EXPERTS), jnp.int32),
                                  csum.reshape(n_blk, 2 * SRC_BLK, N_EXPERTS)[:, -1, :]], axis=0).T
    tile_rank = tile_id * tm - base[tile_expert]
    counts_t = blk_counts[tile_expert]
    tile_blk_lo = jnp.sum((counts_t[:, 1:] <= tile_rank[:, None]).astype(jnp.int32), axis=1)
    tile_blk_hi = jnp.sum((counts_t[:, :-1] < (tile_rank + rows_in_tile)[:, None]).astype(jnp.int32), axis=1)
    tile_blk_n = jnp.where(tile_subs > 0, jnp.maximum(tile_blk_hi - tile_blk_lo, 0), 0)
    tile_blk_lo = jnp.minimum(tile_blk_lo, n_blk - 1)
    posb = jnp.pad(pos.reshape(n_blk, SRC_BLK, 2).transpose(0, 2, 1), ((0, 0), (0, 6), (0, 0)), constant_values=-1)
    ntt = n_tok // CMB_TILE
    through = csum.reshape(ntt, 2 * CMB_TILE, N_EXPERTS)[:, -1, :]
    before = jnp.concatenate([jnp.zeros((1, N_EXPERTS), jnp.int32), through[:-1]], axis=0)
    lo = base[None, :] + before
    hi = base[None, :] + through - 1
    first_blk = lo // Y_BLK
    num_blk = jnp.where(through > before, hi // Y_BLK - first_blk + 1, 0)
    return (posb, pos.reshape(n_tok, 2), tile_expert, tile_subs, tile_blk_lo, tile_blk_n, tile_rank,
            blk_counts.reshape(-1), first_blk.reshape(-1), num_blk.reshape(-1))


def _moe(streams, ada_l, wg, wu, wd, lyr):
    hs = [s[2] for s in streams]
    sels = [s[3][:, :2] for s in streams]
    n_tok = sum(h.shape[0] for h in hs)
    pad = -n_tok % SRC_BLK
    if pad:
        hs.append(jnp.zeros((pad, D_MODEL), BF16))
        sels.append(jnp.full((pad, 2), -1, jnp.int32))
    h_all = hs[0] if len(hs) == 1 else jnp.concatenate(hs, axis=0)
    sel_all = sels[0] if len(sels) == 1 else jnp.concatenate(sels, axis=0)
    (posb, pos, tile_expert, tile_subs, tile_blk_lo, tile_blk_n, tile_rank, blk_counts,
     first_blk, num_blk) = _route(sel_all)
    y = _moe_ffn(h_all, posb, tile_expert, tile_subs, tile_blk_lo, tile_blk_n, tile_rank, blk_counts,
                 wg, wu, wd, lyr)
    outs, off = [], 0
    for x, row, _, _, gt in streams:
        outs.append(_combine(x, y, pos, gt, first_blk, num_blk, off, ada_l, row))
        off += x.shape[0]
    return outs


def _even_layer(x, ctx, ada_l, g1, g2, w_in, g_qn, g_kn, sink, w_gate_up, b_gate_up, g_gla, w_out,
                w_ffn_gate, w_ffn_up, w_ffn_down, lyr, rope_tabs, ctx_out):
    d = D_MODEL
    w_main = jnp.concatenate([w_in[:, 2080:3104], w_in[:, 1024:2048], w_in[:, 3616:4640], w_in[:, 512:1024],
                              w_in[:, 3104:3616], w_in[:, 0:256], w_in[:, 256:512]], axis=1).astype(BF16)
    w_gate = jnp.pad(w_in[:, 2048:2080], ((0, 0), (0, LANES - 2 * GATE_RANK))).astype(BF16)
    nk = B_HEADS * B_DK
    w_up = jnp.zeros((LANES, 2 * nk), F32)
    w_up = w_up.at[0:GATE_RANK, 0:nk].set(w_gate_up[0]).at[GATE_RANK:2 * GATE_RANK, nk:].set(w_gate_up[1])
    b_up = b_gate_up.reshape(1, 2 * nk)
    w_out_b = w_out.astype(BF16)
    gq, gk, gg = g_qn.reshape(1, -1), g_kn.reshape(1, -1), g_gla.reshape(1, -1)
    sink_tab = jnp.broadcast_to(sink[:, None], (A_HEADS, LANES))
    cos, sin = rope_tabs

    pc, la_c = _inproj(ctx, ada_l, 1, g1, w_main, w_gate, w_up, b_up)
    qc, kc = _qkprep(pc, cos, sin, gq, gk, rope=False)
    s0 = jnp.zeros((B_HEADS, B_DV, B_DK), F32)
    oc_f, s_fwd = _gla_scan(pc, la_c, s0, rev=False)
    oc_b, s_bwd = _gla_scan(pc, la_c, s0, rev=True)

    px, la_x = _inproj(x, ada_l, 0, g1, w_main, w_gate, w_up, b_up)
    qx, kx = _qkprep(px, cos, sin, gq, gk, rope=True)
    oa = _attention(qx, kx, px, kc, pc, sink_tab, local=True)
    ox_f, _ = _gla_scan(px, la_x, s_fwd, rev=False)
    ox_b, _ = _gla_scan(px, la_x, s_bwd, rev=True)
    x = _outproj(oa, ox_f, ox_b, px, gg, w_out_b, x, ada_l, 0)
    x = _ffn(x, ada_l, 0, g2, w_ffn_gate, w_ffn_up, w_ffn_down, lyr)
    if ctx_out:
        oa_c = _attention(qc, None, None, kc, pc, sink_tab, local=False)
        ctx = _outproj(oa_c, oc_f, oc_b, pc, gg, w_out_b, ctx, ada_l, 1)
        ctx = _ffn(ctx, ada_l, 1, g2, w_ffn_gate, w_ffn_up, w_ffn_down, lyr)
    return x, ctx


def _odd_layer(x, ctx, ada_l, g1, g2, w_pool, pool_scale, w_router, w_exp_gate, w_exp_up, w_exp_down, lyr, ctx_out):
    w_pool_b = w_pool.astype(BF16)
    ps = pool_scale.reshape(1, -1)
    wr = jnp.pad(w_router, ((0, 0), (0, LANES - N_EXPERTS)))
    streams = [(0,) + tuple(_pool_router(x, ada_l, 0, g1, g2, w_pool_b, ps, wr))]
    if ctx_out:
        streams.append((1,) + tuple(_pool_router(ctx, ada_l, 1, g1, g2, w_pool_b, ps, wr)))
    outs = _moe([(s[1], s[0], s[2], s[3], s[4]) for s in streams], ada_l, w_exp_gate, w_exp_up, w_exp_down, lyr)
    return outs[0], (outs[1] if ctx_out else ctx)


def kernel(x, c, ctx, c_ctx, w_ada, b_ada, norm_g, w_in, g_qn, g_kn, attn_sink, w_gate_up, b_gate_up, g_gla, w_out,
           w_ffn_gate, w_ffn_up, w_ffn_down, w_pool, pool_scale, w_router, w_exp_gate, w_exp_up, w_exp_down):
    depth = w_ada.shape[0]
    xs = x[0]
    cs = ctx[0]
    t = xs.shape[0]
    cond = jnp.zeros((8, D_MODEL), F32).at[0].set(c[0]).at[1].set(c_ctx)
    ada = _ada_all(cond, w_ada, b_ada)
    rope_tabs = _rope_tables(t)
    for l in range(depth):
        ctx_later = any(j % 2 == 0 for j in range(l + 1, depth))
        g1 = norm_g[l, 0].reshape(1, -1)
        g2 = norm_g[l, 1].reshape(1, -1)
        if l % 2 == 0:
            e = l // 2
            xs, cs = _even_layer(xs, cs, ada[l], g1, g2, w_in[e], g_qn[e], g_kn[e], attn_sink[e], w_gate_up[e],
                                 b_gate_up[e], g_gla[e], w_out[e], w_ffn_gate, w_ffn_up, w_ffn_down, e,
                                 rope_tabs, ctx_later)
        else:
            o = l // 2
            xs, cs = _odd_layer(xs, cs, ada[l], g1, g2, w_pool[o], pool_scale[o], w_router[o],
                                w_exp_gate, w_exp_up, w_exp_down, o, ctx_later)
    return xs[None]
```

```python
import functools

import jax
import jax.numpy as jnp
import numpy as np
from jax import lax
from jax.experimental import pallas as pl
from jax.experimental.pallas import tpu as pltpu

F32 = jnp.float32
BF16 = jnp.bfloat16

D_MODEL = 2048
GRID_W = 64
A_HEADS = 8
A_KV_HEADS = 2
A_GROUP = 4
A_HEAD_DIM = 128
A_WIDTH = A_HEADS * A_HEAD_DIM
ATTN_BLOCK = 128
ROPE_BASE = 10000.0
B_HEADS = 4
B_DV = 256
B_DK = 128
B_WIDTH = B_HEADS * B_DV
GATE_RANK = 16
GATE_TAU = 16.0
GLA_CHUNK = 64
GLA_SUB = 256
POOL_GROUPS = 4
POOL_WINDOWS = (2, 4, 8, 16)
POOL_HALO = 8
N_EXPERTS = 8
NORM_EPS = 1e-6
NEG_INF = -1e30
LANES = 128

P_COLS = 4608
VMEM_LIMIT = 56 * 1024 * 1024
MOE_VMEM_LIMIT = 62 * 1024 * 1024


def _params(sem, vmem=VMEM_LIMIT):
    return pltpu.CompilerParams(dimension_semantics=sem, vmem_limit_bytes=vmem)


def _dot(a, b, precision=None):
    return jnp.dot(a, b, preferred_element_type=F32, precision=precision)


def _dot_nt(a, b):
    return lax.dot_general(a, b, (((1,), (1,)), ((), ())), preferred_element_type=F32)


def _dot_tn(a, b):
    return lax.dot_general(a, b, (((0,), (0,)), ((), ())), preferred_element_type=F32)


def _silu(x):
    return x * jax.nn.sigmoid(x)


def _norm_mod(x, g, shift, scale):
    ms = jnp.mean(x * x, axis=-1, keepdims=True)
    return (x * lax.rsqrt(ms + NORM_EPS) * g) * (1.0 + scale) + shift


def _ada_kernel(cond_ref, w_ref, b_ref, o_ref):
    s = _silu(cond_ref[...])
    s_hi = s.astype(BF16)
    s_lo = (s - s_hi.astype(F32)).astype(BF16)
    w = w_ref[0].astype(BF16)
    o_ref[0] = _dot(s_hi, w) + _dot(s_lo, w) + b_ref[0]


def _ada_all(cond, w_ada, b_ada):
    depth, d, n = w_ada.shape
    tn = 1024
    return pl.pallas_call(
        _ada_kernel,
        grid=(depth, n // tn),
        in_specs=[pl.BlockSpec((8, d), lambda l, j: (0, 0)),
                  pl.BlockSpec((1, d, tn), lambda l, j: (l, 0, j)),
                  pl.BlockSpec((1, 1, tn), lambda l, j: (l, 0, j))],
        out_specs=pl.BlockSpec((1, 8, tn), lambda l, j: (l, 0, j)),
        out_shape=jax.ShapeDtypeStruct((depth, 8, n), F32),
        compiler_params=_params(("parallel", "parallel")),
    )(cond, w_ada, b_ada.reshape(depth, 1, n))


def _ada_spec(k):
    return pl.BlockSpec((8, D_MODEL), lambda *_: (0, k))


def _inproj_kernel(x_ref, g_ref, sh_ref, sc_ref, w_ref, wg_ref, wuh_ref, wul_ref, bup_ref, p_ref, la_ref, hn_ref,
                   *, row):
    @pl.when(pl.program_id(1) == 0)
    def _():
        h = _norm_mod(x_ref[...], g_ref[...], sh_ref[row:row + 1, :], sc_ref[row:row + 1, :]).astype(BF16)
        hn_ref[...] = h
        lr = _dot(h, wg_ref[...])
        lr_hi = lr.astype(BF16)
        lr_lo = (lr - lr_hi.astype(F32)).astype(BF16)
        z = _dot(lr_hi, wuh_ref[...]) + _dot(lr_lo, wuh_ref[...]) + _dot(lr_hi, wul_ref[...]) + bup_ref[...]
        la_ref[...] = (jnp.minimum(z, 0.0) - jnp.log1p(jnp.exp(-jnp.abs(z)))) * (1.0 / GATE_TAU)

    p_ref[...] = _dot(hn_ref[...], w_ref[...]).astype(p_ref.dtype)


def _inproj(x, ada_l, row, g, w_main, w_gate, w_up, b_up):
    t, d = x.shape
    tm = min(512, t)
    tn = 1536
    nla = 2 * B_HEADS * B_DK
    return pl.pallas_call(
        functools.partial(_inproj_kernel, row=row),
        grid=(t // tm, P_COLS // tn),
        in_specs=[pl.BlockSpec((tm, d), lambda i, j: (i, 0)),
                  pl.BlockSpec((1, d), lambda i, j: (0, 0)),
                  _ada_spec(0), _ada_spec(1),
                  pl.BlockSpec((d, tn), lambda i, j: (0, j)),
                  pl.BlockSpec((d, LANES), lambda i, j: (0, 0)),
                  pl.BlockSpec((LANES, nla), lambda i, j: (0, 0)),
                  pl.BlockSpec((LANES, nla), lambda i, j: (0, 0)),
                  pl.BlockSpec((1, nla), lambda i, j: (0, 0))],
        out_specs=[pl.BlockSpec((tm, tn), lambda i, j: (i, j)),
                   pl.BlockSpec((tm, nla), lambda i, j: (i, 0))],
        out_shape=[jax.ShapeDtypeStruct((t, P_COLS), BF16),
                   jax.ShapeDtypeStruct((t, nla), F32)],
        scratch_shapes=[pltpu.VMEM((tm, d), BF16)],
        compiler_params=_params(("parallel", "arbitrary")),
    )(x, g, ada_l, ada_l, w_main, w_gate, w_up[0], w_up[1], b_up)


def _qkprep_kernel(q_ref, k_ref, cos_ref, sin_ref, gq_ref, gk_ref, qo_ref, ko_ref, *, rope):
    tm = q_ref.shape[0]
    lane = lax.broadcasted_iota(jnp.int32, (tm, A_HEAD_DIM), 1)
    first_half = (lane % 64) < 32

    def prep(xh, g, scale):
        ms = jnp.mean(xh * xh, axis=-1, keepdims=True)
        y = xh * lax.rsqrt(ms + NORM_EPS) * g
        if rope:
            partner = jnp.where(first_half, pltpu.roll(y, 96, 1), pltpu.roll(y, 32, 1))
            y = y * cos_ref[...] + partner * sin_ref[...]
        return (y * scale).astype(BF16)

    for h in range(A_HEADS):
        cols = slice(h * A_HEAD_DIM, (h + 1) * A_HEAD_DIM)
        qo_ref[:, cols] = prep(q_ref[:, cols].astype(F32), gq_ref[...], A_HEAD_DIM ** -0.5)
    for h in range(A_KV_HEADS):
        cols = slice(h * A_HEAD_DIM, (h + 1) * A_HEAD_DIM)
        ko_ref[:, cols] = prep(k_ref[:, cols].astype(F32), gk_ref[...], 1.0)


def _qkprep(p, cos, sin, g_qn, g_kn, rope):
    t = p.shape[0]
    tm = min(512, t)
    kw = A_KV_HEADS * A_HEAD_DIM
    return pl.pallas_call(
        functools.partial(_qkprep_kernel, rope=rope),
        grid=(t // tm,),
        in_specs=[pl.BlockSpec((tm, A_WIDTH), lambda i: (i, 0)),
                  pl.BlockSpec((tm, kw), lambda i: (i, 4096 // kw)),
                  pl.BlockSpec((tm, A_HEAD_DIM), lambda i: (i, 0)),
                  pl.BlockSpec((tm, A_HEAD_DIM), lambda i: (i, 0)),
                  pl.BlockSpec((1, A_HEAD_DIM), lambda i: (0, 0)),
                  pl.BlockSpec((1, A_HEAD_DIM), lambda i: (0, 0))],
        out_specs=[pl.BlockSpec((tm, A_WIDTH), lambda i: (i, 0)),
                   pl.BlockSpec((tm, kw), lambda i: (i, 0))],
        out_shape=[jax.ShapeDtypeStruct((t, A_WIDTH), BF16),
                   jax.ShapeDtypeStruct((t, kw), BF16)],
        compiler_params=_params(("parallel",)),
    )(p, p, cos, sin, g_qn, g_kn)


def _rope_tables(t):
    half = A_HEAD_DIM // 4
    freqs = ROPE_BASE ** (-np.arange(half, dtype=np.float64) / half)
    tok = np.arange(t)
    row = (tok // GRID_W).astype(np.float64)[:, None] * freqs
    col = (tok % GRID_W).astype(np.float64)[:, None] * freqs
    cos = np.concatenate([np.cos(row), np.cos(row), np.cos(col), np.cos(col)], axis=-1)
    sin = np.concatenate([-np.sin(row), np.sin(row), -np.sin(col), np.sin(col)], axis=-1)
    return jnp.asarray(cos.astype(np.float32)), jnp.asarray(sin.astype(np.float32))


def _softmax_pv(scores, values, sink_col):
    def lane_blocks(xs):
        return [x[:, i * LANES:(i + 1) * LANES] for x in xs for i in range(x.shape[1] // LANES)]

    m = jnp.maximum(jnp.max(functools.reduce(jnp.maximum, lane_blocks(scores)), axis=-1, keepdims=True), sink_col)
    probs = [jnp.exp(s - m) for s in scores]
    denom = jnp.exp(sink_col - m) + jnp.sum(functools.reduce(jnp.add, lane_blocks(probs)), axis=-1, keepdims=True)
    acc = None
    for p, v in zip(probs, values):
        pv = _dot(p.astype(BF16), v)
        acc = pv if acc is None else acc + pv
    return acc / denom


def _attn_kernel(*refs, local):
    if local:
        q_ref, kp_ref, ko_ref, kn_ref, vp_ref, vo_ref, vn_ref, kc_ref, vc_ref, sink_ref, o_ref = refs
    else:
        q_ref, kc_ref, vc_ref, sink_ref, o_ref = refs
    n = pl.program_id(0)
    nb = pl.num_programs(0)
    rows = A_GROUP * ATTN_BLOCK
    if local:
        qi = lax.broadcasted_iota(jnp.int32, (rows, ATTN_BLOCK), 0) % ATTN_BLOCK
        kj = lax.broadcasted_iota(jnp.int32, (rows, ATTN_BLOCK), 1)
        mask_prev = (kj >= qi) & (n > 0)
        mask_next = (kj <= qi) & (n < nb - 1)
    for kv in range(A_KV_HEADS):
        kc = slice(kv * A_HEAD_DIM, (kv + 1) * A_HEAD_DIM)
        heads = [kv * A_GROUP + g for g in range(A_GROUP)]
        q4 = jnp.concatenate([q_ref[:, h * A_HEAD_DIM:(h + 1) * A_HEAD_DIM] for h in heads], axis=0)
        sink_col = jnp.concatenate(
            [jnp.broadcast_to(sink_ref[h:h + 1, 0:1], (ATTN_BLOCK, 1)) for h in heads], axis=0)
        scores, values = [], []
        if local:
            scores.append(jnp.where(mask_prev, _dot_nt(q4, kp_ref[:, kc]), NEG_INF))
            scores.append(_dot_nt(q4, ko_ref[:, kc]))
            scores.append(jnp.where(mask_next, _dot_nt(q4, kn_ref[:, kc]), NEG_INF))
            values += [vp_ref[:, kc], vo_ref[:, kc], vn_ref[:, kc]]
        scores.append(_dot_nt(q4, kc_ref[:, kc]))
        values.append(vc_ref[:, kc])
        o = _softmax_pv(scores, values, sink_col).astype(o_ref.dtype)
        for g, h in enumerate(heads):
            o_ref[:, h * A_HEAD_DIM:(h + 1) * A_HEAD_DIM] = o[g * ATTN_BLOCK:(g + 1) * ATTN_BLOCK, :]


def _attention(q, k, p, k_ctx, p_ctx, sink_tab, local):
    t = q.shape[0]
    nb = t // ATTN_BLOCK
    l = k_ctx.shape[0]
    kw = A_KV_HEADS * A_HEAD_DIM
    vcol = 4352 // kw
    blk = lambda f: pl.BlockSpec((ATTN_BLOCK, kw), f)
    blkv = lambda f: pl.BlockSpec((ATTN_BLOCK, kw), lambda i: (f(i)[0], vcol))
    prev = lambda i: (jnp.maximum(i - 1, 0), 0)
    own = lambda i: (i, 0)
    nxt = lambda i: (jnp.minimum(i + 1, nb - 1), 0)
    in_specs = [pl.BlockSpec((ATTN_BLOCK, A_WIDTH), own)]
    args = [q]
    if local:
        in_specs += [blk(prev), blk(own), blk(nxt), blkv(prev), blkv(own), blkv(nxt)]
        args += [k, k, k, p, p, p]
    in_specs += [pl.BlockSpec((l, kw), lambda i: (0, 0)), pl.BlockSpec((l, kw), lambda i: (0, vcol)),
                 pl.BlockSpec((A_HEADS, LANES), lambda i: (0, 0))]
    args += [k_ctx, p_ctx, sink_tab]
    return pl.pallas_call(
        functools.partial(_attn_kernel, local=local),
        grid=(nb,),
        in_specs=in_specs,
        out_specs=pl.BlockSpec((ATTN_BLOCK, A_WIDTH), own),
        out_shape=jax.ShapeDtypeStruct((t, A_WIDTH), BF16),
        compiler_params=_params(("parallel",)),
    )(*args)


def _gla_kernel(q_ref, k_ref, v_ref, la_ref, s0_ref, o_ref, sf_ref, st_ref, *, rev, nchunk):
    @pl.when(pl.program_id(0) == 0)
    def _():
        st_ref[...] = s0_ref[...]

    r = nchunk * GLA_CHUNK
    sub = min(r, GLA_SUB)
    ii = lax.broadcasted_iota(jnp.int32, (sub, sub), 0)
    jj = lax.broadcasted_iota(jnp.int32, (sub, sub), 1)
    same_chunk = (ii // GLA_CHUNK) == (jj // GLA_CHUNK)
    tri = same_chunk & ((jj >= ii) if rev else (jj <= ii))
    b_all = la_ref[...]
    pos = lax.broadcasted_iota(jnp.int32, b_all.shape, 0) % GLA_CHUNK
    step = 1
    while step < GLA_CHUNK:
        if rev:
            b_all = b_all + jnp.where(pos < GLA_CHUNK - step, pltpu.roll(b_all, r - step, 0), 0.0)
        else:
            b_all = b_all + jnp.where(pos >= step, pltpu.roll(b_all, step, 0), 0.0)
        step *= 2
    heads = []
    for h in range(B_HEADS):
        kc = slice(h * B_DK, (h + 1) * B_DK)
        b = b_all[:, kc]
        b3 = b.reshape(nchunk, GLA_CHUNK, B_DK)
        b_end = b3[:, 0:1, :] if rev else b3[:, GLA_CHUNK - 1:GLA_CHUNK, :]
        k = k_ref[:, kc].astype(F32)
        qe = (q_ref[:, kc].astype(F32) * (B_DK ** -0.5) * jnp.exp(b)).astype(BF16)
        ke = (k * jnp.exp(-b)).astype(BF16)
        kd = (k.reshape(nchunk, GLA_CHUNK, B_DK) * jnp.exp(b_end - b3)).astype(BF16)
        v = v_ref[:, h * B_DV:(h + 1) * B_DV].astype(BF16)
        o_intra = []
        for s0 in range(0, r, sub):
            rs = slice(s0, s0 + sub)
            a = jnp.where(tri, _dot_nt(qe[rs, :], ke[rs, :]), 0.0).astype(BF16)
            o_intra.append(_dot(a, v[rs, :]))
        heads.append((qe, kd, v, jnp.concatenate(o_intra, axis=0), jnp.exp(b_end)))
    st = [st_ref[h] for h in range(B_HEADS)]
    for c in (range(nchunk - 1, -1, -1) if rev else range(nchunk)):
        rows = slice(c * GLA_CHUNK, (c + 1) * GLA_CHUNK)
        for h, (qe, kd, v, o_intra, decay) in enumerate(heads):
            o = o_intra[rows, :] + _dot_nt(qe[rows, :], st[h].astype(BF16))
            o_ref[0, rows, h * B_DV:(h + 1) * B_DV] = o.astype(o_ref.dtype)
            st[h] = st[h] * decay[c] + _dot_tn(v[rows, :], kd[c])
    for h in range(B_HEADS):
        st_ref[h] = st[h]
        sf_ref[h] = st[h]


def _gla_scan(p, la, s0, rev):
    t = p.shape[0]
    r = min(512, t)
    nblk = t // r
    rb = (lambda c: nblk - 1 - c) if rev else (lambda c: c)
    nk = B_HEADS * B_DK
    o, sf = pl.pallas_call(
        functools.partial(_gla_kernel, rev=rev, nchunk=r // GLA_CHUNK),
        grid=(nblk,),
        in_specs=[pl.BlockSpec((r, nk), lambda c: (rb(c), 3584 // nk)),
                  pl.BlockSpec((r, nk), lambda c: (rb(c), 3072 // nk)),
                  pl.BlockSpec((r, B_WIDTH), lambda c: (rb(c), 1024 // B_WIDTH)),
                  pl.BlockSpec((r, nk), lambda c: (rb(c), 1 if rev else 0)),
                  pl.BlockSpec((B_HEADS, B_DV, B_DK), lambda c: (0, 0, 0))],
        out_specs=[pl.BlockSpec((1, r, B_WIDTH), lambda c: (0, rb(c), 0)),
                   pl.BlockSpec((B_HEADS, B_DV, B_DK), lambda c: (0, 0, 0))],
        out_shape=[jax.ShapeDtypeStruct((1, t, B_WIDTH), BF16),
                   jax.ShapeDtypeStruct((B_HEADS, B_DV, B_DK), F32)],
        scratch_shapes=[pltpu.VMEM((B_HEADS, B_DV, B_DK), F32)],
        compiler_params=_params(("arbitrary",)),
    )(p, p, p, la, s0)
    return o[0], sf


def _outproj_kernel(oa_ref, of_ref, ob_ref, og_ref, gg_ref, w_ref, x_ref, gt_ref, o_ref, mix_ref, *, row):
    @pl.when(pl.program_id(1) == 0)
    def _():
        mix_ref[:, 0:A_WIDTH] = oa_ref[...]
        for h in range(B_HEADS):
            cols = slice(h * B_DV, (h + 1) * B_DV)
            o = of_ref[:, cols].astype(F32) + ob_ref[:, cols].astype(F32)
            ms = jnp.mean(o * o, axis=-1, keepdims=True)
            y = o * lax.rsqrt(ms + NORM_EPS) * gg_ref[...]
            mix_ref[:, A_WIDTH + h * B_DV:A_WIDTH + (h + 1) * B_DV] = (y * _silu(og_ref[:, cols].astype(F32))).astype(BF16)

    o_ref[...] = x_ref[...] + gt_ref[row:row + 1, :] * _dot(mix_ref[...], w_ref[...])


def _outproj(oa, o_f, o_b, p, g_gla, w_out, x, ada_l, row):
    t, d = x.shape
    tm = min(512, t)
    tn = 1024
    return pl.pallas_call(
        functools.partial(_outproj_kernel, row=row),
        grid=(t // tm, d // tn),
        in_specs=[pl.BlockSpec((tm, A_WIDTH), lambda i, j: (i, 0)),
                  pl.BlockSpec((tm, B_WIDTH), lambda i, j: (i, 0)),
                  pl.BlockSpec((tm, B_WIDTH), lambda i, j: (i, 0)),
                  pl.BlockSpec((tm, B_WIDTH), lambda i, j: (i, 2048 // B_WIDTH)),
                  pl.BlockSpec((1, B_DV), lambda i, j: (0, 0)),
                  pl.BlockSpec((A_WIDTH + B_WIDTH, tn), lambda i, j: (0, j)),
                  pl.BlockSpec((tm, tn), lambda i, j: (i, j)),
                  pl.BlockSpec((8, tn), lambda i, j: (0, 2 * (d // tn) + j))],
        out_specs=pl.BlockSpec((tm, tn), lambda i, j: (i, j)),
        out_shape=jax.ShapeDtypeStruct((t, d), F32),
        scratch_shapes=[pltpu.VMEM((tm, A_WIDTH + B_WIDTH), BF16)],
        compiler_params=_params(("parallel", "arbitrary")),
    )(oa, o_f, o_b, p, g_gla, w_out, x, ada_l)


def _ffn_kernel(x_ref, g_ref, sh_ref, sc_ref, gt_ref, wg_ref, wu_ref, wd_ref, o_ref, hn_ref, *, row):
    f = pl.program_id(1)

    @pl.when(f == 0)
    def _():
        hn_ref[...] = _norm_mod(x_ref[...], g_ref[...], sh_ref[row:row + 1, :], sc_ref[row:row + 1, :]).astype(BF16)
        o_ref[...] = jnp.zeros_like(o_ref)

    h = hn_ref[...]
    a = (_silu(_dot(h, wg_ref[0].astype(BF16))) * _dot(h, wu_ref[0].astype(BF16))).astype(BF16)
    o_ref[...] += _dot(a, wd_ref[0].astype(BF16))

    @pl.when(f == pl.num_programs(1) - 1)
    def _():
        o_ref[...] = x_ref[...] + gt_ref[row:row + 1, :] * o_ref[...]


def _ffn(x, ada_l, row, g, wg, wu, wd, lyr):
    t, d = x.shape
    ff = wg.shape[2]
    tm = min(1024, t)
    tf = 256
    return pl.pallas_call(
        functools.partial(_ffn_kernel, row=row),
        grid=(t // tm, ff // tf),
        in_specs=[pl.BlockSpec((tm, d), lambda i, f: (i, 0)),
                  pl.BlockSpec((1, d), lambda i, f: (0, 0)),
                  _ada_spec(3), _ada_spec(4), _ada_spec(5),
                  pl.BlockSpec((1, d, tf), lambda i, f: (lyr, 0, f)),
                  pl.BlockSpec((1, d, tf), lambda i, f: (lyr, 0, f)),
                  pl.BlockSpec((1, tf, d), lambda i, f: (lyr, f, 0))],
        out_specs=pl.BlockSpec((tm, d), lambda i, f: (i, 0)),
        out_shape=jax.ShapeDtypeStruct((t, d), F32),
        scratch_shapes=[pltpu.VMEM((tm, d), BF16)],
        compiler_params=_params(("parallel", "arbitrary"), MOE_VMEM_LIMIT),
    )(x, g, ada_l, ada_l, ada_l, wg, wu, wd)


def _pool_router_kernel(x_ref, xp_ref, xn_ref, g_ref, sh_ref, sc_ref, gt_ref, w_ref, ps_ref,
                        g2_ref, sh2_ref, sc2_ref, wrh_ref, wrl_ref, o_ref, hn_ref, sel_ref, gate_ref, hb_ref,
                        *, row, t):
    i = pl.program_id(0)
    tm = x_ref.shape[0]
    g = g_ref[...]
    sh = sh_ref[row:row + 1, :]
    sc = sc_ref[row:row + 1, :]
    hb_ref[0:POOL_HALO, :] = jnp.where(i > 0, _norm_mod(xp_ref[...], g, sh, sc), 0.0)
    hb_ref[POOL_HALO:POOL_HALO + tm, :] = _norm_mod(x_ref[...], g, sh, sc)
    hb_ref[POOL_HALO + tm:, :] = jnp.where(i < pl.num_programs(0) - 1, _norm_mod(xn_ref[...], g, sh, sc), 0.0)
    tpos = i * tm + lax.broadcasted_iota(jnp.int32, (tm, 1), 0)
    gc = D_MODEL // POOL_GROUPS
    for grp, w in enumerate(POOL_WINDOWS):
        cols = slice(grp * gc, (grp + 1) * gc)
        acc = None
        for dlt in range(-(w // 2), w - w // 2):
            piece = hb_ref[POOL_HALO + dlt:POOL_HALO + dlt + tm, cols]
            acc = piece if acc is None else acc + piece
        cnt = jnp.minimum(tpos + (w - w // 2), t) - jnp.maximum(tpos - w // 2, 0)
        pooled = acc / cnt.astype(F32) - hb_ref[POOL_HALO:POOL_HALO + tm, cols]
        y = _dot(pooled.astype(BF16), w_ref[grp])
        o_ref[:, cols] = x_ref[:, cols] + gt_ref[row:row + 1, cols] * (y * ps_ref[:, cols])

    h = _norm_mod(o_ref[...], g2_ref[...], sh2_ref[row:row + 1, :], sc2_ref[row:row + 1, :])
    hn_ref[...] = h.astype(hn_ref.dtype)
    lane = lax.broadcasted_iota(jnp.int32, (h.shape[0], LANES), 1)
    lane_f = lane.astype(F32)
    h_hi = h.astype(BF16)
    h_lo = (h - h_hi.astype(F32)).astype(BF16)
    logits = _dot(h_hi, wrh_ref[...]) + _dot(h_lo, wrh_ref[...]) + _dot(h_hi, wrl_ref[...])
    logits = jnp.where(lane < N_EXPERTS, logits, -jnp.inf)
    m1 = jnp.max(logits, axis=-1, keepdims=True)
    i1 = jnp.min(jnp.where(logits == m1, lane_f, float(LANES)), axis=-1, keepdims=True)
    rest = jnp.where(lane_f == i1, -jnp.inf, logits)
    m2 = jnp.max(rest, axis=-1, keepdims=True)
    i2 = jnp.min(jnp.where(rest == m2, lane_f, float(LANES)), axis=-1, keepdims=True)
    e2 = jnp.exp(m2 - m1)
    den = 1.0 + e2
    sel_ref[...] = jnp.where(lane == 0, i1, jnp.where(lane == 1, i2, 0.0)).astype(jnp.int32)
    gate_ref[...] = jnp.where(lane == 0, 1.0 / den, jnp.where(lane == 1, e2 / den, 0.0))


def _pool_router(x, ada_l, row, g1, g2, w_pool, pool_scale, w_router_pad):
    t, d = x.shape
    tm = min(256, t)
    nh = t // POOL_HALO
    gc = d // POOL_GROUPS
    rows = lambda i: (i, 0)
    const = lambda i: (0, 0)
    return pl.pallas_call(
        functools.partial(_pool_router_kernel, row=row, t=t),
        grid=(t // tm,),
        in_specs=[pl.BlockSpec((tm, d), rows),
                  pl.BlockSpec((POOL_HALO, d), lambda i: (jnp.maximum(i * (tm // POOL_HALO) - 1, 0), 0)),
                  pl.BlockSpec((POOL_HALO, d), lambda i: (jnp.minimum((i + 1) * (tm // POOL_HALO), nh - 1), 0)),
                  pl.BlockSpec((1, d), const),
                  _ada_spec(0), _ada_spec(1), _ada_spec(2),
                  pl.BlockSpec((POOL_GROUPS, gc, gc), lambda i: (0, 0, 0)),
                  pl.BlockSpec((1, d), const),
                  pl.BlockSpec((1, d), const),
                  _ada_spec(3), _ada_spec(4),
                  pl.BlockSpec((d, LANES), const), pl.BlockSpec((d, LANES), const)],
        out_specs=[pl.BlockSpec((tm, d), rows), pl.BlockSpec((tm, d), rows),
                   pl.BlockSpec((tm, LANES), rows), pl.BlockSpec((tm, LANES), rows)],
        out_shape=[jax.ShapeDtypeStruct((t, d), F32),
                   jax.ShapeDtypeStruct((t, d), BF16),
                   jax.ShapeDtypeStruct((t, LANES), jnp.int32),
                   jax.ShapeDtypeStruct((t, LANES), F32)],
        scratch_shapes=[pltpu.VMEM((tm + 2 * POOL_HALO, d), F32)],
        compiler_params=_params(("parallel",)),
    )(x, x, x, g1, ada_l, ada_l, ada_l, w_pool, pool_scale, g2, ada_l, ada_l, w_router_pad[0], w_router_pad[1])


MOE_TILE = 1280
MOE_SUB = 256
MOE_SUBS = MOE_TILE // MOE_SUB
GATHER_SUB = 256
GATHER_SUBS = MOE_TILE // GATHER_SUB
SRC_BLK = 256
Y_BLK = 256
CMB_TILE = 256
MOE_TF = 512
GATHER_BUFS = 3


def _moe_ffn_kernel(te_ref, tn_ref, tlo_ref, tnb_ref, trk_ref, cb_ref, posb_ref, hn_hbm, wg_ref, wu_ref, wd_ref,
                    y_ref, xb_ref, acc_ref, hbuf_ref, sem):
    i = pl.program_id(0)
    f = pl.program_id(1)
    nsub = tn_ref[i]

    @pl.when(f == 0)
    def _():
        xb_ref[...] = jnp.zeros_like(xb_ref)
        acc_ref[...] = jnp.zeros_like(acc_ref)
        row_id = lax.broadcasted_iota(jnp.int32, (GATHER_SUB, SRC_BLK), 0) + i * MOE_TILE
        rows_used = nsub * MOE_SUB
        lo = tlo_ref[i]
        n = tnb_ref[i]
        rank0 = trk_ref[i]
        cb_base = te_ref[i] * (posb_ref.shape[0] + 1)

        def blk_copy(b, slot):
            return pltpu.make_async_copy(hn_hbm.at[pl.ds((lo + b) * SRC_BLK, SRC_BLK), :], hbuf_ref.at[slot],
                                         sem.at[slot])

        for j in range(GATHER_BUFS - 1):
            @pl.when(j < n)
            def _():
                blk_copy(j, j).start()

        def body(b, carry):
            slot = b % GATHER_BUFS
            blk_copy(b, slot).wait()
            ahead = b + GATHER_BUFS - 1

            @pl.when(ahead < n)
            def _():
                blk_copy(ahead, ahead % GATHER_BUFS).start()

            blk = lo + b
            pos = posb_ref[blk]
            pos1, pos2 = pos[0:1, :], pos[1:2, :]
            before = cb_ref[cb_base + blk]
            through = cb_ref[cb_base + blk + 1]
            for q in range(GATHER_SUBS):
                @pl.when((q * GATHER_SUB < rows_used) & (before < rank0 + (q + 1) * GATHER_SUB)
                         & (through > rank0 + q * GATHER_SUB))
                def _():
                    rows = slice(q * GATHER_SUB, (q + 1) * GATHER_SUB)
                    rid = row_id + q * GATHER_SUB
                    onehot = jnp.where((pos1 == rid) | (pos2 == rid), 1.0, 0.0).astype(BF16)
                    xb_ref[rows, :] += _dot(onehot, hbuf_ref[slot]).astype(BF16)
            return carry

        lax.fori_loop(0, n, body, 0)

    for k in range(1, MOE_SUBS + 1):
        @pl.when(nsub == k)
        def _():
            rows = slice(0, k * MOE_SUB)
            h = xb_ref[rows, :]
            a = (_silu(_dot(h, wg_ref[0, 0].astype(BF16))) * _dot(h, wu_ref[0, 0].astype(BF16))).astype(BF16)
            acc_ref[rows, :] += _dot(a, wd_ref[0, 0].astype(BF16))

    @pl.when(f == pl.num_programs(1) - 1)
    def _():
        y_ref[...] = acc_ref[...].astype(y_ref.dtype)


def _moe_ffn(hn, posb, tile_expert, tile_subs, tile_blk_lo, tile_blk_n, tile_rank, blk_counts, wg, wu, wd, lyr):
    d = hn.shape[1]
    tm = MOE_TILE
    ff = wg.shape[3]
    tf = MOE_TF
    nf = ff // tf
    nt = tile_expert.shape[0]
    fidx = lambda i, f, tn: jnp.where(tn[i] > 0, f, nf - 1)
    return pl.pallas_call(
        _moe_ffn_kernel,
        grid_spec=pltpu.PrefetchScalarGridSpec(
            num_scalar_prefetch=6,
            grid=(nt, nf),
            in_specs=[pl.BlockSpec(posb.shape, lambda i, f, te, tn, *_: (0, 0, 0)),
                      pl.BlockSpec(memory_space=pl.ANY),
                      pl.BlockSpec((1, 1, d, tf), lambda i, f, te, tn, *_: (lyr, te[i], 0, fidx(i, f, tn))),
                      pl.BlockSpec((1, 1, d, tf), lambda i, f, te, tn, *_: (lyr, te[i], 0, fidx(i, f, tn))),
                      pl.BlockSpec((1, 1, tf, d), lambda i, f, te, tn, *_: (lyr, te[i], fidx(i, f, tn), 0))],
            out_specs=pl.BlockSpec((tm, d), lambda i, f, te, tn, *_: (i, 0)),
            scratch_shapes=[pltpu.VMEM((tm, d), BF16), pltpu.VMEM((tm, d), F32),
                            pltpu.VMEM((GATHER_BUFS, SRC_BLK, d), BF16),
                            pltpu.SemaphoreType.DMA((GATHER_BUFS,))]),
        out_shape=jax.ShapeDtypeStruct((nt * tm, d), BF16),
        compiler_params=_params(("arbitrary", "arbitrary"), MOE_VMEM_LIMIT),
    )(tile_expert, tile_subs, tile_blk_lo, tile_blk_n, tile_rank, blk_counts, posb, hn, wg, wu, wd)


CMB_SLOTS = [(e, j) for e in range(N_EXPERTS) for j in range(2)]


def _combine_kernel(fb_ref, nb_ref, x_ref, pos_ref, gate_ref, gt_ref, y_hbm, o_ref, acc_ref, ybuf_ref, sem,
                    *, row, tile_off):
    base = (tile_off + pl.program_id(0)) * N_EXPERTS
    tm = x_ref.shape[0]
    pos1, pos2 = pos_ref[:, 0:1], pos_ref[:, 1:2]
    g1, g2 = gate_ref[:, 0:1], gate_ref[:, 1:2]
    col = lax.broadcasted_iota(jnp.int32, (tm, Y_BLK), 1)

    def blk(s):
        e, j = CMB_SLOTS[s]
        return fb_ref[base + e] + j

    def used(s):
        e, j = CMB_SLOTS[s]
        return nb_ref[base + e] > j

    def blk_copy(s):
        return pltpu.make_async_copy(y_hbm.at[pl.ds(blk(s) * Y_BLK, Y_BLK), :], ybuf_ref.at[s], sem.at[s])

    for s in range(len(CMB_SLOTS)):
        @pl.when(used(s))
        def _():
            blk_copy(s).start()

    acc_ref[...] = jnp.zeros_like(acc_ref)
    for s in range(len(CMB_SLOTS)):
        @pl.when(used(s))
        def _():
            blk_copy(s).wait()
            off = blk(s) * Y_BLK
            w = jnp.where(pos1 - off == col, g1, 0.0) + jnp.where(pos2 - off == col, g2, 0.0)
            acc_ref[...] += _dot(w.astype(BF16), ybuf_ref[s])

    o_ref[...] = x_ref[...] + gt_ref[row:row + 1, :] * acc_ref[...]


def _combine(x, y, pos, gates, first_blk, num_blk, tok_off, ada_l, row):
    t, d = x.shape
    tm = CMB_TILE
    tile_off = tok_off // tm
    return pl.pallas_call(
        functools.partial(_combine_kernel, row=row, tile_off=tile_off),
        grid_spec=pltpu.PrefetchScalarGridSpec(
            num_scalar_prefetch=2,
            grid=(t // tm,),
            in_specs=[pl.BlockSpec((tm, d), lambda i, fb, nb: (i, 0)),
                      pl.BlockSpec((tm, 2), lambda i, fb, nb: (tile_off + i, 0)),
                      pl.BlockSpec((tm, LANES), lambda i, fb, nb: (i, 0)),
                      _ada_spec(5),
                      pl.BlockSpec(memory_space=pl.ANY)],
            out_specs=pl.BlockSpec((tm, d), lambda i, fb, nb: (i, 0)),
            scratch_shapes=[pltpu.VMEM((tm, d), F32), pltpu.VMEM((len(CMB_SLOTS), Y_BLK, d), BF16),
                            pltpu.SemaphoreType.DMA((len(CMB_SLOTS),))]),
        out_shape=jax.ShapeDtypeStruct((t, d), F32),
        compiler_params=_params(("arbitrary",)),
    )(first_blk, num_blk, x, pos, gates, ada_l, y)


def _route(sel):
    tm = MOE_TILE
    n_tok = sel.shape[0]
    n_asg = 2 * n_tok
    nt = -(-n_asg // tm) + N_EXPERTS
    e_flat = sel.reshape(-1)
    onehot = (e_flat[:, None] == jnp.arange(N_EXPERTS, dtype=jnp.int32)[None, :]).astype(jnp.int32)
    csum = jnp.cumsum(onehot, axis=0)
    count = csum[-1]
    rank = jnp.sum((csum - onehot) * onehot, axis=1)
    tiles_e = (count + tm - 1) // tm
    tile_end = jnp.cumsum(tiles_e)
    base = (tile_end - tiles_e) * tm
    pos = jnp.sum(onehot * base[None, :], axis=1) + rank
    pos = jnp.where(e_flat >= 0, pos, -1)
    tile_id = jnp.arange(nt, dtype=jnp.int32)
    used = tile_end[-1]
    owner = lambda i: jnp.minimum(jnp.sum((i[:, None] >= tile_end[None, :]).astype(jnp.int32), axis=1), N_EXPERTS - 1)
    tile_expert = owner(jnp.minimum(tile_id, used - 1))
    rows_in_tile = jnp.clip((base + count)[tile_expert] - tile_id * tm, 0, tm)
    tile_subs = jnp.where(tile_id < used, (rows_in_tile + MOE_SUB - 1) // MOE_SUB, 0)
    n_blk = n_tok // SRC_BLK
    blk_counts = jnp.concatenate([jnp.zeros((1, N_EXPERTS), jnp.int32),
                                  csum.reshape(n_blk, 2 * SRC_BLK, N_EXPERTS)[:, -1, :]], axis=0).T
    tile_rank = tile_id * tm - base[tile_expert]
    counts_t = blk_counts[tile_expert]
    tile_blk_lo = jnp.sum((counts_t[:, 1:] <= tile_rank[:, None]).astype(jnp.int32), axis=1)
    tile_blk_hi = jnp.sum((counts_t[:, :-1] < (tile_rank + rows_in_tile)[:, None]).astype(jnp.int32), axis=1)
    tile_blk_n = jnp.where(tile_subs > 0, jnp.maximum(tile_blk_hi - tile_blk_lo, 0), 0)
    tile_blk_lo = jnp.minimum(tile_blk_lo, n_blk - 1)
    posb = jnp.pad(pos.reshape(n_blk, SRC_BLK, 2).transpose(0, 2, 1), ((0, 0), (0, 6), (0, 0)), constant_values=-1)
    ntt = n_tok // CMB_TILE
    through = csum.reshape(ntt, 2 * CMB_TILE, N_EXPERTS)[:, -1, :]
    before = jnp.concatenate([jnp.zeros((1, N_EXPERTS), jnp.int32), through[:-1]], axis=0)
    lo = base[None, :] + before
    hi = base[None, :] + through - 1
    first_blk = lo // Y_BLK
    num_blk = jnp.where(through > before, hi // Y_BLK - first_blk + 1, 0)
    return (posb, pos.reshape(n_tok, 2), tile_expert, tile_subs, tile_blk_lo, tile_blk_n, tile_rank,
            blk_counts.reshape(-1), first_blk.reshape(-1), num_blk.reshape(-1))


def _moe(streams, ada_l, wg, wu, wd, lyr):
    hs = [s[2] for s in streams]
    sels = [s[3][:, :2] for s in streams]
    n_tok = sum(h.shape[0] for h in hs)
    pad = -n_tok % SRC_BLK
    if pad:
        hs.append(jnp.zeros((pad, D_MODEL), BF16))
        sels.append(jnp.full((pad, 2), -1, jnp.int32))
    h_all = hs[0] if len(hs) == 1 else jnp.concatenate(hs, axis=0)
    sel_all = sels[0] if len(sels) == 1 else jnp.concatenate(sels, axis=0)
    (posb, pos, tile_expert, tile_subs, tile_blk_lo, tile_blk_n, tile_rank, blk_counts,
     first_blk, num_blk) = _route(sel_all)
    y = _moe_ffn(h_all, posb, tile_expert, tile_subs, tile_blk_lo, tile_blk_n, tile_rank, blk_counts,
                 wg, wu, wd, lyr)
    outs, off = [], 0
    for x, row, _, _, gt in streams:
        outs.append(_combine(x, y, pos, gt, first_blk, num_blk, off, ada_l, row))
        off += x.shape[0]
    return outs


def _even_layer(x, ctx, ada_l, g1, g2, w_in, g_qn, g_kn, sink, w_gate_up, b_gate_up, g_gla, w_out,
                w_ffn_gate, w_ffn_up, w_ffn_down, lyr, rope_tabs, ctx_out):
    d = D_MODEL
    w_main = jnp.concatenate([w_in[:, 2080:3104], w_in[:, 1024:2048], w_in[:, 3616:4640], w_in[:, 512:1024],
                              w_in[:, 3104:3616], w_in[:, 0:256], w_in[:, 256:512]], axis=1).astype(BF16)
    w_gate = jnp.pad(w_in[:, 2048:2080], ((0, 0), (0, LANES - 2 * GATE_RANK))).astype(BF16)
    nk = B_HEADS * B_DK
    w_up = jnp.concatenate([jnp.pad(w_gate_up[0], ((0, 0), (0, nk))), jnp.pad(w_gate_up[1], ((0, 0), (nk, 0))),
                            jnp.zeros((LANES - 2 * GATE_RANK, 2 * nk), F32)], axis=0)
    w_up_hi = w_up.astype(BF16)
    w_up = (w_up_hi, (w_up - w_up_hi.astype(F32)).astype(BF16))
    b_up = b_gate_up.reshape(1, 2 * nk)
    w_out_b = w_out.astype(BF16)
    gq, gk, gg = g_qn.reshape(1, -1), g_kn.reshape(1, -1), g_gla.reshape(1, -1)
    sink_tab = jnp.broadcast_to(sink[:, None], (A_HEADS, LANES))
    cos, sin = rope_tabs

    pc, la_c = _inproj(ctx, ada_l, 1, g1, w_main, w_gate, w_up, b_up)
    qc, kc = _qkprep(pc, cos, sin, gq, gk, rope=False)
    s0 = jnp.zeros((B_HEADS, B_DV, B_DK), F32)
    oc_f, s_fwd = _gla_scan(pc, la_c, s0, rev=False)
    oc_b, s_bwd = _gla_scan(pc, la_c, s0, rev=True)

    px, la_x = _inproj(x, ada_l, 0, g1, w_main, w_gate, w_up, b_up)
    qx, kx = _qkprep(px, cos, sin, gq, gk, rope=True)
    oa = _attention(qx, kx, px, kc, pc, sink_tab, local=True)
    ox_f, _ = _gla_scan(px, la_x, s_fwd, rev=False)
    ox_b, _ = _gla_scan(px, la_x, s_bwd, rev=True)
    x = _outproj(oa, ox_f, ox_b, px, gg, w_out_b, x, ada_l, 0)
    x = _ffn(x, ada_l, 0, g2, w_ffn_gate, w_ffn_up, w_ffn_down, lyr)
    if ctx_out:
        oa_c = _attention(qc, None, None, kc, pc, sink_tab, local=False)
        ctx = _outproj(oa_c, oc_f, oc_b, pc, gg, w_out_b, ctx, ada_l, 1)
        ctx = _ffn(ctx, ada_l, 1, g2, w_ffn_gate, w_ffn_up, w_ffn_down, lyr)
    return x, ctx


def _odd_layer(x, ctx, ada_l, g1, g2, w_pool, pool_scale, w_router, w_exp_gate, w_exp_up, w_exp_down, lyr, ctx_out):
    w_pool_b = w_pool.astype(BF16)
    ps = pool_scale.reshape(1, -1)
    wr = jnp.pad(w_router, ((0, 0), (0, LANES - N_EXPERTS)))
    wr_hi = wr.astype(BF16)
    wr = (wr_hi, (wr - wr_hi.astype(F32)).astype(BF16))
    streams = [(0,) + tuple(_pool_router(x, ada_l, 0, g1, g2, w_pool_b, ps, wr))]
    if ctx_out:
        streams.append((1,) + tuple(_pool_router(ctx, ada_l, 1, g1, g2, w_pool_b, ps, wr)))
    outs = _moe([(s[1], s[0], s[2], s[3], s[4]) for s in streams], ada_l, w_exp_gate, w_exp_up, w_exp_down, lyr)
    return outs[0], (outs[1] if ctx_out else ctx)


def kernel(x, c, ctx, c_ctx, w_ada, b_ada, norm_g, w_in, g_qn, g_kn, attn_sink, w_gate_up, b_gate_up, g_gla, w_out,
           w_ffn_gate, w_ffn_up, w_ffn_down, w_pool, pool_scale, w_router, w_exp_gate, w_exp_up, w_exp_down):
    depth = w_ada.shape[0]
    xs = x[0]
    cs = ctx[0]
    t = xs.shape[0]
    cond = jnp.concatenate([c, c_ctx[None, :], jnp.zeros((6, D_MODEL), F32)], axis=0)
    ada = _ada_all(cond, w_ada, b_ada)
    rope_tabs = _rope_tables(t)
    for l in range(depth):
        ctx_later = any(j % 2 == 0 for j in range(l + 1, depth))
        g1 = norm_g[l, 0].reshape(1, -1)
        g2 = norm_g[l, 1].reshape(1, -1)
        if l % 2 == 0:
            e = l // 2
            xs, cs = _even_layer(xs, cs, ada[l], g1, g2, w_in[e], g_qn[e], g_kn[e], attn_sink[e], w_gate_up[e],
                                 b_gate_up[e], g_gla[e], w_out[e], w_ffn_gate, w_ffn_up, w_ffn_down, e,
                                 rope_tabs, ctx_later)
        else:
            o = l // 2
            xs, cs = _odd_layer(xs, cs, ada[l], g1, g2, w_pool[o], pool_scale[o], w_router[o],
                                w_exp_gate, w_exp_up, w_exp_down, o, ctx_later)
    return xs[None]
```

```python
import functools

import jax
import jax.numpy as jnp
import numpy as np
from jax import lax
from jax.experimental import pallas as pl
from jax.experimental.pallas import tpu as pltpu

F32 = jnp.float32
BF16 = jnp.bfloat16

D_MODEL = 2048
GRID_W = 64
A_HEADS = 8
A_KV_HEADS = 2
A_GROUP = 4
A_HEAD_DIM = 128
A_WIDTH = A_HEADS * A_HEAD_DIM
ATTN_BLOCK = 128
ROPE_BASE = 10000.0
B_HEADS = 4
B_DV = 256
B_DK = 128
B_WIDTH = B_HEADS * B_DV
GATE_RANK = 16
GATE_TAU = 16.0
GLA_CHUNK = 64
GLA_SUB = 256
POOL_GROUPS = 4
POOL_WINDOWS = (2, 4, 8, 16)
POOL_HALO = 8
N_EXPERTS = 8
NORM_EPS = 1e-6
NEG_INF = -1e30
LANES = 128

P_COLS = 4608
VMEM_LIMIT = 56 * 1024 * 1024
MOE_VMEM_LIMIT = 62 * 1024 * 1024


def _params(sem, vmem=VMEM_LIMIT):
    return pltpu.CompilerParams(dimension_semantics=sem, vmem_limit_bytes=vmem)


def _dot(a, b, precision=None):
    return jnp.dot(a, b, preferred_element_type=F32, precision=precision)


def _dot_nt(a, b):
    return lax.dot_general(a, b, (((1,), (1,)), ((), ())), preferred_element_type=F32)


def _dot_tn(a, b):
    return lax.dot_general(a, b, (((0,), (0,)), ((), ())), preferred_element_type=F32)


def _silu(x):
    return x * jax.nn.sigmoid(x)


def _norm_mod(x, g, shift, scale):
    ms = jnp.mean(x * x, axis=-1, keepdims=True)
    return (x * lax.rsqrt(ms + NORM_EPS) * g) * (1.0 + scale) + shift


def _ada_kernel(cond_ref, w_ref, b_ref, o_ref):
    s = _silu(cond_ref[...])
    s_hi = s.astype(BF16)
    s_lo = (s - s_hi.astype(F32)).astype(BF16)
    w = w_ref[0].astype(BF16)
    o_ref[0] = _dot(s_hi, w) + _dot(s_lo, w) + b_ref[0]


def _ada_all(cond, w_ada, b_ada):
    depth, d, n = w_ada.shape
    tn = 1024
    return pl.pallas_call(
        _ada_kernel,
        grid=(depth, n // tn),
        in_specs=[pl.BlockSpec((8, d), lambda l, j: (0, 0)),
                  pl.BlockSpec((1, d, tn), lambda l, j: (l, 0, j)),
                  pl.BlockSpec((1, 1, tn), lambda l, j: (l, 0, j))],
        out_specs=pl.BlockSpec((1, 8, tn), lambda l, j: (l, 0, j)),
        out_shape=jax.ShapeDtypeStruct((depth, 8, n), F32),
        compiler_params=_params(("parallel", "parallel")),
    )(cond, w_ada, b_ada.reshape(depth, 1, n))


def _ada_spec(k):
    return pl.BlockSpec((8, D_MODEL), lambda *_: (0, k))


def _inproj_kernel(x_ref, g_ref, sh_ref, sc_ref, w_ref, wg_ref, wuh_ref, wul_ref, bup_ref, p_ref, la_ref, hn_ref,
                   *, row):
    @pl.when(pl.program_id(1) == 0)
    def _():
        h = _norm_mod(x_ref[...], g_ref[...], sh_ref[row:row + 1, :], sc_ref[row:row + 1, :]).astype(BF16)
        hn_ref[...] = h
        lr = _dot(h, wg_ref[...])
        lr_hi = lr.astype(BF16)
        lr_lo = (lr - lr_hi.astype(F32)).astype(BF16)
        z = _dot(lr_hi, wuh_ref[...]) + _dot(lr_lo, wuh_ref[...]) + _dot(lr_hi, wul_ref[...]) + bup_ref[...]
        la_ref[...] = (jnp.minimum(z, 0.0) - jnp.log1p(jnp.exp(-jnp.abs(z)))) * (1.0 / GATE_TAU)

    p_ref[...] = _dot(hn_ref[...], w_ref[...]).astype(p_ref.dtype)


def _inproj(x, ada_l, row, g, w_main, w_gate, w_up, b_up):
    t, d = x.shape
    tm = min(512, t)
    tn = 1536
    nla = 2 * B_HEADS * B_DK
    return pl.pallas_call(
        functools.partial(_inproj_kernel, row=row),
        grid=(t // tm, P_COLS // tn),
        in_specs=[pl.BlockSpec((tm, d), lambda i, j: (i, 0)),
                  pl.BlockSpec((1, d), lambda i, j: (0, 0)),
                  _ada_spec(0), _ada_spec(1),
                  pl.BlockSpec((d, tn), lambda i, j: (0, j)),
                  pl.BlockSpec((d, LANES), lambda i, j: (0, 0)),
                  pl.BlockSpec((LANES, nla), lambda i, j: (0, 0)),
                  pl.BlockSpec((LANES, nla), lambda i, j: (0, 0)),
                  pl.BlockSpec((1, nla), lambda i, j: (0, 0))],
        out_specs=[pl.BlockSpec((tm, tn), lambda i, j: (i, j)),
                   pl.BlockSpec((tm, nla), lambda i, j: (i, 0))],
        out_shape=[jax.ShapeDtypeStruct((t, P_COLS), BF16),
                   jax.ShapeDtypeStruct((t, nla), F32)],
        scratch_shapes=[pltpu.VMEM((tm, d), BF16)],
        compiler_params=_params(("parallel", "arbitrary")),
    )(x, g, ada_l, ada_l, w_main, w_gate, w_up[0], w_up[1], b_up)


def _qkprep_kernel(q_ref, k_ref, cos_ref, sin_ref, gq_ref, gk_ref, qo_ref, ko_ref, *, rope):
    tm = q_ref.shape[0]
    lane = lax.broadcasted_iota(jnp.int32, (tm, A_HEAD_DIM), 1)
    first_half = (lane % 64) < 32

    def prep(xh, g, scale):
        ms = jnp.mean(xh * xh, axis=-1, keepdims=True)
        y = xh * lax.rsqrt(ms + NORM_EPS) * g
        if rope:
            partner = jnp.where(first_half, pltpu.roll(y, 96, 1), pltpu.roll(y, 32, 1))
            y = y * cos_ref[...] + partner * sin_ref[...]
        return (y * scale).astype(BF16)

    for h in range(A_HEADS):
        cols = slice(h * A_HEAD_DIM, (h + 1) * A_HEAD_DIM)
        qo_ref[:, cols] = prep(q_ref[:, cols].astype(F32), gq_ref[...], A_HEAD_DIM ** -0.5)
    for h in range(A_KV_HEADS):
        cols = slice(h * A_HEAD_DIM, (h + 1) * A_HEAD_DIM)
        ko_ref[:, cols] = prep(k_ref[:, cols].astype(F32), gk_ref[...], 1.0)


def _qkprep(p, cos, sin, g_qn, g_kn, rope):
    t = p.shape[0]
    tm = min(512, t)
    kw = A_KV_HEADS * A_HEAD_DIM
    return pl.pallas_call(
        functools.partial(_qkprep_kernel, rope=rope),
        grid=(t // tm,),
        in_specs=[pl.BlockSpec((tm, A_WIDTH), lambda i: (i, 0)),
                  pl.BlockSpec((tm, kw), lambda i: (i, 4096 // kw)),
                  pl.BlockSpec((tm, A_HEAD_DIM), lambda i: (i, 0)),
                  pl.BlockSpec((tm, A_HEAD_DIM), lambda i: (i, 0)),
                  pl.BlockSpec((1, A_HEAD_DIM), lambda i: (0, 0)),
                  pl.BlockSpec((1, A_HEAD_DIM), lambda i: (0, 0))],
        out_specs=[pl.BlockSpec((tm, A_WIDTH), lambda i: (i, 0)),
                   pl.BlockSpec((tm, kw), lambda i: (i, 0))],
        out_shape=[jax.ShapeDtypeStruct((t, A_WIDTH), BF16),
                   jax.ShapeDtypeStruct((t, kw), BF16)],
        compiler_params=_params(("parallel",)),
    )(p, p, cos, sin, g_qn, g_kn)


def _rope_tables(t):
    half = A_HEAD_DIM // 4
    freqs = ROPE_BASE ** (-np.arange(half, dtype=np.float64) / half)
    tok = np.arange(t)
    row = (tok // GRID_W).astype(np.float64)[:, None] * freqs
    col = (tok % GRID_W).astype(np.float64)[:, None] * freqs
    cos = np.concatenate([np.cos(row), np.cos(row), np.cos(col), np.cos(col)], axis=-1)
    sin = np.concatenate([-np.sin(row), np.sin(row), -np.sin(col), np.sin(col)], axis=-1)
    return jnp.asarray(cos.astype(np.float32)), jnp.asarray(sin.astype(np.float32))


def _softmax_pv(scores, values, sink_col):
    def lane_blocks(xs):
        return [x[:, i * LANES:(i + 1) * LANES] for x in xs for i in range(x.shape[1] // LANES)]

    m = jnp.maximum(jnp.max(functools.reduce(jnp.maximum, lane_blocks(scores)), axis=-1, keepdims=True), sink_col)
    probs = [jnp.exp(s - m) for s in scores]
    denom = jnp.exp(sink_col - m) + jnp.sum(functools.reduce(jnp.add, lane_blocks(probs)), axis=-1, keepdims=True)
    acc = None
    for p, v in zip(probs, values):
        pv = _dot(p.astype(BF16), v)
        acc = pv if acc is None else acc + pv
    return acc / denom


def _attn_kernel(*refs, local):
    if local:
        q_ref, kp_ref, ko_ref, kn_ref, vp_ref, vo_ref, vn_ref, kc_ref, vc_ref, sink_ref, o_ref = refs
    else:
        q_ref, kc_ref, vc_ref, sink_ref, o_ref = refs
    n = pl.program_id(0)
    nb = pl.num_programs(0)
    rows = A_GROUP * ATTN_BLOCK
    if local:
        qi = lax.broadcasted_iota(jnp.int32, (rows, ATTN_BLOCK), 0) % ATTN_BLOCK
        kj = lax.broadcasted_iota(jnp.int32, (rows, ATTN_BLOCK), 1)
        mask_prev = (kj >= qi) & (n > 0)
        mask_next = (kj <= qi) & (n < nb - 1)
    for kv in range(A_KV_HEADS):
        kc = slice(kv * A_HEAD_DIM, (kv + 1) * A_HEAD_DIM)
        heads = [kv * A_GROUP + g for g in range(A_GROUP)]
        q4 = jnp.concatenate([q_ref[:, h * A_HEAD_DIM:(h + 1) * A_HEAD_DIM] for h in heads], axis=0)
        sink_col = jnp.concatenate(
            [jnp.broadcast_to(sink_ref[h:h + 1, 0:1], (ATTN_BLOCK, 1)) for h in heads], axis=0)
        scores, values = [], []
        if local:
            scores.append(jnp.where(mask_prev, _dot_nt(q4, kp_ref[:, kc]), NEG_INF))
            scores.append(_dot_nt(q4, ko_ref[:, kc]))
            scores.append(jnp.where(mask_next, _dot_nt(q4, kn_ref[:, kc]), NEG_INF))
            values += [vp_ref[:, kc], vo_ref[:, kc], vn_ref[:, kc]]
        scores.append(_dot_nt(q4, kc_ref[:, kc]))
        values.append(vc_ref[:, kc])
        o = _softmax_pv(scores, values, sink_col).astype(o_ref.dtype)
        for g, h in enumerate(heads):
            o_ref[:, h * A_HEAD_DIM:(h + 1) * A_HEAD_DIM] = o[g * ATTN_BLOCK:(g + 1) * ATTN_BLOCK, :]


def _attention(q, k, p, k_ctx, p_ctx, sink_tab, local):
    t = q.shape[0]
    nb = t // ATTN_BLOCK
    l = k_ctx.shape[0]
    kw = A_KV_HEADS * A_HEAD_DIM
    vcol = 4352 // kw
    blk = lambda f: pl.BlockSpec((ATTN_BLOCK, kw), f)
    blkv = lambda f: pl.BlockSpec((ATTN_BLOCK, kw), lambda i: (f(i)[0], vcol))
    prev = lambda i: (jnp.maximum(i - 1, 0), 0)
    own = lambda i: (i, 0)
    nxt = lambda i: (jnp.minimum(i + 1, nb - 1), 0)
    in_specs = [pl.BlockSpec((ATTN_BLOCK, A_WIDTH), own)]
    args = [q]
    if local:
        in_specs += [blk(prev), blk(own), blk(nxt), blkv(prev), blkv(own), blkv(nxt)]
        args += [k, k, k, p, p, p]
    in_specs += [pl.BlockSpec((l, kw), lambda i: (0, 0)), pl.BlockSpec((l, kw), lambda i: (0, vcol)),
                 pl.BlockSpec((A_HEADS, LANES), lambda i: (0, 0))]
    args += [k_ctx, p_ctx, sink_tab]
    return pl.pallas_call(
        functools.partial(_attn_kernel, local=local),
        grid=(nb,),
        in_specs=in_specs,
        out_specs=pl.BlockSpec((ATTN_BLOCK, A_WIDTH), own),
        out_shape=jax.ShapeDtypeStruct((t, A_WIDTH), BF16),
        compiler_params=_params(("parallel",)),
    )(*args)


def _gla_kernel(q_ref, k_ref, v_ref, la_ref, s0_ref, o_ref, sf_ref, st_ref, *, rev, nchunk):
    @pl.when(pl.program_id(0) == 0)
    def _():
        st_ref[...] = s0_ref[...]

    r = nchunk * GLA_CHUNK
    sub = min(r, GLA_SUB)
    ii = lax.broadcasted_iota(jnp.int32, (sub, sub), 0)
    jj = lax.broadcasted_iota(jnp.int32, (sub, sub), 1)
    same_chunk = (ii // GLA_CHUNK) == (jj // GLA_CHUNK)
    tri = same_chunk & ((jj >= ii) if rev else (jj <= ii))
    b_all = la_ref[...]
    pos = lax.broadcasted_iota(jnp.int32, b_all.shape, 0) % GLA_CHUNK
    step = 1
    while step < GLA_CHUNK:
        if rev:
            b_all = b_all + jnp.where(pos < GLA_CHUNK - step, pltpu.roll(b_all, r - step, 0), 0.0)
        else:
            b_all = b_all + jnp.where(pos >= step, pltpu.roll(b_all, step, 0), 0.0)
        step *= 2
    heads = []
    for h in range(B_HEADS):
        kc = slice(h * B_DK, (h + 1) * B_DK)
        b = b_all[:, kc]
        b3 = b.reshape(nchunk, GLA_CHUNK, B_DK)
        b_end = b3[:, 0:1, :] if rev else b3[:, GLA_CHUNK - 1:GLA_CHUNK, :]
        k = k_ref[:, kc].astype(F32)
        qe = (q_ref[:, kc].astype(F32) * (B_DK ** -0.5) * jnp.exp(b)).astype(BF16)
        ke = (k * jnp.exp(-b)).astype(BF16)
        kd = (k.reshape(nchunk, GLA_CHUNK, B_DK) * jnp.exp(b_end - b3)).astype(BF16)
        v = v_ref[:, h * B_DV:(h + 1) * B_DV].astype(BF16)
        o_intra = []
        for s0 in range(0, r, sub):
            rs = slice(s0, s0 + sub)
            a = jnp.where(tri, _dot_nt(qe[rs, :], ke[rs, :]), 0.0).astype(BF16)
            o_intra.append(_dot(a, v[rs, :]))
        heads.append((qe, kd, v, jnp.concatenate(o_intra, axis=0), jnp.exp(b_end)))
    st = [st_ref[h] for h in range(B_HEADS)]
    for c in (range(nchunk - 1, -1, -1) if rev else range(nchunk)):
        rows = slice(c * GLA_CHUNK, (c + 1) * GLA_CHUNK)
        for h, (qe, kd, v, o_intra, decay) in enumerate(heads):
            o = o_intra[rows, :] + _dot_nt(qe[rows, :], st[h].astype(BF16))
            o_ref[0, rows, h * B_DV:(h + 1) * B_DV] = o.astype(o_ref.dtype)
            st[h] = st[h] * decay[c] + _dot_tn(v[rows, :], kd[c])
    for h in range(B_HEADS):
        st_ref[h] = st[h]
        sf_ref[h] = st[h]


def _gla_scan(p, la, s0, rev):
    t = p.shape[0]
    r = min(512, t)
    nblk = t // r
    rb = (lambda c: nblk - 1 - c) if rev else (lambda c: c)
    nk = B_HEADS * B_DK
    o, sf = pl.pallas_call(
        functools.partial(_gla_kernel, rev=rev, nchunk=r // GLA_CHUNK),
        grid=(nblk,),
        in_specs=[pl.BlockSpec((r, nk), lambda c: (rb(c), 3584 // nk)),
                  pl.BlockSpec((r, nk), lambda c: (rb(c), 3072 // nk)),
                  pl.BlockSpec((r, B_WIDTH), lambda c: (rb(c), 1024 // B_WIDTH)),
                  pl.BlockSpec((r, nk), lambda c: (rb(c), 1 if rev else 0)),
                  pl.BlockSpec((B_HEADS, B_DV, B_DK), lambda c: (0, 0, 0))],
        out_specs=[pl.BlockSpec((1, r, B_WIDTH), lambda c: (0, rb(c), 0)),
                   pl.BlockSpec((B_HEADS, B_DV, B_DK), lambda c: (0, 0, 0))],
        out_shape=[jax.ShapeDtypeStruct((1, t, B_WIDTH), BF16),
                   jax.ShapeDtypeStruct((B_HEADS, B_DV, B_DK), F32)],
        scratch_shapes=[pltpu.VMEM((B_HEADS, B_DV, B_DK), F32)],
        compiler_params=_params(("arbitrary",)),
    )(p, p, p, la, s0)
    return o[0], sf


def _outproj_kernel(oa_ref, of_ref, ob_ref, og_ref, gg_ref, w_ref, x_ref, gt_ref, o_ref, mix_ref, *, row):
    @pl.when(pl.program_id(1) == 0)
    def _():
        mix_ref[:, 0:A_WIDTH] = oa_ref[...]
        for h in range(B_HEADS):
            cols = slice(h * B_DV, (h + 1) * B_DV)
            o = of_ref[:, cols].astype(F32) + ob_ref[:, cols].astype(F32)
            ms = jnp.mean(o * o, axis=-1, keepdims=True)
            y = o * lax.rsqrt(ms + NORM_EPS) * gg_ref[...]
            mix_ref[:, A_WIDTH + h * B_DV:A_WIDTH + (h + 1) * B_DV] = (y * _silu(og_ref[:, cols].astype(F32))).astype(BF16)

    o_ref[...] = x_ref[...] + gt_ref[row:row + 1, :] * _dot(mix_ref[...], w_ref[...])


def _outproj(oa, o_f, o_b, p, g_gla, w_out, x, ada_l, row):
    t, d = x.shape
    tm = min(512, t)
    tn = 1024
    return pl.pallas_call(
        functools.partial(_outproj_kernel, row=row),
        grid=(t // tm, d // tn),
        in_specs=[pl.BlockSpec((tm, A_WIDTH), lambda i, j: (i, 0)),
                  pl.BlockSpec((tm, B_WIDTH), lambda i, j: (i, 0)),
                  pl.BlockSpec((tm, B_WIDTH), lambda i, j: (i, 0)),
                  pl.BlockSpec((tm, B_WIDTH), lambda i, j: (i, 2048 // B_WIDTH)),
                  pl.BlockSpec((1, B_DV), lambda i, j: (0, 0)),
                  pl.BlockSpec((A_WIDTH + B_WIDTH, tn), lambda i, j: (0, j)),
                  pl.BlockSpec((tm, tn), lambda i, j: (i, j)),
                  pl.BlockSpec((8, tn), lambda i, j: (0, 2 * (d // tn) + j))],
        out_specs=pl.BlockSpec((tm, tn), lambda i, j: (i, j)),
        out_shape=jax.ShapeDtypeStruct((t, d), F32),
        scratch_shapes=[pltpu.VMEM((tm, A_WIDTH + B_WIDTH), BF16)],
        compiler_params=_params(("parallel", "arbitrary")),
    )(oa, o_f, o_b, p, g_gla, w_out, x, ada_l)


def _ffn_kernel(x_ref, g_ref, sh_ref, sc_ref, gt_ref, wg_ref, wu_ref, wd_ref, o_ref, hn_ref, *, row):
    f = pl.program_id(1)

    @pl.when(f == 0)
    def _():
        hn_ref[...] = _norm_mod(x_ref[...], g_ref[...], sh_ref[row:row + 1, :], sc_ref[row:row + 1, :]).astype(BF16)
        o_ref[...] = jnp.zeros_like(o_ref)

    h = hn_ref[...]
    a = (_silu(_dot(h, wg_ref[0].astype(BF16))) * _dot(h, wu_ref[0].astype(BF16))).astype(BF16)
    o_ref[...] += _dot(a, wd_ref[0].astype(BF16))

    @pl.when(f == pl.num_programs(1) - 1)
    def _():
        o_ref[...] = x_ref[...] + gt_ref[row:row + 1, :] * o_ref[...]


def _ffn(x, ada_l, row, g, wg, wu, wd, lyr):
    t, d = x.shape
    ff = wg.shape[2]
    tm = min(1024, t)
    tf = 256
    return pl.pallas_call(
        functools.partial(_ffn_kernel, row=row),
        grid=(t // tm, ff // tf),
        in_specs=[pl.BlockSpec((tm, d), lambda i, f: (i, 0)),
                  pl.BlockSpec((1, d), lambda i, f: (0, 0)),
                  _ada_spec(3), _ada_spec(4), _ada_spec(5),
                  pl.BlockSpec((1, d, tf), lambda i, f: (lyr, 0, f)),
                  pl.BlockSpec((1, d, tf), lambda i, f: (lyr, 0, f)),
                  pl.BlockSpec((1, tf, d), lambda i, f: (lyr, f, 0))],
        out_specs=pl.BlockSpec((tm, d), lambda i, f: (i, 0)),
        out_shape=jax.ShapeDtypeStruct((t, d), F32),
        scratch_shapes=[pltpu.VMEM((tm, d), BF16)],
        compiler_params=_params(("parallel", "arbitrary"), MOE_VMEM_LIMIT),
    )(x, g, ada_l, ada_l, ada_l, wg, wu, wd)


def _pool_router_kernel(x_ref, xp_ref, xn_ref, g_ref, sh_ref, sc_ref, gt_ref, w_ref, ps_ref,
                        g2_ref, sh2_ref, sc2_ref, wrh_ref, wrl_ref, o_ref, hn_ref, sel_ref, gate_ref, hb_ref,
                        *, row, t):
    i = pl.program_id(0)
    tm = x_ref.shape[0]
    g = g_ref[...]
    sh = sh_ref[row:row + 1, :]
    sc = sc_ref[row:row + 1, :]
    hb_ref[0:POOL_HALO, :] = jnp.where(i > 0, _norm_mod(xp_ref[...], g, sh, sc), 0.0)
    hb_ref[POOL_HALO:POOL_HALO + tm, :] = _norm_mod(x_ref[...], g, sh, sc)
    hb_ref[POOL_HALO + tm:, :] = jnp.where(i < pl.num_programs(0) - 1, _norm_mod(xn_ref[...], g, sh, sc), 0.0)
    tpos = i * tm + lax.broadcasted_iota(jnp.int32, (tm, 1), 0)
    gc = D_MODEL // POOL_GROUPS
    for grp, w in enumerate(POOL_WINDOWS):
        cols = slice(grp * gc, (grp + 1) * gc)
        acc = None
        for dlt in range(-(w // 2), w - w // 2):
            piece = hb_ref[POOL_HALO + dlt:POOL_HALO + dlt + tm, cols]
            acc = piece if acc is None else acc + piece
        cnt = jnp.minimum(tpos + (w - w // 2), t) - jnp.maximum(tpos - w // 2, 0)
        pooled = acc / cnt.astype(F32) - hb_ref[POOL_HALO:POOL_HALO + tm, cols]
        y = _dot(pooled.astype(BF16), w_ref[grp])
        o_ref[:, cols] = x_ref[:, cols] + gt_ref[row:row + 1, cols] * (y * ps_ref[:, cols])

    h = _norm_mod(o_ref[...], g2_ref[...], sh2_ref[row:row + 1, :], sc2_ref[row:row + 1, :])
    hn_ref[...] = h.astype(hn_ref.dtype)
    lane = lax.broadcasted_iota(jnp.int32, (h.shape[0], LANES), 1)
    lane_f = lane.astype(F32)
    h_hi = h.astype(BF16)
    h_lo = (h - h_hi.astype(F32)).astype(BF16)
    logits = _dot(h_hi, wrh_ref[...]) + _dot(h_lo, wrh_ref[...]) + _dot(h_hi, wrl_ref[...])
    logits = jnp.where(lane < N_EXPERTS, logits, -jnp.inf)
    m1 = jnp.max(logits, axis=-1, keepdims=True)
    i1 = jnp.min(jnp.where(logits == m1, lane_f, float(LANES)), axis=-1, keepdims=True)
    rest = jnp.where(lane_f == i1, -jnp.inf, logits)
    m2 = jnp.max(rest, axis=-1, keepdims=True)
    i2 = jnp.min(jnp.where(rest == m2, lane_f, float(LANES)), axis=-1, keepdims=True)
    e2 = jnp.exp(m2 - m1)
    den = 1.0 + e2
    sel_ref[...] = jnp.where(lane == 0, i1, jnp.where(lane == 1, i2, 0.0)).astype(jnp.int32)
    gate_ref[...] = jnp.where(lane == 0, 1.0 / den, jnp.where(lane == 1, e2 / den, 0.0))


def _pool_router(x, ada_l, row, g1, g2, w_pool, pool_scale, w_router_pad):
    t, d = x.shape
    tm = min(256, t)
    nh = t // POOL_HALO
    gc = d // POOL_GROUPS
    rows = lambda i: (i, 0)
    const = lambda i: (0, 0)
    return pl.pallas_call(
        functools.partial(_pool_router_kernel, row=row, t=t),
        grid=(t // tm,),
        in_specs=[pl.BlockSpec((tm, d), rows),
                  pl.BlockSpec((POOL_HALO, d), lambda i: (jnp.maximum(i * (tm // POOL_HALO) - 1, 0), 0)),
                  pl.BlockSpec((POOL_HALO, d), lambda i: (jnp.minimum((i + 1) * (tm // POOL_HALO), nh - 1), 0)),
                  pl.BlockSpec((1, d), const),
                  _ada_spec(0), _ada_spec(1), _ada_spec(2),
                  pl.BlockSpec((POOL_GROUPS, gc, gc), lambda i: (0, 0, 0)),
                  pl.BlockSpec((1, d), const),
                  pl.BlockSpec((1, d), const),
                  _ada_spec(3), _ada_spec(4),
                  pl.BlockSpec((d, LANES), const), pl.BlockSpec((d, LANES), const)],
        out_specs=[pl.BlockSpec((tm, d), rows), pl.BlockSpec((tm, d), rows),
                   pl.BlockSpec((tm, LANES), rows), pl.BlockSpec((tm, LANES), rows)],
        out_shape=[jax.ShapeDtypeStruct((t, d), F32),
                   jax.ShapeDtypeStruct((t, d), BF16),
                   jax.ShapeDtypeStruct((t, LANES), jnp.int32),
                   jax.ShapeDtypeStruct((t, LANES), F32)],
        scratch_shapes=[pltpu.VMEM((tm + 2 * POOL_HALO, d), F32)],
        compiler_params=_params(("parallel",)),
    )(x, x, x, g1, ada_l, ada_l, ada_l, w_pool, pool_scale, g2, ada_l, ada_l, w_router_pad[0], w_router_pad[1])


MOE_TILE = 1280
MOE_SUB = 256
MOE_SUBS = MOE_TILE // MOE_SUB
GATHER_SUB = 256
GATHER_SUBS = MOE_TILE // GATHER_SUB
SRC_BLK = 256
Y_BLK = 256
CMB_TILE = 256
MOE_TF = 512
GATHER_BUFS = 3


def _moe_ffn_kernel(te_ref, tn_ref, tlo_ref, tnb_ref, trk_ref, cb_ref, posb_ref, hn_hbm, wg_ref, wu_ref, wd_ref,
                    y_ref, xb_ref, acc_ref, hbuf_ref, sem):
    i = pl.program_id(0)
    f = pl.program_id(1)
    nsub = tn_ref[i]

    @pl.when(f == 0)
    def _():
        xb_ref[...] = jnp.zeros_like(xb_ref)
        acc_ref[...] = jnp.zeros_like(acc_ref)
        row_id = lax.broadcasted_iota(jnp.int32, (GATHER_SUB, SRC_BLK), 0) + i * MOE_TILE
        rows_used = nsub * MOE_SUB
        lo = tlo_ref[i]
        n = tnb_ref[i]
        rank0 = trk_ref[i]
        cb_base = te_ref[i] * (posb_ref.shape[0] + 1)

        def blk_copy(b, slot):
            return pltpu.make_async_copy(hn_hbm.at[pl.ds((lo + b) * SRC_BLK, SRC_BLK), :], hbuf_ref.at[slot],
                                         sem.at[slot])

        for j in range(GATHER_BUFS - 1):
            @pl.when(j < n)
            def _():
                blk_copy(j, j).start()

        def body(b, carry):
            slot = b % GATHER_BUFS
            blk_copy(b, slot).wait()
            ahead = b + GATHER_BUFS - 1

            @pl.when(ahead < n)
            def _():
                blk_copy(ahead, ahead % GATHER_BUFS).start()

            blk = lo + b
            pos = posb_ref[blk]
            pos1, pos2 = pos[0:1, :], pos[1:2, :]
            before = cb_ref[cb_base + blk]
            through = cb_ref[cb_base + blk + 1]
            for q in range(GATHER_SUBS):
                @pl.when((q * GATHER_SUB < rows_used) & (before < rank0 + (q + 1) * GATHER_SUB)
                         & (through > rank0 + q * GATHER_SUB))
                def _():
                    rows = slice(q * GATHER_SUB, (q + 1) * GATHER_SUB)
                    rid = row_id + q * GATHER_SUB
                    onehot = jnp.where((pos1 == rid) | (pos2 == rid), 1.0, 0.0).astype(BF16)
                    xb_ref[rows, :] += _dot(onehot, hbuf_ref[slot]).astype(BF16)
            return carry

        lax.fori_loop(0, n, body, 0)

    for k in range(1, MOE_SUBS + 1):
        @pl.when(nsub == k)
        def _():
            rows = slice(0, k * MOE_SUB)
            h = xb_ref[rows, :]
            a = (_silu(_dot(h, wg_ref[0, 0].astype(BF16))) * _dot(h, wu_ref[0, 0].astype(BF16))).astype(BF16)
            acc_ref[rows, :] += _dot(a, wd_ref[0, 0].astype(BF16))

    @pl.when(f == pl.num_programs(1) - 1)
    def _():
        y_ref[...] = acc_ref[...].astype(y_ref.dtype)


def _moe_ffn(hn, posb, tile_expert, tile_subs, tile_blk_lo, tile_blk_n, tile_rank, blk_counts, wg, wu, wd, lyr):
    d = hn.shape[1]
    tm = MOE_TILE
    ff = wg.shape[3]
    tf = MOE_TF
    nf = ff // tf
    nt = tile_expert.shape[0]
    fidx = lambda i, f, tn: jnp.where(tn[i] > 0, f, nf - 1)
    return pl.pallas_call(
        _moe_ffn_kernel,
        grid_spec=pltpu.PrefetchScalarGridSpec(
            num_scalar_prefetch=6,
            grid=(nt, nf),
            in_specs=[pl.BlockSpec(posb.shape, lambda i, f, te, tn, *_: (0, 0, 0)),
                      pl.BlockSpec(memory_space=pl.ANY),
                      pl.BlockSpec((1, 1, d, tf), lambda i, f, te, tn, *_: (lyr, te[i], 0, fidx(i, f, tn))),
                      pl.BlockSpec((1, 1, d, tf), lambda i, f, te, tn, *_: (lyr, te[i], 0, fidx(i, f, tn))),
                      pl.BlockSpec((1, 1, tf, d), lambda i, f, te, tn, *_: (lyr, te[i], fidx(i, f, tn), 0))],
            out_specs=pl.BlockSpec((tm, d), lambda i, f, te, tn, *_: (i, 0)),
            scratch_shapes=[pltpu.VMEM((tm, d), BF16), pltpu.VMEM((tm, d), F32),
                            pltpu.VMEM((GATHER_BUFS, SRC_BLK, d), BF16),
                            pltpu.SemaphoreType.DMA((GATHER_BUFS,))]),
        out_shape=jax.ShapeDtypeStruct((nt * tm, d), BF16),
        compiler_params=_params(("arbitrary", "arbitrary"), MOE_VMEM_LIMIT),
    )(tile_expert, tile_subs, tile_blk_lo, tile_blk_n, tile_rank, blk_counts, posb, hn, wg, wu, wd)


CMB_SLOTS = [(e, j) for e in range(N_EXPERTS) for j in range(2)]


def _combine_kernel(fb_ref, nb_ref, x_ref, pos_ref, gate_ref, gt_ref, y_hbm, o_ref, ybuf_ref, sem, *, row, tile_off):
    i = pl.program_id(0)
    nslot = len(CMB_SLOTS)
    tm, d = x_ref.shape
    half = i % 2

    def blk(t, s):
        e, j = CMB_SLOTS[s]
        return fb_ref[(tile_off + t) * N_EXPERTS + e] + j

    def used(t, s):
        e, j = CMB_SLOTS[s]
        return nb_ref[(tile_off + t) * N_EXPERTS + e] > j

    def blk_copy(t, s, h):
        return pltpu.make_async_copy(y_hbm.at[pl.ds(blk(t, s) * Y_BLK, Y_BLK), :], ybuf_ref.at[h, s], sem.at[h, s])

    def start_tile(t, h):
        for s in range(nslot):
            @pl.when(used(t, s))
            def _():
                blk_copy(t, s, h).start()

    @pl.when(i == 0)
    def _():
        ybuf_ref[...] = jnp.zeros_like(ybuf_ref)
        start_tile(0, 0)

    @pl.when(i + 1 < pl.num_programs(0))
    def _():
        start_tile(i + 1, 1 - half)

    pos1, pos2 = pos_ref[:, 0:1], pos_ref[:, 1:2]
    g1, g2 = gate_ref[:, 0:1], gate_ref[:, 1:2]
    col = lax.broadcasted_iota(jnp.int32, (tm, Y_BLK), 1)
    weights = []
    for s in range(nslot):
        off = blk(i, s) * Y_BLK
        w = jnp.where(pos1 - off == col, g1, 0.0) + jnp.where(pos2 - off == col, g2, 0.0)
        weights.append(jnp.where(used(i, s), w, 0.0).astype(BF16))
    w_all = jnp.concatenate(weights, axis=1)

    for s in range(nslot):
        @pl.when(used(i, s))
        def _():
            blk_copy(i, s, half).wait()

    acc = _dot(w_all, ybuf_ref[half].reshape(nslot * Y_BLK, d))
    o_ref[...] = x_ref[...] + gt_ref[row:row + 1, :] * acc


def _combine(x, y, pos, gates, first_blk, num_blk, tok_off, ada_l, row):
    t, d = x.shape
    tm = CMB_TILE
    tile_off = tok_off // tm
    return pl.pallas_call(
        functools.partial(_combine_kernel, row=row, tile_off=tile_off),
        grid_spec=pltpu.PrefetchScalarGridSpec(
            num_scalar_prefetch=2,
            grid=(t // tm,),
            in_specs=[pl.BlockSpec((tm, d), lambda i, fb, nb: (i, 0)),
                      pl.BlockSpec((tm, 2), lambda i, fb, nb: (tile_off + i, 0)),
                      pl.BlockSpec((tm, LANES), lambda i, fb, nb: (i, 0)),
                      _ada_spec(5),
                      pl.BlockSpec(memory_space=pl.ANY)],
            out_specs=pl.BlockSpec((tm, d), lambda i, fb, nb: (i, 0)),
            scratch_shapes=[pltpu.VMEM((2, len(CMB_SLOTS), Y_BLK, d), BF16),
                            pltpu.SemaphoreType.DMA((2, len(CMB_SLOTS)))]),
        out_shape=jax.ShapeDtypeStruct((t, d), F32),
        compiler_params=_params(("arbitrary",)),
    )(first_blk, num_blk, x, pos, gates, ada_l, y)


def _route(sel):
    tm = MOE_TILE
    n_tok = sel.shape[0]
    n_asg = 2 * n_tok
    nt = -(-n_asg // tm) + N_EXPERTS
    e_flat = sel.reshape(-1)
    onehot = (e_flat[:, None] == jnp.arange(N_EXPERTS, dtype=jnp.int32)[None, :]).astype(jnp.int32)
    csum = jnp.cumsum(onehot, axis=0)
    count = csum[-1]
    rank = jnp.sum((csum - onehot) * onehot, axis=1)
    tiles_e = (count + tm - 1) // tm
    tile_end = jnp.cumsum(tiles_e)
    base = (tile_end - tiles_e) * tm
    pos = jnp.sum(onehot * base[None, :], axis=1) + rank
    pos = jnp.where(e_flat >= 0, pos, -1)
    tile_id = jnp.arange(nt, dtype=jnp.int32)
    used = tile_end[-1]
    owner = lambda i: jnp.minimum(jnp.sum((i[:, None] >= tile_end[None, :]).astype(jnp.int32), axis=1), N_EXPERTS - 1)
    tile_expert = owner(jnp.minimum(tile_id, used - 1))
    rows_in_tile = jnp.clip((base + count)[tile_expert] - tile_id * tm, 0, tm)
    tile_subs = jnp.where(tile_id < used, (rows_in_tile + MOE_SUB - 1) // MOE_SUB, 0)
    n_blk = n_tok // SRC_BLK
    blk_counts = jnp.concatenate([jnp.zeros((1, N_EXPERTS), jnp.int32),
                                  csum.reshape(n_blk, 2 * SRC_BLK, N_EXPERTS)[:, -1, :]], axis=0).T
    tile_rank = tile_id * tm - base[tile_expert]
    counts_t = blk_counts[tile_expert]
    tile_blk_lo = jnp.sum((counts_t[:, 1:] <= tile_rank[:, None]).astype(jnp.int32), axis=1)
    tile_blk_hi = jnp.sum((counts_t[:, :-1] < (tile_rank + rows_in_tile)[:, None]).astype(jnp.int32), axis=1)
    tile_blk_n = jnp.where(tile_subs > 0, jnp.maximum(tile_blk_hi - tile_blk_lo, 0), 0)
    tile_blk_lo = jnp.minimum(tile_blk_lo, n_blk - 1)
    posb = jnp.pad(pos.reshape(n_blk, SRC_BLK, 2).transpose(0, 2, 1), ((0, 0), (0, 6), (0, 0)), constant_values=-1)
    ntt = n_tok // CMB_TILE
    through = csum.reshape(ntt, 2 * CMB_TILE, N_EXPERTS)[:, -1, :]
    before = jnp.concatenate([jnp.zeros((1, N_EXPERTS), jnp.int32), through[:-1]], axis=0)
    lo = base[None, :] + before
    hi = base[None, :] + through - 1
    first_blk = lo // Y_BLK
    num_blk = jnp.where(through > before, hi // Y_BLK - first_blk + 1, 0)
    return (posb, pos.reshape(n_tok, 2), tile_expert, tile_subs, tile_blk_lo, tile_blk_n, tile_rank,
            blk_counts.reshape(-1), first_blk.reshape(-1), num_blk.reshape(-1))


def _moe(streams, ada_l, wg, wu, wd, lyr):
    hs = [s[2] for s in streams]
    sels = [s[3][:, :2] for s in streams]
    n_tok = sum(h.shape[0] for h in hs)
    pad = -n_tok % SRC_BLK
    if pad:
        hs.append(jnp.zeros((pad, D_MODEL), BF16))
        sels.append(jnp.full((pad, 2), -1, jnp.int32))
    h_all = hs[0] if len(hs) == 1 else jnp.concatenate(hs, axis=0)
    sel_all = sels[0] if len(sels) == 1 else jnp.concatenate(sels, axis=0)
    (posb, pos, tile_expert, tile_subs, tile_blk_lo, tile_blk_n, tile_rank, blk_counts,
     first_blk, num_blk) = _route(sel_all)
    y = _moe_ffn(h_all, posb, tile_expert, tile_subs, tile_blk_lo, tile_blk_n, tile_rank, blk_counts,
                 wg, wu, wd, lyr)
    outs, off = [], 0
    for x, row, _, _, gt in streams:
        outs.append(_combine(x, y, pos, gt, first_blk, num_blk, off, ada_l, row))
        off += x.shape[0]
    return outs


def _even_layer(x, ctx, ada_l, g1, g2, w_in, g_qn, g_kn, sink, w_gate_up, b_gate_up, g_gla, w_out,
                w_ffn_gate, w_ffn_up, w_ffn_down, lyr, rope_tabs, ctx_out):
    d = D_MODEL
    w_main = jnp.concatenate([w_in[:, 2080:3104], w_in[:, 1024:2048], w_in[:, 3616:4640], w_in[:, 512:1024],
                              w_in[:, 3104:3616], w_in[:, 0:256], w_in[:, 256:512]], axis=1).astype(BF16)
    w_gate = jnp.pad(w_in[:, 2048:2080], ((0, 0), (0, LANES - 2 * GATE_RANK))).astype(BF16)
    nk = B_HEADS * B_DK
    w_up = jnp.concatenate([jnp.pad(w_gate_up[0], ((0, 0), (0, nk))), jnp.pad(w_gate_up[1], ((0, 0), (nk, 0))),
                            jnp.zeros((LANES - 2 * GATE_RANK, 2 * nk), F32)], axis=0)
    w_up_hi = w_up.astype(BF16)
    w_up = (w_up_hi, (w_up - w_up_hi.astype(F32)).astype(BF16))
    b_up = b_gate_up.reshape(1, 2 * nk)
    w_out_b = w_out.astype(BF16)
    gq, gk, gg = g_qn.reshape(1, -1), g_kn.reshape(1, -1), g_gla.reshape(1, -1)
    sink_tab = jnp.broadcast_to(sink[:, None], (A_HEADS, LANES))
    cos, sin = rope_tabs

    pc, la_c = _inproj(ctx, ada_l, 1, g1, w_main, w_gate, w_up, b_up)
    qc, kc = _qkprep(pc, cos, sin, gq, gk, rope=False)
    s0 = jnp.zeros((B_HEADS, B_DV, B_DK), F32)
    oc_f, s_fwd = _gla_scan(pc, la_c, s0, rev=False)
    oc_b, s_bwd = _gla_scan(pc, la_c, s0, rev=True)

    px, la_x = _inproj(x, ada_l, 0, g1, w_main, w_gate, w_up, b_up)
    qx, kx = _qkprep(px, cos, sin, gq, gk, rope=True)
    oa = _attention(qx, kx, px, kc, pc, sink_tab, local=True)
    ox_f, _ = _gla_scan(px, la_x, s_fwd, rev=False)
    ox_b, _ = _gla_scan(px, la_x, s_bwd, rev=True)
    x = _outproj(oa, ox_f, ox_b, px, gg, w_out_b, x, ada_l, 0)
    x = _ffn(x, ada_l, 0, g2, w_ffn_gate, w_ffn_up, w_ffn_down, lyr)
    if ctx_out:
        oa_c = _attention(qc, None, None, kc, pc, sink_tab, local=False)
        ctx = _outproj(oa_c, oc_f, oc_b, pc, gg, w_out_b, ctx, ada_l, 1)
        ctx = _ffn(ctx, ada_l, 1, g2, w_ffn_gate, w_ffn_up, w_ffn_down, lyr)
    return x, ctx


def _odd_layer(x, ctx, ada_l, g1, g2, w_pool, pool_scale, w_router, w_exp_gate, w_exp_up, w_exp_down, lyr, ctx_out):
    w_pool_b = w_pool.astype(BF16)
    ps = pool_scale.reshape(1, -1)
    wr = jnp.pad(w_router, ((0, 0), (0, LANES - N_EXPERTS)))
    wr_hi = wr.astype(BF16)
    wr = (wr_hi, (wr - wr_hi.astype(F32)).astype(BF16))
    streams = [(0,) + tuple(_pool_router(x, ada_l, 0, g1, g2, w_pool_b, ps, wr))]
    if ctx_out:
        streams.append((1,) + tuple(_pool_router(ctx, ada_l, 1, g1, g2, w_pool_b, ps, wr)))
    outs = _moe([(s[1], s[0], s[2], s[3], s[4]) for s in streams], ada_l, w_exp_gate, w_exp_up, w_exp_down, lyr)
    return outs[0], (outs[1] if ctx_out else ctx)


def kernel(x, c, ctx, c_ctx, w_ada, b_ada, norm_g, w_in, g_qn, g_kn, attn_sink, w_gate_up, b_gate_up, g_gla, w_out,
           w_ffn_gate, w_ffn_up, w_ffn_down, w_pool, pool_scale, w_router, w_exp_gate, w_exp_up, w_exp_down):
    depth = w_ada.shape[0]
    xs = x[0]
    cs = ctx[0]
    t = xs.shape[0]
    cond = jnp.concatenate([c, c_ctx[None, :], jnp.zeros((6, D_MODEL), F32)], axis=0)
    ada = _ada_all(cond, w_ada, b_ada)
    rope_tabs = _rope_tables(t)
    for l in range(depth):
        ctx_later = any(j % 2 == 0 for j in range(l + 1, depth))
        g1 = norm_g[l, 0].reshape(1, -1)
        g2 = norm_g[l, 1].reshape(1, -1)
        if l % 2 == 0:
            e = l // 2
            xs, cs = _even_layer(xs, cs, ada[l], g1, g2, w_in[e], g_qn[e], g_kn[e], attn_sink[e], w_gate_up[e],
                                 b_gate_up[e], g_gla[e], w_out[e], w_ffn_gate, w_ffn_up, w_ffn_down, e,
                                 rope_tabs, ctx_later)
        else:
            o = l // 2
            xs, cs = _odd_layer(xs, cs, ada[l], g1, g2, w_pool[o], pool_scale[o], w_router[o],
                                w_exp_gate, w_exp_up, w_exp_down, o, ctx_later)
    return xs[None]
```

```python
import functools

import jax
import jax.numpy as jnp
import numpy as np
from jax import lax
from jax.experimental import pallas as pl
from jax.experimental.pallas import tpu as pltpu

F32 = jnp.float32
BF16 = jnp.bfloat16

D_MODEL = 2048
GRID_W = 64
A_HEADS = 8
A_KV_HEADS = 2
A_GROUP = 4
A_HEAD_DIM = 128
A_WIDTH = A_HEADS * A_HEAD_DIM
ATTN_BLOCK = 128
ROPE_BASE = 10000.0
B_HEADS = 4
B_DV = 256
B_DK = 128
B_WIDTH = B_HEADS * B_DV
GATE_RANK = 16
GATE_TAU = 16.0
GLA_CHUNK = 64
GLA_SUB = 256
POOL_GROUPS = 4
POOL_WINDOWS = (2, 4, 8, 16)
POOL_HALO = 8
N_EXPERTS = 8
NORM_EPS = 1e-6
NEG_INF = -1e30
LANES = 128

P_COLS = 4608
VMEM_LIMIT = 56 * 1024 * 1024
MOE_VMEM_LIMIT = 62 * 1024 * 1024


def _params(sem, vmem=VMEM_LIMIT):
    return pltpu.CompilerParams(dimension_semantics=sem, vmem_limit_bytes=vmem)


def _dot(a, b, precision=None):
    return jnp.dot(a, b, preferred_element_type=F32, precision=precision)


def _dot_nt(a, b):
    return lax.dot_general(a, b, (((1,), (1,)), ((), ())), preferred_element_type=F32)


def _dot_tn(a, b):
    return lax.dot_general(a, b, (((0,), (0,)), ((), ())), preferred_element_type=F32)


def _silu(x):
    return x * jax.nn.sigmoid(x)


def _norm_mod(x, g, shift, scale):
    ms = jnp.mean(x * x, axis=-1, keepdims=True)
    return (x * lax.rsqrt(ms + NORM_EPS) * g) * (1.0 + scale) + shift


def _ada_kernel(cond_ref, w_ref, b_ref, o_ref):
    s = _silu(cond_ref[...])
    s_hi = s.astype(BF16)
    s_lo = (s - s_hi.astype(F32)).astype(BF16)
    w = w_ref[0].astype(BF16)
    o_ref[0] = _dot(s_hi, w) + _dot(s_lo, w) + b_ref[0]


def _ada_all(cond, w_ada, b_ada):
    depth, d, n = w_ada.shape
    tn = 1024
    return pl.pallas_call(
        _ada_kernel,
        grid=(depth, n // tn),
        in_specs=[pl.BlockSpec((8, d), lambda l, j: (0, 0)),
                  pl.BlockSpec((1, d, tn), lambda l, j: (l, 0, j)),
                  pl.BlockSpec((1, 1, tn), lambda l, j: (l, 0, j))],
        out_specs=pl.BlockSpec((1, 8, tn), lambda l, j: (l, 0, j)),
        out_shape=jax.ShapeDtypeStruct((depth, 8, n), F32),
        compiler_params=_params(("parallel", "parallel")),
    )(cond, w_ada, b_ada.reshape(depth, 1, n))


def _ada_spec(k):
    return pl.BlockSpec((8, D_MODEL), lambda *_: (0, k))


def _inproj_kernel(x_ref, g_ref, sh_ref, sc_ref, w_ref, wg_ref, wuh_ref, wul_ref, bup_ref, p_ref, la_ref, hn_ref,
                   *, row):
    @pl.when(pl.program_id(1) == 0)
    def _():
        h = _norm_mod(x_ref[...], g_ref[...], sh_ref[row:row + 1, :], sc_ref[row:row + 1, :]).astype(BF16)
        hn_ref[...] = h
        lr = _dot(h, wg_ref[...])
        lr_hi = lr.astype(BF16)
        lr_lo = (lr - lr_hi.astype(F32)).astype(BF16)
        z = _dot(lr_hi, wuh_ref[...]) + _dot(lr_lo, wuh_ref[...]) + _dot(lr_hi, wul_ref[...]) + bup_ref[...]
        la_ref[...] = (jnp.minimum(z, 0.0) - jnp.log1p(jnp.exp(-jnp.abs(z)))) * (1.0 / GATE_TAU)

    p_ref[...] = _dot(hn_ref[...], w_ref[...]).astype(p_ref.dtype)


def _inproj(x, ada_l, row, g, w_main, w_gate, w_up, b_up):
    t, d = x.shape
    tm = min(512, t)
    tn = 1536
    nla = 2 * B_HEADS * B_DK
    return pl.pallas_call(
        functools.partial(_inproj_kernel, row=row),
        grid=(t // tm, P_COLS // tn),
        in_specs=[pl.BlockSpec((tm, d), lambda i, j: (i, 0)),
                  pl.BlockSpec((1, d), lambda i, j: (0, 0)),
                  _ada_spec(0), _ada_spec(1),
                  pl.BlockSpec((d, tn), lambda i, j: (0, j)),
                  pl.BlockSpec((d, LANES), lambda i, j: (0, 0)),
                  pl.BlockSpec((LANES, nla), lambda i, j: (0, 0)),
                  pl.BlockSpec((LANES, nla), lambda i, j: (0, 0)),
                  pl.BlockSpec((1, nla), lambda i, j: (0, 0))],
        out_specs=[pl.BlockSpec((tm, tn), lambda i, j: (i, j)),
                   pl.BlockSpec((tm, nla), lambda i, j: (i, 0))],
        out_shape=[jax.ShapeDtypeStruct((t, P_COLS), BF16),
                   jax.ShapeDtypeStruct((t, nla), F32)],
        scratch_shapes=[pltpu.VMEM((tm, d), BF16)],
        compiler_params=_params(("parallel", "arbitrary")),
    )(x, g, ada_l, ada_l, w_main, w_gate, w_up[0], w_up[1], b_up)


def _qkprep_kernel(q_ref, k_ref, cos_ref, sin_ref, gq_ref, gk_ref, qo_ref, ko_ref, *, rope):
    tm = q_ref.shape[0]
    lane = lax.broadcasted_iota(jnp.int32, (tm, A_HEAD_DIM), 1)
    first_half = (lane % 64) < 32

    def prep(xh, g, scale):
        ms = jnp.mean(xh * xh, axis=-1, keepdims=True)
        y = xh * lax.rsqrt(ms + NORM_EPS) * g
        if rope:
            partner = jnp.where(first_half, pltpu.roll(y, 96, 1), pltpu.roll(y, 32, 1))
            y = y * cos_ref[...] + partner * sin_ref[...]
        return (y * scale).astype(BF16)

    for h in range(A_HEADS):
        cols = slice(h * A_HEAD_DIM, (h + 1) * A_HEAD_DIM)
        qo_ref[:, cols] = prep(q_ref[:, cols].astype(F32), gq_ref[...], A_HEAD_DIM ** -0.5)
    for h in range(A_KV_HEADS):
        cols = slice(h * A_HEAD_DIM, (h + 1) * A_HEAD_DIM)
        ko_ref[:, cols] = prep(k_ref[:, cols].astype(F32), gk_ref[...], 1.0)


def _qkprep(p, cos, sin, g_qn, g_kn, rope):
    t = p.shape[0]
    tm = min(512, t)
    kw = A_KV_HEADS * A_HEAD_DIM
    return pl.pallas_call(
        functools.partial(_qkprep_kernel, rope=rope),
        grid=(t // tm,),
        in_specs=[pl.BlockSpec((tm, A_WIDTH), lambda i: (i, 0)),
                  pl.BlockSpec((tm, kw), lambda i: (i, 4096 // kw)),
                  pl.BlockSpec((tm, A_HEAD_DIM), lambda i: (i, 0)),
                  pl.BlockSpec((tm, A_HEAD_DIM), lambda i: (i, 0)),
                  pl.BlockSpec((1, A_HEAD_DIM), lambda i: (0, 0)),
                  pl.BlockSpec((1, A_HEAD_DIM), lambda i: (0, 0))],
        out_specs=[pl.BlockSpec((tm, A_WIDTH), lambda i: (i, 0)),
                   pl.BlockSpec((tm, kw), lambda i: (i, 0))],
        out_shape=[jax.ShapeDtypeStruct((t, A_WIDTH), BF16),
                   jax.ShapeDtypeStruct((t, kw), BF16)],
        compiler_params=_params(("parallel",)),
    )(p, p, cos, sin, g_qn, g_kn)


def _rope_tables(t):
    half = A_HEAD_DIM // 4
    freqs = ROPE_BASE ** (-np.arange(half, dtype=np.float64) / half)
    tok = np.arange(t)
    row = (tok // GRID_W).astype(np.float64)[:, None] * freqs
    col = (tok % GRID_W).astype(np.float64)[:, None] * freqs
    cos = np.concatenate([np.cos(row), np.cos(row), np.cos(col), np.cos(col)], axis=-1)
    sin = np.concatenate([-np.sin(row), np.sin(row), -np.sin(col), np.sin(col)], axis=-1)
    return jnp.asarray(cos.astype(np.float32)), jnp.asarray(sin.astype(np.float32))


def _softmax_pv(scores, values, sink_col):
    def lane_blocks(xs):
        return [x[:, i * LANES:(i + 1) * LANES] for x in xs for i in range(x.shape[1] // LANES)]

    m = jnp.maximum(jnp.max(functools.reduce(jnp.maximum, lane_blocks(scores)), axis=-1, keepdims=True), sink_col)
    probs = [jnp.exp(s - m) for s in scores]
    denom = jnp.exp(sink_col - m) + jnp.sum(functools.reduce(jnp.add, lane_blocks(probs)), axis=-1, keepdims=True)
    acc = None
    for p, v in zip(probs, values):
        pv = _dot(p.astype(BF16), v)
        acc = pv if acc is None else acc + pv
    return acc / denom


def _attn_kernel(*refs, local):
    if local:
        q_ref, kp_ref, ko_ref, kn_ref, vp_ref, vo_ref, vn_ref, kc_ref, vc_ref, sink_ref, o_ref = refs
    else:
        q_ref, kc_ref, vc_ref, sink_ref, o_ref = refs
    n = pl.program_id(0)
    nb = pl.num_programs(0)
    rows = A_GROUP * ATTN_BLOCK
    if local:
        qi = lax.broadcasted_iota(jnp.int32, (rows, ATTN_BLOCK), 0) % ATTN_BLOCK
        kj = lax.broadcasted_iota(jnp.int32, (rows, ATTN_BLOCK), 1)
        mask_prev = (kj >= qi) & (n > 0)
        mask_next = (kj <= qi) & (n < nb - 1)
    for kv in range(A_KV_HEADS):
        kc = slice(kv * A_HEAD_DIM, (kv + 1) * A_HEAD_DIM)
        heads = [kv * A_GROUP + g for g in range(A_GROUP)]
        q4 = jnp.concatenate([q_ref[:, h * A_HEAD_DIM:(h + 1) * A_HEAD_DIM] for h in heads], axis=0)
        sink_col = jnp.concatenate(
            [jnp.broadcast_to(sink_ref[h:h + 1, 0:1], (ATTN_BLOCK, 1)) for h in heads], axis=0)
        scores, values = [], []
        if local:
            scores.append(jnp.where(mask_prev, _dot_nt(q4, kp_ref[:, kc]), NEG_INF))
            scores.append(_dot_nt(q4, ko_ref[:, kc]))
            scores.append(jnp.where(mask_next, _dot_nt(q4, kn_ref[:, kc]), NEG_INF))
            values += [vp_ref[:, kc], vo_ref[:, kc], vn_ref[:, kc]]
        scores.append(_dot_nt(q4, kc_ref[:, kc]))
        values.append(vc_ref[:, kc])
        o = _softmax_pv(scores, values, sink_col).astype(o_ref.dtype)
        for g, h in enumerate(heads):
            o_ref[:, h * A_HEAD_DIM:(h + 1) * A_HEAD_DIM] = o[g * ATTN_BLOCK:(g + 1) * ATTN_BLOCK, :]


def _attention(q, k, p, k_ctx, p_ctx, sink_tab, local):
    t = q.shape[0]
    nb = t // ATTN_BLOCK
    l = k_ctx.shape[0]
    kw = A_KV_HEADS * A_HEAD_DIM
    vcol = 4352 // kw
    blk = lambda f: pl.BlockSpec((ATTN_BLOCK, kw), f)
    blkv = lambda f: pl.BlockSpec((ATTN_BLOCK, kw), lambda i: (f(i)[0], vcol))
    prev = lambda i: (jnp.maximum(i - 1, 0), 0)
    own = lambda i: (i, 0)
    nxt = lambda i: (jnp.minimum(i + 1, nb - 1), 0)
    in_specs = [pl.BlockSpec((ATTN_BLOCK, A_WIDTH), own)]
    args = [q]
    if local:
        in_specs += [blk(prev), blk(own), blk(nxt), blkv(prev), blkv(own), blkv(nxt)]
        args += [k, k, k, p, p, p]
    in_specs += [pl.BlockSpec((l, kw), lambda i: (0, 0)), pl.BlockSpec((l, kw), lambda i: (0, vcol)),
                 pl.BlockSpec((A_HEADS, LANES), lambda i: (0, 0))]
    args += [k_ctx, p_ctx, sink_tab]
    return pl.pallas_call(
        functools.partial(_attn_kernel, local=local),
        grid=(nb,),
        in_specs=in_specs,
        out_specs=pl.BlockSpec((ATTN_BLOCK, A_WIDTH), own),
        out_shape=jax.ShapeDtypeStruct((t, A_WIDTH), BF16),
        compiler_params=_params(("parallel",)),
    )(*args)


def _gla_kernel(q_ref, k_ref, v_ref, la_ref, s0_ref, o_ref, sf_ref, st_ref, *, rev, nchunk):
    @pl.when(pl.program_id(0) == 0)
    def _():
        st_ref[...] = s0_ref[...]

    r = nchunk * GLA_CHUNK
    sub = min(r, GLA_SUB)
    ii = lax.broadcasted_iota(jnp.int32, (sub, sub), 0)
    jj = lax.broadcasted_iota(jnp.int32, (sub, sub), 1)
    same_chunk = (ii // GLA_CHUNK) == (jj // GLA_CHUNK)
    tri = same_chunk & ((jj >= ii) if rev else (jj <= ii))
    b_all = la_ref[...]
    pos = lax.broadcasted_iota(jnp.int32, b_all.shape, 0) % GLA_CHUNK
    step = 1
    while step < GLA_CHUNK:
        if rev:
            b_all = b_all + jnp.where(pos < GLA_CHUNK - step, pltpu.roll(b_all, r - step, 0), 0.0)
        else:
            b_all = b_all + jnp.where(pos >= step, pltpu.roll(b_all, step, 0), 0.0)
        step *= 2
    heads = []
    for h in range(B_HEADS):
        kc = slice(h * B_DK, (h + 1) * B_DK)
        b = b_all[:, kc]
        b3 = b.reshape(nchunk, GLA_CHUNK, B_DK)
        b_end = b3[:, 0:1, :] if rev else b3[:, GLA_CHUNK - 1:GLA_CHUNK, :]
        k = k_ref[:, kc].astype(F32)
        qe = (q_ref[:, kc].astype(F32) * (B_DK ** -0.5) * jnp.exp(b)).astype(BF16)
        ke = (k * jnp.exp(-b)).astype(BF16)
        kd = (k.reshape(nchunk, GLA_CHUNK, B_DK) * jnp.exp(b_end - b3)).astype(BF16)
        v = v_ref[:, h * B_DV:(h + 1) * B_DV].astype(BF16)
        o_intra = []
        for s0 in range(0, r, sub):
            rs = slice(s0, s0 + sub)
            a = jnp.where(tri, _dot_nt(qe[rs, :], ke[rs, :]), 0.0).astype(BF16)
            o_intra.append(_dot(a, v[rs, :]))
        heads.append((qe, kd, v, jnp.concatenate(o_intra, axis=0), jnp.exp(b_end)))
    st = [st_ref[h] for h in range(B_HEADS)]
    for c in (range(nchunk - 1, -1, -1) if rev else range(nchunk)):
        rows = slice(c * GLA_CHUNK, (c + 1) * GLA_CHUNK)
        for h, (qe, kd, v, o_intra, decay) in enumerate(heads):
            o = o_intra[rows, :] + _dot_nt(qe[rows, :], st[h].astype(BF16))
            o_ref[0, rows, h * B_DV:(h + 1) * B_DV] = o.astype(o_ref.dtype)
            st[h] = st[h] * decay[c] + _dot_tn(v[rows, :], kd[c])
    for h in range(B_HEADS):
        st_ref[h] = st[h]
        sf_ref[h] = st[h]


def _gla_scan(p, la, s0, rev):
    t = p.shape[0]
    r = min(512, t)
    nblk = t // r
    rb = (lambda c: nblk - 1 - c) if rev else (lambda c: c)
    nk = B_HEADS * B_DK
    o, sf = pl.pallas_call(
        functools.partial(_gla_kernel, rev=rev, nchunk=r // GLA_CHUNK),
        grid=(nblk,),
        in_specs=[pl.BlockSpec((r, nk), lambda c: (rb(c), 3584 // nk)),
                  pl.BlockSpec((r, nk), lambda c: (rb(c), 3072 // nk)),
                  pl.BlockSpec((r, B_WIDTH), lambda c: (rb(c), 1024 // B_WIDTH)),
                  pl.BlockSpec((r, nk), lambda c: (rb(c), 1 if rev else 0)),
                  pl.BlockSpec((B_HEADS, B_DV, B_DK), lambda c: (0, 0, 0))],
        out_specs=[pl.BlockSpec((1, r, B_WIDTH), lambda c: (0, rb(c), 0)),
                   pl.BlockSpec((B_HEADS, B_DV, B_DK), lambda c: (0, 0, 0))],
        out_shape=[jax.ShapeDtypeStruct((1, t, B_WIDTH), BF16),
                   jax.ShapeDtypeStruct((B_HEADS, B_DV, B_DK), F32)],
        scratch_shapes=[pltpu.VMEM((B_HEADS, B_DV, B_DK), F32)],
        compiler_params=_params(("arbitrary",)),
    )(p, p, p, la, s0)
    return o[0], sf


def _outproj_kernel(oa_ref, of_ref, ob_ref, og_ref, gg_ref, w_ref, x_ref, gt_ref, o_ref, mix_ref, *, row):
    @pl.when(pl.program_id(1) == 0)
    def _():
        mix_ref[:, 0:A_WIDTH] = oa_ref[...]
        for h in range(B_HEADS):
            cols = slice(h * B_DV, (h + 1) * B_DV)
            o = of_ref[:, cols].astype(F32) + ob_ref[:, cols].astype(F32)
            ms = jnp.mean(o * o, axis=-1, keepdims=True)
            y = o * lax.rsqrt(ms + NORM_EPS) * gg_ref[...]
            mix_ref[:, A_WIDTH + h * B_DV:A_WIDTH + (h + 1) * B_DV] = (y * _silu(og_ref[:, cols].astype(F32))).astype(BF16)

    o_ref[...] = x_ref[...] + gt_ref[row:row + 1, :] * _dot(mix_ref[...], w_ref[...])


def _outproj(oa, o_f, o_b, p, g_gla, w_out, x, ada_l, row):
    t, d = x.shape
    tm = min(512, t)
    tn = d
    return pl.pallas_call(
        functools.partial(_outproj_kernel, row=row),
        grid=(t // tm, d // tn),
        in_specs=[pl.BlockSpec((tm, A_WIDTH), lambda i, j: (i, 0)),
                  pl.BlockSpec((tm, B_WIDTH), lambda i, j: (i, 0)),
                  pl.BlockSpec((tm, B_WIDTH), lambda i, j: (i, 0)),
                  pl.BlockSpec((tm, B_WIDTH), lambda i, j: (i, 2048 // B_WIDTH)),
                  pl.BlockSpec((1, B_DV), lambda i, j: (0, 0)),
                  pl.BlockSpec((A_WIDTH + B_WIDTH, tn), lambda i, j: (0, j)),
                  pl.BlockSpec((tm, tn), lambda i, j: (i, j)),
                  pl.BlockSpec((8, tn), lambda i, j: (0, 2 * (d // tn) + j))],
        out_specs=pl.BlockSpec((tm, tn), lambda i, j: (i, j)),
        out_shape=jax.ShapeDtypeStruct((t, d), F32),
        scratch_shapes=[pltpu.VMEM((tm, A_WIDTH + B_WIDTH), BF16)],
        compiler_params=_params(("parallel", "arbitrary")),
    )(oa, o_f, o_b, p, g_gla, w_out, x, ada_l)


def _ffn_kernel(x_ref, g_ref, sh_ref, sc_ref, gt_ref, wg_ref, wu_ref, wd_ref, o_ref, hn_ref, *, row):
    f = pl.program_id(1)

    @pl.when(f == 0)
    def _():
        hn_ref[...] = _norm_mod(x_ref[...], g_ref[...], sh_ref[row:row + 1, :], sc_ref[row:row + 1, :]).astype(BF16)
        o_ref[...] = jnp.zeros_like(o_ref)

    h = hn_ref[...]
    a = (_silu(_dot(h, wg_ref[0].astype(BF16))) * _dot(h, wu_ref[0].astype(BF16))).astype(BF16)
    o_ref[...] += _dot(a, wd_ref[0].astype(BF16))

    @pl.when(f == pl.num_programs(1) - 1)
    def _():
        o_ref[...] = x_ref[...] + gt_ref[row:row + 1, :] * o_ref[...]


def _ffn(x, ada_l, row, g, wg, wu, wd, lyr):
    t, d = x.shape
    ff = wg.shape[2]
    tm = min(1024, t)
    tf = 256
    return pl.pallas_call(
        functools.partial(_ffn_kernel, row=row),
        grid=(t // tm, ff // tf),
        in_specs=[pl.BlockSpec((tm, d), lambda i, f: (i, 0)),
                  pl.BlockSpec((1, d), lambda i, f: (0, 0)),
                  _ada_spec(3), _ada_spec(4), _ada_spec(5),
                  pl.BlockSpec((1, d, tf), lambda i, f: (lyr, 0, f)),
                  pl.BlockSpec((1, d, tf), lambda i, f: (lyr, 0, f)),
                  pl.BlockSpec((1, tf, d), lambda i, f: (lyr, f, 0))],
        out_specs=pl.BlockSpec((tm, d), lambda i, f: (i, 0)),
        out_shape=jax.ShapeDtypeStruct((t, d), F32),
        scratch_shapes=[pltpu.VMEM((tm, d), BF16)],
        compiler_params=_params(("parallel", "arbitrary"), MOE_VMEM_LIMIT),
    )(x, g, ada_l, ada_l, ada_l, wg, wu, wd)


def _pool_router_kernel(x_ref, xp_ref, xn_ref, g_ref, sh_ref, sc_ref, gt_ref, w_ref, ps_ref,
                        g2_ref, sh2_ref, sc2_ref, wrh_ref, wrl_ref, o_ref, hn_ref, sel_ref, gate_ref, hb_ref,
                        *, row, t):
    i = pl.program_id(0)
    tm = x_ref.shape[0]
    g = g_ref[...]
    sh = sh_ref[row:row + 1, :]
    sc = sc_ref[row:row + 1, :]
    hb_ref[0:POOL_HALO, :] = jnp.where(i > 0, _norm_mod(xp_ref[...], g, sh, sc), 0.0)
    hb_ref[POOL_HALO:POOL_HALO + tm, :] = _norm_mod(x_ref[...], g, sh, sc)
    hb_ref[POOL_HALO + tm:, :] = jnp.where(i < pl.num_programs(0) - 1, _norm_mod(xn_ref[...], g, sh, sc), 0.0)
    tpos = i * tm + lax.broadcasted_iota(jnp.int32, (tm, 1), 0)
    gc = D_MODEL // POOL_GROUPS
    for grp, w in enumerate(POOL_WINDOWS):
        cols = slice(grp * gc, (grp + 1) * gc)
        acc = None
        for dlt in range(-(w // 2), w - w // 2):
            piece = hb_ref[POOL_HALO + dlt:POOL_HALO + dlt + tm, cols]
            acc = piece if acc is None else acc + piece
        cnt = jnp.minimum(tpos + (w - w // 2), t) - jnp.maximum(tpos - w // 2, 0)
        pooled = acc / cnt.astype(F32) - hb_ref[POOL_HALO:POOL_HALO + tm, cols]
        y = _dot(pooled.astype(BF16), w_ref[grp])
        o_ref[:, cols] = x_ref[:, cols] + gt_ref[row:row + 1, cols] * (y * ps_ref[:, cols])

    h = _norm_mod(o_ref[...], g2_ref[...], sh2_ref[row:row + 1, :], sc2_ref[row:row + 1, :])
    hn_ref[...] = h.astype(hn_ref.dtype)
    lane = lax.broadcasted_iota(jnp.int32, (h.shape[0], LANES), 1)
    lane_f = lane.astype(F32)
    h_hi = h.astype(BF16)
    h_lo = (h - h_hi.astype(F32)).astype(BF16)
    logits = _dot(h_hi, wrh_ref[...]) + _dot(h_lo, wrh_ref[...]) + _dot(h_hi, wrl_ref[...])
    logits = jnp.where(lane < N_EXPERTS, logits, -jnp.inf)
    m1 = jnp.max(logits, axis=-1, keepdims=True)
    i1 = jnp.min(jnp.where(logits == m1, lane_f, float(LANES)), axis=-1, keepdims=True)
    rest = jnp.where(lane_f == i1, -jnp.inf, logits)
    m2 = jnp.max(rest, axis=-1, keepdims=True)
    i2 = jnp.min(jnp.where(rest == m2, lane_f, float(LANES)), axis=-1, keepdims=True)
    e2 = jnp.exp(m2 - m1)
    den = 1.0 + e2
    sel_ref[...] = jnp.where(lane == 0, i1, jnp.where(lane == 1, i2, 0.0)).astype(jnp.int32)
    gate_ref[...] = jnp.where(lane == 0, 1.0 / den, jnp.where(lane == 1, e2 / den, 0.0))


def _pool_router(x, ada_l, row, g1, g2, w_pool, pool_scale, w_router_pad):
    t, d = x.shape
    tm = min(256, t)
    nh = t // POOL_HALO
    gc = d // POOL_GROUPS
    rows = lambda i: (i, 0)
    const = lambda i: (0, 0)
    return pl.pallas_call(
        functools.partial(_pool_router_kernel, row=row, t=t),
        grid=(t // tm,),
        in_specs=[pl.BlockSpec((tm, d), rows),
                  pl.BlockSpec((POOL_HALO, d), lambda i: (jnp.maximum(i * (tm // POOL_HALO) - 1, 0), 0)),
                  pl.BlockSpec((POOL_HALO, d), lambda i: (jnp.minimum((i + 1) * (tm // POOL_HALO), nh - 1), 0)),
                  pl.BlockSpec((1, d), const),
                  _ada_spec(0), _ada_spec(1), _ada_spec(2),
                  pl.BlockSpec((POOL_GROUPS, gc, gc), lambda i: (0, 0, 0)),
                  pl.BlockSpec((1, d), const),
                  pl.BlockSpec((1, d), const),
                  _ada_spec(3), _ada_spec(4),
                  pl.BlockSpec((d, LANES), const), pl.BlockSpec((d, LANES), const)],
        out_specs=[pl.BlockSpec((tm, d), rows), pl.BlockSpec((tm, d), rows),
                   pl.BlockSpec((tm, LANES), rows), pl.BlockSpec((tm, LANES), rows)],
        out_shape=[jax.ShapeDtypeStruct((t, d), F32),
                   jax.ShapeDtypeStruct((t, d), BF16),
                   jax.ShapeDtypeStruct((t, LANES), jnp.int32),
                   jax.ShapeDtypeStruct((t, LANES), F32)],
        scratch_shapes=[pltpu.VMEM((tm + 2 * POOL_HALO, d), F32)],
        compiler_params=_params(("parallel",)),
    )(x, x, x, g1, ada_l, ada_l, ada_l, w_pool, pool_scale, g2, ada_l, ada_l, w_router_pad[0], w_router_pad[1])


MOE_TILE = 1280
MOE_SUB = 256
MOE_SUBS = MOE_TILE // MOE_SUB
GATHER_SUB = 256
GATHER_SUBS = MOE_TILE // GATHER_SUB
SRC_BLK = 256
Y_BLK = 256
CMB_TILE = 256
MOE_TF = 512
GATHER_BUFS = 3


def _moe_ffn_kernel(te_ref, tn_ref, tlo_ref, tnb_ref, trk_ref, cb_ref, posb_ref, hn_hbm, wg_ref, wu_ref, wd_ref,
                    y_ref, xb_ref, acc_ref, hbuf_ref, sem):
    i = pl.program_id(0)
    f = pl.program_id(1)
    nsub = tn_ref[i]

    @pl.when(f == 0)
    def _():
        xb_ref[...] = jnp.zeros_like(xb_ref)
        acc_ref[...] = jnp.zeros_like(acc_ref)
        row_id = lax.broadcasted_iota(jnp.int32, (GATHER_SUB, SRC_BLK), 0) + i * MOE_TILE
        rows_used = nsub * MOE_SUB
        lo = tlo_ref[i]
        n = tnb_ref[i]
        rank0 = trk_ref[i]
        cb_base = te_ref[i] * (posb_ref.shape[0] + 1)

        def blk_copy(b, slot):
            return pltpu.make_async_copy(hn_hbm.at[pl.ds((lo + b) * SRC_BLK, SRC_BLK), :], hbuf_ref.at[slot],
                                         sem.at[slot])

        for j in range(GATHER_BUFS - 1):
            @pl.when(j < n)
            def _():
                blk_copy(j, j).start()

        def body(b, carry):
            slot = b % GATHER_BUFS
            blk_copy(b, slot).wait()
            ahead = b + GATHER_BUFS - 1

            @pl.when(ahead < n)
            def _():
                blk_copy(ahead, ahead % GATHER_BUFS).start()

            blk = lo + b
            pos = posb_ref[blk]
            pos1, pos2 = pos[0:1, :], pos[1:2, :]
            before = cb_ref[cb_base + blk]
            through = cb_ref[cb_base + blk + 1]
            for q in range(GATHER_SUBS):
                @pl.when((q * GATHER_SUB < rows_used) & (before < rank0 + (q + 1) * GATHER_SUB)
                         & (through > rank0 + q * GATHER_SUB))
                def _():
                    rows = slice(q * GATHER_SUB, (q + 1) * GATHER_SUB)
                    rid = row_id + q * GATHER_SUB
                    onehot = jnp.where((pos1 == rid) | (pos2 == rid), 1.0, 0.0).astype(BF16)
                    xb_ref[rows, :] += _dot(onehot, hbuf_ref[slot]).astype(BF16)
            return carry

        lax.fori_loop(0, n, body, 0)

    for k in range(1, MOE_SUBS + 1):
        @pl.when(nsub == k)
        def _():
            rows = slice(0, k * MOE_SUB)
            h = xb_ref[rows, :]
            a = (_silu(_dot(h, wg_ref[0, 0].astype(BF16))) * _dot(h, wu_ref[0, 0].astype(BF16))).astype(BF16)
            acc_ref[rows, :] += _dot(a, wd_ref[0, 0].astype(BF16))

    @pl.when(f == pl.num_programs(1) - 1)
    def _():
        y_ref[...] = acc_ref[...].astype(y_ref.dtype)


def _moe_ffn(hn, posb, tile_expert, tile_subs, tile_blk_lo, tile_blk_n, tile_rank, blk_counts, wg, wu, wd, lyr):
    d = hn.shape[1]
    tm = MOE_TILE
    ff = wg.shape[3]
    tf = MOE_TF
    nf = ff // tf
    nt = tile_expert.shape[0]
    fidx = lambda i, f, tn: jnp.where(tn[i] > 0, f, nf - 1)
    return pl.pallas_call(
        _moe_ffn_kernel,
        grid_spec=pltpu.PrefetchScalarGridSpec(
            num_scalar_prefetch=6,
            grid=(nt, nf),
            in_specs=[pl.BlockSpec(posb.shape, lambda i, f, te, tn, *_: (0, 0, 0)),
                      pl.BlockSpec(memory_space=pl.ANY),
                      pl.BlockSpec((1, 1, d, tf), lambda i, f, te, tn, *_: (lyr, te[i], 0, fidx(i, f, tn))),
                      pl.BlockSpec((1, 1, d, tf), lambda i, f, te, tn, *_: (lyr, te[i], 0, fidx(i, f, tn))),
                      pl.BlockSpec((1, 1, tf, d), lambda i, f, te, tn, *_: (lyr, te[i], fidx(i, f, tn), 0))],
            out_specs=pl.BlockSpec((tm, d), lambda i, f, te, tn, *_: (i, 0)),
            scratch_shapes=[pltpu.VMEM((tm, d), BF16), pltpu.VMEM((tm, d), F32),
                            pltpu.VMEM((GATHER_BUFS, SRC_BLK, d), BF16),
                            pltpu.SemaphoreType.DMA((GATHER_BUFS,))]),
        out_shape=jax.ShapeDtypeStruct((nt * tm, d), BF16),
        compiler_params=_params(("arbitrary", "arbitrary"), MOE_VMEM_LIMIT),
    )(tile_expert, tile_subs, tile_blk_lo, tile_blk_n, tile_rank, blk_counts, posb, hn, wg, wu, wd)


CMB_SLOTS = [(e, j) for e in range(N_EXPERTS) for j in range(2)]


def _combine_kernel(fb_ref, nb_ref, x_ref, pos_ref, gate_ref, gt_ref, y_hbm, o_ref, ybuf_ref, sem, *, row, tile_off):
    i = pl.program_id(0)
    nslot = len(CMB_SLOTS)
    tm, d = x_ref.shape
    half = i % 2

    def blk(t, s):
        e, j = CMB_SLOTS[s]
        return fb_ref[(tile_off + t) * N_EXPERTS + e] + j

    def used(t, s):
        e, j = CMB_SLOTS[s]
        return nb_ref[(tile_off + t) * N_EXPERTS + e] > j

    def blk_copy(t, s, h):
        return pltpu.make_async_copy(y_hbm.at[pl.ds(blk(t, s) * Y_BLK, Y_BLK), :], ybuf_ref.at[h, s], sem.at[h, s])

    def start_tile(t, h):
        for s in range(nslot):
            @pl.when(used(t, s))
            def _():
                blk_copy(t, s, h).start()

    @pl.when(i == 0)
    def _():
        ybuf_ref[...] = jnp.zeros_like(ybuf_ref)
        start_tile(0, 0)

    @pl.when(i + 1 < pl.num_programs(0))
    def _():
        start_tile(i + 1, 1 - half)

    pos1, pos2 = pos_ref[:, 0:1], pos_ref[:, 1:2]
    g1, g2 = gate_ref[:, 0:1], gate_ref[:, 1:2]
    col = lax.broadcasted_iota(jnp.int32, (tm, Y_BLK), 1)
    weights = []
    for s in range(nslot):
        off = blk(i, s) * Y_BLK
        w = jnp.where(pos1 - off == col, g1, 0.0) + jnp.where(pos2 - off == col, g2, 0.0)
        weights.append(jnp.where(used(i, s), w, 0.0).astype(BF16))
    w_all = jnp.concatenate(weights, axis=1)

    for s in range(nslot):
        @pl.when(used(i, s))
        def _():
            blk_copy(i, s, half).wait()

    acc = _dot(w_all, ybuf_ref[half].reshape(nslot * Y_BLK, d))
    o_ref[...] = x_ref[...] + gt_ref[row:row + 1, :] * acc


def _combine(x, y, pos, gates, first_blk, num_blk, tok_off, ada_l, row):
    t, d = x.shape
    tm = CMB_TILE
    tile_off = tok_off // tm
    return pl.pallas_call(
        functools.partial(_combine_kernel, row=row, tile_off=tile_off),
        grid_spec=pltpu.PrefetchScalarGridSpec(
            num_scalar_prefetch=2,
            grid=(t // tm,),
            in_specs=[pl.BlockSpec((tm, d), lambda i, fb, nb: (i, 0)),
                      pl.BlockSpec((tm, 2), lambda i, fb, nb: (tile_off + i, 0)),
                      pl.BlockSpec((tm, LANES), lambda i, fb, nb: (i, 0)),
                      _ada_spec(5),
                      pl.BlockSpec(memory_space=pl.ANY)],
            out_specs=pl.BlockSpec((tm, d), lambda i, fb, nb: (i, 0)),
            scratch_shapes=[pltpu.VMEM((2, len(CMB_SLOTS), Y_BLK, d), BF16),
                            pltpu.SemaphoreType.DMA((2, len(CMB_SLOTS)))]),
        out_shape=jax.ShapeDtypeStruct((t, d), F32),
        compiler_params=_params(("arbitrary",)),
    )(first_blk, num_blk, x, pos, gates, ada_l, y)


def _route(sel):
    tm = MOE_TILE
    n_tok = sel.shape[0]
    n_asg = 2 * n_tok
    nt = -(-n_asg // tm) + N_EXPERTS
    e_flat = sel.reshape(-1)
    onehot = (e_flat[:, None] == jnp.arange(N_EXPERTS, dtype=jnp.int32)[None, :]).astype(jnp.int32)
    csum = jnp.cumsum(onehot, axis=0)
    count = csum[-1]
    rank = jnp.sum((csum - onehot) * onehot, axis=1)
    tiles_e = (count + tm - 1) // tm
    tile_end = jnp.cumsum(tiles_e)
    base = (tile_end - tiles_e) * tm
    pos = jnp.sum(onehot * base[None, :], axis=1) + rank
    pos = jnp.where(e_flat >= 0, pos, -1)
    tile_id = jnp.arange(nt, dtype=jnp.int32)
    used = tile_end[-1]
    owner = lambda i: jnp.minimum(jnp.sum((i[:, None] >= tile_end[None, :]).astype(jnp.int32), axis=1), N_EXPERTS - 1)
    tile_expert = owner(jnp.minimum(tile_id, used - 1))
    rows_in_tile = jnp.clip((base + count)[tile_expert] - tile_id * tm, 0, tm)
    tile_subs = jnp.where(tile_id < used, (rows_in_tile + MOE_SUB - 1) // MOE_SUB, 0)
    n_blk = n_tok // SRC_BLK
    blk_counts = jnp.concatenate([jnp.zeros((1, N_EXPERTS), jnp.int32),
                                  csum.reshape(n_blk, 2 * SRC_BLK, N_EXPERTS)[:, -1, :]], axis=0).T
    tile_rank = tile_id * tm - base[tile_expert]
    counts_t = blk_counts[tile_expert]
    tile_blk_lo = jnp.sum((counts_t[:, 1:] <= tile_rank[:, None]).astype(jnp.int32), axis=1)
    tile_blk_hi = jnp.sum((counts_t[:, :-1] < (tile_rank + rows_in_tile)[:, None]).astype(jnp.int32), axis=1)
    tile_blk_n = jnp.where(tile_subs > 0, jnp.maximum(tile_blk_hi - tile_blk_lo, 0), 0)
    tile_blk_lo = jnp.minimum(tile_blk_lo, n_blk - 1)
    posb = jnp.pad(pos.reshape(n_blk, SRC_BLK, 2).transpose(0, 2, 1), ((0, 0), (0, 6), (0, 0)), constant_values=-1)
    ntt = n_tok // CMB_TILE
    through = csum.reshape(ntt, 2 * CMB_TILE, N_EXPERTS)[:, -1, :]
    before = jnp.concatenate([jnp.zeros((1, N_EXPERTS), jnp.int32), through[:-1]], axis=0)
    lo = base[None, :] + before
    hi = base[None, :] + through - 1
    first_blk = lo // Y_BLK
    num_blk = jnp.where(through > before, hi // Y_BLK - first_blk + 1, 0)
    return (posb, pos.reshape(n_tok, 2), tile_expert, tile_subs, tile_blk_lo, tile_blk_n, tile_rank,
            blk_counts.reshape(-1), first_blk.reshape(-1), num_blk.reshape(-1))


def _moe(streams, ada_l, wg, wu, wd, lyr):
    hs = [s[2] for s in streams]
    sels = [s[3][:, :2] for s in streams]
    n_tok = sum(h.shape[0] for h in hs)
    pad = -n_tok % SRC_BLK
    if pad:
        hs.append(jnp.zeros((pad, D_MODEL), BF16))
        sels.append(jnp.full((pad, 2), -1, jnp.int32))
    h_all = hs[0] if len(hs) == 1 else jnp.concatenate(hs, axis=0)
    sel_all = sels[0] if len(sels) == 1 else jnp.concatenate(sels, axis=0)
    (posb, pos, tile_expert, tile_subs, tile_blk_lo, tile_blk_n, tile_rank, blk_counts,
     first_blk, num_blk) = _route(sel_all)
    y = _moe_ffn(h_all, posb, tile_expert, tile_subs, tile_blk_lo, tile_blk_n, tile_rank, blk_counts,
                 wg, wu, wd, lyr)
    outs, off = [], 0
    for x, row, _, _, gt in streams:
        outs.append(_combine(x, y, pos, gt, first_blk, num_blk, off, ada_l, row))
        off += x.shape[0]
    return outs


def _even_layer(x, ctx, ada_l, g1, g2, w_in, g_qn, g_kn, sink, w_gate_up, b_gate_up, g_gla, w_out,
                w_ffn_gate, w_ffn_up, w_ffn_down, lyr, rope_tabs, ctx_out):
    d = D_MODEL
    w_main = jnp.concatenate([w_in[:, 2080:3104], w_in[:, 1024:2048], w_in[:, 3616:4640], w_in[:, 512:1024],
                              w_in[:, 3104:3616], w_in[:, 0:256], w_in[:, 256:512]], axis=1).astype(BF16)
    w_gate = jnp.pad(w_in[:, 2048:2080], ((0, 0), (0, LANES - 2 * GATE_RANK))).astype(BF16)
    nk = B_HEADS * B_DK
    w_up = jnp.concatenate([jnp.pad(w_gate_up[0], ((0, 0), (0, nk))), jnp.pad(w_gate_up[1], ((0, 0), (nk, 0))),
                            jnp.zeros((LANES - 2 * GATE_RANK, 2 * nk), F32)], axis=0)
    w_up_hi = w_up.astype(BF16)
    w_up = (w_up_hi, (w_up - w_up_hi.astype(F32)).astype(BF16))
    b_up = b_gate_up.reshape(1, 2 * nk)
    w_out_b = w_out.astype(BF16)
    gq, gk, gg = g_qn.reshape(1, -1), g_kn.reshape(1, -1), g_gla.reshape(1, -1)
    sink_tab = jnp.broadcast_to(sink[:, None], (A_HEADS, LANES))
    cos, sin = rope_tabs

    pc, la_c = _inproj(ctx, ada_l, 1, g1, w_main, w_gate, w_up, b_up)
    qc, kc = _qkprep(pc, cos, sin, gq, gk, rope=False)
    s0 = jnp.zeros((B_HEADS, B_DV, B_DK), F32)
    oc_f, s_fwd = _gla_scan(pc, la_c, s0, rev=False)
    oc_b, s_bwd = _gla_scan(pc, la_c, s0, rev=True)

    px, la_x = _inproj(x, ada_l, 0, g1, w_main, w_gate, w_up, b_up)
    qx, kx = _qkprep(px, cos, sin, gq, gk, rope=True)
    oa = _attention(qx, kx, px, kc, pc, sink_tab, local=True)
    ox_f, _ = _gla_scan(px, la_x, s_fwd, rev=False)
    ox_b, _ = _gla_scan(px, la_x, s_bwd, rev=True)
    x = _outproj(oa, ox_f, ox_b, px, gg, w_out_b, x, ada_l, 0)
    x = _ffn(x, ada_l, 0, g2, w_ffn_gate, w_ffn_up, w_ffn_down, lyr)
    if ctx_out:
        oa_c = _attention(qc, None, None, kc, pc, sink_tab, local=False)
        ctx = _outproj(oa_c, oc_f, oc_b, pc, gg, w_out_b, ctx, ada_l, 1)
        ctx = _ffn(ctx, ada_l, 1, g2, w_ffn_gate, w_ffn_up, w_ffn_down, lyr)
    return x, ctx


def _odd_layer(x, ctx, ada_l, g1, g2, w_pool, pool_scale, w_router, w_exp_gate, w_exp_up, w_exp_down, lyr, ctx_out):
    w_pool_b = w_pool.astype(BF16)
    ps = pool_scale.reshape(1, -1)
    wr = jnp.pad(w_router, ((0, 0), (0, LANES - N_EXPERTS)))
    wr_hi = wr.astype(BF16)
    wr = (wr_hi, (wr - wr_hi.astype(F32)).astype(BF16))
    streams = [(0,) + tuple(_pool_router(x, ada_l, 0, g1, g2, w_pool_b, ps, wr))]
    if ctx_out:
        streams.append((1,) + tuple(_pool_router(ctx, ada_l, 1, g1, g2, w_pool_b, ps, wr)))
    outs = _moe([(s[1], s[0], s[2], s[3], s[4]) for s in streams], ada_l, w_exp_gate, w_exp_up, w_exp_down, lyr)
    return outs[0], (outs[1] if ctx_out else ctx)


def kernel(x, c, ctx, c_ctx, w_ada, b_ada, norm_g, w_in, g_qn, g_kn, attn_sink, w_gate_up, b_gate_up, g_gla, w_out,
           w_ffn_gate, w_ffn_up, w_ffn_down, w_pool, pool_scale, w_router, w_exp_gate, w_exp_up, w_exp_down):
    depth = w_ada.shape[0]
    xs = x[0]
    cs = ctx[0]
    t = xs.shape[0]
    cond = jnp.concatenate([c, c_ctx[None, :], jnp.zeros((6, D_MODEL), F32)], axis=0)
    ada = _ada_all(cond, w_ada, b_ada)
    rope_tabs = _rope_tables(t)
    for l in range(depth):
        ctx_later = any(j % 2 == 0 for j in range(l + 1, depth))
        g1 = norm_g[l, 0].reshape(1, -1)
        g2 = norm_g[l, 1].reshape(1, -1)
        if l % 2 == 0:
            e = l // 2
            xs, cs = _even_layer(xs, cs, ada[l], g1, g2, w_in[e], g_qn[e], g_kn[e], attn_sink[e], w_gate_up[e],
                                 b_gate_up[e], g_gla[e], w_out[e], w_ffn_gate, w_ffn_up, w_ffn_down, e,
                                 rope_tabs, ctx_later)
        else:
            o = l // 2
            xs, cs = _odd_layer(xs, cs, ada[l], g1, g2, w_pool[o], pool_scale[o], w_router[o],
                                w_exp_gate, w_exp_up, w_exp_down, o, ctx_later)
    return xs[None]
```

```python
import functools

import jax
import jax.numpy as jnp
import numpy as np
from jax import lax
from jax.experimental import pallas as pl
from jax.experimental.pallas import tpu as pltpu

F32 = jnp.float32
BF16 = jnp.bfloat16

D_MODEL = 2048
GRID_W = 64
A_HEADS = 8
A_KV_HEADS = 2
A_GROUP = 4
A_HEAD_DIM = 128
A_WIDTH = A_HEADS * A_HEAD_DIM
ATTN_BLOCK = 128
ROPE_BASE = 10000.0
B_HEADS = 4
B_DV = 256
B_DK = 128
B_WIDTH = B_HEADS * B_DV
GATE_RANK = 16
GATE_TAU = 16.0
GLA_CHUNK = 64
GLA_SUB = 256
POOL_GROUPS = 4
POOL_WINDOWS = (2, 4, 8, 16)
POOL_HALO = 8
N_EXPERTS = 8
NORM_EPS = 1e-6
NEG_INF = -1e30
LANES = 128

P_COLS = 4608
VMEM_LIMIT = 56 * 1024 * 1024
MOE_VMEM_LIMIT = 62 * 1024 * 1024


def _params(sem, vmem=VMEM_LIMIT):
    return pltpu.CompilerParams(dimension_semantics=sem, vmem_limit_bytes=vmem)


def _dot(a, b):
    return jnp.dot(a, b, preferred_element_type=F32)


def _dot_nt(a, b):
    return lax.dot_general(a, b, (((1,), (1,)), ((), ())), preferred_element_type=F32)


def _dot_tn(a, b):
    return lax.dot_general(a, b, (((0,), (0,)), ((), ())), preferred_element_type=F32)


def _silu(x):
    return x * jax.nn.sigmoid(x)


def _norm_mod(x, g, shift, scale):
    ms = jnp.mean(x * x, axis=-1, keepdims=True)
    return (x * lax.rsqrt(ms + NORM_EPS) * g) * (1.0 + scale) + shift


def _ada_kernel(cond_ref, w_ref, b_ref, o_ref):
    s = _silu(cond_ref[...])
    s_hi = s.astype(BF16)
    s_lo = (s - s_hi.astype(F32)).astype(BF16)
    w = w_ref[0].astype(BF16)
    o_ref[0] = _dot(s_hi, w) + _dot(s_lo, w) + b_ref[0]


def _ada_all(cond, w_ada, b_ada):
    depth, d, n = w_ada.shape
    tn = 1024
    return pl.pallas_call(
        _ada_kernel,
        grid=(depth, n // tn),
        in_specs=[pl.BlockSpec((8, d), lambda l, j: (0, 0)),
                  pl.BlockSpec((1, d, tn), lambda l, j: (l, 0, j)),
                  pl.BlockSpec((1, 1, tn), lambda l, j: (l, 0, j))],
        out_specs=pl.BlockSpec((1, 8, tn), lambda l, j: (l, 0, j)),
        out_shape=jax.ShapeDtypeStruct((depth, 8, n), F32),
        compiler_params=_params(("parallel", "parallel")),
    )(cond, w_ada, b_ada.reshape(depth, 1, n))


def _ada_spec(k):
    return pl.BlockSpec((8, D_MODEL), lambda *_: (0, k))


def _inproj_kernel(x_ref, g_ref, sh_ref, sc_ref, w_ref, wg_ref, wuh_ref, wul_ref, bup_ref, p_ref, la_ref, hn_ref,
                   *, row):
    @pl.when(pl.program_id(1) == 0)
    def _():
        h = _norm_mod(x_ref[...], g_ref[...], sh_ref[row:row + 1, :], sc_ref[row:row + 1, :]).astype(BF16)
        hn_ref[...] = h
        lr = _dot(h, wg_ref[...])
        lr_hi = lr.astype(BF16)
        lr_lo = (lr - lr_hi.astype(F32)).astype(BF16)
        z = _dot(lr_hi, wuh_ref[...]) + _dot(lr_lo, wuh_ref[...]) + _dot(lr_hi, wul_ref[...]) + bup_ref[...]
        la_ref[...] = (jnp.minimum(z, 0.0) - jnp.log1p(jnp.exp(-jnp.abs(z)))) * (1.0 / GATE_TAU)

    p_ref[...] = _dot(hn_ref[...], w_ref[...]).astype(p_ref.dtype)


def _inproj(x, ada_l, row, g, w_main, w_gate, w_up, b_up):
    t, d = x.shape
    tm = min(512, t)
    tn = 1536
    nla = 2 * B_HEADS * B_DK
    return pl.pallas_call(
        functools.partial(_inproj_kernel, row=row),
        grid=(t // tm, P_COLS // tn),
        in_specs=[pl.BlockSpec((tm, d), lambda i, j: (i, 0)),
                  pl.BlockSpec((1, d), lambda i, j: (0, 0)),
                  _ada_spec(0), _ada_spec(1),
                  pl.BlockSpec((d, tn), lambda i, j: (0, j)),
                  pl.BlockSpec((d, LANES), lambda i, j: (0, 0)),
                  pl.BlockSpec((LANES, nla), lambda i, j: (0, 0)),
                  pl.BlockSpec((LANES, nla), lambda i, j: (0, 0)),
                  pl.BlockSpec((1, nla), lambda i, j: (0, 0))],
        out_specs=[pl.BlockSpec((tm, tn), lambda i, j: (i, j)),
                   pl.BlockSpec((tm, nla), lambda i, j: (i, 0))],
        out_shape=[jax.ShapeDtypeStruct((t, P_COLS), BF16),
                   jax.ShapeDtypeStruct((t, nla), F32)],
        scratch_shapes=[pltpu.VMEM((tm, d), BF16)],
        compiler_params=_params(("parallel", "arbitrary")),
    )(x, g, ada_l, ada_l, w_main, w_gate, w_up[0], w_up[1], b_up)


def _qkprep_kernel(q_ref, k_ref, cos_ref, sin_ref, gq_ref, gk_ref, qo_ref, ko_ref, *, rope):
    tm = q_ref.shape[0]
    lane = lax.broadcasted_iota(jnp.int32, (tm, A_HEAD_DIM), 1)
    first_half = (lane % 64) < 32

    def prep(xh, g, scale):
        ms = jnp.mean(xh * xh, axis=-1, keepdims=True)
        y = xh * lax.rsqrt(ms + NORM_EPS) * g
        if rope:
            partner = jnp.where(first_half, pltpu.roll(y, 96, 1), pltpu.roll(y, 32, 1))
            y = y * cos_ref[...] + partner * sin_ref[...]
        return (y * scale).astype(BF16)

    for h in range(A_HEADS):
        cols = slice(h * A_HEAD_DIM, (h + 1) * A_HEAD_DIM)
        qo_ref[:, cols] = prep(q_ref[:, cols].astype(F32), gq_ref[...], A_HEAD_DIM ** -0.5)
    for h in range(A_KV_HEADS):
        cols = slice(h * A_HEAD_DIM, (h + 1) * A_HEAD_DIM)
        ko_ref[:, cols] = prep(k_ref[:, cols].astype(F32), gk_ref[...], 1.0)


def _qkprep(p, cos, sin, g_qn, g_kn, rope):
    t = p.shape[0]
    tm = min(512, t)
    kw = A_KV_HEADS * A_HEAD_DIM
    return pl.pallas_call(
        functools.partial(_qkprep_kernel, rope=rope),
        grid=(t // tm,),
        in_specs=[pl.BlockSpec((tm, A_WIDTH), lambda i: (i, 0)),
                  pl.BlockSpec((tm, kw), lambda i: (i, 4096 // kw)),
                  pl.BlockSpec((tm, A_HEAD_DIM), lambda i: (i, 0)),
                  pl.BlockSpec((tm, A_HEAD_DIM), lambda i: (i, 0)),
                  pl.BlockSpec((1, A_HEAD_DIM), lambda i: (0, 0)),
                  pl.BlockSpec((1, A_HEAD_DIM), lambda i: (0, 0))],
        out_specs=[pl.BlockSpec((tm, A_WIDTH), lambda i: (i, 0)),
                   pl.BlockSpec((tm, kw), lambda i: (i, 0))],
        out_shape=[jax.ShapeDtypeStruct((t, A_WIDTH), BF16),
                   jax.ShapeDtypeStruct((t, kw), BF16)],
        compiler_params=_params(("parallel",)),
    )(p, p, cos, sin, g_qn, g_kn)


def _rope_tables(t):
    half = A_HEAD_DIM // 4
    freqs = ROPE_BASE ** (-np.arange(half, dtype=np.float64) / half)
    tok = np.arange(t)
    row = (tok // GRID_W).astype(np.float64)[:, None] * freqs
    col = (tok % GRID_W).astype(np.float64)[:, None] * freqs
    cos = np.concatenate([np.cos(row), np.cos(row), np.cos(col), np.cos(col)], axis=-1)
    sin = np.concatenate([-np.sin(row), np.sin(row), -np.sin(col), np.sin(col)], axis=-1)
    return jnp.asarray(cos.astype(np.float32)), jnp.asarray(sin.astype(np.float32))


def _softmax_pv(scores, values, sink_col):
    def lane_blocks(xs):
        return [x[:, i * LANES:(i + 1) * LANES] for x in xs for i in range(x.shape[1] // LANES)]

    m = jnp.maximum(jnp.max(functools.reduce(jnp.maximum, lane_blocks(scores)), axis=-1, keepdims=True), sink_col)
    probs = [jnp.exp(s - m) for s in scores]
    denom = jnp.exp(sink_col - m) + jnp.sum(functools.reduce(jnp.add, lane_blocks(probs)), axis=-1, keepdims=True)
    acc = None
    for p, v in zip(probs, values):
        pv = _dot(p.astype(BF16), v)
        acc = pv if acc is None else acc + pv
    return acc / denom


def _attn_kernel(*refs, local):
    if local:
        q_ref, kp_ref, ko_ref, kn_ref, vp_ref, vo_ref, vn_ref, kc_ref, vc_ref, sink_ref, o_ref = refs
    else:
        q_ref, kc_ref, vc_ref, sink_ref, o_ref = refs
    n = pl.program_id(0)
    nb = pl.num_programs(0)
    rows = A_GROUP * ATTN_BLOCK
    if local:
        qi = lax.broadcasted_iota(jnp.int32, (rows, ATTN_BLOCK), 0) % ATTN_BLOCK
        kj = lax.broadcasted_iota(jnp.int32, (rows, ATTN_BLOCK), 1)
        mask_prev = (kj >= qi) & (n > 0)
        mask_next = (kj <= qi) & (n < nb - 1)
    for kv in range(A_KV_HEADS):
        kc = slice(kv * A_HEAD_DIM, (kv + 1) * A_HEAD_DIM)
        heads = [kv * A_GROUP + g for g in range(A_GROUP)]
        q4 = jnp.concatenate([q_ref[:, h * A_HEAD_DIM:(h + 1) * A_HEAD_DIM] for h in heads], axis=0)
        sink_col = jnp.concatenate(
            [jnp.broadcast_to(sink_ref[h:h + 1, 0:1], (ATTN_BLOCK, 1)) for h in heads], axis=0)
        scores, values = [], []
        if local:
            scores.append(jnp.where(mask_prev, _dot_nt(q4, kp_ref[:, kc]), NEG_INF))
            scores.append(_dot_nt(q4, ko_ref[:, kc]))
            scores.append(jnp.where(mask_next, _dot_nt(q4, kn_ref[:, kc]), NEG_INF))
            values += [vp_ref[:, kc], vo_ref[:, kc], vn_ref[:, kc]]
        scores.append(_dot_nt(q4, kc_ref[:, kc]))
        values.append(vc_ref[:, kc])
        o = _softmax_pv(scores, values, sink_col).astype(o_ref.dtype)
        for g, h in enumerate(heads):
            o_ref[:, h * A_HEAD_DIM:(h + 1) * A_HEAD_DIM] = o[g * ATTN_BLOCK:(g + 1) * ATTN_BLOCK, :]


def _attention(q, k, p, k_ctx, p_ctx, sink_tab, local):
    t = q.shape[0]
    nb = t // ATTN_BLOCK
    l = k_ctx.shape[0]
    kw = A_KV_HEADS * A_HEAD_DIM
    vcol = 4352 // kw
    blk = lambda f: pl.BlockSpec((ATTN_BLOCK, kw), f)
    blkv = lambda f: pl.BlockSpec((ATTN_BLOCK, kw), lambda i: (f(i)[0], vcol))
    prev = lambda i: (jnp.maximum(i - 1, 0), 0)
    own = lambda i: (i, 0)
    nxt = lambda i: (jnp.minimum(i + 1, nb - 1), 0)
    in_specs = [pl.BlockSpec((ATTN_BLOCK, A_WIDTH), own)]
    args = [q]
    if local:
        in_specs += [blk(prev), blk(own), blk(nxt), blkv(prev), blkv(own), blkv(nxt)]
        args += [k, k, k, p, p, p]
    in_specs += [pl.BlockSpec((l, kw), lambda i: (0, 0)), pl.BlockSpec((l, kw), lambda i: (0, vcol)),
                 pl.BlockSpec((A_HEADS, LANES), lambda i: (0, 0))]
    args += [k_ctx, p_ctx, sink_tab]
    return pl.pallas_call(
        functools.partial(_attn_kernel, local=local),
        grid=(nb,),
        in_specs=in_specs,
        out_specs=pl.BlockSpec((ATTN_BLOCK, A_WIDTH), own),
        out_shape=jax.ShapeDtypeStruct((t, A_WIDTH), BF16),
        compiler_params=_params(("parallel",)),
    )(*args)


def _gla_kernel(q_ref, k_ref, v_ref, la_ref, s0_ref, o_ref, sf_ref, st_ref, *, rev, nchunk):
    @pl.when(pl.program_id(0) == 0)
    def _():
        st_ref[...] = s0_ref[...]

    r = nchunk * GLA_CHUNK
    sub = min(r, GLA_SUB)
    ii = lax.broadcasted_iota(jnp.int32, (sub, sub), 0)
    jj = lax.broadcasted_iota(jnp.int32, (sub, sub), 1)
    same_chunk = (ii // GLA_CHUNK) == (jj // GLA_CHUNK)
    tri = same_chunk & ((jj >= ii) if rev else (jj <= ii))
    b_all = la_ref[...]
    pos = lax.broadcasted_iota(jnp.int32, b_all.shape, 0) % GLA_CHUNK
    step = 1
    while step < GLA_CHUNK:
        if rev:
            b_all = b_all + jnp.where(pos < GLA_CHUNK - step, pltpu.roll(b_all, r - step, 0), 0.0)
        else:
            b_all = b_all + jnp.where(pos >= step, pltpu.roll(b_all, step, 0), 0.0)
        step *= 2
    heads = []
    for h in range(B_HEADS):
        kc = slice(h * B_DK, (h + 1) * B_DK)
        b = b_all[:, kc]
        b3 = b.reshape(nchunk, GLA_CHUNK, B_DK)
        b_end = b3[:, 0:1, :] if rev else b3[:, GLA_CHUNK - 1:GLA_CHUNK, :]
        k = k_ref[:, kc].astype(F32)
        qe = (q_ref[:, kc].astype(F32) * (B_DK ** -0.5) * jnp.exp(b)).astype(BF16)
        ke = (k * jnp.exp(-b)).astype(BF16)
        kd = (k.reshape(nchunk, GLA_CHUNK, B_DK) * jnp.exp(b_end - b3)).astype(BF16)
        v = v_ref[:, h * B_DV:(h + 1) * B_DV].astype(BF16)
        o_intra = []
        for s0 in range(0, r, sub):
            rs = slice(s0, s0 + sub)
            a = jnp.where(tri, _dot_nt(qe[rs, :], ke[rs, :]), 0.0).astype(BF16)
            o_intra.append(_dot(a, v[rs, :]))
        heads.append((qe, kd, v, jnp.concatenate(o_intra, axis=0), jnp.exp(b_end)))
    st = [st_ref[h] for h in range(B_HEADS)]
    for c in (range(nchunk - 1, -1, -1) if rev else range(nchunk)):
        rows = slice(c * GLA_CHUNK, (c + 1) * GLA_CHUNK)
        for h, (qe, kd, v, o_intra, decay) in enumerate(heads):
            o = o_intra[rows, :] + _dot_nt(qe[rows, :], st[h].astype(BF16))
            o_ref[0, rows, h * B_DV:(h + 1) * B_DV] = o.astype(o_ref.dtype)
            st[h] = st[h] * decay[c] + _dot_tn(v[rows, :], kd[c])
    for h in range(B_HEADS):
        st_ref[h] = st[h]
        sf_ref[h] = st[h]


def _gla_scan(p, la, s0, rev):
    t = p.shape[0]
    r = min(512, t)
    nblk = t // r
    rb = (lambda c: nblk - 1 - c) if rev else (lambda c: c)
    nk = B_HEADS * B_DK
    o, sf = pl.pallas_call(
        functools.partial(_gla_kernel, rev=rev, nchunk=r // GLA_CHUNK),
        grid=(nblk,),
        in_specs=[pl.BlockSpec((r, nk), lambda c: (rb(c), 3584 // nk)),
                  pl.BlockSpec((r, nk), lambda c: (rb(c), 3072 // nk)),
                  pl.BlockSpec((r, B_WIDTH), lambda c: (rb(c), 1024 // B_WIDTH)),
                  pl.BlockSpec((r, nk), lambda c: (rb(c), 1 if rev else 0)),
                  pl.BlockSpec((B_HEADS, B_DV, B_DK), lambda c: (0, 0, 0))],
        out_specs=[pl.BlockSpec((1, r, B_WIDTH), lambda c: (0, rb(c), 0)),
                   pl.BlockSpec((B_HEADS, B_DV, B_DK), lambda c: (0, 0, 0))],
        out_shape=[jax.ShapeDtypeStruct((1, t, B_WIDTH), BF16),
                   jax.ShapeDtypeStruct((B_HEADS, B_DV, B_DK), F32)],
        scratch_shapes=[pltpu.VMEM((B_HEADS, B_DV, B_DK), F32)],
        compiler_params=_params(("arbitrary",)),
    )(p, p, p, la, s0)
    return o[0], sf


def _outproj_kernel(oa_ref, of_ref, ob_ref, og_ref, gg_ref, w_ref, x_ref, gt_ref, o_ref, mix_ref, *, row):
    @pl.when(pl.program_id(1) == 0)
    def _():
        mix_ref[:, 0:A_WIDTH] = oa_ref[...]
        for h in range(B_HEADS):
            cols = slice(h * B_DV, (h + 1) * B_DV)
            o = of_ref[:, cols].astype(F32) + ob_ref[:, cols].astype(F32)
            ms = jnp.mean(o * o, axis=-1, keepdims=True)
            y = o * lax.rsqrt(ms + NORM_EPS) * gg_ref[...]
            mix_ref[:, A_WIDTH + h * B_DV:A_WIDTH + (h + 1) * B_DV] = (y * _silu(og_ref[:, cols].astype(F32))).astype(BF16)

    o_ref[...] = x_ref[...] + gt_ref[row:row + 1, :] * _dot(mix_ref[...], w_ref[...])


def _outproj(oa, o_f, o_b, p, g_gla, w_out, x, ada_l, row):
    t, d = x.shape
    tm = min(512, t)
    tn = d
    return pl.pallas_call(
        functools.partial(_outproj_kernel, row=row),
        grid=(t // tm, d // tn),
        in_specs=[pl.BlockSpec((tm, A_WIDTH), lambda i, j: (i, 0)),
                  pl.BlockSpec((tm, B_WIDTH), lambda i, j: (i, 0)),
                  pl.BlockSpec((tm, B_WIDTH), lambda i, j: (i, 0)),
                  pl.BlockSpec((tm, B_WIDTH), lambda i, j: (i, 2048 // B_WIDTH)),
                  pl.BlockSpec((1, B_DV), lambda i, j: (0, 0)),
                  pl.BlockSpec((A_WIDTH + B_WIDTH, tn), lambda i, j: (0, j)),
                  pl.BlockSpec((tm, tn), lambda i, j: (i, j)),
                  pl.BlockSpec((8, tn), lambda i, j: (0, 2 * (d // tn) + j))],
        out_specs=pl.BlockSpec((tm, tn), lambda i, j: (i, j)),
        out_shape=jax.ShapeDtypeStruct((t, d), F32),
        scratch_shapes=[pltpu.VMEM((tm, A_WIDTH + B_WIDTH), BF16)],
        compiler_params=_params(("parallel", "arbitrary")),
    )(oa, o_f, o_b, p, g_gla, w_out, x, ada_l)


def _ffn_kernel(x_ref, g_ref, sh_ref, sc_ref, gt_ref, wg_ref, wu_ref, wd_ref, o_ref, hn_ref, *, row):
    f = pl.program_id(1)

    @pl.when(f == 0)
    def _():
        hn_ref[...] = _norm_mod(x_ref[...], g_ref[...], sh_ref[row:row + 1, :], sc_ref[row:row + 1, :]).astype(BF16)
        o_ref[...] = jnp.zeros_like(o_ref)

    h = hn_ref[...]
    a = (_silu(_dot(h, wg_ref[0].astype(BF16))) * _dot(h, wu_ref[0].astype(BF16))).astype(BF16)
    o_ref[...] += _dot(a, wd_ref[0].astype(BF16))

    @pl.when(f == pl.num_programs(1) - 1)
    def _():
        o_ref[...] = x_ref[...] + gt_ref[row:row + 1, :] * o_ref[...]


def _ffn(x, ada_l, row, g, wg, wu, wd, lyr):
    t, d = x.shape
    ff = wg.shape[2]
    tm = min(1024, t)
    tf = 256
    return pl.pallas_call(
        functools.partial(_ffn_kernel, row=row),
        grid=(t // tm, ff // tf),
        in_specs=[pl.BlockSpec((tm, d), lambda i, f: (i, 0)),
                  pl.BlockSpec((1, d), lambda i, f: (0, 0)),
                  _ada_spec(3), _ada_spec(4), _ada_spec(5),
                  pl.BlockSpec((1, d, tf), lambda i, f: (lyr, 0, f)),
                  pl.BlockSpec((1, d, tf), lambda i, f: (lyr, 0, f)),
                  pl.BlockSpec((1, tf, d), lambda i, f: (lyr, f, 0))],
        out_specs=pl.BlockSpec((tm, d), lambda i, f: (i, 0)),
        out_shape=jax.ShapeDtypeStruct((t, d), F32),
        scratch_shapes=[pltpu.VMEM((tm, d), BF16)],
        compiler_params=_params(("parallel", "arbitrary"), MOE_VMEM_LIMIT),
    )(x, g, ada_l, ada_l, ada_l, wg, wu, wd)


def _pool_router_kernel(x_ref, xp_ref, xn_ref, g_ref, sh_ref, sc_ref, gt_ref, w_ref, ps_ref,
                        g2_ref, sh2_ref, sc2_ref, wrh_ref, wrl_ref, o_ref, hn_ref, sel_ref, gate_ref, hb_ref,
                        *, row, t):
    i = pl.program_id(0)
    tm = x_ref.shape[0]
    g = g_ref[...]
    sh = sh_ref[row:row + 1, :]
    sc = sc_ref[row:row + 1, :]
    hb_ref[0:POOL_HALO, :] = jnp.where(i > 0, _norm_mod(xp_ref[...], g, sh, sc), 0.0)
    hb_ref[POOL_HALO:POOL_HALO + tm, :] = _norm_mod(x_ref[...], g, sh, sc)
    hb_ref[POOL_HALO + tm:, :] = jnp.where(i < pl.num_programs(0) - 1, _norm_mod(xn_ref[...], g, sh, sc), 0.0)
    tpos = i * tm + lax.broadcasted_iota(jnp.int32, (tm, 1), 0)
    gc = D_MODEL // POOL_GROUPS
    for grp, w in enumerate(POOL_WINDOWS):
        cols = slice(grp * gc, (grp + 1) * gc)
        acc = None
        for dlt in range(-(w // 2), w - w // 2):
            piece = hb_ref[POOL_HALO + dlt:POOL_HALO + dlt + tm, cols]
            acc = piece if acc is None else acc + piece
        cnt = jnp.minimum(tpos + (w - w // 2), t) - jnp.maximum(tpos - w // 2, 0)
        pooled = acc / cnt.astype(F32) - hb_ref[POOL_HALO:POOL_HALO + tm, cols]
        y = _dot(pooled.astype(BF16), w_ref[grp])
        o_ref[:, cols] = x_ref[:, cols] + gt_ref[row:row + 1, cols] * (y * ps_ref[:, cols])

    h = _norm_mod(o_ref[...], g2_ref[...], sh2_ref[row:row + 1, :], sc2_ref[row:row + 1, :])
    hn_ref[...] = h.astype(hn_ref.dtype)
    lane = lax.broadcasted_iota(jnp.int32, (h.shape[0], LANES), 1)
    lane_f = lane.astype(F32)
    h_hi = h.astype(BF16)
    h_lo = (h - h_hi.astype(F32)).astype(BF16)
    logits = _dot(h_hi, wrh_ref[...]) + _dot(h_lo, wrh_ref[...]) + _dot(h_hi, wrl_ref[...])
    logits = jnp.where(lane < N_EXPERTS, logits, -jnp.inf)
    m1 = jnp.max(logits, axis=-1, keepdims=True)
    i1 = jnp.min(jnp.where(logits == m1, lane_f, float(LANES)), axis=-1, keepdims=True)
    rest = jnp.where(lane_f == i1, -jnp.inf, logits)
    m2 = jnp.max(rest, axis=-1, keepdims=True)
    i2 = jnp.min(jnp.where(rest == m2, lane_f, float(LANES)), axis=-1, keepdims=True)
    e2 = jnp.exp(m2 - m1)
    den = 1.0 + e2
    sel_ref[...] = jnp.where(lane == 0, i1, jnp.where(lane == 1, i2, 0.0)).astype(jnp.int32)
    gate_ref[...] = jnp.where(lane == 0, 1.0 / den, jnp.where(lane == 1, e2 / den, 0.0))


def _pool_router(x, ada_l, row, g1, g2, w_pool, pool_scale, w_router_pad):
    t, d = x.shape
    tm = min(256, t)
    nh = t // POOL_HALO
    gc = d // POOL_GROUPS
    rows = lambda i: (i, 0)
    const = lambda i: (0, 0)
    return pl.pallas_call(
        functools.partial(_pool_router_kernel, row=row, t=t),
        grid=(t // tm,),
        in_specs=[pl.BlockSpec((tm, d), rows),
                  pl.BlockSpec((POOL_HALO, d), lambda i: (jnp.maximum(i * (tm // POOL_HALO) - 1, 0), 0)),
                  pl.BlockSpec((POOL_HALO, d), lambda i: (jnp.minimum((i + 1) * (tm // POOL_HALO), nh - 1), 0)),
                  pl.BlockSpec((1, d), const),
                  _ada_spec(0), _ada_spec(1), _ada_spec(2),
                  pl.BlockSpec((POOL_GROUPS, gc, gc), lambda i: (0, 0, 0)),
                  pl.BlockSpec((1, d), const),
                  pl.BlockSpec((1, d), const),
                  _ada_spec(3), _ada_spec(4),
                  pl.BlockSpec((d, LANES), const), pl.BlockSpec((d, LANES), const)],
        out_specs=[pl.BlockSpec((tm, d), rows), pl.BlockSpec((tm, d), rows),
                   pl.BlockSpec((tm, LANES), rows), pl.BlockSpec((tm, LANES), rows)],
        out_shape=[jax.ShapeDtypeStruct((t, d), F32),
                   jax.ShapeDtypeStruct((t, d), BF16),
                   jax.ShapeDtypeStruct((t, LANES), jnp.int32),
                   jax.ShapeDtypeStruct((t, LANES), F32)],
        scratch_shapes=[pltpu.VMEM((tm + 2 * POOL_HALO, d), F32)],
        compiler_params=_params(("parallel",)),
    )(x, x, x, g1, ada_l, ada_l, ada_l, w_pool, pool_scale, g2, ada_l, ada_l, w_router_pad[0], w_router_pad[1])


MOE_TILE = 1280
MOE_SUB = 256
MOE_SUBS = MOE_TILE // MOE_SUB
GATHER_SUB = 256
GATHER_SUBS = MOE_TILE // GATHER_SUB
SRC_BLK = 256
Y_BLK = 256
CMB_TILE = 256
MOE_TF = 512
GATHER_BUFS = 3


def _moe_ffn_kernel(te_ref, tn_ref, tlo_ref, tnb_ref, trk_ref, cb_ref, posb_ref, hn_hbm, wg_ref, wu_ref, wd_ref,
                    y_ref, xb_ref, acc_ref, hbuf_ref, sem):
    i = pl.program_id(0)
    f = pl.program_id(1)
    nsub = tn_ref[i]

    @pl.when(f == 0)
    def _():
        xb_ref[...] = jnp.zeros_like(xb_ref)
        acc_ref[...] = jnp.zeros_like(acc_ref)
        row_id = lax.broadcasted_iota(jnp.int32, (GATHER_SUB, SRC_BLK), 0) + i * MOE_TILE
        rows_used = nsub * MOE_SUB
        lo = tlo_ref[i]
        n = tnb_ref[i]
        rank0 = trk_ref[i]
        cb_base = te_ref[i] * (posb_ref.shape[0] + 1)

        def blk_copy(b, slot):
            return pltpu.make_async_copy(hn_hbm.at[pl.ds((lo + b) * SRC_BLK, SRC_BLK), :], hbuf_ref.at[slot],
                                         sem.at[slot])

        for j in range(GATHER_BUFS - 1):
            @pl.when(j < n)
            def _():
                blk_copy(j, j).start()

        def body(b, carry):
            slot = b % GATHER_BUFS
            blk_copy(b, slot).wait()
            ahead = b + GATHER_BUFS - 1

            @pl.when(ahead < n)
            def _():
                blk_copy(ahead, ahead % GATHER_BUFS).start()

            blk = lo + b
            pos = posb_ref[blk]
            pos1, pos2 = pos[0:1, :], pos[1:2, :]
            before = cb_ref[cb_base + blk]
            through = cb_ref[cb_base + blk + 1]
            for q in range(GATHER_SUBS):
                @pl.when((q * GATHER_SUB < rows_used) & (before < rank0 + (q + 1) * GATHER_SUB)
                         & (through > rank0 + q * GATHER_SUB))
                def _():
                    rows = slice(q * GATHER_SUB, (q + 1) * GATHER_SUB)
                    rid = row_id + q * GATHER_SUB
                    onehot = jnp.where((pos1 == rid) | (pos2 == rid), 1.0, 0.0).astype(BF16)
                    xb_ref[rows, :] += _dot(onehot, hbuf_ref[slot]).astype(BF16)
            return carry

        lax.fori_loop(0, n, body, 0)

    for k in range(1, MOE_SUBS + 1):
        @pl.when(nsub == k)
        def _():
            rows = slice(0, k * MOE_SUB)
            h = xb_ref[rows, :]
            a = (_silu(_dot(h, wg_ref[0, 0].astype(BF16))) * _dot(h, wu_ref[0, 0].astype(BF16))).astype(BF16)
            acc_ref[rows, :] += _dot(a, wd_ref[0, 0].astype(BF16))

    @pl.when(f == pl.num_programs(1) - 1)
    def _():
        y_ref[...] = acc_ref[...].astype(y_ref.dtype)


def _moe_ffn(hn, posb, tile_expert, tile_subs, tile_blk_lo, tile_blk_n, tile_rank, blk_counts, wg, wu, wd, lyr):
    d = hn.shape[1]
    tm = MOE_TILE
    ff = wg.shape[3]
    tf = MOE_TF
    nf = ff // tf
    nt = tile_expert.shape[0]
    fidx = lambda i, f, tn: jnp.where(tn[i] > 0, f, nf - 1)
    return pl.pallas_call(
        _moe_ffn_kernel,
        grid_spec=pltpu.PrefetchScalarGridSpec(
            num_scalar_prefetch=6,
            grid=(nt, nf),
            in_specs=[pl.BlockSpec(posb.shape, lambda i, f, te, tn, *_: (0, 0, 0)),
                      pl.BlockSpec(memory_space=pl.ANY),
                      pl.BlockSpec((1, 1, d, tf), lambda i, f, te, tn, *_: (lyr, te[i], 0, fidx(i, f, tn))),
                      pl.BlockSpec((1, 1, d, tf), lambda i, f, te, tn, *_: (lyr, te[i], 0, fidx(i, f, tn))),
                      pl.BlockSpec((1, 1, tf, d), lambda i, f, te, tn, *_: (lyr, te[i], fidx(i, f, tn), 0))],
            out_specs=pl.BlockSpec((tm, d), lambda i, f, te, tn, *_: (i, 0)),
            scratch_shapes=[pltpu.VMEM((tm, d), BF16), pltpu.VMEM((tm, d), F32),
                            pltpu.VMEM((GATHER_BUFS, SRC_BLK, d), BF16),
                            pltpu.SemaphoreType.DMA((GATHER_BUFS,))]),
        out_shape=jax.ShapeDtypeStruct((nt * tm, d), BF16),
        compiler_params=_params(("arbitrary", "arbitrary"), MOE_VMEM_LIMIT),
    )(tile_expert, tile_subs, tile_blk_lo, tile_blk_n, tile_rank, blk_counts, posb, hn, wg, wu, wd)


CMB_SLOTS = [(e, j) for e in range(N_EXPERTS) for j in range(2)]


def _combine_kernel(fb_ref, nb_ref, x_ref, pos_ref, gate_ref, gt_ref, y_hbm, o_ref, ybuf_ref, sem, *, row, tile_off):
    i = pl.program_id(0)
    nslot = len(CMB_SLOTS)
    tm, d = x_ref.shape
    half = i % 2

    def blk(t, s):
        e, j = CMB_SLOTS[s]
        return fb_ref[(tile_off + t) * N_EXPERTS + e] + j

    def used(t, s):
        e, j = CMB_SLOTS[s]
        return nb_ref[(tile_off + t) * N_EXPERTS + e] > j

    def blk_copy(t, s, h):
        return pltpu.make_async_copy(y_hbm.at[pl.ds(blk(t, s) * Y_BLK, Y_BLK), :], ybuf_ref.at[h, s], sem.at[h, s])

    def start_tile(t, h):
        for s in range(nslot):
            @pl.when(used(t, s))
            def _():
                blk_copy(t, s, h).start()

    @pl.when(i == 0)
    def _():
        ybuf_ref[...] = jnp.zeros_like(ybuf_ref)
        start_tile(0, 0)

    @pl.when(i + 1 < pl.num_programs(0))
    def _():
        start_tile(i + 1, 1 - half)

    pos1, pos2 = pos_ref[:, 0:1], pos_ref[:, 1:2]
    g1, g2 = gate_ref[:, 0:1], gate_ref[:, 1:2]
    col = lax.broadcasted_iota(jnp.int32, (tm, Y_BLK), 1)
    weights = []
    for s in range(nslot):
        off = blk(i, s) * Y_BLK
        w = jnp.where(pos1 - off == col, g1, 0.0) + jnp.where(pos2 - off == col, g2, 0.0)
        weights.append(jnp.where(used(i, s), w, 0.0).astype(BF16))
    w_all = jnp.concatenate(weights, axis=1)

    for s in range(nslot):
        @pl.when(used(i, s))
        def _():
            blk_copy(i, s, half).wait()

    acc = _dot(w_all, ybuf_ref[half].reshape(nslot * Y_BLK, d))
    o_ref[...] = x_ref[...] + gt_ref[row:row + 1, :] * acc


def _combine(x, y, pos, gates, first_blk, num_blk, tok_off, ada_l, row):
    t, d = x.shape
    tm = CMB_TILE
    tile_off = tok_off // tm
    return pl.pallas_call(
        functools.partial(_combine_kernel, row=row, tile_off=tile_off),
        grid_spec=pltpu.PrefetchScalarGridSpec(
            num_scalar_prefetch=2,
            grid=(t // tm,),
            in_specs=[pl.BlockSpec((tm, d), lambda i, fb, nb: (i, 0)),
                      pl.BlockSpec((tm, 2), lambda i, fb, nb: (tile_off + i, 0)),
                      pl.BlockSpec((tm, LANES), lambda i, fb, nb: (i, 0)),
                      _ada_spec(5),
                      pl.BlockSpec(memory_space=pl.ANY)],
            out_specs=pl.BlockSpec((tm, d), lambda i, fb, nb: (i, 0)),
            scratch_shapes=[pltpu.VMEM((2, len(CMB_SLOTS), Y_BLK, d), BF16),
                            pltpu.SemaphoreType.DMA((2, len(CMB_SLOTS)))]),
        out_shape=jax.ShapeDtypeStruct((t, d), F32),
        compiler_params=_params(("arbitrary",)),
    )(first_blk, num_blk, x, pos, gates, ada_l, y)


def _route(sel):
    tm = MOE_TILE
    n_tok = sel.shape[0]
    n_asg = 2 * n_tok
    nt = -(-n_asg // tm) + N_EXPERTS
    e_flat = sel.reshape(-1)
    onehot = (e_flat[:, None] == jnp.arange(N_EXPERTS, dtype=jnp.int32)[None, :]).astype(jnp.int32)
    csum = jnp.cumsum(onehot, axis=0)
    count = csum[-1]
    rank = jnp.sum((csum - onehot) * onehot, axis=1)
    tiles_e = (count + tm - 1) // tm
    tile_end = jnp.cumsum(tiles_e)
    base = (tile_end - tiles_e) * tm
    pos = jnp.sum(onehot * base[None, :], axis=1) + rank
    pos = jnp.where(e_flat >= 0, pos, -1)
    tile_id = jnp.arange(nt, dtype=jnp.int32)
    used = tile_end[-1]
    owner = lambda i: jnp.minimum(jnp.sum((i[:, None] >= tile_end[None, :]).astype(jnp.int32), axis=1), N_EXPERTS - 1)
    tile_expert = owner(jnp.minimum(tile_id, used - 1))
    rows_in_tile = jnp.clip((base + count)[tile_expert] - tile_id * tm, 0, tm)
    tile_subs = jnp.where(tile_id < used, (rows_in_tile + MOE_SUB - 1) // MOE_SUB, 0)
    n_blk = n_tok // SRC_BLK
    blk_counts = jnp.concatenate([jnp.zeros((1, N_EXPERTS), jnp.int32),
                                  csum.reshape(n_blk, 2 * SRC_BLK, N_EXPERTS)[:, -1, :]], axis=0).T
    tile_rank = tile_id * tm - base[tile_expert]
    counts_t = blk_counts[tile_expert]
    tile_blk_lo = jnp.sum((counts_t[:, 1:] <= tile_rank[:, None]).astype(jnp.int32), axis=1)
    tile_blk_hi = jnp.sum((counts_t[:, :-1] < (tile_rank + rows_in_tile)[:, None]).astype(jnp.int32), axis=1)
    tile_blk_n = jnp.where(tile_subs > 0, jnp.maximum(tile_blk_hi - tile_blk_lo, 0), 0)
    tile_blk_lo = jnp.minimum(tile_blk_lo, n_blk - 1)
    posb = jnp.pad(pos.reshape(n_blk, SRC_BLK, 2).transpose(0, 2, 1), ((0, 0), (0, 6), (0, 0)), constant_values=-1)
    ntt = n_tok // CMB_TILE
    through = csum.reshape(ntt, 2 * CMB_TILE, N_EXPERTS)[:, -1, :]
    before = jnp.concatenate([jnp.zeros((1, N_EXPERTS), jnp.int32), through[:-1]], axis=0)
    lo = base[None, :] + before
    hi = base[None, :] + through - 1
    first_blk = lo // Y_BLK
    num_blk = jnp.where(through > before, hi // Y_BLK - first_blk + 1, 0)
    return (posb, pos.reshape(n_tok, 2), tile_expert, tile_subs, tile_blk_lo, tile_blk_n, tile_rank,
            blk_counts.reshape(-1), first_blk.reshape(-1), num_blk.reshape(-1))


def _moe(streams, ada_l, wg, wu, wd, lyr):
    hs = [s[2] for s in streams]
    sels = [s[3][:, :2] for s in streams]
    n_tok = sum(h.shape[0] for h in hs)
    pad = -n_tok % SRC_BLK
    if pad:
        hs.append(jnp.zeros((pad, D_MODEL), BF16))
        sels.append(jnp.full((pad, 2), -1, jnp.int32))
    h_all = hs[0] if len(hs) == 1 else jnp.concatenate(hs, axis=0)
    sel_all = sels[0] if len(sels) == 1 else jnp.concatenate(sels, axis=0)
    (posb, pos, tile_expert, tile_subs, tile_blk_lo, tile_blk_n, tile_rank, blk_counts,
     first_blk, num_blk) = _route(sel_all)
    y = _moe_ffn(h_all, posb, tile_expert, tile_subs, tile_blk_lo, tile_blk_n, tile_rank, blk_counts,
                 wg, wu, wd, lyr)
    outs, off = [], 0
    for x, row, _, _, gt in streams:
        outs.append(_combine(x, y, pos, gt, first_blk, num_blk, off, ada_l, row))
        off += x.shape[0]
    return outs


def _even_layer(x, ctx, ada_l, g1, g2, w_in, g_qn, g_kn, sink, w_gate_up, b_gate_up, g_gla, w_out,
                w_ffn_gate, w_ffn_up, w_ffn_down, lyr, rope_tabs, ctx_out):
    d = D_MODEL
    w_main = jnp.concatenate([w_in[:, 2080:3104], w_in[:, 1024:2048], w_in[:, 3616:4640], w_in[:, 512:1024],
                              w_in[:, 3104:3616], w_in[:, 0:256], w_in[:, 256:512]], axis=1).astype(BF16)
    w_gate = jnp.pad(w_in[:, 2048:2080], ((0, 0), (0, LANES - 2 * GATE_RANK))).astype(BF16)
    nk = B_HEADS * B_DK
    w_up = jnp.concatenate([jnp.pad(w_gate_up[0], ((0, 0), (0, nk))), jnp.pad(w_gate_up[1], ((0, 0), (nk, 0))),
                            jnp.zeros((LANES - 2 * GATE_RANK, 2 * nk), F32)], axis=0)
    w_up_hi = w_up.astype(BF16)
    w_up = (w_up_hi, (w_up - w_up_hi.astype(F32)).astype(BF16))
    b_up = b_gate_up.reshape(1, 2 * nk)
    w_out_b = w_out.astype(BF16)
    gq, gk, gg = g_qn.reshape(1, -1), g_kn.reshape(1, -1), g_gla.reshape(1, -1)
    sink_tab = jnp.broadcast_to(sink[:, None], (A_HEADS, LANES))
    cos, sin = rope_tabs

    pc, la_c = _inproj(ctx, ada_l, 1, g1, w_main, w_gate, w_up, b_up)
    qc, kc = _qkprep(pc, cos, sin, gq, gk, rope=False)
    s0 = jnp.zeros((B_HEADS, B_DV, B_DK), F32)
    oc_f, s_fwd = _gla_scan(pc, la_c, s0, rev=False)
    oc_b, s_bwd = _gla_scan(pc, la_c, s0, rev=True)

    px, la_x = _inproj(x, ada_l, 0, g1, w_main, w_gate, w_up, b_up)
    qx, kx = _qkprep(px, cos, sin, gq, gk, rope=True)
    oa = _attention(qx, kx, px, kc, pc, sink_tab, local=True)
    ox_f, _ = _gla_scan(px, la_x, s_fwd, rev=False)
    ox_b, _ = _gla_scan(px, la_x, s_bwd, rev=True)
    x = _outproj(oa, ox_f, ox_b, px, gg, w_out_b, x, ada_l, 0)
    x = _ffn(x, ada_l, 0, g2, w_ffn_gate, w_ffn_up, w_ffn_down, lyr)
    if ctx_out:
        oa_c = _attention(qc, None, None, kc, pc, sink_tab, local=False)
        ctx = _outproj(oa_c, oc_f, oc_b, pc, gg, w_out_b, ctx, ada_l, 1)
        ctx = _ffn(ctx, ada_l, 1, g2, w_ffn_gate, w_ffn_up, w_ffn_down, lyr)
    return x, ctx


def _odd_layer(x, ctx, ada_l, g1, g2, w_pool, pool_scale, w_router, w_exp_gate, w_exp_up, w_exp_down, lyr, ctx_out):
    w_pool_b = w_pool.astype(BF16)
    ps = pool_scale.reshape(1, -1)
    wr = jnp.pad(w_router, ((0, 0), (0, LANES - N_EXPERTS)))
    wr_hi = wr.astype(BF16)
    wr = (wr_hi, (wr - wr_hi.astype(F32)).astype(BF16))
    streams = [(0,) + tuple(_pool_router(x, ada_l, 0, g1, g2, w_pool_b, ps, wr))]
    if ctx_out:
        streams.append((1,) + tuple(_pool_router(ctx, ada_l, 1, g1, g2, w_pool_b, ps, wr)))
    outs = _moe([(s[1], s[0], s[2], s[3], s[4]) for s in streams], ada_l, w_exp_gate, w_exp_up, w_exp_down, lyr)
    return outs[0], (outs[1] if ctx_out else ctx)


def kernel(x, c, ctx, c_ctx, w_ada, b_ada, norm_g, w_in, g_qn, g_kn, attn_sink, w_gate_up, b_gate_up, g_gla, w_out,
           w_ffn_gate, w_ffn_up, w_ffn_down, w_pool, pool_scale, w_router, w_exp_gate, w_exp_up, w_exp_down):
    depth = w_ada.shape[0]
    xs = x[0]
    cs = ctx[0]
    t = xs.shape[0]
    cond = jnp.concatenate([c, c_ctx[None, :], jnp.zeros((6, D_MODEL), F32)], axis=0)
    ada = _ada_all(cond, w_ada, b_ada)
    rope_tabs = _rope_tables(t)
    for l in range(depth):
        ctx_later = any(j % 2 == 0 for j in range(l + 1, depth))
        g1 = norm_g[l, 0].reshape(1, -1)
        g2 = norm_g[l, 1].reshape(1, -1)
        if l % 2 == 0:
            e = l // 2
            xs, cs = _even_layer(xs, cs, ada[l], g1, g2, w_in[e], g_qn[e], g_kn[e], attn_sink[e], w_gate_up[e],
                                 b_gate_up[e], g_gla[e], w_out[e], w_ffn_gate, w_ffn_up, w_ffn_down, e,
                                 rope_tabs, ctx_later)
        else:
            o = l // 2
            xs, cs = _odd_layer(xs, cs, ada[l], g1, g2, w_pool[o], pool_scale[o], w_router[o],
                                w_exp_gate, w_exp_up, w_exp_down, o, ctx_later)
    return xs[None]
```

```python
import functools

import jax
import jax.numpy as jnp
import numpy as np
from jax import lax
from jax.experimental import pallas as pl
from jax.experimental.pallas import tpu as pltpu

F32 = jnp.float32
BF16 = jnp.bfloat16

D_MODEL = 2048
GRID_W = 64
A_HEADS = 8
A_KV_HEADS = 2
A_GROUP = 4
A_HEAD_DIM = 128
A_WIDTH = A_HEADS * A_HEAD_DIM
ATTN_BLOCK = 128
ROPE_BASE = 10000.0
B_HEADS = 4
B_DV = 256
B_DK = 128
B_WIDTH = B_HEADS * B_DV
GATE_RANK = 16
GATE_TAU = 16.0
GLA_CHUNK = 64
GLA_SUB = 256
POOL_GROUPS = 4
POOL_WINDOWS = (2, 4, 8, 16)
POOL_HALO = 8
N_EXPERTS = 8
NORM_EPS = 1e-6
NEG_INF = -1e30
LANES = 128

P_COLS = 4608
VMEM_LIMIT = 56 * 1024 * 1024
MOE_VMEM_LIMIT = 62 * 1024 * 1024


def _params(sem, vmem=VMEM_LIMIT):
    return pltpu.CompilerParams(dimension_semantics=sem, vmem_limit_bytes=vmem)


def _dot(a, b):
    return jnp.dot(a, b, preferred_element_type=F32)


def _dot_nt(a, b):
    return lax.dot_general(a, b, (((1,), (1,)), ((), ())), preferred_element_type=F32)


def _dot_tn(a, b):
    return lax.dot_general(a, b, (((0,), (0,)), ((), ())), preferred_element_type=F32)


def _silu(x):
    return x * jax.nn.sigmoid(x)


def _norm_mod(x, g, shift, scale):
    ms = jnp.mean(x * x, axis=-1, keepdims=True)
    return (x * lax.rsqrt(ms + NORM_EPS) * g) * (1.0 + scale) + shift


def _ada_kernel(cond_ref, w_ref, b_ref, o_ref):
    s = _silu(cond_ref[...])
    s_hi = s.astype(BF16)
    s_lo = (s - s_hi.astype(F32)).astype(BF16)
    w = w_ref[0].astype(BF16)
    o_ref[0] = _dot(s_hi, w) + _dot(s_lo, w) + b_ref[0]


def _ada_all(cond, w_ada, b_ada):
    depth, d, n = w_ada.shape
    tn = 1024
    return pl.pallas_call(
        _ada_kernel,
        grid=(depth, n // tn),
        in_specs=[pl.BlockSpec((8, d), lambda l, j: (0, 0)),
                  pl.BlockSpec((1, d, tn), lambda l, j: (l, 0, j)),
                  pl.BlockSpec((1, 1, tn), lambda l, j: (l, 0, j))],
        out_specs=pl.BlockSpec((1, 8, tn), lambda l, j: (l, 0, j)),
        out_shape=jax.ShapeDtypeStruct((depth, 8, n), F32),
        compiler_params=_params(("parallel", "parallel")),
    )(cond, w_ada, b_ada.reshape(depth, 1, n))


def _ada_spec(k):
    return pl.BlockSpec((8, D_MODEL), lambda *_: (0, k))


def _inproj_kernel(x_ref, g_ref, sh_ref, sc_ref, w_ref, wg_ref, wuh_ref, wul_ref, bup_ref, p_ref, la_ref, hn_ref,
                   *, row):
    @pl.when(pl.program_id(1) == 0)
    def _():
        h = _norm_mod(x_ref[...], g_ref[...], sh_ref[row:row + 1, :], sc_ref[row:row + 1, :]).astype(BF16)
        hn_ref[...] = h
        lr = _dot(h, wg_ref[...])
        lr_hi = lr.astype(BF16)
        lr_lo = (lr - lr_hi.astype(F32)).astype(BF16)
        z = _dot(lr_hi, wuh_ref[...]) + _dot(lr_lo, wuh_ref[...]) + _dot(lr_hi, wul_ref[...]) + bup_ref[...]
        la_ref[...] = (jnp.minimum(z, 0.0) - jnp.log1p(jnp.exp(-jnp.abs(z)))) * (1.0 / GATE_TAU)

    p_ref[...] = _dot(hn_ref[...], w_ref[...]).astype(p_ref.dtype)


def _inproj(x, ada_l, row, g, w_main, w_gate, w_up, b_up):
    t, d = x.shape
    tm = min(512, t)
    tn = 1536
    nla = 2 * B_HEADS * B_DK
    return pl.pallas_call(
        functools.partial(_inproj_kernel, row=row),
        grid=(t // tm, P_COLS // tn),
        in_specs=[pl.BlockSpec((tm, d), lambda i, j: (i, 0)),
                  pl.BlockSpec((1, d), lambda i, j: (0, 0)),
                  _ada_spec(0), _ada_spec(1),
                  pl.BlockSpec((d, tn), lambda i, j: (0, j)),
                  pl.BlockSpec((d, LANES), lambda i, j: (0, 0)),
                  pl.BlockSpec((LANES, nla), lambda i, j: (0, 0)),
                  pl.BlockSpec((LANES, nla), lambda i, j: (0, 0)),
                  pl.BlockSpec((1, nla), lambda i, j: (0, 0))],
        out_specs=[pl.BlockSpec((tm, tn), lambda i, j: (i, j)),
                   pl.BlockSpec((tm, nla), lambda i, j: (i, 0))],
        out_shape=[jax.ShapeDtypeStruct((t, P_COLS), BF16),
                   jax.ShapeDtypeStruct((t, nla), F32)],
        scratch_shapes=[pltpu.VMEM((tm, d), BF16)],
        compiler_params=_params(("parallel", "arbitrary")),
    )(x, g, ada_l, ada_l, w_main, w_gate, w_up[0], w_up[1], b_up)


def _qkprep_kernel(q_ref, k_ref, cos_ref, sin_ref, gq_ref, gk_ref, qo_ref, ko_ref, *, rope):
    tm = q_ref.shape[0]
    lane = lax.broadcasted_iota(jnp.int32, (tm, A_HEAD_DIM), 1)
    first_half = (lane % 64) < 32

    def prep(xh, g, scale):
        ms = jnp.mean(xh * xh, axis=-1, keepdims=True)
        y = xh * lax.rsqrt(ms + NORM_EPS) * g
        if rope:
            partner = jnp.where(first_half, pltpu.roll(y, 96, 1), pltpu.roll(y, 32, 1))
            y = y * cos_ref[...] + partner * sin_ref[...]
        return (y * scale).astype(BF16)

    for h in range(A_HEADS):
        cols = slice(h * A_HEAD_DIM, (h + 1) * A_HEAD_DIM)
        qo_ref[:, cols] = prep(q_ref[:, cols].astype(F32), gq_ref[...], A_HEAD_DIM ** -0.5)
    for h in range(A_KV_HEADS):
        cols = slice(h * A_HEAD_DIM, (h + 1) * A_HEAD_DIM)
        ko_ref[:, cols] = prep(k_ref[:, cols].astype(F32), gk_ref[...], 1.0)


def _qkprep(p, cos, sin, g_qn, g_kn, rope):
    t = p.shape[0]
    tm = min(512, t)
    kw = A_KV_HEADS * A_HEAD_DIM
    return pl.pallas_call(
        functools.partial(_qkprep_kernel, rope=rope),
        grid=(t // tm,),
        in_specs=[pl.BlockSpec((tm, A_WIDTH), lambda i: (i, 0)),
                  pl.BlockSpec((tm, kw), lambda i: (i, 4096 // kw)),
                  pl.BlockSpec((tm, A_HEAD_DIM), lambda i: (i, 0)),
                  pl.BlockSpec((tm, A_HEAD_DIM), lambda i: (i, 0)),
                  pl.BlockSpec((1, A_HEAD_DIM), lambda i: (0, 0)),
                  pl.BlockSpec((1, A_HEAD_DIM), lambda i: (0, 0))],
        out_specs=[pl.BlockSpec((tm, A_WIDTH), lambda i: (i, 0)),
                   pl.BlockSpec((tm, kw), lambda i: (i, 0))],
        out_shape=[jax.ShapeDtypeStruct((t, A_WIDTH), BF16),
                   jax.ShapeDtypeStruct((t, kw), BF16)],
        compiler_params=_params(("parallel",)),
    )(p, p, cos, sin, g_qn, g_kn)


def _rope_tables(t):
    half = A_HEAD_DIM // 4
    freqs = ROPE_BASE ** (-np.arange(half, dtype=np.float64) / half)
    tok = np.arange(t)
    row = (tok // GRID_W).astype(np.float64)[:, None] * freqs
    col = (tok % GRID_W).astype(np.float64)[:, None] * freqs
    cos = np.concatenate([np.cos(row), np.cos(row), np.cos(col), np.cos(col)], axis=-1)
    sin = np.concatenate([-np.sin(row), np.sin(row), -np.sin(col), np.sin(col)], axis=-1)
    return jnp.asarray(cos.astype(np.float32)), jnp.asarray(sin.astype(np.float32))


def _softmax_pv(scores, values, sink_col):
    def lane_blocks(xs):
        return [x[:, i * LANES:(i + 1) * LANES] for x in xs for i in range(x.shape[1] // LANES)]

    m = jnp.maximum(jnp.max(functools.reduce(jnp.maximum, lane_blocks(scores)), axis=-1, keepdims=True), sink_col)
    probs = [jnp.exp(s - m) for s in scores]
    denom = jnp.exp(sink_col - m) + jnp.sum(functools.reduce(jnp.add, lane_blocks(probs)), axis=-1, keepdims=True)
    acc = None
    for p, v in zip(probs, values):
        pv = _dot(p.astype(BF16), v)
        acc = pv if acc is None else acc + pv
    return acc / denom


def _attn_kernel(*refs, local):
    if local:
        q_ref, kp_ref, ko_ref, kn_ref, vp_ref, vo_ref, vn_ref, kc_ref, vc_ref, sink_ref, o_ref = refs
    else:
        q_ref, kc_ref, vc_ref, sink_ref, o_ref = refs
    i = pl.program_id(0)
    last = pl.num_programs(0) - 1
    rows = A_GROUP * ATTN_BLOCK
    lo, hi = slice(0, ATTN_BLOCK), slice(ATTN_BLOCK, 2 * ATTN_BLOCK)
    if local:
        qi = lax.broadcasted_iota(jnp.int32, (rows, ATTN_BLOCK), 0) % ATTN_BLOCK
        kj = lax.broadcasted_iota(jnp.int32, (rows, ATTN_BLOCK), 1)
        near_prev = kj >= qi
        near_next = kj <= qi
    for a, qr in enumerate((lo, hi)):
        for kv in range(A_KV_HEADS):
            kc = slice(kv * A_HEAD_DIM, (kv + 1) * A_HEAD_DIM)
            heads = [kv * A_GROUP + g for g in range(A_GROUP)]
            q4 = jnp.concatenate([q_ref[qr, h * A_HEAD_DIM:(h + 1) * A_HEAD_DIM] for h in heads], axis=0)
            sink_col = jnp.concatenate(
                [jnp.broadcast_to(sink_ref[h:h + 1, 0:1], (ATTN_BLOCK, 1)) for h in heads], axis=0)
            scores, values = [], []
            if local:
                if a == 0:
                    k_prev, v_prev, mask_prev = kp_ref[:, kc], vp_ref[:, kc], near_prev & (i > 0)
                    k_next, v_next, mask_next = ko_ref[hi, kc], vo_ref[hi, kc], near_next
                else:
                    k_prev, v_prev, mask_prev = ko_ref[lo, kc], vo_ref[lo, kc], near_prev
                    k_next, v_next, mask_next = kn_ref[:, kc], vn_ref[:, kc], near_next & (i < last)
                scores.append(jnp.where(mask_prev, _dot_nt(q4, k_prev), NEG_INF))
                scores.append(_dot_nt(q4, ko_ref[qr, kc]))
                scores.append(jnp.where(mask_next, _dot_nt(q4, k_next), NEG_INF))
                values += [v_prev, vo_ref[qr, kc], v_next]
            scores.append(_dot_nt(q4, kc_ref[:, kc]))
            values.append(vc_ref[:, kc])
            o = _softmax_pv(scores, values, sink_col).astype(o_ref.dtype)
            for g, h in enumerate(heads):
                o_ref[qr, h * A_HEAD_DIM:(h + 1) * A_HEAD_DIM] = o[g * ATTN_BLOCK:(g + 1) * ATTN_BLOCK, :]


def _attention(q, k, p, k_ctx, p_ctx, sink_tab, local):
    t = q.shape[0]
    nb = t // ATTN_BLOCK
    l = k_ctx.shape[0]
    kw = A_KV_HEADS * A_HEAD_DIM
    vcol = 4352 // kw
    qrows = 2 * ATTN_BLOCK
    prev = lambda i: jnp.maximum(2 * i - 1, 0)
    nxt = lambda i: jnp.minimum(2 * i + 2, nb - 1)
    own = lambda i: (i, 0)
    in_specs = [pl.BlockSpec((qrows, A_WIDTH), own)]
    args = [q]
    if local:
        for col, src in ((0, k), (vcol, p)):
            in_specs += [pl.BlockSpec((ATTN_BLOCK, kw), lambda i, col=col: (prev(i), col)),
                         pl.BlockSpec((qrows, kw), lambda i, col=col: (i, col)),
                         pl.BlockSpec((ATTN_BLOCK, kw), lambda i, col=col: (nxt(i), col))]
            args += [src, src, src]
    in_specs += [pl.BlockSpec((l, kw), lambda i: (0, 0)), pl.BlockSpec((l, kw), lambda i: (0, vcol)),
                 pl.BlockSpec((A_HEADS, LANES), lambda i: (0, 0))]
    args += [k_ctx, p_ctx, sink_tab]
    return pl.pallas_call(
        functools.partial(_attn_kernel, local=local),
        grid=(nb // 2,),
        in_specs=in_specs,
        out_specs=pl.BlockSpec((qrows, A_WIDTH), own),
        out_shape=jax.ShapeDtypeStruct((t, A_WIDTH), BF16),
        compiler_params=_params(("parallel",)),
    )(*args)


def _gla_kernel(q_ref, k_ref, v_ref, la_ref, s0_ref, o_ref, sf_ref, st_ref, *, rev, nchunk):
    @pl.when(pl.program_id(0) == 0)
    def _():
        st_ref[...] = s0_ref[...]

    r = nchunk * GLA_CHUNK
    sub = min(r, GLA_SUB)
    ii = lax.broadcasted_iota(jnp.int32, (sub, sub), 0)
    jj = lax.broadcasted_iota(jnp.int32, (sub, sub), 1)
    same_chunk = (ii // GLA_CHUNK) == (jj // GLA_CHUNK)
    tri = same_chunk & ((jj >= ii) if rev else (jj <= ii))
    b_all = la_ref[...]
    pos = lax.broadcasted_iota(jnp.int32, b_all.shape, 0) % GLA_CHUNK
    step = 1
    while step < GLA_CHUNK:
        if rev:
            b_all = b_all + jnp.where(pos < GLA_CHUNK - step, pltpu.roll(b_all, r - step, 0), 0.0)
        else:
            b_all = b_all + jnp.where(pos >= step, pltpu.roll(b_all, step, 0), 0.0)
        step *= 2
    heads = []
    for h in range(B_HEADS):
        kc = slice(h * B_DK, (h + 1) * B_DK)
        b = b_all[:, kc]
        b3 = b.reshape(nchunk, GLA_CHUNK, B_DK)
        b_end = b3[:, 0:1, :] if rev else b3[:, GLA_CHUNK - 1:GLA_CHUNK, :]
        k = k_ref[:, kc].astype(F32)
        qe = (q_ref[:, kc].astype(F32) * (B_DK ** -0.5) * jnp.exp(b)).astype(BF16)
        ke = (k * jnp.exp(-b)).astype(BF16)
        kd = (k.reshape(nchunk, GLA_CHUNK, B_DK) * jnp.exp(b_end - b3)).astype(BF16)
        v = v_ref[:, h * B_DV:(h + 1) * B_DV].astype(BF16)
        o_intra = []
        for s0 in range(0, r, sub):
            rs = slice(s0, s0 + sub)
            a = jnp.where(tri, _dot_nt(qe[rs, :], ke[rs, :]), 0.0).astype(BF16)
            o_intra.append(_dot(a, v[rs, :]))
        heads.append((qe, kd, v, jnp.concatenate(o_intra, axis=0), jnp.exp(b_end)))
    st = [st_ref[h] for h in range(B_HEADS)]
    for c in (range(nchunk - 1, -1, -1) if rev else range(nchunk)):
        rows = slice(c * GLA_CHUNK, (c + 1) * GLA_CHUNK)
        for h, (qe, kd, v, o_intra, decay) in enumerate(heads):
            o = o_intra[rows, :] + _dot_nt(qe[rows, :], st[h].astype(BF16))
            o_ref[0, rows, h * B_DV:(h + 1) * B_DV] = o.astype(o_ref.dtype)
            st[h] = st[h] * decay[c] + _dot_tn(v[rows, :], kd[c])
    for h in range(B_HEADS):
        st_ref[h] = st[h]
        sf_ref[h] = st[h]


def _gla_scan(p, la, s0, rev):
    t = p.shape[0]
    r = min(512, t)
    nblk = t // r
    rb = (lambda c: nblk - 1 - c) if rev else (lambda c: c)
    nk = B_HEADS * B_DK
    o, sf = pl.pallas_call(
        functools.partial(_gla_kernel, rev=rev, nchunk=r // GLA_CHUNK),
        grid=(nblk,),
        in_specs=[pl.BlockSpec((r, nk), lambda c: (rb(c), 3584 // nk)),
                  pl.BlockSpec((r, nk), lambda c: (rb(c), 3072 // nk)),
                  pl.BlockSpec((r, B_WIDTH), lambda c: (rb(c), 1024 // B_WIDTH)),
                  pl.BlockSpec((r, nk), lambda c: (rb(c), 1 if rev else 0)),
                  pl.BlockSpec((B_HEADS, B_DV, B_DK), lambda c: (0, 0, 0))],
        out_specs=[pl.BlockSpec((1, r, B_WIDTH), lambda c: (0, rb(c), 0)),
                   pl.BlockSpec((B_HEADS, B_DV, B_DK), lambda c: (0, 0, 0))],
        out_shape=[jax.ShapeDtypeStruct((1, t, B_WIDTH), BF16),
                   jax.ShapeDtypeStruct((B_HEADS, B_DV, B_DK), F32)],
        scratch_shapes=[pltpu.VMEM((B_HEADS, B_DV, B_DK), F32)],
        compiler_params=_params(("arbitrary",)),
    )(p, p, p, la, s0)
    return o[0], sf


def _outproj_kernel(oa_ref, of_ref, ob_ref, og_ref, gg_ref, w_ref, x_ref, gt_ref, o_ref, mix_ref, *, row):
    @pl.when(pl.program_id(1) == 0)
    def _():
        mix_ref[:, 0:A_WIDTH] = oa_ref[...]
        for h in range(B_HEADS):
            cols = slice(h * B_DV, (h + 1) * B_DV)
            o = of_ref[:, cols].astype(F32) + ob_ref[:, cols].astype(F32)
            ms = jnp.mean(o * o, axis=-1, keepdims=True)
            y = o * lax.rsqrt(ms + NORM_EPS) * gg_ref[...]
            mix_ref[:, A_WIDTH + h * B_DV:A_WIDTH + (h + 1) * B_DV] = (y * _silu(og_ref[:, cols].astype(F32))).astype(BF16)

    o_ref[...] = x_ref[...] + gt_ref[row:row + 1, :] * _dot(mix_ref[...], w_ref[...])


def _outproj(oa, o_f, o_b, p, g_gla, w_out, x, ada_l, row):
    t, d = x.shape
    tm = min(512, t)
    tn = d
    return pl.pallas_call(
        functools.partial(_outproj_kernel, row=row),
        grid=(t // tm, d // tn),
        in_specs=[pl.BlockSpec((tm, A_WIDTH), lambda i, j: (i, 0)),
                  pl.BlockSpec((tm, B_WIDTH), lambda i, j: (i, 0)),
                  pl.BlockSpec((tm, B_WIDTH), lambda i, j: (i, 0)),
                  pl.BlockSpec((tm, B_WIDTH), lambda i, j: (i, 2048 // B_WIDTH)),
                  pl.BlockSpec((1, B_DV), lambda i, j: (0, 0)),
                  pl.BlockSpec((A_WIDTH + B_WIDTH, tn), lambda i, j: (0, j)),
                  pl.BlockSpec((tm, tn), lambda i, j: (i, j)),
                  pl.BlockSpec((8, tn), lambda i, j: (0, 2 * (d // tn) + j))],
        out_specs=pl.BlockSpec((tm, tn), lambda i, j: (i, j)),
        out_shape=jax.ShapeDtypeStruct((t, d), F32),
        scratch_shapes=[pltpu.VMEM((tm, A_WIDTH + B_WIDTH), BF16)],
        compiler_params=_params(("parallel", "arbitrary")),
    )(oa, o_f, o_b, p, g_gla, w_out, x, ada_l)


def _ffn_kernel(x_ref, g_ref, sh_ref, sc_ref, gt_ref, wg_ref, wu_ref, wd_ref, o_ref, hn_ref, *, row):
    f = pl.program_id(1)

    @pl.when(f == 0)
    def _():
        hn_ref[...] = _norm_mod(x_ref[...], g_ref[...], sh_ref[row:row + 1, :], sc_ref[row:row + 1, :]).astype(BF16)
        o_ref[...] = jnp.zeros_like(o_ref)

    h = hn_ref[...]
    a = (_silu(_dot(h, wg_ref[0].astype(BF16))) * _dot(h, wu_ref[0].astype(BF16))).astype(BF16)
    o_ref[...] += _dot(a, wd_ref[0].astype(BF16))

    @pl.when(f == pl.num_programs(1) - 1)
    def _():
        o_ref[...] = x_ref[...] + gt_ref[row:row + 1, :] * o_ref[...]


def _ffn(x, ada_l, row, g, wg, wu, wd, lyr):
    t, d = x.shape
    ff = wg.shape[2]
    tm = min(1024, t)
    tf = 256
    return pl.pallas_call(
        functools.partial(_ffn_kernel, row=row),
        grid=(t // tm, ff // tf),
        in_specs=[pl.BlockSpec((tm, d), lambda i, f: (i, 0)),
                  pl.BlockSpec((1, d), lambda i, f: (0, 0)),
                  _ada_spec(3), _ada_spec(4), _ada_spec(5),
                  pl.BlockSpec((1, d, tf), lambda i, f: (lyr, 0, f)),
                  pl.BlockSpec((1, d, tf), lambda i, f: (lyr, 0, f)),
                  pl.BlockSpec((1, tf, d), lambda i, f: (lyr, f, 0))],
        out_specs=pl.BlockSpec((tm, d), lambda i, f: (i, 0)),
        out_shape=jax.ShapeDtypeStruct((t, d), F32),
        scratch_shapes=[pltpu.VMEM((tm, d), BF16)],
        compiler_params=_params(("parallel", "arbitrary"), MOE_VMEM_LIMIT),
    )(x, g, ada_l, ada_l, ada_l, wg, wu, wd)


def _pool_router_kernel(x_ref, xp_ref, xn_ref, g_ref, sh_ref, sc_ref, gt_ref, w_ref, ps_ref,
                        g2_ref, sh2_ref, sc2_ref, wrh_ref, wrl_ref, o_ref, hn_ref, sel_ref, gate_ref, hb_ref,
                        *, row, t):
    i = pl.program_id(0)
    tm = x_ref.shape[0]
    g = g_ref[...]
    sh = sh_ref[row:row + 1, :]
    sc = sc_ref[row:row + 1, :]
    hb_ref[0:POOL_HALO, :] = jnp.where(i > 0, _norm_mod(xp_ref[...], g, sh, sc), 0.0)
    hb_ref[POOL_HALO:POOL_HALO + tm, :] = _norm_mod(x_ref[...], g, sh, sc)
    hb_ref[POOL_HALO + tm:, :] = jnp.where(i < pl.num_programs(0) - 1, _norm_mod(xn_ref[...], g, sh, sc), 0.0)
    tpos = i * tm + lax.broadcasted_iota(jnp.int32, (tm, 1), 0)
    gc = D_MODEL // POOL_GROUPS
    for grp, w in enumerate(POOL_WINDOWS):
        cols = slice(grp * gc, (grp + 1) * gc)
        acc = None
        for dlt in range(-(w // 2), w - w // 2):
            piece = hb_ref[POOL_HALO + dlt:POOL_HALO + dlt + tm, cols]
            acc = piece if acc is None else acc + piece
        cnt = jnp.minimum(tpos + (w - w // 2), t) - jnp.maximum(tpos - w // 2, 0)
        pooled = acc / cnt.astype(F32) - hb_ref[POOL_HALO:POOL_HALO + tm, cols]
        y = _dot(pooled.astype(BF16), w_ref[grp])
        o_ref[:, cols] = x_ref[:, cols] + gt_ref[row:row + 1, cols] * (y * ps_ref[:, cols])

    h = _norm_mod(o_ref[...], g2_ref[...], sh2_ref[row:row + 1, :], sc2_ref[row:row + 1, :])
    hn_ref[...] = h.astype(hn_ref.dtype)
    lane = lax.broadcasted_iota(jnp.int32, (h.shape[0], LANES), 1)
    lane_f = lane.astype(F32)
    h_hi = h.astype(BF16)
    h_lo = (h - h_hi.astype(F32)).astype(BF16)
    logits = _dot(h_hi, wrh_ref[...]) + _dot(h_lo, wrh_ref[...]) + _dot(h_hi, wrl_ref[...])
    logits = jnp.where(lane < N_EXPERTS, logits, -jnp.inf)
    m1 = jnp.max(logits, axis=-1, keepdims=True)
    i1 = jnp.min(jnp.where(logits == m1, lane_f, float(LANES)), axis=-1, keepdims=True)
    rest = jnp.where(lane_f == i1, -jnp.inf, logits)
    m2 = jnp.max(rest, axis=-1, keepdims=True)
    i2 = jnp.min(jnp.where(rest == m2, lane_f, float(LANES)), axis=-1, keepdims=True)
    e2 = jnp.exp(m2 - m1)
    den = 1.0 + e2
    sel_ref[...] = jnp.where(lane == 0, i1, jnp.where(lane == 1, i2, 0.0)).astype(jnp.int32)
    gate_ref[...] = jnp.where(lane == 0, 1.0 / den, jnp.where(lane == 1, e2 / den, 0.0))


def _pool_router(x, ada_l, row, g1, g2, w_pool, pool_scale, w_router_pad):
    t, d = x.shape
    tm = min(256, t)
    nh = t // POOL_HALO
    gc = d // POOL_GROUPS
    rows = lambda i: (i, 0)
    const = lambda i: (0, 0)
    return pl.pallas_call(
        functools.partial(_pool_router_kernel, row=row, t=t),
        grid=(t // tm,),
        in_specs=[pl.BlockSpec((tm, d), rows),
                  pl.BlockSpec((POOL_HALO, d), lambda i: (jnp.maximum(i * (tm // POOL_HALO) - 1, 0), 0)),
                  pl.BlockSpec((POOL_HALO, d), lambda i: (jnp.minimum((i + 1) * (tm // POOL_HALO), nh - 1), 0)),
                  pl.BlockSpec((1, d), const),
                  _ada_spec(0), _ada_spec(1), _ada_spec(2),
                  pl.BlockSpec((POOL_GROUPS, gc, gc), lambda i: (0, 0, 0)),
                  pl.BlockSpec((1, d), const),
                  pl.BlockSpec((1, d), const),
                  _ada_spec(3), _ada_spec(4),
                  pl.BlockSpec((d, LANES), const), pl.BlockSpec((d, LANES), const)],
        out_specs=[pl.BlockSpec((tm, d), rows), pl.BlockSpec((tm, d), rows),
                   pl.BlockSpec((tm, LANES), rows), pl.BlockSpec((tm, LANES), rows)],
        out_shape=[jax.ShapeDtypeStruct((t, d), F32),
                   jax.ShapeDtypeStruct((t, d), BF16),
                   jax.ShapeDtypeStruct((t, LANES), jnp.int32),
                   jax.ShapeDtypeStruct((t, LANES), F32)],
        scratch_shapes=[pltpu.VMEM((tm + 2 * POOL_HALO, d), F32)],
        compiler_params=_params(("parallel",)),
    )(x, x, x, g1, ada_l, ada_l, ada_l, w_pool, pool_scale, g2, ada_l, ada_l, w_router_pad[0], w_router_pad[1])


MOE_TILE = 1280
MOE_SUB = 256
MOE_SUBS = MOE_TILE // MOE_SUB
GATHER_SUB = 256
GATHER_SUBS = MOE_TILE // GATHER_SUB
SRC_BLK = 256
Y_BLK = 256
CMB_TILE = 256
MOE_TF = 512
GATHER_BUFS = 3


def _moe_ffn_kernel(te_ref, tn_ref, tlo_ref, tnb_ref, trk_ref, cb_ref, posb_ref, hn_hbm, wg_ref, wu_ref, wd_ref,
                    y_ref, xb_ref, acc_ref, hbuf_ref, sem):
    i = pl.program_id(0)
    f = pl.program_id(1)
    nsub = tn_ref[i]

    @pl.when(f == 0)
    def _():
        xb_ref[...] = jnp.zeros_like(xb_ref)
        acc_ref[...] = jnp.zeros_like(acc_ref)
        row_id = lax.broadcasted_iota(jnp.int32, (GATHER_SUB, SRC_BLK), 0) + i * MOE_TILE
        rows_used = nsub * MOE_SUB
        lo = tlo_ref[i]
        n = tnb_ref[i]
        rank0 = trk_ref[i]
        cb_base = te_ref[i] * (posb_ref.shape[0] + 1)

        def blk_copy(b, slot):
            return pltpu.make_async_copy(hn_hbm.at[pl.ds((lo + b) * SRC_BLK, SRC_BLK), :], hbuf_ref.at[slot],
                                         sem.at[slot])

        for j in range(GATHER_BUFS - 1):
            @pl.when(j < n)
            def _():
                blk_copy(j, j).start()

        def body(b, carry):
            slot = b % GATHER_BUFS
            blk_copy(b, slot).wait()
            ahead = b + GATHER_BUFS - 1

            @pl.when(ahead < n)
            def _():
                blk_copy(ahead, ahead % GATHER_BUFS).start()

            blk = lo + b
            pos = posb_ref[blk]
            pos1, pos2 = pos[0:1, :], pos[1:2, :]
            before = cb_ref[cb_base + blk]
            through = cb_ref[cb_base + blk + 1]
            for q in range(GATHER_SUBS):
                @pl.when((q * GATHER_SUB < rows_used) & (before < rank0 + (q + 1) * GATHER_SUB)
                         & (through > rank0 + q * GATHER_SUB))
                def _():
                    rows = slice(q * GATHER_SUB, (q + 1) * GATHER_SUB)
                    rid = row_id + q * GATHER_SUB
                    onehot = jnp.where((pos1 == rid) | (pos2 == rid), 1.0, 0.0).astype(BF16)
                    xb_ref[rows, :] += _dot(onehot, hbuf_ref[slot]).astype(BF16)
            return carry

        lax.fori_loop(0, n, body, 0)

    for k in range(1, MOE_SUBS + 1):
        @pl.when(nsub == k)
        def _():
            rows = slice(0, k * MOE_SUB)
            h = xb_ref[rows, :]
            a = (_silu(_dot(h, wg_ref[0, 0].astype(BF16))) * _dot(h, wu_ref[0, 0].astype(BF16))).astype(BF16)
            acc_ref[rows, :] += _dot(a, wd_ref[0, 0].astype(BF16))

    @pl.when(f == pl.num_programs(1) - 1)
    def _():
        y_ref[...] = acc_ref[...].astype(y_ref.dtype)


def _moe_ffn(hn, posb, tile_expert, tile_subs, tile_blk_lo, tile_blk_n, tile_rank, blk_counts, wg, wu, wd, lyr):
    d = hn.shape[1]
    tm = MOE_TILE
    ff = wg.shape[3]
    tf = MOE_TF
    nf = ff // tf
    nt = tile_expert.shape[0]
    fidx = lambda i, f, tn: jnp.where(tn[i] > 0, f, nf - 1)
    return pl.pallas_call(
        _moe_ffn_kernel,
        grid_spec=pltpu.PrefetchScalarGridSpec(
            num_scalar_prefetch=6,
            grid=(nt, nf),
            in_specs=[pl.BlockSpec(posb.shape, lambda i, f, te, tn, *_: (0, 0, 0)),
                      pl.BlockSpec(memory_space=pl.ANY),
                      pl.BlockSpec((1, 1, d, tf), lambda i, f, te, tn, *_: (lyr, te[i], 0, fidx(i, f, tn))),
                      pl.BlockSpec((1, 1, d, tf), lambda i, f, te, tn, *_: (lyr, te[i], 0, fidx(i, f, tn))),
                      pl.BlockSpec((1, 1, tf, d), lambda i, f, te, tn, *_: (lyr, te[i], fidx(i, f, tn), 0))],
            out_specs=pl.BlockSpec((tm, d), lambda i, f, te, tn, *_: (i, 0)),
            scratch_shapes=[pltpu.VMEM((tm, d), BF16), pltpu.VMEM((tm, d), F32),
                            pltpu.VMEM((GATHER_BUFS, SRC_BLK, d), BF16),
                            pltpu.SemaphoreType.DMA((GATHER_BUFS,))]),
        out_shape=jax.ShapeDtypeStruct((nt * tm, d), BF16),
        compiler_params=_params(("arbitrary", "arbitrary"), MOE_VMEM_LIMIT),
    )(tile_expert, tile_subs, tile_blk_lo, tile_blk_n, tile_rank, blk_counts, posb, hn, wg, wu, wd)


CMB_SLOTS = [(e, j) for e in range(N_EXPERTS) for j in range(2)]


def _combine_kernel(fb_ref, nb_ref, x_ref, pos_ref, gate_ref, gt_ref, y_hbm, o_ref, ybuf_ref, sem, *, row, tile_off):
    i = pl.program_id(0)
    nslot = len(CMB_SLOTS)
    tm, d = x_ref.shape
    half = i % 2

    def blk(t, s):
        e, j = CMB_SLOTS[s]
        return fb_ref[(tile_off + t) * N_EXPERTS + e] + j

    def used(t, s):
        e, j = CMB_SLOTS[s]
        return nb_ref[(tile_off + t) * N_EXPERTS + e] > j

    def blk_copy(t, s, h):
        return pltpu.make_async_copy(y_hbm.at[pl.ds(blk(t, s) * Y_BLK, Y_BLK), :], ybuf_ref.at[h, s], sem.at[h, s])

    def start_tile(t, h):
        for s in range(nslot):
            @pl.when(used(t, s))
            def _():
                blk_copy(t, s, h).start()

    @pl.when(i == 0)
    def _():
        ybuf_ref[...] = jnp.zeros_like(ybuf_ref)
        start_tile(0, 0)

    @pl.when(i + 1 < pl.num_programs(0))
    def _():
        start_tile(i + 1, 1 - half)

    pos1, pos2 = pos_ref[:, 0:1], pos_ref[:, 1:2]
    g1, g2 = gate_ref[:, 0:1], gate_ref[:, 1:2]
    col = lax.broadcasted_iota(jnp.int32, (tm, Y_BLK), 1)
    weights = []
    for s in range(nslot):
        off = blk(i, s) * Y_BLK
        w = jnp.where(pos1 - off == col, g1, 0.0) + jnp.where(pos2 - off == col, g2, 0.0)
        weights.append(jnp.where(used(i, s), w, 0.0).astype(BF16))
    w_all = jnp.concatenate(weights, axis=1)

    for s in range(nslot):
        @pl.when(used(i, s))
        def _():
            blk_copy(i, s, half).wait()

    acc = _dot(w_all, ybuf_ref[half].reshape(nslot * Y_BLK, d))
    o_ref[...] = x_ref[...] + gt_ref[row:row + 1, :] * acc


def _combine(x, y, pos, gates, first_blk, num_blk, tok_off, ada_l, row):
    t, d = x.shape
    tm = CMB_TILE
    tile_off = tok_off // tm
    return pl.pallas_call(
        functools.partial(_combine_kernel, row=row, tile_off=tile_off),
        grid_spec=pltpu.PrefetchScalarGridSpec(
            num_scalar_prefetch=2,
            grid=(t // tm,),
            in_specs=[pl.BlockSpec((tm, d), lambda i, fb, nb: (i, 0)),
                      pl.BlockSpec((tm, 2), lambda i, fb, nb: (tile_off + i, 0)),
                      pl.BlockSpec((tm, LANES), lambda i, fb, nb: (i, 0)),
                      _ada_spec(5),
                      pl.BlockSpec(memory_space=pl.ANY)],
            out_specs=pl.BlockSpec((tm, d), lambda i, fb, nb: (i, 0)),
            scratch_shapes=[pltpu.VMEM((2, len(CMB_SLOTS), Y_BLK, d), BF16),
                            pltpu.SemaphoreType.DMA((2, len(CMB_SLOTS)))]),
        out_shape=jax.ShapeDtypeStruct((t, d), F32),
        compiler_params=_params(("arbitrary",)),
    )(first_blk, num_blk, x, pos, gates, ada_l, y)


def _route(sel):
    tm = MOE_TILE
    n_tok = sel.shape[0]
    n_asg = 2 * n_tok
    nt = -(-n_asg // tm) + N_EXPERTS
    e_flat = sel.reshape(-1)
    onehot = (e_flat[:, None] == jnp.arange(N_EXPERTS, dtype=jnp.int32)[None, :]).astype(jnp.int32)
    csum = jnp.cumsum(onehot, axis=0)
    count = csum[-1]
    rank = jnp.sum((csum - onehot) * onehot, axis=1)
    tiles_e = (count + tm - 1) // tm
    tile_end = jnp.cumsum(tiles_e)
    base = (tile_end - tiles_e) * tm
    pos = jnp.sum(onehot * base[None, :], axis=1) + rank
    pos = jnp.where(e_flat >= 0, pos, -1)
    tile_id = jnp.arange(nt, dtype=jnp.int32)
    used = tile_end[-1]
    owner = lambda i: jnp.minimum(jnp.sum((i[:, None] >= tile_end[None, :]).astype(jnp.int32), axis=1), N_EXPERTS - 1)
    tile_expert = owner(jnp.minimum(tile_id, used - 1))
    rows_in_tile = jnp.clip((base + count)[tile_expert] - tile_id * tm, 0, tm)
    tile_subs = jnp.where(tile_id < used, (rows_in_tile + MOE_SUB - 1) // MOE_SUB, 0)
    n_blk = n_tok // SRC_BLK
    blk_counts = jnp.concatenate([jnp.zeros((1, N_EXPERTS), jnp.int32),
                                  csum.reshape(n_blk, 2 * SRC_BLK, N_EXPERTS)[:, -1, :]], axis=0).T
    tile_rank = tile_id * tm - base[tile_expert]
    counts_t = blk_counts[tile_expert]
    tile_blk_lo = jnp.sum((counts_t[:, 1:] <= tile_rank[:, None]).astype(jnp.int32), axis=1)
    tile_blk_hi = jnp.sum((counts_t[:, :-1] < (tile_rank + rows_in_tile)[:, None]).astype(jnp.int32), axis=1)
    tile_blk_n = jnp.where(tile_subs > 0, jnp.maximum(tile_blk_hi - tile_blk_lo, 0), 0)
    tile_blk_lo = jnp.minimum(tile_blk_lo, n_blk - 1)
    posb = jnp.pad(pos.reshape(n_blk, SRC_BLK, 2).transpose(0, 2, 1), ((0, 0), (0, 6), (0, 0)), constant_values=-1)
    ntt = n_tok // CMB_TILE
    through = csum.reshape(ntt, 2 * CMB_TILE, N_EXPERTS)[:, -1, :]
    before = jnp.concatenate([jnp.zeros((1, N_EXPERTS), jnp.int32), through[:-1]], axis=0)
    lo = base[None, :] + before
    hi = base[None, :] + through - 1
    first_blk = lo // Y_BLK
    num_blk = jnp.where(through > before, hi // Y_BLK - first_blk + 1, 0)
    return (posb, pos.reshape(n_tok, 2), tile_expert, tile_subs, tile_blk_lo, tile_blk_n, tile_rank,
            blk_counts.reshape(-1), first_blk.reshape(-1), num_blk.reshape(-1))


def _moe(streams, ada_l, wg, wu, wd, lyr):
    hs = [s[2] for s in streams]
    sels = [s[3][:, :2] for s in streams]
    n_tok = sum(h.shape[0] for h in hs)
    pad = -n_tok % SRC_BLK
    if pad:
        hs.append(jnp.zeros((pad, D_MODEL), BF16))
        sels.append(jnp.full((pad, 2), -1, jnp.int32))
    h_all = hs[0] if len(hs) == 1 else jnp.concatenate(hs, axis=0)
    sel_all = sels[0] if len(sels) == 1 else jnp.concatenate(sels, axis=0)
    (posb, pos, tile_expert, tile_subs, tile_blk_lo, tile_blk_n, tile_rank, blk_counts,
     first_blk, num_blk) = _route(sel_all)
    y = _moe_ffn(h_all, posb, tile_expert, tile_subs, tile_blk_lo, tile_blk_n, tile_rank, blk_counts,
                 wg, wu, wd, lyr)
    outs, off = [], 0
    for x, row, _, _, gt in streams:
        outs.append(_combine(x, y, pos, gt, first_blk, num_blk, off, ada_l, row))
        off += x.shape[0]
    return outs


def _even_layer(x, ctx, ada_l, g1, g2, w_in, g_qn, g_kn, sink, w_gate_up, b_gate_up, g_gla, w_out,
                w_ffn_gate, w_ffn_up, w_ffn_down, lyr, rope_tabs, ctx_out):
    d = D_MODEL
    w_main = jnp.concatenate([w_in[:, 2080:3104], w_in[:, 1024:2048], w_in[:, 3616:4640], w_in[:, 512:1024],
                              w_in[:, 3104:3616], w_in[:, 0:256], w_in[:, 256:512]], axis=1).astype(BF16)
    w_gate = jnp.pad(w_in[:, 2048:2080], ((0, 0), (0, LANES - 2 * GATE_RANK))).astype(BF16)
    nk = B_HEADS * B_DK
    w_up = jnp.concatenate([jnp.pad(w_gate_up[0], ((0, 0), (0, nk))), jnp.pad(w_gate_up[1], ((0, 0), (nk, 0))),
                            jnp.zeros((LANES - 2 * GATE_RANK, 2 * nk), F32)], axis=0)
    w_up_hi = w_up.astype(BF16)
    w_up = (w_up_hi, (w_up - w_up_hi.astype(F32)).astype(BF16))
    b_up = b_gate_up.reshape(1, 2 * nk)
    w_out_b = w_out.astype(BF16)
    gq, gk, gg = g_qn.reshape(1, -1), g_kn.reshape(1, -1), g_gla.reshape(1, -1)
    sink_tab = jnp.broadcast_to(sink[:, None], (A_HEADS, LANES))
    cos, sin = rope_tabs

    pc, la_c = _inproj(ctx, ada_l, 1, g1, w_main, w_gate, w_up, b_up)
    qc, kc = _qkprep(pc, cos, sin, gq, gk, rope=False)
    s0 = jnp.zeros((B_HEADS, B_DV, B_DK), F32)
    oc_f, s_fwd = _gla_scan(pc, la_c, s0, rev=False)
    oc_b, s_bwd = _gla_scan(pc, la_c, s0, rev=True)

    px, la_x = _inproj(x, ada_l, 0, g1, w_main, w_gate, w_up, b_up)
    qx, kx = _qkprep(px, cos, sin, gq, gk, rope=True)
    oa = _attention(qx, kx, px, kc, pc, sink_tab, local=True)
    ox_f, _ = _gla_scan(px, la_x, s_fwd, rev=False)
    ox_b, _ = _gla_scan(px, la_x, s_bwd, rev=True)
    x = _outproj(oa, ox_f, ox_b, px, gg, w_out_b, x, ada_l, 0)
    x = _ffn(x, ada_l, 0, g2, w_ffn_gate, w_ffn_up, w_ffn_down, lyr)
    if ctx_out:
        oa_c = _attention(qc, None, None, kc, pc, sink_tab, local=False)
        ctx = _outproj(oa_c, oc_f, oc_b, pc, gg, w_out_b, ctx, ada_l, 1)
        ctx = _ffn(ctx, ada_l, 1, g2, w_ffn_gate, w_ffn_up, w_ffn_down, lyr)
    return x, ctx


def _odd_layer(x, ctx, ada_l, g1, g2, w_pool, pool_scale, w_router, w_exp_gate, w_exp_up, w_exp_down, lyr, ctx_out):
    w_pool_b = w_pool.astype(BF16)
    ps = pool_scale.reshape(1, -1)
    wr = jnp.pad(w_router, ((0, 0), (0, LANES - N_EXPERTS)))
    wr_hi = wr.astype(BF16)
    wr = (wr_hi, (wr - wr_hi.astype(F32)).astype(BF16))
    streams = [(0,) + tuple(_pool_router(x, ada_l, 0, g1, g2, w_pool_b, ps, wr))]
    if ctx_out:
        streams.append((1,) + tuple(_pool_router(ctx, ada_l, 1, g1, g2, w_pool_b, ps, wr)))
    outs = _moe([(s[1], s[0], s[2], s[3], s[4]) for s in streams], ada_l, w_exp_gate, w_exp_up, w_exp_down, lyr)
    return outs[0], (outs[1] if ctx_out else ctx)


def kernel(x, c, ctx, c_ctx, w_ada, b_ada, norm_g, w_in, g_qn, g_kn, attn_sink, w_gate_up, b_gate_up, g_gla, w_out,
           w_ffn_gate, w_ffn_up, w_ffn_down, w_pool, pool_scale, w_router, w_exp_gate, w_exp_up, w_exp_down):
    depth = w_ada.shape[0]
    xs = x[0]
    cs = ctx[0]
    t = xs.shape[0]
    cond = jnp.concatenate([c, c_ctx[None, :], jnp.zeros((6, D_MODEL), F32)], axis=0)
    ada = _ada_all(cond, w_ada, b_ada)
    rope_tabs = _rope_tables(t)
    for l in range(depth):
        ctx_later = any(j % 2 == 0 for j in range(l + 1, depth))
        g1 = norm_g[l, 0].reshape(1, -1)
        g2 = norm_g[l, 1].reshape(1, -1)
        if l % 2 == 0:
            e = l // 2
            xs, cs = _even_layer(xs, cs, ada[l], g1, g2, w_in[e], g_qn[e], g_kn[e], attn_sink[e], w_gate_up[e],
                                 b_gate_up[e], g_gla[e], w_out[e], w_ffn_gate, w_ffn_up, w_ffn_down, e,
                                 rope_tabs, ctx_later)
        else:
            o = l // 2
            xs, cs = _odd_layer(xs, cs, ada[l], g1, g2, w_pool[o], pool_scale[o], w_router[o],
                                w_exp_gate, w_exp_up, w_exp_down, o, ctx_later)
    return xs[None]
```
